```python
import math
import jax
import jax.numpy as jnp
from jax import lax
import numpy as np

D_MODEL = 1024
BATCH = 8
SEQ = 2048
DEPTH = 1
DEC_BATCH = 2
DEC_SEQ = 16384
PAST_LEN = 128

SSD_EXPAND = 2
D_INNER = SSD_EXPAND * D_MODEL
SSD_HEAD_DIM = 64
SSD_HEADS = D_INNER // SSD_HEAD_DIM
SSD_GROUPS = 8
SSD_STATE = 128
SSD_CHUNK = 128
CONV_WIDTH = 5
CONV_PAD = CONV_WIDTH // 2
CONV_DIM = D_INNER + 2 * SSD_GROUPS * SSD_STATE

ATTN_PATTERNS = ((128, 1), (512, 4), (2048, 16))
ATTN_HEADS_PER_GROUP = 8
ATTN_HEAD_DIM = 64
ATTN_HEADS = ATTN_HEADS_PER_GROUP * len(ATTN_PATTERNS)
ATTN_DIM = ATTN_HEADS * ATTN_HEAD_DIM
ATTN_OUT = ATTN_HEADS_PER_GROUP * ATTN_HEAD_DIM
ROT_DIM = ATTN_HEAD_DIM // 4
ROPE_THETA = 500000.0

N_EXPERTS = 256
EXPERT_HIDDEN = 256
TOP_K = 8
N_EXPERT_GROUPS = 8
TOPK_GROUPS = 4
ROUTED_SCALE = 2.5
SHARED_HIDDEN = 256
EXPERT_BLOCK = 128

ALPHA = (2.0 * DEPTH) ** 0.25
BETA = (8.0 * DEPTH) ** -0.25
NORM_EPS = 1e-5

IN_SPLITS = (D_INNER, CONV_DIM, SSD_HEADS, SSD_HEADS, ATTN_DIM, ATTN_DIM, ATTN_DIM, D_MODEL, D_MODEL)
SPLIT_POINTS = tuple(sum(IN_SPLITS[:i + 1]) for i in range(len(IN_SPLITS) - 1))
N_IN = sum(IN_SPLITS)

kernel_name = 'hybrid_ssd_dilated_attn_moe_encoder'


def layer_norm(x, g, b):
    xf = x.astype(jnp.float32)
    mu = jnp.mean(xf, axis=-1, keepdims=True)
    var = jnp.mean(jnp.square(xf - mu), axis=-1, keepdims=True)
    return ((xf - mu) * lax.rsqrt(var + NORM_EPS) * g + b).astype(x.dtype)


def rms_norm(x, g):
    xf = x.astype(jnp.float32)
    return xf * lax.rsqrt(jnp.mean(jnp.square(xf), axis=-1, keepdims=True) + NORM_EPS) * g


def centred_depthwise_conv(x, w, bias):
    out = lax.conv_general_dilated(x, w[:, None, :], window_strides=(1,), padding=[(CONV_PAD, CONV_PAD)],
                                   dimension_numbers=('NWC', 'WIO', 'NWC'), feature_group_count=x.shape[-1])
    return out + bias


def ssd_chunked(x, dt, a, bmat, cmat):
    b, l, h, p = x.shape
    g, n = bmat.shape[2], bmat.shape[3]
    e = h // g
    c = l // SSD_CHUNK
    xs = (x * dt[..., None]).reshape(b, c, SSD_CHUNK, g, e, p)
    da = (dt * a).reshape(b, c, SSD_CHUNK, g, e)
    bs = bmat.reshape(b, c, SSD_CHUNK, g, n)
    cs = cmat.reshape(b, c, SSD_CHUNK, g, n)
    cum = jnp.cumsum(da, axis=2)
    tril = jnp.tril(jnp.ones((SSD_CHUNK, SSD_CHUNK), dtype=bool))
    seg = cum[:, :, :, None] - cum[:, :, None, :]
    decay = jnp.exp(jnp.where(tril[None, None, :, :, None, None], seg, -jnp.inf))
    cb = jnp.einsum('bclgn,bcsgn->bclsg', cs, bs)
    y_diag = jnp.einsum('bclsg,bclsge,bcsgep->bclgep', cb, decay, xs)
    decay_to_end = jnp.exp(cum[:, :, -1:] - cum)
    states = jnp.einsum('bclgn,bclge,bclgep->bcgepn', bs, decay_to_end, xs)
    chunk_decay = jnp.exp(cum[:, :, -1])

    def step(h_prev, inp):
        st, dec = inp
        return dec[..., None, None] * h_prev + st, h_prev

    _, prev = lax.scan(step, jnp.zeros_like(states[:, 0]),
                       (jnp.moveaxis(states, 1, 0), jnp.moveaxis(chunk_decay, 1, 0)))
    prev = jnp.moveaxis(prev, 0, 1)
    y_off = jnp.einsum('bclgn,bclge,bcgepn->bclgep', cs, jnp.exp(cum), prev)
    return (y_diag + y_off).reshape(b, l, h, p)


def partial_rotary(t, pos):
    half = ROT_DIM // 2
    inv_freq = ROPE_THETA ** (-jnp.arange(half, dtype=jnp.float32) * 2.0 / ROT_DIM)
    ang = pos[:, None] * inv_freq[None, :]
    cos = jnp.cos(ang)[None, :, None, :].astype(t.dtype)
    sin = jnp.sin(ang)[None, :, None, :].astype(t.dtype)
    x1 = t[..., :half]
    x2 = t[..., half:ROT_DIM]
    return jnp.concatenate([x1 * cos - x2 * sin, x2 * cos + x1 * sin, t[..., ROT_DIM:]], axis=-1)


def banded_attention(q, k, v, radius):
    n, m, h, d = q.shape
    blk = radius
    nb = -(-m // blk)
    mp = nb * blk
    qp = jnp.pad(q, ((0, 0), (0, mp - m), (0, 0), (0, 0)))
    kp = jnp.pad(k, ((0, 0), (blk, mp - m + blk), (0, 0), (0, 0)))
    vp = jnp.pad(v, ((0, 0), (blk, mp - m + blk), (0, 0), (0, 0)))
    qb = qp.reshape(n, nb, blk, h, d)
    kb = kp.reshape(n, nb + 2, blk, h, d)
    vb = vp.reshape(n, nb + 2, blk, h, d)
    kw = jnp.concatenate([kb[:, :-2], kb[:, 1:-1], kb[:, 2:]], axis=2)
    vw = jnp.concatenate([vb[:, :-2], vb[:, 1:-1], vb[:, 2:]], axis=2)
    s = jnp.einsum('njqhd,njkhd->njhqk', qb, kw).astype(jnp.float32) * (d ** -0.5)
    qpos = jnp.arange(nb)[:, None] * blk + jnp.arange(blk)[None, :]
    kpos = jnp.arange(nb)[:, None] * blk - blk + jnp.arange(3 * blk)[None, :]
    valid = ((jnp.abs(qpos[:, :, None] - kpos[:, None, :]) <= radius)
             & (kpos[:, None, :] >= 0) & (kpos[:, None, :] < m))
    s = jnp.where(valid[None, :, None], s, -jnp.inf)
    lse = jax.nn.logsumexp(s, axis=-1)
    prob = jnp.exp(s - lse[..., None])
    o = jnp.einsum('njhqk,njkhd->njqhd', prob.astype(v.dtype), vw)
    out = o.reshape(n, mp, h, d)[:, :m]
    lse = lse.transpose(0, 1, 3, 2).reshape(n, mp, h)[:, :m]
    return out, lse


def dilated_group_attention(q, k, v, dilation, radius):
    b, l, h, d = q.shape
    m = l // dilation

    def to_phase(t):
        return t.reshape(b, m, dilation, h, d).transpose(0, 2, 1, 3, 4).reshape(b * dilation, m, h, d)

    out, lse = banded_attention(to_phase(q), to_phase(k), to_phase(v), radius)
    out = out.reshape(b, dilation, m, h, d).transpose(0, 2, 1, 3, 4).reshape(b, l, h, d)
    lse = lse.reshape(b, dilation, m, h).transpose(0, 2, 1, 3).reshape(b, l, h)
    return out, lse


def token_mixer(h, p):
    b, l, _ = h.shape
    proj = h @ p['w_in']
    z, xbc, dt_f, dt_b, q, k, v, g_ssd, g_attn = jnp.split(proj, SPLIT_POINTS, axis=-1)

    xbc = jax.nn.silu(centred_depthwise_conv(xbc, p['conv_w'], p['conv_b']))
    xs, bm, cm = jnp.split(xbc, (D_INNER, D_INNER + SSD_GROUPS * SSD_STATE), axis=-1)
    xs = xs.reshape(b, l, SSD_HEADS, SSD_HEAD_DIM)
    bm = bm.reshape(b, l, SSD_GROUPS, SSD_STATE)
    cm = cm.reshape(b, l, SSD_GROUPS, SSD_STATE)
    dtf = jax.nn.softplus(dt_f.astype(jnp.float32) + p['dt_bias_fwd'].astype(jnp.float32))
    dtb = jax.nn.softplus(dt_b.astype(jnp.float32) + p['dt_bias_bwd'].astype(jnp.float32))
    a_f = -jnp.exp(p['a_log_fwd'].astype(jnp.float32))
    a_b = -jnp.exp(p['a_log_bwd'].astype(jnp.float32))
    y_fwd = ssd_chunked(xs, dtf, a_f, bm, cm)
    y_bwd = ssd_chunked(xs[:, ::-1], dtb[:, ::-1], a_b, bm[:, ::-1], cm[:, ::-1])[:, ::-1]
    y = y_fwd + y_bwd + p['d_skip'].astype(jnp.float32)[:, None] * xs
    y = y.reshape(b, l, D_INNER) * jax.nn.silu(z.astype(jnp.float32))
    y_ssd = rms_norm(y, p['ssd_norm_w']).astype(h.dtype) @ p['w_branch_ssd']

    pos = jnp.arange(l, dtype=jnp.float32)
    q = partial_rotary(q.reshape(b, l, ATTN_HEADS, ATTN_HEAD_DIM), pos)
    k = partial_rotary(k.reshape(b, l, ATTN_HEADS, ATTN_HEAD_DIM), pos)
    v = v.reshape(b, l, ATTN_HEADS, ATTN_HEAD_DIM)
    outs, lses = [], []
    for gi, (window, dilation) in enumerate(ATTN_PATTERNS):
        hs = slice(gi * ATTN_HEADS_PER_GROUP, (gi + 1) * ATTN_HEADS_PER_GROUP)
        o, s = dilated_group_attention(q[:, :, hs], k[:, :, hs], v[:, :, hs], dilation, window // (2 * dilation))
        outs.append(o)
        lses.append(s)
    wts = jax.nn.softmax(jnp.stack(lses), axis=0)
    y_att = jnp.sum(wts[..., None] * jnp.stack(outs).astype(jnp.float32), axis=0)
    y_attn = y_att.reshape(b, l, ATTN_OUT).astype(h.dtype) @ p['w_branch_attn']

    merged = jax.nn.sigmoid(g_ssd) * y_ssd + jax.nn.sigmoid(g_attn) * y_attn
    return merged @ p['w_out']


def swiglu(x, wg, wu, wd):
    return (jax.nn.silu(x @ wg) * (x @ wu)) @ wd


def route(h, w_router, router_bias):
    t = h.shape[0]
    scores = jax.nn.sigmoid((h @ w_router).astype(jnp.float32))
    sel = scores + router_bias.astype(jnp.float32)
    per_group = N_EXPERTS // N_EXPERT_GROUPS
    grp_score = lax.top_k(sel.reshape(t, N_EXPERT_GROUPS, per_group), 2)[0].sum(-1)
    _, top_grp = lax.top_k(grp_score, TOPK_GROUPS)
    grp_keep = jax.nn.one_hot(top_grp, N_EXPERT_GROUPS, dtype=jnp.float32).sum(axis=1) > 0
    keep = jnp.repeat(grp_keep, per_group, axis=1)
    _, top_idx = lax.top_k(jnp.where(keep, sel, -jnp.inf), TOP_K)
    top_w = jnp.take_along_axis(scores, top_idx, axis=1)
    top_w = top_w / jnp.sum(top_w, axis=-1, keepdims=True) * ROUTED_SCALE
    return top_idx, top_w


def routed_experts(h, top_idx, top_w, w_gate, w_up, w_down):
    t, d = h.shape
    n_assign = t * TOP_K
    n_blocks = (n_assign + N_EXPERTS * (EXPERT_BLOCK - 1) + EXPERT_BLOCK - 1) // EXPERT_BLOCK
    n_rows = n_blocks * EXPERT_BLOCK
    e_flat = top_idx.reshape(-1)
    order = jnp.argsort(e_flat)
    e_sorted = e_flat[order]
    counts = jnp.zeros((N_EXPERTS,), jnp.int32).at[e_flat].add(1)
    padded = (counts + EXPERT_BLOCK - 1) // EXPERT_BLOCK * EXPERT_BLOCK
    pad_end = jnp.cumsum(padded)
    pad_start = pad_end - padded
    start = jnp.cumsum(counts) - counts
    dest = pad_start[e_sorted] + jnp.arange(n_assign, dtype=jnp.int32) - start[e_sorted]
    row_tok = jnp.zeros((n_rows,), jnp.int32).at[dest].set((order // TOP_K).astype(jnp.int32))
    row_w = jnp.zeros((n_rows,), h.dtype).at[dest].set(top_w.reshape(-1)[order].astype(h.dtype))
    block_exp = jnp.minimum(jnp.searchsorted(pad_end, jnp.arange(n_blocks, dtype=jnp.int32) * EXPERT_BLOCK,
                                             side='right'), N_EXPERTS - 1)

    def body(acc, blk):
        tok, wt, ex = blk
        xb = h[tok]
        yb = swiglu(xb, w_gate[ex], w_up[ex], w_down[ex]) * wt[:, None]
        return acc.at[tok].add(yb.astype(acc.dtype)), None

    acc, _ = lax.scan(body, jnp.zeros_like(h),
                      (row_tok.reshape(n_blocks, EXPERT_BLOCK), row_w.reshape(n_blocks, EXPERT_BLOCK), block_exp))
    return acc


def moe_ffn(h, p):
    b, l, d = h.shape
    ht = h.reshape(b * l, d)
    top_idx, top_w = route(ht, p['w_router'], p['router_bias'])
    routed = routed_experts(ht, top_idx, top_w, p['w_exp_gate'], p['w_exp_up'], p['w_exp_down'])
    shared = swiglu(ht, p['w_sh_gate'], p['w_sh_up'], p['w_sh_down'])
    return (routed + shared).reshape(b, l, d)


def encoder_layer(x, c, p):
    ada = jax.nn.silu(c) @ p['w_ada'] + p['b_ada']
    shift1, scale1, gate1, shift2, scale2, gate2 = jnp.split(ada[:, None, :], 6, axis=-1)
    mix = token_mixer(x * (1 + scale1) + shift1, p)
    x = layer_norm(ALPHA * x + gate1 * mix, p['ln1_g'], p['ln1_b'])
    ffn = moe_ffn(x * (1 + scale2) + shift2, p)
    return layer_norm(ALPHA * x + gate2 * ffn, p['ln2_g'], p['ln2_b'])


def run_trunk(x, c, params):
    for layer in range(DEPTH):
        p = {name: arr[layer] for name, arr in params.items()}
        x = encoder_layer(x, c, p)
    return x


def setup_inputs(seed: int = 0) -> dict:
    key = jax.random.key(seed)
    ks = jax.random.split(key, 32)
    f32 = jnp.float32

    def nrm(k, shape, scale):
        return jax.random.normal(k, shape, f32) * scale

    def dt_bias_init(k):
        dt = jnp.exp(jax.random.uniform(k, (DEPTH, SSD_HEADS), f32, math.log(1e-3), math.log(1e-1)))
        return dt + jnp.log(-jnp.expm1(-dt))

    return {
        'x_prompt': nrm(ks[0], (BATCH, SEQ, D_MODEL), 1.0),
        'x_sample': nrm(ks[1], (DEC_BATCH, DEC_SEQ, D_MODEL), 1.0),
        'c_prompt': nrm(ks[2], (BATCH, D_MODEL), 1.0),
        'c_sample': nrm(ks[3], (DEC_BATCH, D_MODEL), 1.0),
        'w_ada': nrm(ks[4], (DEPTH, D_MODEL, 6 * D_MODEL), D_MODEL ** -0.5),
        'b_ada': nrm(ks[5], (DEPTH, 6 * D_MODEL), 0.01),
        'w_in': nrm(ks[6], (DEPTH, D_MODEL, N_IN), D_MODEL ** -0.5),
        'conv_w': nrm(ks[7], (DEPTH, CONV_WIDTH, CONV_DIM), CONV_WIDTH ** -0.5),
        'conv_b': nrm(ks[8], (DEPTH, CONV_DIM), 0.01),
        'dt_bias_fwd': dt_bias_init(ks[9]),
        'dt_bias_bwd': dt_bias_init(ks[10]),
        'a_log_fwd': jnp.log(jax.random.uniform(ks[11], (DEPTH, SSD_HEADS), f32, 1.0, 16.0)),
        'a_log_bwd': jnp.log(jax.random.uniform(ks[12], (DEPTH, SSD_HEADS), f32, 1.0, 16.0)),
        'd_skip': 1.0 + nrm(ks[13], (DEPTH, SSD_HEADS), 0.01),
        'ssd_norm_w': 1.0 + nrm(ks[14], (DEPTH, D_INNER), 0.01),
        'w_branch_ssd': nrm(ks[15], (DEPTH, D_INNER, D_MODEL), D_INNER ** -0.5),
        'w_branch_attn': nrm(ks[16], (DEPTH, ATTN_OUT, D_MODEL), ATTN_OUT ** -0.5),
        'w_out': nrm(ks[17], (DEPTH, D_MODEL, D_MODEL), BETA * D_MODEL ** -0.5),
        'ln1_g': 1.0 + nrm(ks[18], (DEPTH, D_MODEL), 0.01),
        'ln1_b': nrm(ks[19], (DEPTH, D_MODEL), 0.01),
        'w_router': nrm(ks[20], (DEPTH, D_MODEL, N_EXPERTS), D_MODEL ** -0.5),
        'router_bias': nrm(ks[21], (DEPTH, N_EXPERTS), 0.01),
        'w_exp_gate': nrm(ks[22], (DEPTH, N_EXPERTS, D_MODEL, EXPERT_HIDDEN), D_MODEL ** -0.5),
        'w_exp_up': nrm(ks[23], (DEPTH, N_EXPERTS, D_MODEL, EXPERT_HIDDEN), D_MODEL ** -0.5),
        'w_exp_down': nrm(ks[24], (DEPTH, N_EXPERTS, EXPERT_HIDDEN, D_MODEL), BETA * EXPERT_HIDDEN ** -0.5),
        'w_sh_gate': nrm(ks[25], (DEPTH, D_MODEL, SHARED_HIDDEN), D_MODEL ** -0.5),
        'w_sh_up': nrm(ks[26], (DEPTH, D_MODEL, SHARED_HIDDEN), D_MODEL ** -0.5),
        'w_sh_down': nrm(ks[27], (DEPTH, SHARED_HIDDEN, D_MODEL), BETA * SHARED_HIDDEN ** -0.5),
        'ln2_g': 1.0 + nrm(ks[28], (DEPTH, D_MODEL), 0.01),
        'ln2_b': nrm(ks[29], (DEPTH, D_MODEL), 0.01),
    }


def reference(x_prompt, x_sample, c_prompt, c_sample, w_ada, b_ada, w_in, conv_w, conv_b,
              dt_bias_fwd, dt_bias_bwd, a_log_fwd, a_log_bwd, d_skip, ssd_norm_w,
              w_branch_ssd, w_branch_attn, w_out, ln1_g, ln1_b, w_router, router_bias,
              w_exp_gate, w_exp_up, w_exp_down, w_sh_gate, w_sh_up, w_sh_down, ln2_g, ln2_b):
    params = {
        'w_ada': w_ada, 'b_ada': b_ada, 'w_in': w_in, 'conv_w': conv_w, 'conv_b': conv_b,
        'dt_bias_fwd': dt_bias_fwd, 'dt_bias_bwd': dt_bias_bwd, 'a_log_fwd': a_log_fwd,
        'a_log_bwd': a_log_bwd, 'd_skip': d_skip, 'ssd_norm_w': ssd_norm_w,
        'w_branch_ssd': w_branch_ssd, 'w_branch_attn': w_branch_attn, 'w_out': w_out,
        'ln1_g': ln1_g, 'ln1_b': ln1_b, 'w_router': w_router, 'router_bias': router_bias,
        'w_exp_gate': w_exp_gate, 'w_exp_up': w_exp_up, 'w_exp_down': w_exp_down,
        'w_sh_gate': w_sh_gate, 'w_sh_up': w_sh_up, 'w_sh_down': w_sh_down,
        'ln2_g': ln2_g, 'ln2_b': ln2_b,
    }
    y_prompt = run_trunk(x_prompt, c_prompt, params)
    y_sample = run_trunk(x_sample, c_sample, params)
    return (y_prompt, y_sample)
```

```python
import functools
import math

import jax
import jax.numpy as jnp
from jax import lax
from jax.experimental import pallas as pl
from jax.experimental.pallas import tpu as pltpu

F32 = jnp.float32
BF16 = jnp.bfloat16
I32 = jnp.int32

D_MODEL = 1024
D_INNER = 2048
SSD_HEADS = 32
SSD_HEAD_DIM = 64
SSD_GROUPS = 8
SSD_STATE = 128
SSD_CHUNK = 128
HEADS_PER_SSD_GROUP = SSD_HEADS // SSD_GROUPS
CONV_WIDTH = 5
CONV_PAD = CONV_WIDTH // 2
CONV_DIM = D_INNER + 2 * SSD_GROUPS * SSD_STATE
ATTN_PATTERNS = ((128, 1), (512, 4), (2048, 16))
ATTN_GROUPS = len(ATTN_PATTERNS)
ATTN_HEADS_PER_GROUP = 8
ATTN_HEAD_DIM = 64
ATTN_GROUP_COLS = ATTN_HEADS_PER_GROUP * ATTN_HEAD_DIM
ATTN_DIM = ATTN_GROUPS * ATTN_GROUP_COLS
ROT_DIM = ATTN_HEAD_DIM // 4
ROT_HALF = ROT_DIM // 2
ROPE_THETA = 500000.0
N_EXPERTS = 256
EXPERT_HIDDEN = 256
TOP_K = 8
N_EXPERT_GROUPS = 8
EXPERTS_PER_GROUP = N_EXPERTS // N_EXPERT_GROUPS
TOPK_GROUPS = 4
ROUTED_SCALE = 2.5
SHARED_HIDDEN = 256
EXPERT_BLOCK = 128
DEPTH = 1
ALPHA = (2.0 * DEPTH) ** 0.25
NORM_EPS = 1e-5

COL_Z = 0
COL_XBC = COL_Z + D_INNER
COL_GSSD = COL_XBC + CONV_DIM
COL_GATTN = COL_GSSD + D_MODEL
COL_Q = COL_GATTN + D_MODEL
COL_K = COL_Q + ATTN_DIM
COL_V = COL_K + ATTN_DIM
N_MAIN = COL_V + ATTN_DIM
DT_COLS = 128

LANES = 128
SUBLANES = 8
VMEM_LIMIT = 56 * 1024 * 1024

BAND_RADIUS = 64
ATTN_TQ = 128
NEG_INF = float("-inf")


def _cparams(sem):
    return pltpu.CompilerParams(dimension_semantics=sem, vmem_limit_bytes=VMEM_LIMIT)


def _sigmoid(x):
    return 1.0 / (1.0 + jnp.exp(-x))


def _silu(x):
    return x * _sigmoid(x)


def _softplus(x):
    return jnp.maximum(x, 0.0) + jnp.log(1.0 + jnp.exp(-jnp.abs(x)))


def _ada_kernel(c_ref, w_ref, b_ref, o_ref):
    c = _silu(c_ref[...]).astype(BF16)
    o_ref[...] = jnp.dot(c, w_ref[...].astype(BF16), preferred_element_type=F32) + b_ref[...]


def _ada(c_pad, w_ada, b_ada):
    rows = c_pad.shape[0]
    n = w_ada.shape[1]
    tn = 1536
    return pl.pallas_call(
        _ada_kernel,
        grid=(n // tn,),
        in_specs=[pl.BlockSpec((rows, D_MODEL), lambda j: (0, 0)),
                  pl.BlockSpec((D_MODEL, tn), lambda j: (0, j)),
                  pl.BlockSpec((1, tn), lambda j: (0, j))],
        out_specs=pl.BlockSpec((rows, tn), lambda j: (0, j)),
        out_shape=jax.ShapeDtypeStruct((rows, n), F32),
        compiler_params=_cparams(("arbitrary",)),
        name="ada",
    )(c_pad, w_ada, b_ada)


def _inproj_kernel(x_ref, sc_ref, sh_ref, w_ref, o_ref, h_ref):
    @pl.when(pl.program_id(1) == 0)
    def _():
        h_ref[...] = (x_ref[...] * (1.0 + sc_ref[0]) + sh_ref[0]).astype(BF16)

    o_ref[...] = jnp.dot(h_ref[...], w_ref[...], preferred_element_type=F32).astype(o_ref.dtype)


def _inproj(x, scale, shift, w, seg, tm, tn, out_dtype):
    t = x.shape[0]
    n = w.shape[1]
    per_seg = seg // tm
    return pl.pallas_call(
        _inproj_kernel,
        grid=(t // tm, n // tn),
        in_specs=[pl.BlockSpec((tm, D_MODEL), lambda i, j: (i, 0)),
                  pl.BlockSpec((1, 1, D_MODEL), lambda i, j: (i // per_seg, 0, 0)),
                  pl.BlockSpec((1, 1, D_MODEL), lambda i, j: (i // per_seg, 0, 0)),
                  pl.BlockSpec((D_MODEL, tn), lambda i, j: (0, j))],
        out_specs=pl.BlockSpec((tm, tn), lambda i, j: (i, j)),
        out_shape=jax.ShapeDtypeStruct((t, n), out_dtype),
        scratch_shapes=[pltpu.VMEM((tm, D_MODEL), BF16)],
        compiler_params=_cparams(("arbitrary", "arbitrary")),
        name="inproj",
    )(x, scale, shift, w)


def _conv_kernel(xp_ref, x_ref, xn_ref, w_ref, b_ref, o_ref, *, n_l):
    l = pl.program_id(1)
    tl = x_ref.shape[0]
    xp = jnp.where(l > 0, xp_ref[...], 0.0)
    xn = jnp.where(l < n_l - 1, xn_ref[...], 0.0)
    xx = jnp.concatenate([xp, x_ref[...], xn], axis=0)
    acc = jnp.zeros(x_ref.shape, F32) + b_ref[...]
    for k in range(CONV_WIDTH):
        lo = SUBLANES - CONV_PAD + k
        acc = acc + w_ref[k:k + 1, :] * xx[lo:lo + tl, :]
    o_ref[...] = _silu(acc)


def _conv(proj, conv_w, conv_b, row_off, batch, seq, tl=512, tc=512):
    t = proj.shape[0]
    n_l = seq // tl
    n_c = CONV_DIM // tc
    c0 = COL_XBC // tc
    r8 = tl // SUBLANES
    last8 = t // SUBLANES - 1

    def row_blk(b, l):
        return (row_off + b * seq) // tl + l

    return pl.pallas_call(
        functools.partial(_conv_kernel, n_l=n_l),
        grid=(batch, n_l, n_c),
        in_specs=[pl.BlockSpec((SUBLANES, tc), lambda b, l, c: (jnp.maximum(row_blk(b, l) * r8 - 1, 0), c0 + c)),
                  pl.BlockSpec((tl, tc), lambda b, l, c: (row_blk(b, l), c0 + c)),
                  pl.BlockSpec((SUBLANES, tc), lambda b, l, c: (jnp.minimum((row_blk(b, l) + 1) * r8, last8), c0 + c)),
                  pl.BlockSpec((CONV_WIDTH, tc), lambda b, l, c: (0, c)),
                  pl.BlockSpec((1, tc), lambda b, l, c: (0, c))],
        out_specs=pl.BlockSpec((tl, tc), lambda b, l, c: (b * n_l + l, c)),
        out_shape=jax.ShapeDtypeStruct((batch * seq, CONV_DIM), F32),
        compiler_params=_cparams(("arbitrary", "arbitrary", "arbitrary")),
        name="conv",
    )(proj, proj, proj, conv_w, conv_b)


def _exact_tri_matmul(tri_bf16, x):
    hi = x.astype(BF16)
    r1 = x - hi.astype(F32)
    mid = r1.astype(BF16)
    lo = (r1 - mid.astype(F32)).astype(BF16)
    out = jnp.dot(tri_bf16, hi, preferred_element_type=F32)
    out = out + jnp.dot(tri_bf16, mid, preferred_element_type=F32)
    return out + jnp.dot(tri_bf16, lo, preferred_element_type=F32)


def _ssd_kernel(*refs, reverse, accumulate):
    if accumulate:
        x_ref, b_ref, c_ref, dt_ref, dtb_ref, alog_ref, yin_ref, y_ref, state_ref = refs
    else:
        x_ref, b_ref, c_ref, dt_ref, dtb_ref, alog_ref, y_ref, state_ref = refs
        yin_ref = None
    lane0 = SSD_HEADS if reverse else 0

    @pl.when(pl.program_id(1) == 0)
    def _():
        state_ref[...] = jnp.zeros(state_ref.shape, F32)

    dt = _softplus(dt_ref[...] + dtb_ref[...])
    da = dt * (-jnp.exp(alog_ref[...]))
    row = lax.broadcasted_iota(I32, (SSD_CHUNK, SSD_CHUNK), 0)
    col = lax.broadcasted_iota(I32, (SSD_CHUNK, SSD_CHUNK), 1)
    tri = (col >= row) if reverse else (col <= row)
    cum = _exact_tri_matmul(tri.astype(BF16), da)
    cum_t = cum.T
    edge = 0 if reverse else SSD_CHUNK - 1
    total = cum[edge:edge + 1, :]
    decay_to_end = jnp.exp(total - cum)
    exp_cum = jnp.exp(cum)
    chunk_decay = jnp.exp(total)

    for g in range(SSD_GROUPS):
        bg = b_ref[:, g * SSD_STATE:(g + 1) * SSD_STATE].astype(BF16)
        cg = c_ref[:, g * SSD_STATE:(g + 1) * SSD_STATE].astype(BF16)
        cb = lax.dot_general(cg, bg, (((1,), (1,)), ((), ())), preferred_element_type=F32)
        state = state_ref[g]
        y_off = lax.dot_general(cg, state.astype(BF16), (((1,), (1,)), ((), ())),
                                preferred_element_type=F32)
        ys = []
        ws = []
        decs = []
        for e in range(HEADS_PER_SSD_GROUP):
            h = g * HEADS_PER_SSD_GROUP + e
            ln = lane0 + h
            xh = x_ref[:, h * SSD_HEAD_DIM:(h + 1) * SSD_HEAD_DIM]
            xs = xh * dt[:, ln:ln + 1]
            seg = cum[:, ln:ln + 1] - cum_t[ln:ln + 1, :]
            decay = jnp.exp(jnp.where(tri, seg, NEG_INF))
            m = (cb * decay).astype(BF16)
            y_diag = jnp.dot(m, xs.astype(BF16), preferred_element_type=F32)
            ys.append(y_diag + y_off[:, e * SSD_HEAD_DIM:(e + 1) * SSD_HEAD_DIM] * exp_cum[:, ln:ln + 1])
            ws.append(xs * decay_to_end[:, ln:ln + 1])
            decs.append(jnp.broadcast_to(chunk_decay[:, ln:ln + 1], (SSD_HEAD_DIM, SSD_STATE)))
        yg = jnp.concatenate(ys, axis=1)
        cols = slice(g * HEADS_PER_SSD_GROUP * SSD_HEAD_DIM, (g + 1) * HEADS_PER_SSD_GROUP * SSD_HEAD_DIM)
        if accumulate:
            yg = yg + yin_ref[:, cols]
        y_ref[:, cols] = yg
        w = jnp.concatenate(ws, axis=1).astype(BF16)
        new_state = lax.dot_general(w, bg, (((0,), (0,)), ((), ())), preferred_element_type=F32)
        state_ref[g] = state * jnp.concatenate(decs, axis=0) + new_state


def _ssd(xbc, dt_raw, dt_bias, a_log, y_in, row_off, batch, seq, reverse):
    n_c = seq // SSD_CHUNK
    accumulate = y_in is not None

    def loc(b, c):
        return b * n_c + (n_c - 1 - c if reverse else c)

    def glob(b, c):
        return row_off // SSD_CHUNK + loc(b, c)

    in_specs = [pl.BlockSpec((SSD_CHUNK, D_INNER), lambda b, c: (loc(b, c), 0)),
                pl.BlockSpec((SSD_CHUNK, SSD_GROUPS * SSD_STATE), lambda b, c: (loc(b, c), 2)),
                pl.BlockSpec((SSD_CHUNK, SSD_GROUPS * SSD_STATE), lambda b, c: (loc(b, c), 3)),
                pl.BlockSpec((SSD_CHUNK, DT_COLS), lambda b, c: (glob(b, c), 0)),
                pl.BlockSpec((1, DT_COLS), lambda b, c: (0, 0)),
                pl.BlockSpec((1, DT_COLS), lambda b, c: (0, 0))]
    args = [xbc, xbc, xbc, dt_raw, dt_bias, a_log]
    aliases = {}
    if accumulate:
        in_specs.append(pl.BlockSpec((SSD_CHUNK, D_INNER), lambda b, c: (loc(b, c), 0)))
        args.append(y_in)
        aliases = {len(args) - 1: 0}
    return pl.pallas_call(
        functools.partial(_ssd_kernel, reverse=reverse, accumulate=accumulate),
        grid=(batch, n_c),
        in_specs=in_specs,
        out_specs=pl.BlockSpec((SSD_CHUNK, D_INNER), lambda b, c: (loc(b, c), 0)),
        out_shape=jax.ShapeDtypeStruct((batch * seq, D_INNER), F32),
        scratch_shapes=[pltpu.VMEM((SSD_GROUPS, HEADS_PER_SSD_GROUP * SSD_HEAD_DIM, SSD_STATE), F32)],
        input_output_aliases=aliases,
        compiler_params=_cparams(("arbitrary", "arbitrary")),
        name="ssd_bwd" if reverse else "ssd_fwd",
    )(*args)


def _ssd_out_kernel(y_ref, x_ref, z_ref, dskip_ref, nw_ref, w_ref, o_ref):
    y = y_ref[...] + dskip_ref[...] * x_ref[...]
    y = y * _silu(z_ref[...])
    ms = jnp.mean(y * y, axis=-1, keepdims=True)
    yn = y * lax.rsqrt(ms + NORM_EPS) * nw_ref[...]
    o_ref[...] = jnp.dot(yn.astype(BF16), w_ref[...], preferred_element_type=F32)


def _ssd_out(y, xbc, proj, d_skip_cols, norm_w, w_branch, row_off, tm=256):
    t = y.shape[0]
    return pl.pallas_call(
        _ssd_out_kernel,
        grid=(t // tm,),
        in_specs=[pl.BlockSpec((tm, D_INNER), lambda i: (i, 0)),
                  pl.BlockSpec((tm, D_INNER), lambda i: (i, 0)),
                  pl.BlockSpec((tm, D_INNER), lambda i: (row_off // tm + i, COL_Z // D_INNER)),
                  pl.BlockSpec((1, D_INNER), lambda i: (0, 0)),
                  pl.BlockSpec((1, D_INNER), lambda i: (0, 0)),
                  pl.BlockSpec((D_INNER, D_MODEL), lambda i: (0, 0))],
        out_specs=pl.BlockSpec((tm, D_MODEL), lambda i: (i, 0)),
        out_shape=jax.ShapeDtypeStruct((t, D_MODEL), F32),
        compiler_params=_cparams(("arbitrary",)),
        name="ssd_out",
    )(y, xbc, proj, d_skip_cols, norm_w, w_branch)


def _prep_kernel(q_ref, k_ref, v_ref, cos_ref, sa_ref, sb_ref, qo_ref, ko_ref, vo_ref, *, dil):
    rows = q_ref.shape[0] // dil

    def phase(ref, p):
        if dil == 1:
            return ref[...]
        return ref[pl.ds(p, rows, stride=dil), :]

    for p in range(dil):
        cos = phase(cos_ref, p)
        sa = phase(sa_ref, p)
        sb = phase(sb_ref, p)

        def rot(t):
            return (t * cos + pltpu.roll(t, LANES - ROT_HALF, axis=1) * sa
                    + pltpu.roll(t, ROT_HALF, axis=1) * sb)

        qo_ref[0, p] = rot(phase(q_ref, p)) * (ATTN_HEAD_DIM ** -0.5)
        ko_ref[0, p] = rot(phase(k_ref, p))
        vo_ref[0, p] = phase(v_ref, p)


def _prep(proj, tables, gi, dil, row_off, batch, seq, rows_per_step=2048):
    r = rows_per_step
    n_i = seq // r
    n_c = ATTN_GROUP_COLS // LANES
    m = seq // dil
    cos_t, sa_t, sb_t = tables

    def in_blk(col):
        cb = (col + gi * ATTN_GROUP_COLS) // LANES
        return pl.BlockSpec((r, LANES), lambda b, i, c: ((row_off + b * seq) // r + i, cb + c))

    tab = pl.BlockSpec((r, LANES), lambda b, i, c: (i, 0))
    out = pl.BlockSpec((1, dil, r // dil, LANES), lambda b, i, c: (b, 0, i, c))
    shape = jax.ShapeDtypeStruct((batch, dil, m, ATTN_GROUP_COLS), F32)
    return pl.pallas_call(
        functools.partial(_prep_kernel, dil=dil),
        grid=(batch, n_i, n_c),
        in_specs=[in_blk(COL_Q), in_blk(COL_K), in_blk(COL_V), tab, tab, tab],
        out_specs=[out, out, out],
        out_shape=[shape, shape, shape],
        compiler_params=_cparams(("arbitrary", "arbitrary", "arbitrary")),
        name=f"attn_prep_d{dil}",
    )(proj, proj, proj, cos_t, sa_t, sb_t)


def _band_attn_kernel(q_ref, kp_ref, kc_ref, kn_ref, vp_ref, vc_ref, vn_ref, o_ref, lse_ref, *, m_len):
    i = pl.program_id(1)
    tq = q_ref.shape[1]
    tk = tq + 2 * BAND_RADIUS
    q = q_ref[0].astype(BF16)
    k = jnp.concatenate([kp_ref[0], kc_ref[0], kn_ref[0]], axis=0).astype(BF16)
    v = jnp.concatenate([vp_ref[0], vc_ref[0], vn_ref[0]], axis=0).astype(BF16)
    qpos = i * tq + lax.broadcasted_iota(I32, (tq, tk), 0)
    kpos = i * tq - BAND_RADIUS + lax.broadcasted_iota(I32, (tq, tk), 1)
    valid = (jnp.abs(qpos - kpos) <= BAND_RADIUS) & (kpos >= 0) & (kpos < m_len)
    for h in range(ATTN_HEADS_PER_GROUP):
        cols = slice(h * ATTN_HEAD_DIM, (h + 1) * ATTN_HEAD_DIM)
        s = lax.dot_general(q[:, cols], k[:, cols], (((1,), (1,)), ((), ())), preferred_element_type=F32)
        s = jnp.where(valid, s, NEG_INF)
        mx = jnp.max(s, axis=-1, keepdims=True)
        p = jnp.exp(s - mx)
        den = jnp.sum(p, axis=-1, keepdims=True)
        o = jnp.dot(p.astype(BF16), v[:, cols], preferred_element_type=F32)
        o_ref[0, :, cols] = o / den
        lse_ref[0, :, cols] = jnp.broadcast_to(mx + jnp.log(den), (tq, ATTN_HEAD_DIM))


def _band_attn(q, k, v):
    b, d, m, c = q.shape
    n = b * d
    q, k, v = (a.reshape(n, m, c) for a in (q, k, v))
    tq = ATTN_TQ
    per = tq // BAND_RADIUS
    last = m // BAND_RADIUS - 1
    main = pl.BlockSpec((1, tq, c), lambda s, i: (s, i, 0))
    prev = pl.BlockSpec((1, BAND_RADIUS, c), lambda s, i: (s, jnp.maximum(i * per - 1, 0), 0))
    nxt = pl.BlockSpec((1, BAND_RADIUS, c), lambda s, i: (s, jnp.minimum((i + 1) * per, last), 0))
    shape = jax.ShapeDtypeStruct((n, m, c), F32)
    o, lse = pl.pallas_call(
        functools.partial(_band_attn_kernel, m_len=m),
        grid=(n, m // tq),
        in_specs=[main, prev, main, nxt, prev, main, nxt],
        out_specs=[main, main],
        out_shape=[shape, shape],
        compiler_params=_cparams(("arbitrary", "arbitrary")),
        name=f"band_attn_m{m}",
    )(q, k, k, k, v, v, v)
    return o.reshape(b, d, m, c), lse.reshape(b, d, m, c)


def _layer_norm(x, g, b):
    mu = jnp.mean(x, axis=-1, keepdims=True)
    xc = x - mu
    var = jnp.mean(xc * xc, axis=-1, keepdims=True)
    return xc * lax.rsqrt(var + NORM_EPS) * g + b


def _merge_kernel(x_ref, yssd_ref, gs_ref, ga_ref, o1_ref, l1_ref, o2_ref, l2_ref, o3_ref, l3_ref,
                  g1_ref, sc2_ref, sh2_ref, wba_ref, wout_ref, lng_ref, lnb_ref, wr_ref,
                  x1_ref, h2_ref, logit_ref, scr_ref):
    tm = x_ref.shape[0]

    def interleaved(ref, dil, slot):
        if dil == 1:
            return ref[0, 0]
        rows = tm // dil
        n_c = ATTN_GROUP_COLS // LANES
        for p in range(dil):
            for c in range(n_c):
                scr_ref[slot * n_c + c, pl.ds(p, rows, stride=dil), :] = ref[0, p, :, c * LANES:(c + 1) * LANES]
        return jnp.concatenate([scr_ref[slot * n_c + c] for c in range(n_c)], axis=1)

    outs = []
    lses = []
    slot = 0
    for (o_ref, l_ref), (_, dil) in zip(((o1_ref, l1_ref), (o2_ref, l2_ref), (o3_ref, l3_ref)), ATTN_PATTERNS):
        outs.append(interleaved(o_ref, dil, slot))
        lses.append(interleaved(l_ref, dil, slot + 1))
        slot += 2
    mx = jnp.maximum(jnp.maximum(lses[0], lses[1]), lses[2])
    es = [jnp.exp(l - mx) for l in lses]
    den = es[0] + es[1] + es[2]
    y_att = (es[0] * outs[0] + es[1] * outs[1] + es[2] * outs[2]) / den
    y_attn = jnp.dot(y_att.astype(BF16), wba_ref[...], preferred_element_type=F32)
    merged = _sigmoid(gs_ref[...]) * yssd_ref[...] + _sigmoid(ga_ref[...]) * y_attn
    mix = jnp.dot(merged.astype(BF16), wout_ref[...], preferred_element_type=F32)
    x1 = _layer_norm(ALPHA * x_ref[...] + g1_ref[0] * mix, lng_ref[...], lnb_ref[...])
    x1_ref[...] = x1
    h2 = x1 * (1.0 + sc2_ref[0]) + sh2_ref[0]
    h2_ref[...] = h2
    logit_ref[...] = lax.dot_general(wr_ref[...], h2.astype(BF16), (((1,), (1,)), ((), ())),
                                     preferred_element_type=F32)


def _merge(x, y_ssd, proj, attn, gate1, scale2, shift2, w_ba, w_out, ln_g, ln_b, w_router_t,
           row_off, batch, seq, seg, tm=256):
    t = batch * seq
    n_i = seq // tm
    per_seg = seg // tm
    seg0 = row_off // seg

    def rows(col_blk):
        return pl.BlockSpec((tm, D_MODEL), lambda b, i: (row_off // tm + b * n_i + i, col_blk))

    def local(width):
        return pl.BlockSpec((tm, width), lambda b, i: (b * n_i + i, 0))

    def modv():
        return pl.BlockSpec((1, 1, D_MODEL), lambda b, i: (seg0 + (b * n_i + i) // per_seg, 0, 0))

    def full(shape):
        return pl.BlockSpec(shape, lambda b, i: tuple(0 for _ in shape))

    attn_specs = []
    attn_args = []
    for (o, lse), (_, dil) in zip(attn, ATTN_PATTERNS):
        spec = pl.BlockSpec((1, dil, tm // dil, ATTN_GROUP_COLS), lambda b, i: (b, 0, i, 0))
        attn_specs += [spec, spec]
        attn_args += [o, lse]
    return pl.pallas_call(
        _merge_kernel,
        grid=(batch, n_i),
        in_specs=[rows(0), local(D_MODEL), rows(COL_GSSD // D_MODEL), rows(COL_GATTN // D_MODEL)] + attn_specs
                 + [modv(), modv(), modv(), full((ATTN_GROUP_COLS, D_MODEL)), full((D_MODEL, D_MODEL)),
                    full((1, D_MODEL)), full((1, D_MODEL)), full((N_EXPERTS, D_MODEL))],
        out_specs=[local(D_MODEL), local(D_MODEL),
                   pl.BlockSpec((N_EXPERTS, tm), lambda b, i: (0, b * n_i + i))],
        out_shape=[jax.ShapeDtypeStruct((t, D_MODEL), F32), jax.ShapeDtypeStruct((t, D_MODEL), F32),
                   jax.ShapeDtypeStruct((N_EXPERTS, t), F32)],
        scratch_shapes=[pltpu.VMEM((2 * ATTN_GROUPS * (ATTN_GROUP_COLS // LANES), tm, LANES), F32)],
        compiler_params=_cparams(("arbitrary", "arbitrary")),
        name="mixer_merge",
    )(x, y_ssd, proj, proj, *attn_args, gate1, scale2, shift2, w_ba, w_out, ln_g, ln_b, w_router_t)


def _route_kernel(logit_ref, bias_ref, idx_ref, w_ref, rank_ref, cnt_ref, run_ref):
    tm = logit_ref.shape[1]

    @pl.when(pl.program_id(0) == 0)
    def _():
        run_ref[...] = jnp.zeros(run_ref.shape, F32)

    scores = _sigmoid(logit_ref[...])
    sel = scores + bias_ref[...]
    iota_g = lax.broadcasted_iota(I32, (EXPERTS_PER_GROUP, tm), 0)
    grp = []
    for g in range(N_EXPERT_GROUPS):
        sg = sel[g * EXPERTS_PER_GROUP:(g + 1) * EXPERTS_PER_GROUP, :]
        top1 = jnp.max(sg, axis=0, keepdims=True)
        first = jnp.min(jnp.where(sg == top1, iota_g, EXPERTS_PER_GROUP), axis=0, keepdims=True)
        top2 = jnp.max(jnp.where(iota_g == first, NEG_INF, sg), axis=0, keepdims=True)
        grp.append(top1 + top2)
    gs = jnp.concatenate(grp, axis=0)
    iota_n = lax.broadcasted_iota(I32, (N_EXPERT_GROUPS, tm), 0)
    keep_g = jnp.zeros((N_EXPERT_GROUPS, tm), F32)
    for _ in range(TOPK_GROUPS):
        best = jnp.max(gs, axis=0, keepdims=True)
        first = jnp.min(jnp.where(gs == best, iota_n, N_EXPERT_GROUPS), axis=0, keepdims=True)
        hit = iota_n == first
        keep_g = jnp.where(hit, 1.0, keep_g)
        gs = jnp.where(hit, NEG_INF, gs)
    cand = jnp.concatenate(
        [jnp.where(keep_g[g:g + 1, :] > 0.0, sel[g * EXPERTS_PER_GROUP:(g + 1) * EXPERTS_PER_GROUP, :], NEG_INF)
         for g in range(N_EXPERT_GROUPS)], axis=0)
    iota_e = lax.broadcasted_iota(I32, (N_EXPERTS, tm), 0)
    chosen = jnp.zeros((N_EXPERTS, tm), F32)
    idxs = []
    ws = []
    for _ in range(TOP_K):
        best = jnp.max(cand, axis=0, keepdims=True)
        first = jnp.min(jnp.where(cand == best, iota_e, N_EXPERTS), axis=0, keepdims=True)
        hit = iota_e == first
        idxs.append(first)
        ws.append(jnp.sum(jnp.where(hit, scores, 0.0), axis=0, keepdims=True))
        chosen = jnp.where(hit, 1.0, chosen)
        cand = jnp.where(hit, NEG_INF, cand)
    top_w = jnp.concatenate(ws, axis=0)
    top_w = top_w / jnp.sum(top_w, axis=0, keepdims=True) * ROUTED_SCALE
    idx_ref[...] = jnp.concatenate(idxs, axis=0)
    w_ref[...] = top_w
    s_i = lax.broadcasted_iota(I32, (tm, tm), 0)
    t_i = lax.broadcasted_iota(I32, (tm, tm), 1)
    before = (s_i < t_i).astype(BF16)
    prior = jnp.dot(chosen.astype(BF16), before, preferred_element_type=F32) + run_ref[...]
    ranks = [jnp.sum(jnp.where(iota_e == idxs[k], prior, 0.0), axis=0, keepdims=True) for k in range(TOP_K)]
    rank_ref[...] = jnp.concatenate(ranks, axis=0).astype(I32)
    run_ref[...] = run_ref[...] + jnp.sum(chosen, axis=1, keepdims=True)
    cnt_ref[...] = run_ref[...].astype(I32)


def _route(logits_t, bias_col, tm=256):
    t = logits_t.shape[1]
    tok = pl.BlockSpec((TOP_K, tm), lambda i: (0, i))
    return pl.pallas_call(
        _route_kernel,
        grid=(t // tm,),
        in_specs=[pl.BlockSpec((N_EXPERTS, tm), lambda i: (0, i)),
                  pl.BlockSpec((N_EXPERTS, 1), lambda i: (0, 0))],
        out_specs=[tok, tok, tok, pl.BlockSpec((N_EXPERTS, 1), lambda i: (0, 0))],
        out_shape=[jax.ShapeDtypeStruct((TOP_K, t), I32), jax.ShapeDtypeStruct((TOP_K, t), F32),
                   jax.ShapeDtypeStruct((TOP_K, t), I32), jax.ShapeDtypeStruct((N_EXPERTS, 1), I32)],
        scratch_shapes=[pltpu.VMEM((N_EXPERTS, 1), F32)],
        compiler_params=_cparams(("arbitrary",)),
        name="route",
    )(logits_t, bias_col)


def _dispatch_kernel(pad_end_ref, dest_hbm, h_hbm, xs_hbm, dest_smem, zeros_ref, sem_idx, sem_zero, sem_row,
                     *, tm):
    i = pl.program_id(0)
    idx_copy = pltpu.make_async_copy(dest_hbm.at[i], dest_smem, sem_idx)
    idx_copy.start()

    def zero_copy(e):
        start = pl.multiple_of(jnp.maximum(pad_end_ref[e] - EXPERT_BLOCK, 0), EXPERT_BLOCK)
        return pltpu.make_async_copy(zeros_ref, xs_hbm.at[pl.ds(start, EXPERT_BLOCK)], sem_zero)

    @pl.when(i == 0)
    def _():
        zeros_ref[...] = jnp.zeros(zeros_ref.shape, F32)

        def start_zero(e, carry):
            zero_copy(e).start()
            return carry

        def wait_zero(e, carry):
            zero_copy(e).wait()
            return carry

        lax.fori_loop(0, N_EXPERTS, start_zero, 0)
        lax.fori_loop(0, N_EXPERTS, wait_zero, 0)

    idx_copy.wait()

    def row_copy(t, k):
        return pltpu.make_async_copy(h_hbm.at[pl.ds(i * tm + t, 1)],
                                     xs_hbm.at[pl.ds(dest_smem[k * tm + t], 1)], sem_row)

    def start_rows(t, carry):
        for k in range(TOP_K):
            row_copy(t, k).start()
        return carry

    def wait_rows(t, carry):
        for k in range(TOP_K):
            row_copy(t, k).wait()
        return carry

    lax.fori_loop(0, tm, start_rows, 0)
    lax.fori_loop(0, tm, wait_rows, 0)


def _dispatch(pad_end, dest_tiles, h2, n_rows, tm):
    n_tiles = dest_tiles.shape[0]
    return pl.pallas_call(
        functools.partial(_dispatch_kernel, tm=tm),
        grid_spec=pltpu.PrefetchScalarGridSpec(
            num_scalar_prefetch=1,
            grid=(n_tiles,),
            in_specs=[pl.BlockSpec(memory_space=pl.ANY), pl.BlockSpec(memory_space=pl.ANY)],
            out_specs=pl.BlockSpec(memory_space=pl.ANY),
            scratch_shapes=[pltpu.SMEM((TOP_K * tm,), I32),
                            pltpu.VMEM((EXPERT_BLOCK, D_MODEL), F32),
                            pltpu.SemaphoreType.DMA, pltpu.SemaphoreType.DMA, pltpu.SemaphoreType.DMA]),
        out_shape=jax.ShapeDtypeStruct((n_rows, D_MODEL), F32),
        compiler_params=_cparams(("arbitrary",)),
        name="moe_dispatch",
    )(pad_end, dest_tiles, h2)


def _experts_kernel(blk_exp_ref, n_valid_ref, xs_ref, wg_ref, wu_ref, wd_ref, ys_ref):
    @pl.when(pl.program_id(0) < n_valid_ref[0])
    def _():
        x = xs_ref[...].astype(BF16)
        g = jnp.dot(x, wg_ref[0].astype(BF16), preferred_element_type=F32)
        u = jnp.dot(x, wu_ref[0].astype(BF16), preferred_element_type=F32)
        hmid = (_silu(g) * u).astype(BF16)
        ys_ref[...] = jnp.dot(hmid, wd_ref[0].astype(BF16), preferred_element_type=F32)


def _experts(blk_exp, n_valid, xs, w_gate, w_up, w_down):
    n_blocks = xs.shape[0] // EXPERT_BLOCK

    def row_map(i, be, nv):
        return (jnp.minimum(i, nv[0] - 1), 0)

    def w_map(i, be, nv):
        return (be[i], 0, 0)

    return pl.pallas_call(
        _experts_kernel,
        grid_spec=pltpu.PrefetchScalarGridSpec(
            num_scalar_prefetch=2,
            grid=(n_blocks,),
            in_specs=[pl.BlockSpec((EXPERT_BLOCK, D_MODEL), row_map),
                      pl.BlockSpec((1, D_MODEL, EXPERT_HIDDEN), w_map),
                      pl.BlockSpec((1, D_MODEL, EXPERT_HIDDEN), w_map),
                      pl.BlockSpec((1, EXPERT_HIDDEN, D_MODEL), w_map)],
            out_specs=pl.BlockSpec((EXPERT_BLOCK, D_MODEL), row_map)),
        out_shape=jax.ShapeDtypeStruct(xs.shape, F32),
        compiler_params=_cparams(("arbitrary",)),
        name="moe_experts",
    )(blk_exp, n_valid, xs, w_gate, w_up, w_down)


def _combine_kernel(dest_hbm, ys_hbm, w_ref, x1_ref, h2_ref, g2_ref, wsg_ref, wsu_ref, wsd_ref,
                    lng_ref, lnb_ref, o_ref, dest_smem, rows_ref, sem_idx, sem_row, *, tm):
    i = pl.program_id(0)
    idx_copy = pltpu.make_async_copy(dest_hbm.at[i], dest_smem, sem_idx)
    idx_copy.start()
    idx_copy.wait()

    def row_copy(t, k):
        return pltpu.make_async_copy(ys_hbm.at[pl.ds(dest_smem[k * tm + t], 1)],
                                     rows_ref.at[k, pl.ds(t, 1)], sem_row)

    def start_rows(t, carry):
        for k in range(TOP_K):
            row_copy(t, k).start()
        return carry

    def wait_rows(t, carry):
        for k in range(TOP_K):
            row_copy(t, k).wait()
        return carry

    lax.fori_loop(0, tm, start_rows, 0)
    h2 = h2_ref[...].astype(BF16)
    g = jnp.dot(h2, wsg_ref[...], preferred_element_type=F32)
    u = jnp.dot(h2, wsu_ref[...], preferred_element_type=F32)
    ffn = jnp.dot((_silu(g) * u).astype(BF16), wsd_ref[...], preferred_element_type=F32)
    lax.fori_loop(0, tm, wait_rows, 0)
    w = w_ref[...]
    for k in range(TOP_K):
        ffn = ffn + w[:, k:k + 1] * rows_ref[k]
    o_ref[...] = _layer_norm(ALPHA * x1_ref[...] + g2_ref[0] * ffn, lng_ref[...], lnb_ref[...])


def _combine(dest_tiles, ys, top_w_rows, x1, h2, gate2, w_sg, w_su, w_sd, ln_g, ln_b, seg, tm):
    t = x1.shape[0]
    per_seg = seg // tm

    def full(shape):
        return pl.BlockSpec(shape, lambda i: tuple(0 for _ in shape))

    tok = pl.BlockSpec((tm, D_MODEL), lambda i: (i, 0))
    return pl.pallas_call(
        functools.partial(_combine_kernel, tm=tm),
        grid=(t // tm,),
        in_specs=[pl.BlockSpec(memory_space=pl.ANY), pl.BlockSpec(memory_space=pl.ANY),
                  pl.BlockSpec((tm, TOP_K), lambda i: (i, 0)), tok, tok,
                  pl.BlockSpec((1, 1, D_MODEL), lambda i: (i // per_seg, 0, 0)),
                  full((D_MODEL, SHARED_HIDDEN)), full((D_MODEL, SHARED_HIDDEN)), full((SHARED_HIDDEN, D_MODEL)),
                  full((1, D_MODEL)), full((1, D_MODEL))],
        out_specs=tok,
        out_shape=jax.ShapeDtypeStruct((t, D_MODEL), F32),
        scratch_shapes=[pltpu.SMEM((TOP_K * tm,), I32), pltpu.VMEM((TOP_K, tm, D_MODEL), F32),
                        pltpu.SemaphoreType.DMA, pltpu.SemaphoreType.DMA],
        compiler_params=_cparams(("arbitrary",)),
        name="moe_combine",
    )(dest_tiles, ys, top_w_rows, x1, h2, gate2, w_sg, w_su, w_sd, ln_g, ln_b)


def _rope_tables(seq):
    inv_freq = ROPE_THETA ** (-jnp.arange(ROT_HALF, dtype=F32) * 2.0 / ROT_DIM)
    ang = jnp.arange(seq, dtype=F32)[:, None] * inv_freq[None, :]
    cos, sin = jnp.cos(ang), jnp.sin(ang)
    rest = ATTN_HEAD_DIM - ROT_DIM
    ones = jnp.ones((seq, rest), F32)
    zeros = jnp.zeros((seq, rest), F32)
    zh = jnp.zeros((seq, ROT_HALF), F32)
    reps = LANES // ATTN_HEAD_DIM
    cos_t = jnp.tile(jnp.concatenate([cos, cos, ones], axis=1), (1, reps))
    sa_t = jnp.tile(jnp.concatenate([-sin, zh, zeros], axis=1), (1, reps))
    sb_t = jnp.tile(jnp.concatenate([zh, sin, zeros], axis=1), (1, reps))
    return cos_t, sa_t, sb_t


def _split_in_weights(w_in):
    cuts = (D_INNER, CONV_DIM, SSD_HEADS, SSD_HEADS, ATTN_DIM, ATTN_DIM, ATTN_DIM, D_MODEL, D_MODEL)
    offs = [0]
    for c in cuts:
        offs.append(offs[-1] + c)
    z, xbc, dtf, dtb, q, k, v, gs, ga = (w_in[:, offs[i]:offs[i + 1]] for i in range(len(cuts)))
    w_main = jnp.concatenate([z, xbc, gs, ga, q, k, v], axis=1).astype(BF16)
    assert COL_GSSD % D_MODEL == 0 and COL_Q % ATTN_GROUP_COLS == 0 and w_main.shape[1] == N_MAIN
    pad = jnp.zeros((D_MODEL, DT_COLS - 2 * SSD_HEADS), F32)
    w_dt = jnp.concatenate([dtf, dtb, pad], axis=1).astype(BF16)
    return w_main, w_dt


def kernel(x_prompt, x_sample, c_prompt, c_sample, w_ada, b_ada, w_in, conv_w, conv_b, dt_bias_fwd, dt_bias_bwd, a_log_fwd, a_log_bwd, d_skip, ssd_norm_w, w_branch_ssd, w_branch_attn, w_out, ln1_g, ln1_b, w_router, router_bias, w_exp_gate, w_exp_up, w_exp_down, w_sh_gate, w_sh_up, w_sh_down, ln2_g, ln2_b):
    assert w_ada.shape[0] == DEPTH
    groups = ((x_prompt, c_prompt), (x_sample, c_sample))
    shapes = [(x.shape[0], x.shape[1]) for x, _ in groups]
    seg = math.gcd(*(s for _, s in shapes))
    tokens = [b * s for b, s in shapes]
    t_all = sum(tokens)
    row_offs = [0, tokens[0]]

    c_all = jnp.concatenate([c_prompt, c_sample], axis=0)
    n_req = c_all.shape[0]
    c_pad = jnp.zeros((-(-n_req // SUBLANES) * SUBLANES, D_MODEL), F32).at[:n_req].set(c_all)
    ada = _ada(c_pad, w_ada[0], b_ada)[:n_req]
    reps = jnp.array([s // seg for b, s in shapes for _ in range(b)], dtype=I32)
    n_seg = t_all // seg
    ada_seg = jnp.repeat(ada, reps, axis=0, total_repeat_length=n_seg)
    shift1, scale1, gate1, shift2, scale2, gate2 = (
        a.reshape(n_seg, 1, D_MODEL) for a in jnp.split(ada_seg, 6, axis=-1))

    x_all = jnp.concatenate([x_prompt.reshape(-1, D_MODEL), x_sample.reshape(-1, D_MODEL)], axis=0)
    w_main, w_dt = _split_in_weights(w_in[0])
    proj = _inproj(x_all, scale1, shift1, w_main, seg, tm=1024, tn=1280, out_dtype=F32)
    dt_raw = _inproj(x_all, scale1, shift1, w_dt, seg, tm=1024, tn=DT_COLS, out_dtype=F32)

    dt_pad = jnp.zeros((DT_COLS - 2 * SSD_HEADS,), F32)
    dt_bias = jnp.concatenate([dt_bias_fwd[0], dt_bias_bwd[0], dt_pad]).reshape(1, DT_COLS)
    a_log = jnp.concatenate([a_log_fwd[0], a_log_bwd[0], dt_pad]).reshape(1, DT_COLS)
    d_skip_cols = jnp.repeat(d_skip[0], SSD_HEAD_DIM).reshape(1, D_INNER)
    w_bs = w_branch_ssd[0].astype(BF16)
    w_ba = w_branch_attn[0].astype(BF16)
    w_o = w_out[0].astype(BF16)
    w_router_t = w_router[0].T.astype(BF16)

    x1_parts, h2_parts, logit_parts = [], [], []
    for (batch, seq), row_off in zip(shapes, row_offs):
        xbc = _conv(proj, conv_w[0], conv_b, row_off, batch, seq)
        y = _ssd(xbc, dt_raw, dt_bias, a_log, None, row_off, batch, seq, reverse=False)
        y = _ssd(xbc, dt_raw, dt_bias, a_log, y, row_off, batch, seq, reverse=True)
        y_ssd = _ssd_out(y, xbc, proj, d_skip_cols, ssd_norm_w, w_bs, row_off)
        tables = _rope_tables(seq)
        attn = []
        for gi, (_, dil) in enumerate(ATTN_PATTERNS):
            q, k, v = _prep(proj, tables, gi, dil, row_off, batch, seq)
            attn.append(_band_attn(q, k, v))
        x1, h2, logits_t = _merge(x_all, y_ssd, proj, attn, gate1, scale2, shift2, w_ba, w_o,
                                  ln1_g, ln1_b, w_router_t, row_off, batch, seq, seg)
        x1_parts.append(x1)
        h2_parts.append(h2)
        logit_parts.append(logits_t)
    x1 = jnp.concatenate(x1_parts, axis=0)
    h2 = jnp.concatenate(h2_parts, axis=0)
    logits_t = jnp.concatenate(logit_parts, axis=1)

    top_idx, top_w, rank, counts = _route(logits_t, router_bias[0].reshape(N_EXPERTS, 1))
    counts = counts[:, 0]
    padded = (counts + EXPERT_BLOCK - 1) // EXPERT_BLOCK * EXPERT_BLOCK
    pad_end = jnp.cumsum(padded)
    pad_start = pad_end - padded
    n_blocks = (t_all * TOP_K + N_EXPERTS * (EXPERT_BLOCK - 1)) // EXPERT_BLOCK
    n_rows = n_blocks * EXPERT_BLOCK
    dest = jnp.take(pad_start, top_idx, axis=0) + rank
    tm_moe = 256
    dest_tiles = dest.reshape(TOP_K, t_all // tm_moe, tm_moe).transpose(1, 0, 2).reshape(-1, TOP_K * tm_moe)
    n_valid = (pad_end[-1] // EXPERT_BLOCK).astype(I32).reshape(1)
    blk_start = jnp.arange(n_blocks, dtype=I32) * EXPERT_BLOCK
    blk_exp = jnp.searchsorted(pad_end, jnp.minimum(blk_start, pad_end[-1] - 1), side="right").astype(I32)
    blk_exp = jnp.minimum(blk_exp, N_EXPERTS - 1)

    xs = _dispatch(pad_end.astype(I32), dest_tiles, h2, n_rows, tm_moe)
    ys = _experts(blk_exp, n_valid, xs, w_exp_gate[0], w_exp_up[0], w_exp_down[0])
    out = _combine(dest_tiles, ys, top_w.T, x1, h2, gate2, w_sh_gate[0].astype(BF16), w_sh_up[0].astype(BF16),
                   w_sh_down[0].astype(BF16), ln2_g, ln2_b, seg, tm_moe)
    y_prompt = out[:tokens[0]].reshape(x_prompt.shape)
    y_sample = out[tokens[0]:].reshape(x_sample.shape)
    return (y_prompt, y_sample)
```

```python
import functools
import math

import jax
import jax.numpy as jnp
from jax import lax
from jax.experimental import pallas as pl
from jax.experimental.pallas import tpu as pltpu

F32 = jnp.float32
BF16 = jnp.bfloat16
I32 = jnp.int32

D_MODEL = 1024
D_INNER = 2048
SSD_HEADS = 32
SSD_HEAD_DIM = 64
SSD_GROUPS = 8
SSD_STATE = 128
SSD_CHUNK = 128
HEADS_PER_SSD_GROUP = SSD_HEADS // SSD_GROUPS
CONV_WIDTH = 5
CONV_PAD = CONV_WIDTH // 2
CONV_DIM = D_INNER + 2 * SSD_GROUPS * SSD_STATE
ATTN_PATTERNS = ((128, 1), (512, 4), (2048, 16))
ATTN_GROUPS = len(ATTN_PATTERNS)
ATTN_HEADS_PER_GROUP = 8
ATTN_HEAD_DIM = 64
ATTN_GROUP_COLS = ATTN_HEADS_PER_GROUP * ATTN_HEAD_DIM
ATTN_DIM = ATTN_GROUPS * ATTN_GROUP_COLS
ROT_DIM = ATTN_HEAD_DIM // 4
ROT_HALF = ROT_DIM // 2
ROPE_THETA = 500000.0
N_EXPERTS = 256
EXPERT_HIDDEN = 256
TOP_K = 8
N_EXPERT_GROUPS = 8
EXPERTS_PER_GROUP = N_EXPERTS // N_EXPERT_GROUPS
TOPK_GROUPS = 4
ROUTED_SCALE = 2.5
SHARED_HIDDEN = 256
EXPERT_BLOCK = 128
DEPTH = 1
ALPHA = (2.0 * DEPTH) ** 0.25
NORM_EPS = 1e-5

COL_Z = 0
COL_XBC = COL_Z + D_INNER
COL_GSSD = COL_XBC + CONV_DIM
COL_GATTN = COL_GSSD + D_MODEL
COL_Q = COL_GATTN + D_MODEL
COL_K = COL_Q + ATTN_DIM
COL_V = COL_K + ATTN_DIM
N_MAIN = COL_V + ATTN_DIM
DT_COLS = 128

LANES = 128
SUBLANES = 8
VMEM_LIMIT = 56 * 1024 * 1024

BAND_RADIUS = 64
ATTN_TQ = 128
NEG_INF = float("-inf")


def _cparams(sem):
    return pltpu.CompilerParams(dimension_semantics=sem, vmem_limit_bytes=VMEM_LIMIT)


def _sigmoid(x):
    return 1.0 / (1.0 + jnp.exp(-x))


def _silu(x):
    return x * _sigmoid(x)


def _softplus(x):
    return jnp.maximum(x, 0.0) + jnp.log(1.0 + jnp.exp(-jnp.abs(x)))


def _ada_kernel(c_ref, w_ref, b_ref, o_ref):
    c = _silu(c_ref[...]).astype(BF16)
    o_ref[...] = jnp.dot(c, w_ref[...].astype(BF16), preferred_element_type=F32) + b_ref[...]


def _ada(c_pad, w_ada, b_ada):
    rows = c_pad.shape[0]
    n = w_ada.shape[1]
    tn = 1536
    return pl.pallas_call(
        _ada_kernel,
        grid=(n // tn,),
        in_specs=[pl.BlockSpec((rows, D_MODEL), lambda j: (0, 0)),
                  pl.BlockSpec((D_MODEL, tn), lambda j: (0, j)),
                  pl.BlockSpec((1, tn), lambda j: (0, j))],
        out_specs=pl.BlockSpec((rows, tn), lambda j: (0, j)),
        out_shape=jax.ShapeDtypeStruct((rows, n), F32),
        compiler_params=_cparams(("arbitrary",)),
        name="ada",
    )(c_pad, w_ada, b_ada)


def _inproj_kernel(x_ref, sc_ref, sh_ref, w_ref, o_ref, h_ref):
    @pl.when(pl.program_id(1) == 0)
    def _():
        h_ref[...] = (x_ref[...] * (1.0 + sc_ref[0]) + sh_ref[0]).astype(BF16)

    o_ref[...] = jnp.dot(h_ref[...], w_ref[...], preferred_element_type=F32).astype(o_ref.dtype)


def _inproj(x, scale, shift, w, seg, tm, tn, out_dtype):
    t = x.shape[0]
    n = w.shape[1]
    per_seg = seg // tm
    return pl.pallas_call(
        _inproj_kernel,
        grid=(t // tm, n // tn),
        in_specs=[pl.BlockSpec((tm, D_MODEL), lambda i, j: (i, 0)),
                  pl.BlockSpec((1, 1, D_MODEL), lambda i, j: (i // per_seg, 0, 0)),
                  pl.BlockSpec((1, 1, D_MODEL), lambda i, j: (i // per_seg, 0, 0)),
                  pl.BlockSpec((D_MODEL, tn), lambda i, j: (0, j))],
        out_specs=pl.BlockSpec((tm, tn), lambda i, j: (i, j)),
        out_shape=jax.ShapeDtypeStruct((t, n), out_dtype),
        scratch_shapes=[pltpu.VMEM((tm, D_MODEL), BF16)],
        compiler_params=_cparams(("arbitrary", "arbitrary")),
        name="inproj",
    )(x, scale, shift, w)


def _conv_kernel(xp_ref, x_ref, xn_ref, w_ref, b_ref, o_ref, *, n_l):
    l = pl.program_id(1)
    tl = x_ref.shape[0]
    xp = jnp.where(l > 0, xp_ref[...], 0.0)
    xn = jnp.where(l < n_l - 1, xn_ref[...], 0.0)
    xx = jnp.concatenate([xp, x_ref[...], xn], axis=0)
    acc = jnp.zeros(x_ref.shape, F32) + b_ref[...]
    for k in range(CONV_WIDTH):
        lo = SUBLANES - CONV_PAD + k
        acc = acc + w_ref[k:k + 1, :] * xx[lo:lo + tl, :]
    o_ref[...] = _silu(acc)


def _conv(proj, conv_w, conv_b, row_off, batch, seq, tl=512, tc=512):
    t = proj.shape[0]
    n_l = seq // tl
    n_c = CONV_DIM // tc
    c0 = COL_XBC // tc
    r8 = tl // SUBLANES
    last8 = t // SUBLANES - 1

    def row_blk(b, l):
        return (row_off + b * seq) // tl + l

    return pl.pallas_call(
        functools.partial(_conv_kernel, n_l=n_l),
        grid=(batch, n_l, n_c),
        in_specs=[pl.BlockSpec((SUBLANES, tc), lambda b, l, c: (jnp.maximum(row_blk(b, l) * r8 - 1, 0), c0 + c)),
                  pl.BlockSpec((tl, tc), lambda b, l, c: (row_blk(b, l), c0 + c)),
                  pl.BlockSpec((SUBLANES, tc), lambda b, l, c: (jnp.minimum((row_blk(b, l) + 1) * r8, last8), c0 + c)),
                  pl.BlockSpec((CONV_WIDTH, tc), lambda b, l, c: (0, c)),
                  pl.BlockSpec((1, tc), lambda b, l, c: (0, c))],
        out_specs=pl.BlockSpec((tl, tc), lambda b, l, c: (b * n_l + l, c)),
        out_shape=jax.ShapeDtypeStruct((batch * seq, CONV_DIM), F32),
        compiler_params=_cparams(("arbitrary", "arbitrary", "arbitrary")),
        name="conv",
    )(proj, proj, proj, conv_w, conv_b)


def _exact_tri_matmul(tri_bf16, x):
    hi = x.astype(BF16)
    r1 = x - hi.astype(F32)
    mid = r1.astype(BF16)
    lo = (r1 - mid.astype(F32)).astype(BF16)
    out = jnp.dot(tri_bf16, hi, preferred_element_type=F32)
    out = out + jnp.dot(tri_bf16, mid, preferred_element_type=F32)
    return out + jnp.dot(tri_bf16, lo, preferred_element_type=F32)


def _ssd_kernel(*refs, reverse, accumulate):
    if accumulate:
        x_ref, b_ref, c_ref, dt_ref, dtb_ref, alog_ref, yin_ref, y_ref, state_ref = refs
    else:
        x_ref, b_ref, c_ref, dt_ref, dtb_ref, alog_ref, y_ref, state_ref = refs
        yin_ref = None
    lane0 = SSD_HEADS if reverse else 0

    @pl.when(pl.program_id(1) == 0)
    def _():
        state_ref[...] = jnp.zeros(state_ref.shape, F32)

    dt = _softplus(dt_ref[...] + dtb_ref[...])
    da = dt * (-jnp.exp(alog_ref[...]))
    row = lax.broadcasted_iota(I32, (SSD_CHUNK, SSD_CHUNK), 0)
    col = lax.broadcasted_iota(I32, (SSD_CHUNK, SSD_CHUNK), 1)
    tri = (col >= row) if reverse else (col <= row)
    cum = _exact_tri_matmul(tri.astype(BF16), da)
    cum_t = cum.T
    edge = 0 if reverse else SSD_CHUNK - 1
    total = cum[edge:edge + 1, :]
    decay_to_end = jnp.exp(total - cum)
    exp_cum = jnp.exp(cum)
    chunk_decay = jnp.exp(total)

    for g in range(SSD_GROUPS):
        bg = b_ref[:, g * SSD_STATE:(g + 1) * SSD_STATE].astype(BF16)
        cg = c_ref[:, g * SSD_STATE:(g + 1) * SSD_STATE].astype(BF16)
        cb = lax.dot_general(cg, bg, (((1,), (1,)), ((), ())), preferred_element_type=F32)
        state = state_ref[g]
        y_off = lax.dot_general(cg, state.astype(BF16), (((1,), (1,)), ((), ())),
                                preferred_element_type=F32)
        ys = []
        ws = []
        decs = []
        for e in range(HEADS_PER_SSD_GROUP):
            h = g * HEADS_PER_SSD_GROUP + e
            ln = lane0 + h
            xh = x_ref[:, h * SSD_HEAD_DIM:(h + 1) * SSD_HEAD_DIM]
            xs = xh * dt[:, ln:ln + 1]
            seg = cum[:, ln:ln + 1] - cum_t[ln:ln + 1, :]
            decay = jnp.exp(jnp.where(tri, seg, NEG_INF))
            m = (cb * decay).astype(BF16)
            y_diag = jnp.dot(m, xs.astype(BF16), preferred_element_type=F32)
            ys.append(y_diag + y_off[:, e * SSD_HEAD_DIM:(e + 1) * SSD_HEAD_DIM] * exp_cum[:, ln:ln + 1])
            ws.append(xs * decay_to_end[:, ln:ln + 1])
            decs.append(jnp.broadcast_to(chunk_decay[:, ln:ln + 1], (SSD_HEAD_DIM, SSD_STATE)))
        yg = jnp.concatenate(ys, axis=1)
        cols = slice(g * HEADS_PER_SSD_GROUP * SSD_HEAD_DIM, (g + 1) * HEADS_PER_SSD_GROUP * SSD_HEAD_DIM)
        if accumulate:
            yg = yg + yin_ref[:, cols]
        y_ref[:, cols] = yg
        w = jnp.concatenate(ws, axis=1).astype(BF16)
        new_state = lax.dot_general(w, bg, (((0,), (0,)), ((), ())), preferred_element_type=F32)
        state_ref[g] = state * jnp.concatenate(decs, axis=0) + new_state


def _ssd(xbc, dt_raw, dt_bias, a_log, y_in, row_off, batch, seq, reverse):
    n_c = seq // SSD_CHUNK
    accumulate = y_in is not None

    def loc(b, c):
        return b * n_c + (n_c - 1 - c if reverse else c)

    def glob(b, c):
        return row_off // SSD_CHUNK + loc(b, c)

    in_specs = [pl.BlockSpec((SSD_CHUNK, D_INNER), lambda b, c: (loc(b, c), 0)),
                pl.BlockSpec((SSD_CHUNK, SSD_GROUPS * SSD_STATE), lambda b, c: (loc(b, c), 2)),
                pl.BlockSpec((SSD_CHUNK, SSD_GROUPS * SSD_STATE), lambda b, c: (loc(b, c), 3)),
                pl.BlockSpec((SSD_CHUNK, DT_COLS), lambda b, c: (glob(b, c), 0)),
                pl.BlockSpec((1, DT_COLS), lambda b, c: (0, 0)),
                pl.BlockSpec((1, DT_COLS), lambda b, c: (0, 0))]
    args = [xbc, xbc, xbc, dt_raw, dt_bias, a_log]
    aliases = {}
    if accumulate:
        in_specs.append(pl.BlockSpec((SSD_CHUNK, D_INNER), lambda b, c: (loc(b, c), 0)))
        args.append(y_in)
        aliases = {len(args) - 1: 0}
    return pl.pallas_call(
        functools.partial(_ssd_kernel, reverse=reverse, accumulate=accumulate),
        grid=(batch, n_c),
        in_specs=in_specs,
        out_specs=pl.BlockSpec((SSD_CHUNK, D_INNER), lambda b, c: (loc(b, c), 0)),
        out_shape=jax.ShapeDtypeStruct((batch * seq, D_INNER), F32),
        scratch_shapes=[pltpu.VMEM((SSD_GROUPS, HEADS_PER_SSD_GROUP * SSD_HEAD_DIM, SSD_STATE), F32)],
        input_output_aliases=aliases,
        compiler_params=_cparams(("arbitrary", "arbitrary")),
        name="ssd_bwd" if reverse else "ssd_fwd",
    )(*args)


def _ssd_out_kernel(y_ref, x_ref, z_ref, dskip_ref, nw_ref, w_ref, o_ref):
    y = y_ref[...] + dskip_ref[...] * x_ref[...]
    y = y * _silu(z_ref[...])
    ms = jnp.mean(y * y, axis=-1, keepdims=True)
    yn = y * lax.rsqrt(ms + NORM_EPS) * nw_ref[...]
    o_ref[...] = jnp.dot(yn.astype(BF16), w_ref[...], preferred_element_type=F32)


def _ssd_out(y, xbc, proj, d_skip_cols, norm_w, w_branch, row_off, tm=256):
    t = y.shape[0]
    return pl.pallas_call(
        _ssd_out_kernel,
        grid=(t // tm,),
        in_specs=[pl.BlockSpec((tm, D_INNER), lambda i: (i, 0)),
                  pl.BlockSpec((tm, D_INNER), lambda i: (i, 0)),
                  pl.BlockSpec((tm, D_INNER), lambda i: (row_off // tm + i, COL_Z // D_INNER)),
                  pl.BlockSpec((1, D_INNER), lambda i: (0, 0)),
                  pl.BlockSpec((1, D_INNER), lambda i: (0, 0)),
                  pl.BlockSpec((D_INNER, D_MODEL), lambda i: (0, 0))],
        out_specs=pl.BlockSpec((tm, D_MODEL), lambda i: (i, 0)),
        out_shape=jax.ShapeDtypeStruct((t, D_MODEL), F32),
        compiler_params=_cparams(("arbitrary",)),
        name="ssd_out",
    )(y, xbc, proj, d_skip_cols, norm_w, w_branch)


def _prep_kernel(q_ref, k_ref, v_ref, cos_ref, sa_ref, sb_ref, qo_ref, ko_ref, vo_ref, *, dil):
    rows = q_ref.shape[0] // dil

    def phase(ref, p):
        if dil == 1:
            return ref[...]
        return ref[pl.ds(p, rows, stride=dil), :]

    for p in range(dil):
        cos = phase(cos_ref, p)
        sa = phase(sa_ref, p)
        sb = phase(sb_ref, p)

        def rot(t):
            return (t * cos + pltpu.roll(t, LANES - ROT_HALF, axis=1) * sa
                    + pltpu.roll(t, ROT_HALF, axis=1) * sb)

        qo_ref[0, p] = rot(phase(q_ref, p)) * (ATTN_HEAD_DIM ** -0.5)
        ko_ref[0, p] = rot(phase(k_ref, p))
        vo_ref[0, p] = phase(v_ref, p)


def _prep(proj, tables, gi, dil, row_off, batch, seq, rows_per_step=2048):
    r = rows_per_step
    n_i = seq // r
    n_c = ATTN_GROUP_COLS // LANES
    m = seq // dil
    cos_t, sa_t, sb_t = tables

    def in_blk(col):
        cb = (col + gi * ATTN_GROUP_COLS) // LANES
        return pl.BlockSpec((r, LANES), lambda b, i, c: ((row_off + b * seq) // r + i, cb + c))

    tab = pl.BlockSpec((r, LANES), lambda b, i, c: (i, 0))
    out = pl.BlockSpec((1, dil, r // dil, LANES), lambda b, i, c: (b, 0, i, c))
    shape = jax.ShapeDtypeStruct((batch, dil, m, ATTN_GROUP_COLS), F32)
    return pl.pallas_call(
        functools.partial(_prep_kernel, dil=dil),
        grid=(batch, n_i, n_c),
        in_specs=[in_blk(COL_Q), in_blk(COL_K), in_blk(COL_V), tab, tab, tab],
        out_specs=[out, out, out],
        out_shape=[shape, shape, shape],
        compiler_params=_cparams(("arbitrary", "arbitrary", "arbitrary")),
        name=f"attn_prep_d{dil}",
    )(proj, proj, proj, cos_t, sa_t, sb_t)


def _band_attn_kernel(q_ref, kp_ref, kc_ref, kn_ref, vp_ref, vc_ref, vn_ref, o_ref, lse_ref, *, m_len):
    i = pl.program_id(1)
    tq = q_ref.shape[1]
    tk = tq + 2 * BAND_RADIUS
    q = q_ref[0].astype(BF16)
    k = jnp.concatenate([kp_ref[0], kc_ref[0], kn_ref[0]], axis=0).astype(BF16)
    v = jnp.concatenate([vp_ref[0], vc_ref[0], vn_ref[0]], axis=0).astype(BF16)
    qpos = i * tq + lax.broadcasted_iota(I32, (tq, tk), 0)
    kpos = i * tq - BAND_RADIUS + lax.broadcasted_iota(I32, (tq, tk), 1)
    valid = (jnp.abs(qpos - kpos) <= BAND_RADIUS) & (kpos >= 0) & (kpos < m_len)
    for h in range(ATTN_HEADS_PER_GROUP):
        cols = slice(h * ATTN_HEAD_DIM, (h + 1) * ATTN_HEAD_DIM)
        s = lax.dot_general(q[:, cols], k[:, cols], (((1,), (1,)), ((), ())), preferred_element_type=F32)
        s = jnp.where(valid, s, NEG_INF)
        mx = jnp.max(s, axis=-1, keepdims=True)
        p = jnp.exp(s - mx)
        den = jnp.sum(p, axis=-1, keepdims=True)
        o = jnp.dot(p.astype(BF16), v[:, cols], preferred_element_type=F32)
        o_ref[0, :, cols] = o / den
        lse_ref[0, :, cols] = jnp.broadcast_to(mx + jnp.log(den), (tq, ATTN_HEAD_DIM))


def _band_attn(q, k, v):
    b, d, m, c = q.shape
    n = b * d
    q, k, v = (a.reshape(n, m, c) for a in (q, k, v))
    tq = ATTN_TQ
    per = tq // BAND_RADIUS
    last = m // BAND_RADIUS - 1
    main = pl.BlockSpec((1, tq, c), lambda s, i: (s, i, 0))
    prev = pl.BlockSpec((1, BAND_RADIUS, c), lambda s, i: (s, jnp.maximum(i * per - 1, 0), 0))
    nxt = pl.BlockSpec((1, BAND_RADIUS, c), lambda s, i: (s, jnp.minimum((i + 1) * per, last), 0))
    shape = jax.ShapeDtypeStruct((n, m, c), F32)
    o, lse = pl.pallas_call(
        functools.partial(_band_attn_kernel, m_len=m),
        grid=(n, m // tq),
        in_specs=[main, prev, main, nxt, prev, main, nxt],
        out_specs=[main, main],
        out_shape=[shape, shape],
        compiler_params=_cparams(("arbitrary", "arbitrary")),
        name=f"band_attn_m{m}",
    )(q, k, k, k, v, v, v)
    return o.reshape(b, d, m, c), lse.reshape(b, d, m, c)


def _layer_norm(x, g, b):
    mu = jnp.mean(x, axis=-1, keepdims=True)
    xc = x - mu
    var = jnp.mean(xc * xc, axis=-1, keepdims=True)
    return xc * lax.rsqrt(var + NORM_EPS) * g + b


def _merge_kernel(x_ref, yssd_ref, gs_ref, ga_ref, o1_ref, l1_ref, o2_ref, l2_ref, o3_ref, l3_ref,
                  g1_ref, sc2_ref, sh2_ref, wba_ref, wout_ref, lng_ref, lnb_ref, wr_ref,
                  x1_ref, h2_ref, logit_ref, scr_ref):
    tm = x_ref.shape[0]

    def interleaved(ref, dil, slot):
        if dil == 1:
            return ref[0, 0]
        rows = tm // dil
        n_c = ATTN_GROUP_COLS // LANES
        for p in range(dil):
            for c in range(n_c):
                scr_ref[slot * n_c + c, pl.ds(p, rows, stride=dil), :] = ref[0, p, :, c * LANES:(c + 1) * LANES]
        return jnp.concatenate([scr_ref[slot * n_c + c] for c in range(n_c)], axis=1)

    outs = []
    lses = []
    slot = 0
    for (o_ref, l_ref), (_, dil) in zip(((o1_ref, l1_ref), (o2_ref, l2_ref), (o3_ref, l3_ref)), ATTN_PATTERNS):
        outs.append(interleaved(o_ref, dil, slot))
        lses.append(interleaved(l_ref, dil, slot + 1))
        slot += 2
    mx = jnp.maximum(jnp.maximum(lses[0], lses[1]), lses[2])
    es = [jnp.exp(l - mx) for l in lses]
    den = es[0] + es[1] + es[2]
    y_att = (es[0] * outs[0] + es[1] * outs[1] + es[2] * outs[2]) / den
    y_attn = jnp.dot(y_att.astype(BF16), wba_ref[...], preferred_element_type=F32)
    merged = _sigmoid(gs_ref[...]) * yssd_ref[...] + _sigmoid(ga_ref[...]) * y_attn
    mix = jnp.dot(merged.astype(BF16), wout_ref[...], preferred_element_type=F32)
    x1 = _layer_norm(ALPHA * x_ref[...] + g1_ref[0] * mix, lng_ref[...], lnb_ref[...])
    x1_ref[...] = x1
    h2 = x1 * (1.0 + sc2_ref[0]) + sh2_ref[0]
    h2_ref[...] = h2
    logit_ref[...] = lax.dot_general(wr_ref[...], h2.astype(BF16), (((1,), (1,)), ((), ())),
                                     preferred_element_type=F32)


def _merge(x, y_ssd, proj, attn, gate1, scale2, shift2, w_ba, w_out, ln_g, ln_b, w_router_t,
           row_off, batch, seq, seg, tm=256):
    t = batch * seq
    n_i = seq // tm
    per_seg = seg // tm
    seg0 = row_off // seg

    def rows(col_blk):
        return pl.BlockSpec((tm, D_MODEL), lambda b, i: (row_off // tm + b * n_i + i, col_blk))

    def local(width):
        return pl.BlockSpec((tm, width), lambda b, i: (b * n_i + i, 0))

    def modv():
        return pl.BlockSpec((1, 1, D_MODEL), lambda b, i: (seg0 + (b * n_i + i) // per_seg, 0, 0))

    def full(shape):
        return pl.BlockSpec(shape, lambda b, i: tuple(0 for _ in shape))

    attn_specs = []
    attn_args = []
    for (o, lse), (_, dil) in zip(attn, ATTN_PATTERNS):
        spec = pl.BlockSpec((1, dil, tm // dil, ATTN_GROUP_COLS), lambda b, i: (b, 0, i, 0))
        attn_specs += [spec, spec]
        attn_args += [o, lse]
    return pl.pallas_call(
        _merge_kernel,
        grid=(batch, n_i),
        in_specs=[rows(0), local(D_MODEL), rows(COL_GSSD // D_MODEL), rows(COL_GATTN // D_MODEL)] + attn_specs
                 + [modv(), modv(), modv(), full((ATTN_GROUP_COLS, D_MODEL)), full((D_MODEL, D_MODEL)),
                    full((1, D_MODEL)), full((1, D_MODEL)), full((N_EXPERTS, D_MODEL))],
        out_specs=[local(D_MODEL), local(D_MODEL),
                   pl.BlockSpec((N_EXPERTS, tm), lambda b, i: (0, b * n_i + i))],
        out_shape=[jax.ShapeDtypeStruct((t, D_MODEL), F32), jax.ShapeDtypeStruct((t, D_MODEL), F32),
                   jax.ShapeDtypeStruct((N_EXPERTS, t), F32)],
        scratch_shapes=[pltpu.VMEM((2 * ATTN_GROUPS * (ATTN_GROUP_COLS // LANES), tm, LANES), F32)],
        compiler_params=_cparams(("arbitrary", "arbitrary")),
        name="mixer_merge",
    )(x, y_ssd, proj, proj, *attn_args, gate1, scale2, shift2, w_ba, w_out, ln_g, ln_b, w_router_t)


def _route_kernel(logit_ref, bias_ref, idx_ref, w_ref, rank_ref, cnt_ref, run_ref):
    tm = logit_ref.shape[1]

    @pl.when(pl.program_id(0) == 0)
    def _():
        run_ref[...] = jnp.zeros(run_ref.shape, F32)

    scores = _sigmoid(logit_ref[...])
    sel = scores + bias_ref[...]
    iota_g = lax.broadcasted_iota(I32, (EXPERTS_PER_GROUP, tm), 0)
    grp = []
    for g in range(N_EXPERT_GROUPS):
        sg = sel[g * EXPERTS_PER_GROUP:(g + 1) * EXPERTS_PER_GROUP, :]
        top1 = jnp.max(sg, axis=0, keepdims=True)
        first = jnp.min(jnp.where(sg == top1, iota_g, EXPERTS_PER_GROUP), axis=0, keepdims=True)
        top2 = jnp.max(jnp.where(iota_g == first, NEG_INF, sg), axis=0, keepdims=True)
        grp.append(top1 + top2)
    gs = jnp.concatenate(grp, axis=0)
    iota_n = lax.broadcasted_iota(I32, (N_EXPERT_GROUPS, tm), 0)
    keep_g = jnp.zeros((N_EXPERT_GROUPS, tm), F32)
    for _ in range(TOPK_GROUPS):
        best = jnp.max(gs, axis=0, keepdims=True)
        first = jnp.min(jnp.where(gs == best, iota_n, N_EXPERT_GROUPS), axis=0, keepdims=True)
        hit = iota_n == first
        keep_g = jnp.where(hit, 1.0, keep_g)
        gs = jnp.where(hit, NEG_INF, gs)
    cand = jnp.concatenate(
        [jnp.where(keep_g[g:g + 1, :] > 0.0, sel[g * EXPERTS_PER_GROUP:(g + 1) * EXPERTS_PER_GROUP, :], NEG_INF)
         for g in range(N_EXPERT_GROUPS)], axis=0)
    iota_e = lax.broadcasted_iota(I32, (N_EXPERTS, tm), 0)
    chosen = jnp.zeros((N_EXPERTS, tm), F32)
    idxs = []
    ws = []
    for _ in range(TOP_K):
        best = jnp.max(cand, axis=0, keepdims=True)
        first = jnp.min(jnp.where(cand == best, iota_e, N_EXPERTS), axis=0, keepdims=True)
        hit = iota_e == first
        idxs.append(first)
        ws.append(jnp.sum(jnp.where(hit, scores, 0.0), axis=0, keepdims=True))
        chosen = jnp.where(hit, 1.0, chosen)
        cand = jnp.where(hit, NEG_INF, cand)
    top_w = jnp.concatenate(ws, axis=0)
    top_w = top_w / jnp.sum(top_w, axis=0, keepdims=True) * ROUTED_SCALE
    idx_ref[...] = jnp.concatenate(idxs, axis=0)
    w_ref[...] = top_w
    s_i = lax.broadcasted_iota(I32, (tm, tm), 0)
    t_i = lax.broadcasted_iota(I32, (tm, tm), 1)
    before = (s_i < t_i).astype(BF16)
    prior = jnp.dot(chosen.astype(BF16), before, preferred_element_type=F32) + run_ref[...]
    ranks = [jnp.sum(jnp.where(iota_e == idxs[k], prior, 0.0), axis=0, keepdims=True) for k in range(TOP_K)]
    rank_ref[...] = jnp.concatenate(ranks, axis=0).astype(I32)
    run_ref[...] = run_ref[...] + jnp.sum(chosen, axis=1, keepdims=True)
    cnt_ref[...] = run_ref[...].astype(I32)


def _route(logits_t, bias_col, tm=256):
    t = logits_t.shape[1]
    tok = pl.BlockSpec((TOP_K, tm), lambda i: (0, i))
    return pl.pallas_call(
        _route_kernel,
        grid=(t // tm,),
        in_specs=[pl.BlockSpec((N_EXPERTS, tm), lambda i: (0, i)),
                  pl.BlockSpec((N_EXPERTS, 1), lambda i: (0, 0))],
        out_specs=[tok, tok, tok, pl.BlockSpec((N_EXPERTS, 1), lambda i: (0, 0))],
        out_shape=[jax.ShapeDtypeStruct((TOP_K, t), I32), jax.ShapeDtypeStruct((TOP_K, t), F32),
                   jax.ShapeDtypeStruct((TOP_K, t), I32), jax.ShapeDtypeStruct((N_EXPERTS, 1), I32)],
        scratch_shapes=[pltpu.VMEM((N_EXPERTS, 1), F32)],
        compiler_params=_cparams(("arbitrary",)),
        name="route",
    )(logits_t, bias_col)


def _dest_kernel(idx_ref, rank_ref, start_ref, o_ref):
    tm = idx_ref.shape[1]
    iota_e = lax.broadcasted_iota(I32, (N_EXPERTS, tm), 0)
    start = start_ref[...].astype(F32)
    for k in range(TOP_K):
        base = jnp.sum(jnp.where(iota_e == idx_ref[k:k + 1, :], start, 0.0), axis=0, keepdims=True)
        o_ref[0, :, k * tm:(k + 1) * tm] = base.astype(I32) + rank_ref[k:k + 1, :]


def _dest(top_idx, rank, pad_start_col, tm):
    t = top_idx.shape[1]
    tok = pl.BlockSpec((TOP_K, tm), lambda i: (0, i))
    out = pl.pallas_call(
        _dest_kernel,
        grid=(t // tm,),
        in_specs=[tok, tok, pl.BlockSpec((N_EXPERTS, 1), lambda i: (0, 0))],
        out_specs=pl.BlockSpec((1, 1, TOP_K * tm), lambda i: (i, 0, 0)),
        out_shape=jax.ShapeDtypeStruct((t // tm, 1, TOP_K * tm), I32),
        compiler_params=_cparams(("arbitrary",)),
        name="moe_dest",
    )(top_idx, rank, pad_start_col)
    return out.reshape(t // tm, TOP_K * tm)


def _dispatch_kernel(pad_end_ref, dest_hbm, h_ref, xs_hbm, dest_smem, zeros_ref, sem_idx, sem_zero, sem_row,
                     *, tm):
    i = pl.program_id(0)
    idx_copy = pltpu.make_async_copy(dest_hbm.at[i], dest_smem, sem_idx)
    idx_copy.start()

    def zero_copy(e):
        start = pl.multiple_of(jnp.maximum(pad_end_ref[e] - EXPERT_BLOCK, 0), EXPERT_BLOCK)
        return pltpu.make_async_copy(zeros_ref, xs_hbm.at[pl.ds(start, EXPERT_BLOCK)], sem_zero)

    @pl.when(i == 0)
    def _():
        zeros_ref[...] = jnp.zeros(zeros_ref.shape, F32)

        def start_zero(e, carry):
            zero_copy(e).start()
            return carry

        def wait_zero(e, carry):
            zero_copy(e).wait()
            return carry

        lax.fori_loop(0, N_EXPERTS, start_zero, 0)
        lax.fori_loop(0, N_EXPERTS, wait_zero, 0)

    idx_copy.wait()

    def row_copy(t, k):
        return pltpu.make_async_copy(h_ref.at[pl.ds(t, 1)],
                                     xs_hbm.at[pl.ds(dest_smem[k * tm + t], 1)], sem_row)

    def start_rows(t, carry):
        for k in range(TOP_K):
            row_copy(t, k).start()
        return carry

    def wait_rows(t, carry):
        for k in range(TOP_K):
            row_copy(t, k).wait()
        return carry

    lax.fori_loop(0, tm, start_rows, 0)
    lax.fori_loop(0, tm, wait_rows, 0)


def _dispatch(pad_end, dest_tiles, h2, n_rows, tm):
    n_tiles = dest_tiles.shape[0]
    return pl.pallas_call(
        functools.partial(_dispatch_kernel, tm=tm),
        grid_spec=pltpu.PrefetchScalarGridSpec(
            num_scalar_prefetch=1,
            grid=(n_tiles,),
            in_specs=[pl.BlockSpec(memory_space=pl.ANY),
                      pl.BlockSpec((tm, D_MODEL), lambda i, pe: (i, 0))],
            out_specs=pl.BlockSpec(memory_space=pl.ANY),
            scratch_shapes=[pltpu.SMEM((TOP_K * tm,), I32),
                            pltpu.VMEM((EXPERT_BLOCK, D_MODEL), F32),
                            pltpu.SemaphoreType.DMA, pltpu.SemaphoreType.DMA, pltpu.SemaphoreType.DMA]),
        out_shape=jax.ShapeDtypeStruct((n_rows, D_MODEL), F32),
        compiler_params=_cparams(("arbitrary",)),
        name="moe_dispatch",
    )(pad_end, dest_tiles, h2)


def _experts_kernel(blk_exp_ref, n_valid_ref, xs_ref, wg_ref, wu_ref, wd_ref, ys_ref, wgu_ref, wdn_ref):
    i = pl.program_id(0)

    @pl.when(i < n_valid_ref[0])
    def _():
        @pl.when((i == 0) | (blk_exp_ref[i] != blk_exp_ref[jnp.maximum(i - 1, 0)]))
        def _():
            wgu_ref[:, :EXPERT_HIDDEN] = wg_ref[0].astype(BF16)
            wgu_ref[:, EXPERT_HIDDEN:] = wu_ref[0].astype(BF16)
            wdn_ref[...] = wd_ref[0].astype(BF16)

        x = xs_ref[...].astype(BF16)
        gu = jnp.dot(x, wgu_ref[...], preferred_element_type=F32)
        hmid = (_silu(gu[:, :EXPERT_HIDDEN]) * gu[:, EXPERT_HIDDEN:]).astype(BF16)
        ys_ref[...] = jnp.dot(hmid, wdn_ref[...], preferred_element_type=F32)


def _experts(blk_exp, n_valid, xs, w_gate, w_up, w_down):
    n_blocks = xs.shape[0] // EXPERT_BLOCK

    def row_map(i, be, nv):
        return (jnp.minimum(i, nv[0] - 1), 0)

    def w_map(i, be, nv):
        return (be[i], 0, 0)

    return pl.pallas_call(
        _experts_kernel,
        grid_spec=pltpu.PrefetchScalarGridSpec(
            num_scalar_prefetch=2,
            grid=(n_blocks,),
            in_specs=[pl.BlockSpec((EXPERT_BLOCK, D_MODEL), row_map),
                      pl.BlockSpec((1, D_MODEL, EXPERT_HIDDEN), w_map),
                      pl.BlockSpec((1, D_MODEL, EXPERT_HIDDEN), w_map),
                      pl.BlockSpec((1, EXPERT_HIDDEN, D_MODEL), w_map)],
            out_specs=pl.BlockSpec((EXPERT_BLOCK, D_MODEL), row_map),
            scratch_shapes=[pltpu.VMEM((D_MODEL, 2 * EXPERT_HIDDEN), BF16),
                            pltpu.VMEM((EXPERT_HIDDEN, D_MODEL), BF16)]),
        out_shape=jax.ShapeDtypeStruct(xs.shape, F32),
        compiler_params=_cparams(("arbitrary",)),
        name="moe_experts",
    )(blk_exp, n_valid, xs, w_gate, w_up, w_down)


def _combine_kernel(dest_hbm, ys_hbm, w_ref, x1_ref, h2_ref, g2_ref, wsg_ref, wsu_ref, wsd_ref,
                    lng_ref, lnb_ref, o_ref, dest_smem, rows_ref, sem_idx, sem_row, *, tm):
    i = pl.program_id(0)
    idx_copy = pltpu.make_async_copy(dest_hbm.at[i], dest_smem, sem_idx)
    idx_copy.start()
    idx_copy.wait()

    def row_copy(t, k):
        return pltpu.make_async_copy(ys_hbm.at[pl.ds(dest_smem[k * tm + t], 1)],
                                     rows_ref.at[k, pl.ds(t, 1)], sem_row)

    def start_rows(t, carry):
        for k in range(TOP_K):
            row_copy(t, k).start()
        return carry

    def wait_rows(t, carry):
        for k in range(TOP_K):
            row_copy(t, k).wait()
        return carry

    lax.fori_loop(0, tm, start_rows, 0)
    h2 = h2_ref[...].astype(BF16)
    g = jnp.dot(h2, wsg_ref[...], preferred_element_type=F32)
    u = jnp.dot(h2, wsu_ref[...], preferred_element_type=F32)
    ffn = jnp.dot((_silu(g) * u).astype(BF16), wsd_ref[...], preferred_element_type=F32)
    lax.fori_loop(0, tm, wait_rows, 0)
    w = w_ref[...]
    for k in range(TOP_K):
        ffn = ffn + w[:, k:k + 1] * rows_ref[k]
    o_ref[...] = _layer_norm(ALPHA * x1_ref[...] + g2_ref[0] * ffn, lng_ref[...], lnb_ref[...])


def _combine(dest_tiles, ys, top_w_rows, x1, h2, gate2, w_sg, w_su, w_sd, ln_g, ln_b, seg, tm):
    t = x1.shape[0]
    per_seg = seg // tm

    def full(shape):
        return pl.BlockSpec(shape, lambda i: tuple(0 for _ in shape))

    tok = pl.BlockSpec((tm, D_MODEL), lambda i: (i, 0))
    return pl.pallas_call(
        functools.partial(_combine_kernel, tm=tm),
        grid=(t // tm,),
        in_specs=[pl.BlockSpec(memory_space=pl.ANY), pl.BlockSpec(memory_space=pl.ANY),
                  pl.BlockSpec((tm, TOP_K), lambda i: (i, 0)), tok, tok,
                  pl.BlockSpec((1, 1, D_MODEL), lambda i: (i // per_seg, 0, 0)),
                  full((D_MODEL, SHARED_HIDDEN)), full((D_MODEL, SHARED_HIDDEN)), full((SHARED_HIDDEN, D_MODEL)),
                  full((1, D_MODEL)), full((1, D_MODEL))],
        out_specs=tok,
        out_shape=jax.ShapeDtypeStruct((t, D_MODEL), F32),
        scratch_shapes=[pltpu.SMEM((TOP_K * tm,), I32), pltpu.VMEM((TOP_K, tm, D_MODEL), F32),
                        pltpu.SemaphoreType.DMA, pltpu.SemaphoreType.DMA],
        compiler_params=_cparams(("arbitrary",)),
        name="moe_combine",
    )(dest_tiles, ys, top_w_rows, x1, h2, gate2, w_sg, w_su, w_sd, ln_g, ln_b)


def _rope_tables(seq):
    inv_freq = ROPE_THETA ** (-jnp.arange(ROT_HALF, dtype=F32) * 2.0 / ROT_DIM)
    ang = jnp.arange(seq, dtype=F32)[:, None] * inv_freq[None, :]
    cos, sin = jnp.cos(ang), jnp.sin(ang)
    rest = ATTN_HEAD_DIM - ROT_DIM
    ones = jnp.ones((seq, rest), F32)
    zeros = jnp.zeros((seq, rest), F32)
    zh = jnp.zeros((seq, ROT_HALF), F32)
    reps = LANES // ATTN_HEAD_DIM
    cos_t = jnp.tile(jnp.concatenate([cos, cos, ones], axis=1), (1, reps))
    sa_t = jnp.tile(jnp.concatenate([-sin, zh, zeros], axis=1), (1, reps))
    sb_t = jnp.tile(jnp.concatenate([zh, sin, zeros], axis=1), (1, reps))
    return cos_t, sa_t, sb_t


def _split_in_weights(w_in):
    cuts = (D_INNER, CONV_DIM, SSD_HEADS, SSD_HEADS, ATTN_DIM, ATTN_DIM, ATTN_DIM, D_MODEL, D_MODEL)
    offs = [0]
    for c in cuts:
        offs.append(offs[-1] + c)
    z, xbc, dtf, dtb, q, k, v, gs, ga = (w_in[:, offs[i]:offs[i + 1]] for i in range(len(cuts)))
    w_main = jnp.concatenate([z, xbc, gs, ga, q, k, v], axis=1).astype(BF16)
    assert COL_GSSD % D_MODEL == 0 and COL_Q % ATTN_GROUP_COLS == 0 and w_main.shape[1] == N_MAIN
    pad = jnp.zeros((D_MODEL, DT_COLS - 2 * SSD_HEADS), F32)
    w_dt = jnp.concatenate([dtf, dtb, pad], axis=1).astype(BF16)
    return w_main, w_dt


def kernel(x_prompt, x_sample, c_prompt, c_sample, w_ada, b_ada, w_in, conv_w, conv_b, dt_bias_fwd, dt_bias_bwd, a_log_fwd, a_log_bwd, d_skip, ssd_norm_w, w_branch_ssd, w_branch_attn, w_out, ln1_g, ln1_b, w_router, router_bias, w_exp_gate, w_exp_up, w_exp_down, w_sh_gate, w_sh_up, w_sh_down, ln2_g, ln2_b):
    assert w_ada.shape[0] == DEPTH
    groups = ((x_prompt, c_prompt), (x_sample, c_sample))
    shapes = [(x.shape[0], x.shape[1]) for x, _ in groups]
    seg = math.gcd(*(s for _, s in shapes))
    tokens = [b * s for b, s in shapes]
    t_all = sum(tokens)
    row_offs = [0, tokens[0]]

    c_all = jnp.concatenate([c_prompt, c_sample], axis=0)
    n_req = c_all.shape[0]
    c_pad = jnp.zeros((-(-n_req // SUBLANES) * SUBLANES, D_MODEL), F32).at[:n_req].set(c_all)
    ada = _ada(c_pad, w_ada[0], b_ada)[:n_req]
    reps = jnp.array([s // seg for b, s in shapes for _ in range(b)], dtype=I32)
    n_seg = t_all // seg
    ada_seg = jnp.repeat(ada, reps, axis=0, total_repeat_length=n_seg)
    shift1, scale1, gate1, shift2, scale2, gate2 = (
        a.reshape(n_seg, 1, D_MODEL) for a in jnp.split(ada_seg, 6, axis=-1))

    x_all = jnp.concatenate([x_prompt.reshape(-1, D_MODEL), x_sample.reshape(-1, D_MODEL)], axis=0)
    w_main, w_dt = _split_in_weights(w_in[0])
    proj = _inproj(x_all, scale1, shift1, w_main, seg, tm=1024, tn=1280, out_dtype=F32)
    dt_raw = _inproj(x_all, scale1, shift1, w_dt, seg, tm=1024, tn=DT_COLS, out_dtype=F32)

    dt_pad = jnp.zeros((DT_COLS - 2 * SSD_HEADS,), F32)
    dt_bias = jnp.concatenate([dt_bias_fwd[0], dt_bias_bwd[0], dt_pad]).reshape(1, DT_COLS)
    a_log = jnp.concatenate([a_log_fwd[0], a_log_bwd[0], dt_pad]).reshape(1, DT_COLS)
    d_skip_cols = jnp.repeat(d_skip[0], SSD_HEAD_DIM).reshape(1, D_INNER)
    w_bs = w_branch_ssd[0].astype(BF16)
    w_ba = w_branch_attn[0].astype(BF16)
    w_o = w_out[0].astype(BF16)
    w_router_t = w_router[0].T.astype(BF16)

    x1_parts, h2_parts, logit_parts = [], [], []
    for (batch, seq), row_off in zip(shapes, row_offs):
        xbc = _conv(proj, conv_w[0], conv_b, row_off, batch, seq)
        y = _ssd(xbc, dt_raw, dt_bias, a_log, None, row_off, batch, seq, reverse=False)
        y = _ssd(xbc, dt_raw, dt_bias, a_log, y, row_off, batch, seq, reverse=True)
        y_ssd = _ssd_out(y, xbc, proj, d_skip_cols, ssd_norm_w, w_bs, row_off)
        tables = _rope_tables(seq)
        attn = []
        for gi, (_, dil) in enumerate(ATTN_PATTERNS):
            q, k, v = _prep(proj, tables, gi, dil, row_off, batch, seq)
            attn.append(_band_attn(q, k, v))
        x1, h2, logits_t = _merge(x_all, y_ssd, proj, attn, gate1, scale2, shift2, w_ba, w_o,
                                  ln1_g, ln1_b, w_router_t, row_off, batch, seq, seg)
        x1_parts.append(x1)
        h2_parts.append(h2)
        logit_parts.append(logits_t)
    x1 = jnp.concatenate(x1_parts, axis=0)
    h2 = jnp.concatenate(h2_parts, axis=0)
    logits_t = jnp.concatenate(logit_parts, axis=1)

    top_idx, top_w, rank, counts = _route(logits_t, router_bias[0].reshape(N_EXPERTS, 1))
    counts = counts[:, 0]
    padded = (counts + EXPERT_BLOCK - 1) // EXPERT_BLOCK * EXPERT_BLOCK
    pad_end = jnp.cumsum(padded)
    pad_start = pad_end - padded
    n_blocks = (t_all * TOP_K + N_EXPERTS * (EXPERT_BLOCK - 1)) // EXPERT_BLOCK
    n_rows = n_blocks * EXPERT_BLOCK
    tm_moe = 256
    dest_tiles = _dest(top_idx, rank, pad_start.astype(I32).reshape(N_EXPERTS, 1), tm_moe)
    n_valid = (pad_end[-1] // EXPERT_BLOCK).astype(I32).reshape(1)
    blk_start = jnp.arange(n_blocks, dtype=I32) * EXPERT_BLOCK
    blk_exp = jnp.searchsorted(pad_end, jnp.minimum(blk_start, pad_end[-1] - 1), side="right").astype(I32)
    blk_exp = jnp.minimum(blk_exp, N_EXPERTS - 1)

    xs = _dispatch(pad_end.astype(I32), dest_tiles, h2, n_rows, tm_moe)
    ys = _experts(blk_exp, n_valid, xs, w_exp_gate[0], w_exp_up[0], w_exp_down[0])
    out = _combine(dest_tiles, ys, top_w.T, x1, h2, gate2, w_sh_gate[0].astype(BF16), w_sh_up[0].astype(BF16),
                   w_sh_down[0].astype(BF16), ln2_g, ln2_b, seg, tm_moe)
    y_prompt = out[:tokens[0]].reshape(x_prompt.shape)
    y_sample = out[tokens[0]:].reshape(x_sample.shape)
    return (y_prompt, y_sample)
```

```python
import functools
import math

import jax
import jax.numpy as jnp
from jax import lax
from jax.experimental import pallas as pl
from jax.experimental.pallas import tpu as pltpu

F32 = jnp.float32
BF16 = jnp.bfloat16
I32 = jnp.int32

D_MODEL = 1024
D_INNER = 2048
SSD_HEADS = 32
SSD_HEAD_DIM = 64
SSD_GROUPS = 8
SSD_STATE = 128
SSD_CHUNK = 128
HEADS_PER_SSD_GROUP = SSD_HEADS // SSD_GROUPS
CONV_WIDTH = 5
CONV_PAD = CONV_WIDTH // 2
CONV_DIM = D_INNER + 2 * SSD_GROUPS * SSD_STATE
ATTN_PATTERNS = ((128, 1), (512, 4), (2048, 16))
ATTN_GROUPS = len(ATTN_PATTERNS)
ATTN_HEADS_PER_GROUP = 8
ATTN_HEAD_DIM = 64
ATTN_GROUP_COLS = ATTN_HEADS_PER_GROUP * ATTN_HEAD_DIM
ATTN_DIM = ATTN_GROUPS * ATTN_GROUP_COLS
ROT_DIM = ATTN_HEAD_DIM // 4
ROT_HALF = ROT_DIM // 2
ROPE_THETA = 500000.0
N_EXPERTS = 256
EXPERT_HIDDEN = 256
TOP_K = 8
N_EXPERT_GROUPS = 8
EXPERTS_PER_GROUP = N_EXPERTS // N_EXPERT_GROUPS
TOPK_GROUPS = 4
ROUTED_SCALE = 2.5
SHARED_HIDDEN = 256
EXPERT_BLOCK = 256
PACKED_COLS = D_MODEL // 2
DEPTH = 1
ALPHA = (2.0 * DEPTH) ** 0.25
NORM_EPS = 1e-5

COL_Z = 0
COL_XBC = COL_Z + D_INNER
COL_GSSD = COL_XBC + CONV_DIM
COL_GATTN = COL_GSSD + D_MODEL
COL_Q = COL_GATTN + D_MODEL
COL_K = COL_Q + ATTN_DIM
COL_V = COL_K + ATTN_DIM
N_MAIN = COL_V + ATTN_DIM
DT_COLS = 128

LANES = 128
SUBLANES = 8
VMEM_LIMIT = 56 * 1024 * 1024

DMA_PRIORITIES = 2

BAND_RADIUS = 64
ATTN_TQ = 128
NEG_INF = float("-inf")


def _cparams(sem):
    return pltpu.CompilerParams(dimension_semantics=sem, vmem_limit_bytes=VMEM_LIMIT)


def _sigmoid(x):
    return 1.0 / (1.0 + jnp.exp(-x))


def _silu(x):
    return x * _sigmoid(x)


def _softplus(x):
    return jnp.maximum(x, 0.0) + jnp.log(1.0 + jnp.exp(-jnp.abs(x)))


def _pack_halves(x):
    c = x.shape[1] // 2
    hi = lax.bitcast_convert_type(x[:, :c].astype(BF16).astype(F32), jnp.uint32)
    lo = lax.bitcast_convert_type(x[:, c:].astype(BF16).astype(F32), jnp.uint32)
    return hi | (lo >> 16)


def _unpack_halves(u):
    left = lax.bitcast_convert_type(u & jnp.uint32(0xFFFF0000), F32)
    right = lax.bitcast_convert_type(u << 16, F32)
    return left, right


def _ada_kernel(c_ref, w_ref, b_ref, o_ref):
    c = _silu(c_ref[...]).astype(BF16)
    o_ref[...] = jnp.dot(c, w_ref[...].astype(BF16), preferred_element_type=F32) + b_ref[...]


def _ada(c_pad, w_ada, b_ada):
    rows = c_pad.shape[0]
    n = w_ada.shape[1]
    tn = 1536
    return pl.pallas_call(
        _ada_kernel,
        grid=(n // tn,),
        in_specs=[pl.BlockSpec((rows, D_MODEL), lambda j: (0, 0)),
                  pl.BlockSpec((D_MODEL, tn), lambda j: (0, j)),
                  pl.BlockSpec((1, tn), lambda j: (0, j))],
        out_specs=pl.BlockSpec((rows, tn), lambda j: (0, j)),
        out_shape=jax.ShapeDtypeStruct((rows, n), F32),
        compiler_params=_cparams(("arbitrary",)),
        name="ada",
    )(c_pad, w_ada, b_ada)


def _inproj_kernel(x_ref, sc_ref, sh_ref, w_ref, o_ref, h_ref):
    @pl.when(pl.program_id(1) == 0)
    def _():
        h_ref[...] = (x_ref[...] * (1.0 + sc_ref[0]) + sh_ref[0]).astype(BF16)

    o_ref[...] = jnp.dot(h_ref[...], w_ref[...], preferred_element_type=F32).astype(o_ref.dtype)


def _inproj(x, scale, shift, w, seg, tm, tn, out_dtype):
    t = x.shape[0]
    n = w.shape[1]
    per_seg = seg // tm
    return pl.pallas_call(
        _inproj_kernel,
        grid=(t // tm, n // tn),
        in_specs=[pl.BlockSpec((tm, D_MODEL), lambda i, j: (i, 0)),
                  pl.BlockSpec((1, 1, D_MODEL), lambda i, j: (i // per_seg, 0, 0)),
                  pl.BlockSpec((1, 1, D_MODEL), lambda i, j: (i // per_seg, 0, 0)),
                  pl.BlockSpec((D_MODEL, tn), lambda i, j: (0, j))],
        out_specs=pl.BlockSpec((tm, tn), lambda i, j: (i, j)),
        out_shape=jax.ShapeDtypeStruct((t, n), out_dtype),
        scratch_shapes=[pltpu.VMEM((tm, D_MODEL), BF16)],
        compiler_params=_cparams(("arbitrary", "arbitrary")),
        name="inproj",
    )(x, scale, shift, w)


def _conv_kernel(xp_ref, x_ref, xn_ref, w_ref, b_ref, o_ref, *, n_l):
    l = pl.program_id(1)
    tl = x_ref.shape[0]
    xp = jnp.where(l > 0, xp_ref[...], 0.0)
    xn = jnp.where(l < n_l - 1, xn_ref[...], 0.0)
    xx = jnp.concatenate([xp, x_ref[...], xn], axis=0)
    acc = jnp.zeros(x_ref.shape, F32) + b_ref[...]
    for k in range(CONV_WIDTH):
        lo = SUBLANES - CONV_PAD + k
        acc = acc + w_ref[k:k + 1, :] * xx[lo:lo + tl, :]
    o_ref[...] = _silu(acc)


def _conv(proj, conv_w, conv_b, row_off, batch, seq, tl=512, tc=512):
    t = proj.shape[0]
    n_l = seq // tl
    n_c = CONV_DIM // tc
    c0 = COL_XBC // tc
    r8 = tl // SUBLANES
    last8 = t // SUBLANES - 1

    def row_blk(b, l):
        return (row_off + b * seq) // tl + l

    return pl.pallas_call(
        functools.partial(_conv_kernel, n_l=n_l),
        grid=(batch, n_l, n_c),
        in_specs=[pl.BlockSpec((SUBLANES, tc), lambda b, l, c: (jnp.maximum(row_blk(b, l) * r8 - 1, 0), c0 + c)),
                  pl.BlockSpec((tl, tc), lambda b, l, c: (row_blk(b, l), c0 + c)),
                  pl.BlockSpec((SUBLANES, tc), lambda b, l, c: (jnp.minimum((row_blk(b, l) + 1) * r8, last8), c0 + c)),
                  pl.BlockSpec((CONV_WIDTH, tc), lambda b, l, c: (0, c)),
                  pl.BlockSpec((1, tc), lambda b, l, c: (0, c))],
        out_specs=pl.BlockSpec((tl, tc), lambda b, l, c: (b * n_l + l, c)),
        out_shape=jax.ShapeDtypeStruct((batch * seq, CONV_DIM), F32),
        compiler_params=_cparams(("arbitrary", "arbitrary", "arbitrary")),
        name="conv",
    )(proj, proj, proj, conv_w, conv_b)


def _exact_tri_matmul(tri_bf16, x):
    hi = x.astype(BF16)
    r1 = x - hi.astype(F32)
    mid = r1.astype(BF16)
    lo = (r1 - mid.astype(F32)).astype(BF16)
    out = jnp.dot(tri_bf16, hi, preferred_element_type=F32)
    out = out + jnp.dot(tri_bf16, mid, preferred_element_type=F32)
    return out + jnp.dot(tri_bf16, lo, preferred_element_type=F32)


def _ssd_kernel(*refs, reverse, accumulate):
    if accumulate:
        x_ref, b_ref, c_ref, dt_ref, dtb_ref, alog_ref, yin_ref, y_ref, state_ref = refs
    else:
        x_ref, b_ref, c_ref, dt_ref, dtb_ref, alog_ref, y_ref, state_ref = refs
        yin_ref = None
    lane0 = SSD_HEADS if reverse else 0

    @pl.when(pl.program_id(1) == 0)
    def _():
        state_ref[...] = jnp.zeros(state_ref.shape, F32)

    dt = _softplus(dt_ref[...] + dtb_ref[...])
    da = dt * (-jnp.exp(alog_ref[...]))
    row = lax.broadcasted_iota(I32, (SSD_CHUNK, SSD_CHUNK), 0)
    col = lax.broadcasted_iota(I32, (SSD_CHUNK, SSD_CHUNK), 1)
    tri = (col >= row) if reverse else (col <= row)
    cum = _exact_tri_matmul(tri.astype(BF16), da)
    cum_t = cum.T
    edge = 0 if reverse else SSD_CHUNK - 1
    total = cum[edge:edge + 1, :]
    decay_to_end = jnp.exp(total - cum)
    exp_cum = jnp.exp(cum)
    chunk_decay = jnp.exp(total)

    for g in range(SSD_GROUPS):
        bg = b_ref[:, g * SSD_STATE:(g + 1) * SSD_STATE].astype(BF16)
        cg = c_ref[:, g * SSD_STATE:(g + 1) * SSD_STATE].astype(BF16)
        cb = lax.dot_general(cg, bg, (((1,), (1,)), ((), ())), preferred_element_type=F32)
        state = state_ref[g]
        y_off = lax.dot_general(cg, state.astype(BF16), (((1,), (1,)), ((), ())),
                                preferred_element_type=F32)
        ys = []
        ws = []
        decs = []
        for e in range(HEADS_PER_SSD_GROUP):
            h = g * HEADS_PER_SSD_GROUP + e
            ln = lane0 + h
            xh = x_ref[:, h * SSD_HEAD_DIM:(h + 1) * SSD_HEAD_DIM]
            xs = xh * dt[:, ln:ln + 1]
            seg = cum[:, ln:ln + 1] - cum_t[ln:ln + 1, :]
            decay = jnp.exp(jnp.where(tri, seg, NEG_INF))
            m = (cb * decay).astype(BF16)
            y_diag = jnp.dot(m, xs.astype(BF16), preferred_element_type=F32)
            ys.append(y_diag + y_off[:, e * SSD_HEAD_DIM:(e + 1) * SSD_HEAD_DIM] * exp_cum[:, ln:ln + 1])
            ws.append(xs * decay_to_end[:, ln:ln + 1])
            decs.append(jnp.broadcast_to(chunk_decay[:, ln:ln + 1], (SSD_HEAD_DIM, SSD_STATE)))
        yg = jnp.concatenate(ys, axis=1)
        cols = slice(g * HEADS_PER_SSD_GROUP * SSD_HEAD_DIM, (g + 1) * HEADS_PER_SSD_GROUP * SSD_HEAD_DIM)
        if accumulate:
            yg = yg + yin_ref[:, cols]
        y_ref[:, cols] = yg
        w = jnp.concatenate(ws, axis=1).astype(BF16)
        new_state = lax.dot_general(w, bg, (((0,), (0,)), ((), ())), preferred_element_type=F32)
        state_ref[g] = state * jnp.concatenate(decs, axis=0) + new_state


def _ssd(xbc, dt_raw, dt_bias, a_log, y_in, row_off, batch, seq, reverse):
    n_c = seq // SSD_CHUNK
    accumulate = y_in is not None

    def loc(b, c):
        return b * n_c + (n_c - 1 - c if reverse else c)

    def glob(b, c):
        return row_off // SSD_CHUNK + loc(b, c)

    in_specs = [pl.BlockSpec((SSD_CHUNK, D_INNER), lambda b, c: (loc(b, c), 0)),
                pl.BlockSpec((SSD_CHUNK, SSD_GROUPS * SSD_STATE), lambda b, c: (loc(b, c), 2)),
                pl.BlockSpec((SSD_CHUNK, SSD_GROUPS * SSD_STATE), lambda b, c: (loc(b, c), 3)),
                pl.BlockSpec((SSD_CHUNK, DT_COLS), lambda b, c: (glob(b, c), 0)),
                pl.BlockSpec((1, DT_COLS), lambda b, c: (0, 0)),
                pl.BlockSpec((1, DT_COLS), lambda b, c: (0, 0))]
    args = [xbc, xbc, xbc, dt_raw, dt_bias, a_log]
    aliases = {}
    if accumulate:
        in_specs.append(pl.BlockSpec((SSD_CHUNK, D_INNER), lambda b, c: (loc(b, c), 0)))
        args.append(y_in)
        aliases = {len(args) - 1: 0}
    return pl.pallas_call(
        functools.partial(_ssd_kernel, reverse=reverse, accumulate=accumulate),
        grid=(batch, n_c),
        in_specs=in_specs,
        out_specs=pl.BlockSpec((SSD_CHUNK, D_INNER), lambda b, c: (loc(b, c), 0)),
        out_shape=jax.ShapeDtypeStruct((batch * seq, D_INNER), F32),
        scratch_shapes=[pltpu.VMEM((SSD_GROUPS, HEADS_PER_SSD_GROUP * SSD_HEAD_DIM, SSD_STATE), F32)],
        input_output_aliases=aliases,
        compiler_params=_cparams(("arbitrary", "arbitrary")),
        name="ssd_bwd" if reverse else "ssd_fwd",
    )(*args)


def _ssd_out_kernel(y_ref, x_ref, z_ref, dskip_ref, nw_ref, w_ref, o_ref):
    y = y_ref[...] + dskip_ref[...] * x_ref[...]
    y = y * _silu(z_ref[...])
    ms = jnp.mean(y * y, axis=-1, keepdims=True)
    yn = y * lax.rsqrt(ms + NORM_EPS) * nw_ref[...]
    o_ref[...] = jnp.dot(yn.astype(BF16), w_ref[...], preferred_element_type=F32)


def _ssd_out(y, xbc, proj, d_skip_cols, norm_w, w_branch, row_off, tm=256):
    t = y.shape[0]
    return pl.pallas_call(
        _ssd_out_kernel,
        grid=(t // tm,),
        in_specs=[pl.BlockSpec((tm, D_INNER), lambda i: (i, 0)),
                  pl.BlockSpec((tm, D_INNER), lambda i: (i, 0)),
                  pl.BlockSpec((tm, D_INNER), lambda i: (row_off // tm + i, COL_Z // D_INNER)),
                  pl.BlockSpec((1, D_INNER), lambda i: (0, 0)),
                  pl.BlockSpec((1, D_INNER), lambda i: (0, 0)),
                  pl.BlockSpec((D_INNER, D_MODEL), lambda i: (0, 0))],
        out_specs=pl.BlockSpec((tm, D_MODEL), lambda i: (i, 0)),
        out_shape=jax.ShapeDtypeStruct((t, D_MODEL), F32),
        compiler_params=_cparams(("arbitrary",)),
        name="ssd_out",
    )(y, xbc, proj, d_skip_cols, norm_w, w_branch)


def _prep_kernel(q_ref, k_ref, v_ref, cos_ref, sa_ref, sb_ref, qo_ref, ko_ref, vo_ref, *, dil):
    rows = q_ref.shape[0] // dil

    def phase(ref, p):
        if dil == 1:
            return ref[...]
        return ref[pl.ds(p, rows, stride=dil), :]

    for p in range(dil):
        cos = phase(cos_ref, p)
        sa = phase(sa_ref, p)
        sb = phase(sb_ref, p)

        def rot(t):
            return (t * cos + pltpu.roll(t, LANES - ROT_HALF, axis=1) * sa
                    + pltpu.roll(t, ROT_HALF, axis=1) * sb)

        qo_ref[0, p] = rot(phase(q_ref, p)) * (ATTN_HEAD_DIM ** -0.5)
        ko_ref[0, p] = rot(phase(k_ref, p))
        vo_ref[0, p] = phase(v_ref, p)


def _prep(proj, tables, gi, dil, row_off, batch, seq, rows_per_step=2048):
    r = rows_per_step
    n_i = seq // r
    n_c = ATTN_GROUP_COLS // LANES
    m = seq // dil
    cos_t, sa_t, sb_t = tables

    def in_blk(col):
        cb = (col + gi * ATTN_GROUP_COLS) // LANES
        return pl.BlockSpec((r, LANES), lambda b, i, c: ((row_off + b * seq) // r + i, cb + c))

    tab = pl.BlockSpec((r, LANES), lambda b, i, c: (i, 0))
    out = pl.BlockSpec((1, dil, r // dil, LANES), lambda b, i, c: (b, 0, i, c))
    shape = jax.ShapeDtypeStruct((batch, dil, m, ATTN_GROUP_COLS), F32)
    return pl.pallas_call(
        functools.partial(_prep_kernel, dil=dil),
        grid=(batch, n_i, n_c),
        in_specs=[in_blk(COL_Q), in_blk(COL_K), in_blk(COL_V), tab, tab, tab],
        out_specs=[out, out, out],
        out_shape=[shape, shape, shape],
        compiler_params=_cparams(("arbitrary", "arbitrary", "arbitrary")),
        name=f"attn_prep_d{dil}",
    )(proj, proj, proj, cos_t, sa_t, sb_t)


def _band_attn_kernel(q_ref, kp_ref, kc_ref, kn_ref, vp_ref, vc_ref, vn_ref, o_ref, lse_ref, *, m_len):
    i = pl.program_id(1)
    tq = q_ref.shape[1]
    tk = tq + 2 * BAND_RADIUS
    q = q_ref[0].astype(BF16)
    k = jnp.concatenate([kp_ref[0], kc_ref[0], kn_ref[0]], axis=0).astype(BF16)
    v = jnp.concatenate([vp_ref[0], vc_ref[0], vn_ref[0]], axis=0).astype(BF16)
    qpos = i * tq + lax.broadcasted_iota(I32, (tq, tk), 0)
    kpos = i * tq - BAND_RADIUS + lax.broadcasted_iota(I32, (tq, tk), 1)
    valid = (jnp.abs(qpos - kpos) <= BAND_RADIUS) & (kpos >= 0) & (kpos < m_len)
    for h in range(ATTN_HEADS_PER_GROUP):
        cols = slice(h * ATTN_HEAD_DIM, (h + 1) * ATTN_HEAD_DIM)
        s = lax.dot_general(q[:, cols], k[:, cols], (((1,), (1,)), ((), ())), preferred_element_type=F32)
        s = jnp.where(valid, s, NEG_INF)
        mx = jnp.max(s, axis=-1, keepdims=True)
        p = jnp.exp(s - mx)
        den = jnp.sum(p, axis=-1, keepdims=True)
        o = jnp.dot(p.astype(BF16), v[:, cols], preferred_element_type=F32)
        o_ref[0, :, cols] = o / den
        lse_ref[0, :, cols] = jnp.broadcast_to(mx + jnp.log(den), (tq, ATTN_HEAD_DIM))


def _band_attn(q, k, v):
    b, d, m, c = q.shape
    n = b * d
    q, k, v = (a.reshape(n, m, c) for a in (q, k, v))
    tq = ATTN_TQ
    per = tq // BAND_RADIUS
    last = m // BAND_RADIUS - 1
    main = pl.BlockSpec((1, tq, c), lambda s, i: (s, i, 0))
    prev = pl.BlockSpec((1, BAND_RADIUS, c), lambda s, i: (s, jnp.maximum(i * per - 1, 0), 0))
    nxt = pl.BlockSpec((1, BAND_RADIUS, c), lambda s, i: (s, jnp.minimum((i + 1) * per, last), 0))
    shape = jax.ShapeDtypeStruct((n, m, c), F32)
    o, lse = pl.pallas_call(
        functools.partial(_band_attn_kernel, m_len=m),
        grid=(n, m // tq),
        in_specs=[main, prev, main, nxt, prev, main, nxt],
        out_specs=[main, main],
        out_shape=[shape, shape],
        compiler_params=_cparams(("arbitrary", "arbitrary")),
        name=f"band_attn_m{m}",
    )(q, k, k, k, v, v, v)
    return o.reshape(b, d, m, c), lse.reshape(b, d, m, c)


def _layer_norm(x, g, b):
    mu = jnp.mean(x, axis=-1, keepdims=True)
    xc = x - mu
    var = jnp.mean(xc * xc, axis=-1, keepdims=True)
    return xc * lax.rsqrt(var + NORM_EPS) * g + b


def _merge_kernel(x_ref, yssd_ref, gs_ref, ga_ref, o1_ref, l1_ref, o2_ref, l2_ref, o3_ref, l3_ref,
                  g1_ref, sc2_ref, sh2_ref, wba_ref, wout_ref, lng_ref, lnb_ref, wr_ref,
                  x1_ref, h2_ref, logit_ref, scr_ref):
    tm = x_ref.shape[0]

    def interleaved(ref, dil, slot):
        if dil == 1:
            return ref[0, 0]
        rows = tm // dil
        n_c = ATTN_GROUP_COLS // LANES
        for p in range(dil):
            for c in range(n_c):
                scr_ref[slot * n_c + c, pl.ds(p, rows, stride=dil), :] = ref[0, p, :, c * LANES:(c + 1) * LANES]
        return jnp.concatenate([scr_ref[slot * n_c + c] for c in range(n_c)], axis=1)

    outs = []
    lses = []
    slot = 0
    for (o_ref, l_ref), (_, dil) in zip(((o1_ref, l1_ref), (o2_ref, l2_ref), (o3_ref, l3_ref)), ATTN_PATTERNS):
        outs.append(interleaved(o_ref, dil, slot))
        lses.append(interleaved(l_ref, dil, slot + 1))
        slot += 2
    mx = jnp.maximum(jnp.maximum(lses[0], lses[1]), lses[2])
    es = [jnp.exp(l - mx) for l in lses]
    den = es[0] + es[1] + es[2]
    y_att = (es[0] * outs[0] + es[1] * outs[1] + es[2] * outs[2]) / den
    y_attn = jnp.dot(y_att.astype(BF16), wba_ref[...], preferred_element_type=F32)
    merged = _sigmoid(gs_ref[...]) * yssd_ref[...] + _sigmoid(ga_ref[...]) * y_attn
    mix = jnp.dot(merged.astype(BF16), wout_ref[...], preferred_element_type=F32)
    x1 = _layer_norm(ALPHA * x_ref[...] + g1_ref[0] * mix, lng_ref[...], lnb_ref[...])
    x1_ref[...] = x1
    h2 = x1 * (1.0 + sc2_ref[0]) + sh2_ref[0]
    h2_ref[...] = _pack_halves(h2)
    logit_ref[...] = lax.dot_general(wr_ref[...], h2.astype(BF16), (((1,), (1,)), ((), ())),
                                     preferred_element_type=F32)


def _merge(x, y_ssd, proj, attn, gate1, scale2, shift2, w_ba, w_out, ln_g, ln_b, w_router_t,
           row_off, batch, seq, seg, tm=256):
    t = batch * seq
    n_i = seq // tm
    per_seg = seg // tm
    seg0 = row_off // seg

    def rows(col_blk):
        return pl.BlockSpec((tm, D_MODEL), lambda b, i: (row_off // tm + b * n_i + i, col_blk))

    def local(width):
        return pl.BlockSpec((tm, width), lambda b, i: (b * n_i + i, 0))

    def modv():
        return pl.BlockSpec((1, 1, D_MODEL), lambda b, i: (seg0 + (b * n_i + i) // per_seg, 0, 0))

    def full(shape):
        return pl.BlockSpec(shape, lambda b, i: tuple(0 for _ in shape))

    attn_specs = []
    attn_args = []
    for (o, lse), (_, dil) in zip(attn, ATTN_PATTERNS):
        spec = pl.BlockSpec((1, dil, tm // dil, ATTN_GROUP_COLS), lambda b, i: (b, 0, i, 0))
        attn_specs += [spec, spec]
        attn_args += [o, lse]
    return pl.pallas_call(
        _merge_kernel,
        grid=(batch, n_i),
        in_specs=[rows(0), local(D_MODEL), rows(COL_GSSD // D_MODEL), rows(COL_GATTN // D_MODEL)] + attn_specs
                 + [modv(), modv(), modv(), full((ATTN_GROUP_COLS, D_MODEL)), full((D_MODEL, D_MODEL)),
                    full((1, D_MODEL)), full((1, D_MODEL)), full((N_EXPERTS, D_MODEL))],
        out_specs=[local(D_MODEL), local(PACKED_COLS),
                   pl.BlockSpec((N_EXPERTS, tm), lambda b, i: (0, b * n_i + i))],
        out_shape=[jax.ShapeDtypeStruct((t, D_MODEL), F32), jax.ShapeDtypeStruct((t, PACKED_COLS), jnp.uint32),
                   jax.ShapeDtypeStruct((N_EXPERTS, t), F32)],
        scratch_shapes=[pltpu.VMEM((2 * ATTN_GROUPS * (ATTN_GROUP_COLS // LANES), tm, LANES), F32)],
        compiler_params=_cparams(("arbitrary", "arbitrary")),
        name="mixer_merge",
    )(x, y_ssd, proj, proj, *attn_args, gate1, scale2, shift2, w_ba, w_out, ln_g, ln_b, w_router_t)


def _route_kernel(logit_ref, bias_ref, idx_ref, w_ref, rank_ref, cnt_ref, run_ref):
    tm = logit_ref.shape[1]

    @pl.when(pl.program_id(0) == 0)
    def _():
        run_ref[...] = jnp.zeros(run_ref.shape, F32)

    scores = _sigmoid(logit_ref[...])
    sel = scores + bias_ref[...]
    iota_g = lax.broadcasted_iota(I32, (EXPERTS_PER_GROUP, tm), 0)
    grp = []
    for g in range(N_EXPERT_GROUPS):
        sg = sel[g * EXPERTS_PER_GROUP:(g + 1) * EXPERTS_PER_GROUP, :]
        top1 = jnp.max(sg, axis=0, keepdims=True)
        first = jnp.min(jnp.where(sg == top1, iota_g, EXPERTS_PER_GROUP), axis=0, keepdims=True)
        top2 = jnp.max(jnp.where(iota_g == first, NEG_INF, sg), axis=0, keepdims=True)
        grp.append(top1 + top2)
    gs = jnp.concatenate(grp, axis=0)
    iota_n = lax.broadcasted_iota(I32, (N_EXPERT_GROUPS, tm), 0)
    keep_g = jnp.zeros((N_EXPERT_GROUPS, tm), F32)
    for _ in range(TOPK_GROUPS):
        best = jnp.max(gs, axis=0, keepdims=True)
        first = jnp.min(jnp.where(gs == best, iota_n, N_EXPERT_GROUPS), axis=0, keepdims=True)
        hit = iota_n == first
        keep_g = jnp.where(hit, 1.0, keep_g)
        gs = jnp.where(hit, NEG_INF, gs)
    cand = jnp.concatenate(
        [jnp.where(keep_g[g:g + 1, :] > 0.0, sel[g * EXPERTS_PER_GROUP:(g + 1) * EXPERTS_PER_GROUP, :], NEG_INF)
         for g in range(N_EXPERT_GROUPS)], axis=0)
    iota_e = lax.broadcasted_iota(I32, (N_EXPERTS, tm), 0)
    chosen = jnp.zeros((N_EXPERTS, tm), F32)
    idxs = []
    ws = []
    for _ in range(TOP_K):
        best = jnp.max(cand, axis=0, keepdims=True)
        first = jnp.min(jnp.where(cand == best, iota_e, N_EXPERTS), axis=0, keepdims=True)
        hit = iota_e == first
        idxs.append(first)
        ws.append(jnp.sum(jnp.where(hit, scores, 0.0), axis=0, keepdims=True))
        chosen = jnp.where(hit, 1.0, chosen)
        cand = jnp.where(hit, NEG_INF, cand)
    top_w = jnp.concatenate(ws, axis=0)
    top_w = top_w / jnp.sum(top_w, axis=0, keepdims=True) * ROUTED_SCALE
    idx_ref[...] = jnp.concatenate(idxs, axis=0)
    w_ref[...] = top_w
    s_i = lax.broadcasted_iota(I32, (tm, tm), 0)
    t_i = lax.broadcasted_iota(I32, (tm, tm), 1)
    before = (s_i < t_i).astype(BF16)
    prior = jnp.dot(chosen.astype(BF16), before, preferred_element_type=F32) + run_ref[...]
    ranks = [jnp.sum(jnp.where(iota_e == idxs[k], prior, 0.0), axis=0, keepdims=True) for k in range(TOP_K)]
    rank_ref[...] = jnp.concatenate(ranks, axis=0).astype(I32)
    run_ref[...] = run_ref[...] + jnp.sum(chosen, axis=1, keepdims=True)
    cnt_ref[...] = run_ref[...].astype(I32)


def _route(logits_t, bias_col, tm=256):
    t = logits_t.shape[1]
    tok = pl.BlockSpec((TOP_K, tm), lambda i: (0, i))
    return pl.pallas_call(
        _route_kernel,
        grid=(t // tm,),
        in_specs=[pl.BlockSpec((N_EXPERTS, tm), lambda i: (0, i)),
                  pl.BlockSpec((N_EXPERTS, 1), lambda i: (0, 0))],
        out_specs=[tok, tok, tok, pl.BlockSpec((N_EXPERTS, 1), lambda i: (0, 0))],
        out_shape=[jax.ShapeDtypeStruct((TOP_K, t), I32), jax.ShapeDtypeStruct((TOP_K, t), F32),
                   jax.ShapeDtypeStruct((TOP_K, t), I32), jax.ShapeDtypeStruct((N_EXPERTS, 1), I32)],
        scratch_shapes=[pltpu.VMEM((N_EXPERTS, 1), F32)],
        compiler_params=_cparams(("arbitrary",)),
        name="route",
    )(logits_t, bias_col)


def _dest_kernel(idx_ref, rank_ref, start_ref, o_ref):
    tm = idx_ref.shape[1]
    iota_e = lax.broadcasted_iota(I32, (N_EXPERTS, tm), 0)
    start = start_ref[...].astype(F32)
    for k in range(TOP_K):
        base = jnp.sum(jnp.where(iota_e == idx_ref[k:k + 1, :], start, 0.0), axis=0, keepdims=True)
        o_ref[0, :, k * tm:(k + 1) * tm] = base.astype(I32) + rank_ref[k:k + 1, :]


def _dest(top_idx, rank, pad_start_col, tm):
    t = top_idx.shape[1]
    tok = pl.BlockSpec((TOP_K, tm), lambda i: (0, i))
    out = pl.pallas_call(
        _dest_kernel,
        grid=(t // tm,),
        in_specs=[tok, tok, pl.BlockSpec((N_EXPERTS, 1), lambda i: (0, 0))],
        out_specs=pl.BlockSpec((1, 1, TOP_K * tm), lambda i: (i, 0, 0)),
        out_shape=jax.ShapeDtypeStruct((t // tm, 1, TOP_K * tm), I32),
        compiler_params=_cparams(("arbitrary",)),
        name="moe_dest",
    )(top_idx, rank, pad_start_col)
    return out.reshape(t // tm, TOP_K * tm)


def _dispatch_kernel(pad_end_ref, dest_hbm, h_ref, xs_hbm, dest_smem, zeros_ref, sem_idx, sem_zero, sem_row,
                     *, tm):
    i = pl.program_id(0)
    idx_copy = pltpu.make_async_copy(dest_hbm.at[i], dest_smem, sem_idx)
    idx_copy.start()

    def zero_copy(e):
        start = pl.multiple_of(jnp.maximum(pad_end_ref[e] - EXPERT_BLOCK, 0), EXPERT_BLOCK)
        return pltpu.make_async_copy(zeros_ref, xs_hbm.at[pl.ds(start, EXPERT_BLOCK)], sem_zero)

    @pl.when(i == 0)
    def _():
        zeros_ref[...] = jnp.zeros(zeros_ref.shape, zeros_ref.dtype)

        def start_zero(e, carry):
            zero_copy(e).start()
            return carry

        def wait_zero(e, carry):
            zero_copy(e).wait()
            return carry

        lax.fori_loop(0, N_EXPERTS, start_zero, 0)
        lax.fori_loop(0, N_EXPERTS, wait_zero, 0)

    idx_copy.wait()

    def row_copy(t, k):
        return pltpu.make_async_copy(h_ref.at[pl.ds(t, 1)],
                                     xs_hbm.at[pl.ds(dest_smem[k * tm + t], 1)], sem_row)

    def start_rows(t, carry):
        for k in range(TOP_K):
            row_copy(t, k).start(priority=k % DMA_PRIORITIES)
        return carry

    def wait_rows(t, carry):
        for k in range(TOP_K):
            row_copy(t, k).wait()
        return carry

    lax.fori_loop(0, tm, start_rows, 0)
    lax.fori_loop(0, tm, wait_rows, 0)


def _dispatch(pad_end, dest_tiles, h2, n_rows, tm):
    n_tiles = dest_tiles.shape[0]
    return pl.pallas_call(
        functools.partial(_dispatch_kernel, tm=tm),
        grid_spec=pltpu.PrefetchScalarGridSpec(
            num_scalar_prefetch=1,
            grid=(n_tiles,),
            in_specs=[pl.BlockSpec(memory_space=pl.ANY),
                      pl.BlockSpec((tm, PACKED_COLS), lambda i, pe: (i, 0))],
            out_specs=pl.BlockSpec(memory_space=pl.ANY),
            scratch_shapes=[pltpu.SMEM((TOP_K * tm,), I32),
                            pltpu.VMEM((EXPERT_BLOCK, PACKED_COLS), jnp.uint32),
                            pltpu.SemaphoreType.DMA, pltpu.SemaphoreType.DMA, pltpu.SemaphoreType.DMA]),
        out_shape=jax.ShapeDtypeStruct((n_rows, PACKED_COLS), jnp.uint32),
        compiler_params=_cparams(("arbitrary",)),
        name="moe_dispatch",
    )(pad_end, dest_tiles, h2)


def _experts_kernel(blk_exp_ref, n_valid_ref, xs_ref, wg_ref, wu_ref, wd_ref, ys_ref, wgu_ref, wdn_ref):
    i = pl.program_id(0)

    @pl.when(i < n_valid_ref[0])
    def _():
        @pl.when((i == 0) | (blk_exp_ref[i] != blk_exp_ref[jnp.maximum(i - 1, 0)]))
        def _():
            wgu_ref[:, :EXPERT_HIDDEN] = wg_ref[0].astype(BF16)
            wgu_ref[:, EXPERT_HIDDEN:] = wu_ref[0].astype(BF16)
            wdn_ref[...] = wd_ref[0].astype(BF16)

        left, right = _unpack_halves(xs_ref[...])
        x = jnp.concatenate([left.astype(BF16), right.astype(BF16)], axis=1)
        gu = jnp.dot(x, wgu_ref[...], preferred_element_type=F32)
        hmid = (_silu(gu[:, :EXPERT_HIDDEN]) * gu[:, EXPERT_HIDDEN:]).astype(BF16)
        ys_ref[...] = _pack_halves(jnp.dot(hmid, wdn_ref[...], preferred_element_type=F32))


def _experts(blk_exp, n_valid, xs, w_gate, w_up, w_down):
    n_blocks = xs.shape[0] // EXPERT_BLOCK

    def row_map(i, be, nv):
        return (jnp.minimum(i, nv[0] - 1), 0)

    def w_map(i, be, nv):
        return (be[i], 0, 0)

    return pl.pallas_call(
        _experts_kernel,
        grid_spec=pltpu.PrefetchScalarGridSpec(
            num_scalar_prefetch=2,
            grid=(n_blocks,),
            in_specs=[pl.BlockSpec((EXPERT_BLOCK, PACKED_COLS), row_map),
                      pl.BlockSpec((1, D_MODEL, EXPERT_HIDDEN), w_map),
                      pl.BlockSpec((1, D_MODEL, EXPERT_HIDDEN), w_map),
                      pl.BlockSpec((1, EXPERT_HIDDEN, D_MODEL), w_map)],
            out_specs=pl.BlockSpec((EXPERT_BLOCK, PACKED_COLS), row_map),
            scratch_shapes=[pltpu.VMEM((D_MODEL, 2 * EXPERT_HIDDEN), BF16),
                            pltpu.VMEM((EXPERT_HIDDEN, D_MODEL), BF16)]),
        out_shape=jax.ShapeDtypeStruct(xs.shape, jnp.uint32),
        compiler_params=_cparams(("arbitrary",)),
        name="moe_experts",
    )(blk_exp, n_valid, xs, w_gate, w_up, w_down)


def _combine_kernel(dest_hbm, ys_hbm, w_ref, x1_ref, h2_ref, g2_ref, wsg_ref, wsu_ref, wsd_ref,
                    lng_ref, lnb_ref, o_ref, dest_smem, rows_ref, sem_idx, sem_row, *, tm):
    i = pl.program_id(0)
    idx_copy = pltpu.make_async_copy(dest_hbm.at[i], dest_smem, sem_idx)
    idx_copy.start()
    idx_copy.wait()

    def row_copy(t, k):
        return pltpu.make_async_copy(ys_hbm.at[pl.ds(dest_smem[k * tm + t], 1)],
                                     rows_ref.at[k, pl.ds(t, 1)], sem_row)

    def start_rows(t, carry):
        for k in range(TOP_K):
            row_copy(t, k).start(priority=k % DMA_PRIORITIES)
        return carry

    def wait_rows(t, carry):
        for k in range(TOP_K):
            row_copy(t, k).wait()
        return carry

    lax.fori_loop(0, tm, start_rows, 0)
    left, right = _unpack_halves(h2_ref[...])
    h2 = jnp.concatenate([left.astype(BF16), right.astype(BF16)], axis=1)
    g = jnp.dot(h2, wsg_ref[...], preferred_element_type=F32)
    u = jnp.dot(h2, wsu_ref[...], preferred_element_type=F32)
    ffn = jnp.dot((_silu(g) * u).astype(BF16), wsd_ref[...], preferred_element_type=F32)
    lax.fori_loop(0, tm, wait_rows, 0)
    w = w_ref[...]
    acc_l = ffn[:, :PACKED_COLS]
    acc_r = ffn[:, PACKED_COLS:]
    for k in range(TOP_K):
        left, right = _unpack_halves(rows_ref[k])
        acc_l = acc_l + w[:, k:k + 1] * left
        acc_r = acc_r + w[:, k:k + 1] * right
    ffn = jnp.concatenate([acc_l, acc_r], axis=1)
    o_ref[...] = _layer_norm(ALPHA * x1_ref[...] + g2_ref[0] * ffn, lng_ref[...], lnb_ref[...])


def _combine(dest_tiles, ys, top_w_rows, x1, h2, gate2, w_sg, w_su, w_sd, ln_g, ln_b, seg, tm):
    t = x1.shape[0]
    per_seg = seg // tm

    def full(shape):
        return pl.BlockSpec(shape, lambda i: tuple(0 for _ in shape))

    tok = pl.BlockSpec((tm, D_MODEL), lambda i: (i, 0))
    return pl.pallas_call(
        functools.partial(_combine_kernel, tm=tm),
        grid=(t // tm,),
        in_specs=[pl.BlockSpec(memory_space=pl.ANY), pl.BlockSpec(memory_space=pl.ANY),
                  pl.BlockSpec((tm, TOP_K), lambda i: (i, 0)), tok,
                  pl.BlockSpec((tm, PACKED_COLS), lambda i: (i, 0)),
                  pl.BlockSpec((1, 1, D_MODEL), lambda i: (i // per_seg, 0, 0)),
                  full((D_MODEL, SHARED_HIDDEN)), full((D_MODEL, SHARED_HIDDEN)), full((SHARED_HIDDEN, D_MODEL)),
                  full((1, D_MODEL)), full((1, D_MODEL))],
        out_specs=tok,
        out_shape=jax.ShapeDtypeStruct((t, D_MODEL), F32),
        scratch_shapes=[pltpu.SMEM((TOP_K * tm,), I32), pltpu.VMEM((TOP_K, tm, PACKED_COLS), jnp.uint32),
                        pltpu.SemaphoreType.DMA, pltpu.SemaphoreType.DMA],
        compiler_params=_cparams(("arbitrary",)),
        name="moe_combine",
    )(dest_tiles, ys, top_w_rows, x1, h2, gate2, w_sg, w_su, w_sd, ln_g, ln_b)


def _rope_tables(seq):
    inv_freq = ROPE_THETA ** (-jnp.arange(ROT_HALF, dtype=F32) * 2.0 / ROT_DIM)
    ang = jnp.arange(seq, dtype=F32)[:, None] * inv_freq[None, :]
    cos, sin = jnp.cos(ang), jnp.sin(ang)
    rest = ATTN_HEAD_DIM - ROT_DIM
    ones = jnp.ones((seq, rest), F32)
    zeros = jnp.zeros((seq, rest), F32)
    zh = jnp.zeros((seq, ROT_HALF), F32)
    reps = LANES // ATTN_HEAD_DIM
    cos_t = jnp.tile(jnp.concatenate([cos, cos, ones], axis=1), (1, reps))
    sa_t = jnp.tile(jnp.concatenate([-sin, zh, zeros], axis=1), (1, reps))
    sb_t = jnp.tile(jnp.concatenate([zh, sin, zeros], axis=1), (1, reps))
    return cos_t, sa_t, sb_t


def _split_in_weights(w_in):
    cuts = (D_INNER, CONV_DIM, SSD_HEADS, SSD_HEADS, ATTN_DIM, ATTN_DIM, ATTN_DIM, D_MODEL, D_MODEL)
    offs = [0]
    for c in cuts:
        offs.append(offs[-1] + c)
    z, xbc, dtf, dtb, q, k, v, gs, ga = (w_in[:, offs[i]:offs[i + 1]] for i in range(len(cuts)))
    w_main = jnp.concatenate([z, xbc, gs, ga, q, k, v], axis=1).astype(BF16)
    assert COL_GSSD % D_MODEL == 0 and COL_Q % ATTN_GROUP_COLS == 0 and w_main.shape[1] == N_MAIN
    pad = jnp.zeros((D_MODEL, DT_COLS - 2 * SSD_HEADS), F32)
    w_dt = jnp.concatenate([dtf, dtb, pad], axis=1).astype(BF16)
    return w_main, w_dt


def kernel(x_prompt, x_sample, c_prompt, c_sample, w_ada, b_ada, w_in, conv_w, conv_b, dt_bias_fwd, dt_bias_bwd, a_log_fwd, a_log_bwd, d_skip, ssd_norm_w, w_branch_ssd, w_branch_attn, w_out, ln1_g, ln1_b, w_router, router_bias, w_exp_gate, w_exp_up, w_exp_down, w_sh_gate, w_sh_up, w_sh_down, ln2_g, ln2_b):
    assert w_ada.shape[0] == DEPTH
    groups = ((x_prompt, c_prompt), (x_sample, c_sample))
    shapes = [(x.shape[0], x.shape[1]) for x, _ in groups]
    seg = math.gcd(*(s for _, s in shapes))
    tokens = [b * s for b, s in shapes]
    t_all = sum(tokens)
    row_offs = [0, tokens[0]]

    c_all = jnp.concatenate([c_prompt, c_sample], axis=0)
    n_req = c_all.shape[0]
    c_pad = jnp.zeros((-(-n_req // SUBLANES) * SUBLANES, D_MODEL), F32).at[:n_req].set(c_all)
    ada = _ada(c_pad, w_ada[0], b_ada)[:n_req]
    reps = jnp.array([s // seg for b, s in shapes for _ in range(b)], dtype=I32)
    n_seg = t_all // seg
    ada_seg = jnp.repeat(ada, reps, axis=0, total_repeat_length=n_seg)
    shift1, scale1, gate1, shift2, scale2, gate2 = (
        a.reshape(n_seg, 1, D_MODEL) for a in jnp.split(ada_seg, 6, axis=-1))

    x_all = jnp.concatenate([x_prompt.reshape(-1, D_MODEL), x_sample.reshape(-1, D_MODEL)], axis=0)
    w_main, w_dt = _split_in_weights(w_in[0])
    proj = _inproj(x_all, scale1, shift1, w_main, seg, tm=1024, tn=1280, out_dtype=F32)
    dt_raw = _inproj(x_all, scale1, shift1, w_dt, seg, tm=1024, tn=DT_COLS, out_dtype=F32)

    dt_pad = jnp.zeros((DT_COLS - 2 * SSD_HEADS,), F32)
    dt_bias = jnp.concatenate([dt_bias_fwd[0], dt_bias_bwd[0], dt_pad]).reshape(1, DT_COLS)
    a_log = jnp.concatenate([a_log_fwd[0], a_log_bwd[0], dt_pad]).reshape(1, DT_COLS)
    d_skip_cols = jnp.repeat(d_skip[0], SSD_HEAD_DIM).reshape(1, D_INNER)
    w_bs = w_branch_ssd[0].astype(BF16)
    w_ba = w_branch_attn[0].astype(BF16)
    w_o = w_out[0].astype(BF16)
    w_router_t = w_router[0].T.astype(BF16)

    x1_parts, h2_parts, logit_parts = [], [], []
    for (batch, seq), row_off in zip(shapes, row_offs):
        xbc = _conv(proj, conv_w[0], conv_b, row_off, batch, seq)
        y = _ssd(xbc, dt_raw, dt_bias, a_log, None, row_off, batch, seq, reverse=False)
        y = _ssd(xbc, dt_raw, dt_bias, a_log, y, row_off, batch, seq, reverse=True)
        y_ssd = _ssd_out(y, xbc, proj, d_skip_cols, ssd_norm_w, w_bs, row_off)
        tables = _rope_tables(seq)
        attn = []
        for gi, (_, dil) in enumerate(ATTN_PATTERNS):
            q, k, v = _prep(proj, tables, gi, dil, row_off, batch, seq)
            attn.append(_band_attn(q, k, v))
        x1, h2, logits_t = _merge(x_all, y_ssd, proj, attn, gate1, scale2, shift2, w_ba, w_o,
                                  ln1_g, ln1_b, w_router_t, row_off, batch, seq, seg)
        x1_parts.append(x1)
        h2_parts.append(h2)
        logit_parts.append(logits_t)
    x1 = jnp.concatenate(x1_parts, axis=0)
    h2 = jnp.concatenate(h2_parts, axis=0)
    logits_t = jnp.concatenate(logit_parts, axis=1)

    top_idx, top_w, rank, counts = _route(logits_t, router_bias[0].reshape(N_EXPERTS, 1))
    counts = counts[:, 0]
    padded = (counts + EXPERT_BLOCK - 1) // EXPERT_BLOCK * EXPERT_BLOCK
    pad_end = jnp.cumsum(padded)
    pad_start = pad_end - padded
    n_blocks = (t_all * TOP_K + N_EXPERTS * (EXPERT_BLOCK - 1)) // EXPERT_BLOCK
    n_rows = n_blocks * EXPERT_BLOCK
    tm_moe = 256
    dest_tiles = _dest(top_idx, rank, pad_start.astype(I32).reshape(N_EXPERTS, 1), tm_moe)
    n_valid = (pad_end[-1] // EXPERT_BLOCK).astype(I32).reshape(1)
    blk_start = jnp.arange(n_blocks, dtype=I32) * EXPERT_BLOCK
    blk_exp = jnp.searchsorted(pad_end, jnp.minimum(blk_start, pad_end[-1] - 1), side="right").astype(I32)
    blk_exp = jnp.minimum(blk_exp, N_EXPERTS - 1)

    xs = _dispatch(pad_end.astype(I32), dest_tiles, h2, n_rows, tm_moe)
    ys = _experts(blk_exp, n_valid, xs, w_exp_gate[0], w_exp_up[0], w_exp_down[0])
    out = _combine(dest_tiles, ys, top_w.T, x1, h2, gate2, w_sh_gate[0].astype(BF16), w_sh_up[0].astype(BF16),
                   w_sh_down[0].astype(BF16), ln2_g, ln2_b, seg, tm_moe)
    y_prompt = out[:tokens[0]].reshape(x_prompt.shape)
    y_sample = out[tokens[0]:].reshape(x_sample.shape)
    return (y_prompt, y_sample)
```

```python
import functools
import math

import jax
import jax.numpy as jnp
from jax import lax
from jax.experimental import pallas as pl
from jax.experimental.pallas import tpu as pltpu

F32 = jnp.float32
BF16 = jnp.bfloat16
I32 = jnp.int32

D_MODEL = 1024
D_INNER = 2048
SSD_HEADS = 32
SSD_HEAD_DIM = 64
SSD_GROUPS = 8
SSD_STATE = 128
SSD_CHUNK = 128
HEADS_PER_SSD_GROUP = SSD_HEADS // SSD_GROUPS
CONV_WIDTH = 5
CONV_PAD = CONV_WIDTH // 2
CONV_DIM = D_INNER + 2 * SSD_GROUPS * SSD_STATE
ATTN_PATTERNS = ((128, 1), (512, 4), (2048, 16))
ATTN_GROUPS = len(ATTN_PATTERNS)
ATTN_HEADS_PER_GROUP = 8
ATTN_HEAD_DIM = 64
ATTN_GROUP_COLS = ATTN_HEADS_PER_GROUP * ATTN_HEAD_DIM
ATTN_DIM = ATTN_GROUPS * ATTN_GROUP_COLS
ROT_DIM = ATTN_HEAD_DIM // 4
ROT_HALF = ROT_DIM // 2
ROPE_THETA = 500000.0
N_EXPERTS = 256
EXPERT_HIDDEN = 256
TOP_K = 8
N_EXPERT_GROUPS = 8
EXPERTS_PER_GROUP = N_EXPERTS // N_EXPERT_GROUPS
TOPK_GROUPS = 4
ROUTED_SCALE = 2.5
SHARED_HIDDEN = 256
EXPERT_BLOCK = 256
PACKED_COLS = D_MODEL // 2
DEPTH = 1
ALPHA = (2.0 * DEPTH) ** 0.25
NORM_EPS = 1e-5

COL_Z = 0
COL_XBC = COL_Z + D_INNER
COL_GSSD = COL_XBC + CONV_DIM
COL_GATTN = COL_GSSD + D_MODEL
COL_Q = COL_GATTN + D_MODEL
COL_K = COL_Q + ATTN_DIM
COL_V = COL_K + ATTN_DIM
N_MAIN = COL_V + ATTN_DIM
DT_COLS = 128

LANES = 128
SUBLANES = 8
VMEM_LIMIT = 56 * 1024 * 1024

DMA_PRIORITIES = 2

BAND_RADIUS = 64
ATTN_TQ = 128
NEG_INF = float("-inf")


def _cparams(sem):
    return pltpu.CompilerParams(dimension_semantics=sem, vmem_limit_bytes=VMEM_LIMIT)


def _sigmoid(x):
    return 1.0 / (1.0 + jnp.exp(-x))


def _silu(x):
    return x * _sigmoid(x)


def _softplus(x):
    return jnp.maximum(x, 0.0) + jnp.log(1.0 + jnp.exp(-jnp.abs(x)))


def _pack_halves(x):
    c = x.shape[1] // 2
    hi = lax.bitcast_convert_type(x[:, :c].astype(BF16).astype(F32), jnp.uint32)
    lo = lax.bitcast_convert_type(x[:, c:].astype(BF16).astype(F32), jnp.uint32)
    return hi | (lo >> 16)


def _unpack_halves(u):
    left = lax.bitcast_convert_type(u & jnp.uint32(0xFFFF0000), F32)
    right = lax.bitcast_convert_type(u << 16, F32)
    return left, right


def _ada_kernel(c_ref, w_ref, b_ref, o_ref):
    c = _silu(c_ref[...]).astype(BF16)
    o_ref[...] = jnp.dot(c, w_ref[...].astype(BF16), preferred_element_type=F32) + b_ref[...]


def _ada(c_pad, w_ada, b_ada):
    rows = c_pad.shape[0]
    n = w_ada.shape[1]
    tn = 1536
    return pl.pallas_call(
        _ada_kernel,
        grid=(n // tn,),
        in_specs=[pl.BlockSpec((rows, D_MODEL), lambda j: (0, 0)),
                  pl.BlockSpec((D_MODEL, tn), lambda j: (0, j)),
                  pl.BlockSpec((1, tn), lambda j: (0, j))],
        out_specs=pl.BlockSpec((rows, tn), lambda j: (0, j)),
        out_shape=jax.ShapeDtypeStruct((rows, n), F32),
        compiler_params=_cparams(("arbitrary",)),
        name="ada",
    )(c_pad, w_ada, b_ada)


def _inproj_kernel(x_ref, sc_ref, sh_ref, w_ref, o_ref, h_ref):
    @pl.when(pl.program_id(1) == 0)
    def _():
        h_ref[...] = (x_ref[...] * (1.0 + sc_ref[0]) + sh_ref[0]).astype(BF16)

    o_ref[...] = jnp.dot(h_ref[...], w_ref[...], preferred_element_type=F32).astype(o_ref.dtype)


def _inproj(x, scale, shift, w, seg, tm, tn, out_dtype):
    t = x.shape[0]
    n = w.shape[1]
    per_seg = seg // tm
    return pl.pallas_call(
        _inproj_kernel,
        grid=(t // tm, n // tn),
        in_specs=[pl.BlockSpec((tm, D_MODEL), lambda i, j: (i, 0)),
                  pl.BlockSpec((1, 1, D_MODEL), lambda i, j: (i // per_seg, 0, 0)),
                  pl.BlockSpec((1, 1, D_MODEL), lambda i, j: (i // per_seg, 0, 0)),
                  pl.BlockSpec((D_MODEL, tn), lambda i, j: (0, j))],
        out_specs=pl.BlockSpec((tm, tn), lambda i, j: (i, j)),
        out_shape=jax.ShapeDtypeStruct((t, n), out_dtype),
        scratch_shapes=[pltpu.VMEM((tm, D_MODEL), BF16)],
        compiler_params=_cparams(("arbitrary", "arbitrary")),
        name="inproj",
    )(x, scale, shift, w)


def _conv_kernel(xp_ref, x_ref, xn_ref, w_ref, b_ref, o_ref, *, n_l):
    l = pl.program_id(1)
    tl = x_ref.shape[0]
    xp = jnp.where(l > 0, xp_ref[...], 0.0)
    xn = jnp.where(l < n_l - 1, xn_ref[...], 0.0)
    xx = jnp.concatenate([xp, x_ref[...], xn], axis=0)
    acc = jnp.zeros(x_ref.shape, F32) + b_ref[...]
    for k in range(CONV_WIDTH):
        lo = SUBLANES - CONV_PAD + k
        acc = acc + w_ref[k:k + 1, :] * xx[lo:lo + tl, :]
    o_ref[...] = _silu(acc)


def _conv(proj, conv_w, conv_b, row_off, batch, seq, tl=512, tc=512):
    t = proj.shape[0]
    n_l = seq // tl
    n_c = CONV_DIM // tc
    c0 = COL_XBC // tc
    r8 = tl // SUBLANES
    last8 = t // SUBLANES - 1

    def row_blk(b, l):
        return (row_off + b * seq) // tl + l

    return pl.pallas_call(
        functools.partial(_conv_kernel, n_l=n_l),
        grid=(batch, n_l, n_c),
        in_specs=[pl.BlockSpec((SUBLANES, tc), lambda b, l, c: (jnp.maximum(row_blk(b, l) * r8 - 1, 0), c0 + c)),
                  pl.BlockSpec((tl, tc), lambda b, l, c: (row_blk(b, l), c0 + c)),
                  pl.BlockSpec((SUBLANES, tc), lambda b, l, c: (jnp.minimum((row_blk(b, l) + 1) * r8, last8), c0 + c)),
                  pl.BlockSpec((CONV_WIDTH, tc), lambda b, l, c: (0, c)),
                  pl.BlockSpec((1, tc), lambda b, l, c: (0, c))],
        out_specs=pl.BlockSpec((tl, tc), lambda b, l, c: (b * n_l + l, c)),
        out_shape=jax.ShapeDtypeStruct((batch * seq, CONV_DIM), F32),
        compiler_params=_cparams(("arbitrary", "arbitrary", "arbitrary")),
        name="conv",
    )(proj, proj, proj, conv_w, conv_b)


def _exact_tri_matmul(tri_bf16, x):
    hi = x.astype(BF16)
    r1 = x - hi.astype(F32)
    mid = r1.astype(BF16)
    lo = (r1 - mid.astype(F32)).astype(BF16)
    out = jnp.dot(tri_bf16, hi, preferred_element_type=F32)
    out = out + jnp.dot(tri_bf16, mid, preferred_element_type=F32)
    return out + jnp.dot(tri_bf16, lo, preferred_element_type=F32)


def _ssd_kernel(*refs, reverse, accumulate):
    if accumulate:
        x_ref, b_ref, c_ref, dt_ref, dtb_ref, alog_ref, yin_ref, y_ref, state_ref = refs
    else:
        x_ref, b_ref, c_ref, dt_ref, dtb_ref, alog_ref, y_ref, state_ref = refs
        yin_ref = None
    lane0 = SSD_HEADS if reverse else 0

    @pl.when(pl.program_id(1) == 0)
    def _():
        state_ref[...] = jnp.zeros(state_ref.shape, F32)

    dt = _softplus(dt_ref[...] + dtb_ref[...])
    da = dt * (-jnp.exp(alog_ref[...]))
    row = lax.broadcasted_iota(I32, (SSD_CHUNK, SSD_CHUNK), 0)
    col = lax.broadcasted_iota(I32, (SSD_CHUNK, SSD_CHUNK), 1)
    tri = (col >= row) if reverse else (col <= row)
    cum = _exact_tri_matmul(tri.astype(BF16), da)
    cum_t = cum.T
    dt_t = dt.T
    edge = 0 if reverse else SSD_CHUNK - 1
    total_b = jnp.broadcast_to(cum_t[:, edge:edge + 1], (LANES, SSD_CHUNK))
    to_end_t = dt_t * jnp.exp(total_b - cum_t)
    chunk_decay_b = jnp.exp(total_b)
    src_t = cum_t - jnp.log(dt_t)
    first_head = lax.broadcasted_iota(I32, (1, 2 * SSD_HEAD_DIM), 1) < SSD_HEAD_DIM
    pairs = HEADS_PER_SSD_GROUP // 2

    for g in range(SSD_GROUPS):
        bg = b_ref[:, g * SSD_STATE:(g + 1) * SSD_STATE]
        cg = c_ref[:, g * SSD_STATE:(g + 1) * SSD_STATE]
        cb = lax.dot_general(cg.astype(BF16), bg.astype(BF16), (((1,), (1,)), ((), ())),
                             preferred_element_type=F32)
        bg_t = bg.T
        for pr in range(pairs):
            h0 = g * HEADS_PER_SSD_GROUP + 2 * pr
            cols = slice(h0 * SSD_HEAD_DIM, (h0 + 2) * SSD_HEAD_DIM)
            x_pair = x_ref[:, cols].astype(BF16)
            state = state_ref[g * pairs + pr]
            rhs = jnp.concatenate([x_pair, state.astype(BF16)], axis=0)
            res = []
            new = []
            for e in range(2):
                ln = lane0 + h0 + e
                cum_l = jnp.broadcast_to(cum[:, ln:ln + 1], (SSD_CHUNK, SSD_CHUNK))
                decay = jnp.exp(jnp.where(tri, cum_l - src_t[ln:ln + 1, :], NEG_INF))
                lhs = jnp.concatenate([(cb * decay).astype(BF16), (cg * jnp.exp(cum_l)).astype(BF16)], axis=1)
                res.append(jnp.dot(lhs, rhs, preferred_element_type=F32))
                b_scaled = (bg_t * to_end_t[ln:ln + 1, :]).astype(BF16)
                new.append(jnp.dot(b_scaled, x_pair, preferred_element_type=F32))
            y_pair = jnp.where(first_head, res[0], res[1])
            if accumulate:
                y_pair = y_pair + yin_ref[:, cols]
            y_ref[:, cols] = y_pair
            ln0 = lane0 + h0
            dec = jnp.where(first_head, chunk_decay_b[ln0:ln0 + 1, :], chunk_decay_b[ln0 + 1:ln0 + 2, :])
            state_ref[g * pairs + pr] = state * dec + jnp.where(first_head, new[0], new[1])


def _ssd(xbc, dt_raw, dt_bias, a_log, y_in, row_off, batch, seq, reverse):
    n_c = seq // SSD_CHUNK
    accumulate = y_in is not None

    def loc(b, c):
        return b * n_c + (n_c - 1 - c if reverse else c)

    def glob(b, c):
        return row_off // SSD_CHUNK + loc(b, c)

    in_specs = [pl.BlockSpec((SSD_CHUNK, D_INNER), lambda b, c: (loc(b, c), 0)),
                pl.BlockSpec((SSD_CHUNK, SSD_GROUPS * SSD_STATE), lambda b, c: (loc(b, c), 2)),
                pl.BlockSpec((SSD_CHUNK, SSD_GROUPS * SSD_STATE), lambda b, c: (loc(b, c), 3)),
                pl.BlockSpec((SSD_CHUNK, DT_COLS), lambda b, c: (glob(b, c), 0)),
                pl.BlockSpec((1, DT_COLS), lambda b, c: (0, 0)),
                pl.BlockSpec((1, DT_COLS), lambda b, c: (0, 0))]
    args = [xbc, xbc, xbc, dt_raw, dt_bias, a_log]
    aliases = {}
    if accumulate:
        in_specs.append(pl.BlockSpec((SSD_CHUNK, D_INNER), lambda b, c: (loc(b, c), 0)))
        args.append(y_in)
        aliases = {len(args) - 1: 0}
    return pl.pallas_call(
        functools.partial(_ssd_kernel, reverse=reverse, accumulate=accumulate),
        grid=(batch, n_c),
        in_specs=in_specs,
        out_specs=pl.BlockSpec((SSD_CHUNK, D_INNER), lambda b, c: (loc(b, c), 0)),
        out_shape=jax.ShapeDtypeStruct((batch * seq, D_INNER), F32),
        scratch_shapes=[pltpu.VMEM((SSD_HEADS // 2, SSD_STATE, 2 * SSD_HEAD_DIM), F32)],
        input_output_aliases=aliases,
        compiler_params=_cparams(("arbitrary", "arbitrary")),
        name="ssd_bwd" if reverse else "ssd_fwd",
    )(*args)


def _ssd_out_kernel(y_ref, x_ref, z_ref, dskip_ref, nw_ref, w_ref, o_ref):
    y = y_ref[...] + dskip_ref[...] * x_ref[...]
    y = y * _silu(z_ref[...])
    ms = jnp.mean(y * y, axis=-1, keepdims=True)
    yn = y * lax.rsqrt(ms + NORM_EPS) * nw_ref[...]
    o_ref[...] = jnp.dot(yn.astype(BF16), w_ref[...], preferred_element_type=F32)


def _ssd_out(y, xbc, proj, d_skip_cols, norm_w, w_branch, row_off, tm=256):
    t = y.shape[0]
    return pl.pallas_call(
        _ssd_out_kernel,
        grid=(t // tm,),
        in_specs=[pl.BlockSpec((tm, D_INNER), lambda i: (i, 0)),
                  pl.BlockSpec((tm, D_INNER), lambda i: (i, 0)),
                  pl.BlockSpec((tm, D_INNER), lambda i: (row_off // tm + i, COL_Z // D_INNER)),
                  pl.BlockSpec((1, D_INNER), lambda i: (0, 0)),
                  pl.BlockSpec((1, D_INNER), lambda i: (0, 0)),
                  pl.BlockSpec((D_INNER, D_MODEL), lambda i: (0, 0))],
        out_specs=pl.BlockSpec((tm, D_MODEL), lambda i: (i, 0)),
        out_shape=jax.ShapeDtypeStruct((t, D_MODEL), F32),
        compiler_params=_cparams(("arbitrary",)),
        name="ssd_out",
    )(y, xbc, proj, d_skip_cols, norm_w, w_branch)


def _prep_kernel(q_ref, k_ref, v_ref, cos_ref, sa_ref, sb_ref, qo_ref, ko_ref, vo_ref, *, dil):
    rows = q_ref.shape[0] // dil

    def phase(ref, p):
        if dil == 1:
            return ref[...]
        return ref[pl.ds(p, rows, stride=dil), :]

    for p in range(dil):
        cos = phase(cos_ref, p)
        sa = phase(sa_ref, p)
        sb = phase(sb_ref, p)

        def rot(t):
            return (t * cos + pltpu.roll(t, LANES - ROT_HALF, axis=1) * sa
                    + pltpu.roll(t, ROT_HALF, axis=1) * sb)

        qo_ref[0, p] = rot(phase(q_ref, p)) * (ATTN_HEAD_DIM ** -0.5)
        ko_ref[0, p] = rot(phase(k_ref, p))
        vo_ref[0, p] = phase(v_ref, p)


def _prep(proj, tables, gi, dil, row_off, batch, seq, rows_per_step=2048):
    r = rows_per_step
    n_i = seq // r
    n_c = ATTN_GROUP_COLS // LANES
    m = seq // dil
    cos_t, sa_t, sb_t = tables

    def in_blk(col):
        cb = (col + gi * ATTN_GROUP_COLS) // LANES
        return pl.BlockSpec((r, LANES), lambda b, i, c: ((row_off + b * seq) // r + i, cb + c))

    tab = pl.BlockSpec((r, LANES), lambda b, i, c: (i, 0))
    out = pl.BlockSpec((1, dil, r // dil, LANES), lambda b, i, c: (b, 0, i, c))
    shape = jax.ShapeDtypeStruct((batch, dil, m, ATTN_GROUP_COLS), F32)
    return pl.pallas_call(
        functools.partial(_prep_kernel, dil=dil),
        grid=(batch, n_i, n_c),
        in_specs=[in_blk(COL_Q), in_blk(COL_K), in_blk(COL_V), tab, tab, tab],
        out_specs=[out, out, out],
        out_shape=[shape, shape, shape],
        compiler_params=_cparams(("arbitrary", "arbitrary", "arbitrary")),
        name=f"attn_prep_d{dil}",
    )(proj, proj, proj, cos_t, sa_t, sb_t)


def _band_attn_kernel(q_ref, kp_ref, kc_ref, kn_ref, vp_ref, vc_ref, vn_ref, o_ref, lse_ref, *, m_len):
    i = pl.program_id(1)
    tq = q_ref.shape[1]
    tk = tq + 2 * BAND_RADIUS
    q = q_ref[0].astype(BF16)
    k = jnp.concatenate([kp_ref[0], kc_ref[0], kn_ref[0]], axis=0).astype(BF16)
    v = jnp.concatenate([vp_ref[0], vc_ref[0], vn_ref[0]], axis=0).astype(BF16)
    qpos = i * tq + lax.broadcasted_iota(I32, (tq, tk), 0)
    kpos = i * tq - BAND_RADIUS + lax.broadcasted_iota(I32, (tq, tk), 1)
    valid = (jnp.abs(qpos - kpos) <= BAND_RADIUS) & (kpos >= 0) & (kpos < m_len)
    for h in range(ATTN_HEADS_PER_GROUP):
        cols = slice(h * ATTN_HEAD_DIM, (h + 1) * ATTN_HEAD_DIM)
        s = lax.dot_general(q[:, cols], k[:, cols], (((1,), (1,)), ((), ())), preferred_element_type=F32)
        s = jnp.where(valid, s, NEG_INF)
        mx = jnp.max(s, axis=-1, keepdims=True)
        p = jnp.exp(s - mx)
        den = jnp.sum(p, axis=-1, keepdims=True)
        o = jnp.dot(p.astype(BF16), v[:, cols], preferred_element_type=F32)
        o_ref[0, :, cols] = o / den
        lse_ref[0, :, cols] = jnp.broadcast_to(mx + jnp.log(den), (tq, ATTN_HEAD_DIM))


def _band_attn(q, k, v):
    b, d, m, c = q.shape
    n = b * d
    q, k, v = (a.reshape(n, m, c) for a in (q, k, v))
    tq = ATTN_TQ
    per = tq // BAND_RADIUS
    last = m // BAND_RADIUS - 1
    main = pl.BlockSpec((1, tq, c), lambda s, i: (s, i, 0))
    prev = pl.BlockSpec((1, BAND_RADIUS, c), lambda s, i: (s, jnp.maximum(i * per - 1, 0), 0))
    nxt = pl.BlockSpec((1, BAND_RADIUS, c), lambda s, i: (s, jnp.minimum((i + 1) * per, last), 0))
    shape = jax.ShapeDtypeStruct((n, m, c), F32)
    o, lse = pl.pallas_call(
        functools.partial(_band_attn_kernel, m_len=m),
        grid=(n, m // tq),
        in_specs=[main, prev, main, nxt, prev, main, nxt],
        out_specs=[main, main],
        out_shape=[shape, shape],
        compiler_params=_cparams(("arbitrary", "arbitrary")),
        name=f"band_attn_m{m}",
    )(q, k, k, k, v, v, v)
    return o.reshape(b, d, m, c), lse.reshape(b, d, m, c)


def _layer_norm(x, g, b):
    mu = jnp.mean(x, axis=-1, keepdims=True)
    xc = x - mu
    var = jnp.mean(xc * xc, axis=-1, keepdims=True)
    return xc * lax.rsqrt(var + NORM_EPS) * g + b


def _merge_kernel(x_ref, yssd_ref, gs_ref, ga_ref, o1_ref, l1_ref, o2_ref, l2_ref, o3_ref, l3_ref,
                  g1_ref, sc2_ref, sh2_ref, wba_ref, wout_ref, lng_ref, lnb_ref, wr_ref,
                  x1_ref, h2_ref, logit_ref, scr_ref):
    tm = x_ref.shape[0]

    def interleaved(ref, dil, slot):
        if dil == 1:
            return ref[0, 0]
        rows = tm // dil
        n_c = ATTN_GROUP_COLS // LANES
        for p in range(dil):
            for c in range(n_c):
                scr_ref[slot * n_c + c, pl.ds(p, rows, stride=dil), :] = ref[0, p, :, c * LANES:(c + 1) * LANES]
        return jnp.concatenate([scr_ref[slot * n_c + c] for c in range(n_c)], axis=1)

    outs = []
    lses = []
    slot = 0
    for (o_ref, l_ref), (_, dil) in zip(((o1_ref, l1_ref), (o2_ref, l2_ref), (o3_ref, l3_ref)), ATTN_PATTERNS):
        outs.append(interleaved(o_ref, dil, slot))
        lses.append(interleaved(l_ref, dil, slot + 1))
        slot += 2
    mx = jnp.maximum(jnp.maximum(lses[0], lses[1]), lses[2])
    es = [jnp.exp(l - mx) for l in lses]
    den = es[0] + es[1] + es[2]
    y_att = (es[0] * outs[0] + es[1] * outs[1] + es[2] * outs[2]) / den
    y_attn = jnp.dot(y_att.astype(BF16), wba_ref[...], preferred_element_type=F32)
    merged = _sigmoid(gs_ref[...]) * yssd_ref[...] + _sigmoid(ga_ref[...]) * y_attn
    mix = jnp.dot(merged.astype(BF16), wout_ref[...], preferred_element_type=F32)
    x1 = _layer_norm(ALPHA * x_ref[...] + g1_ref[0] * mix, lng_ref[...], lnb_ref[...])
    x1_ref[...] = x1
    h2 = x1 * (1.0 + sc2_ref[0]) + sh2_ref[0]
    h2_ref[...] = _pack_halves(h2)
    logit_ref[...] = lax.dot_general(wr_ref[...], h2.astype(BF16), (((1,), (1,)), ((), ())),
                                     preferred_element_type=F32)


def _merge(x, y_ssd, proj, attn, gate1, scale2, shift2, w_ba, w_out, ln_g, ln_b, w_router_t,
           row_off, batch, seq, seg, tm=256):
    t = batch * seq
    n_i = seq // tm
    per_seg = seg // tm
    seg0 = row_off // seg

    def rows(col_blk):
        return pl.BlockSpec((tm, D_MODEL), lambda b, i: (row_off // tm + b * n_i + i, col_blk))

    def local(width):
        return pl.BlockSpec((tm, width), lambda b, i: (b * n_i + i, 0))

    def modv():
        return pl.BlockSpec((1, 1, D_MODEL), lambda b, i: (seg0 + (b * n_i + i) // per_seg, 0, 0))

    def full(shape):
        return pl.BlockSpec(shape, lambda b, i: tuple(0 for _ in shape))

    attn_specs = []
    attn_args = []
    for (o, lse), (_, dil) in zip(attn, ATTN_PATTERNS):
        spec = pl.BlockSpec((1, dil, tm // dil, ATTN_GROUP_COLS), lambda b, i: (b, 0, i, 0))
        attn_specs += [spec, spec]
        attn_args += [o, lse]
    return pl.pallas_call(
        _merge_kernel,
        grid=(batch, n_i),
        in_specs=[rows(0), local(D_MODEL), rows(COL_GSSD // D_MODEL), rows(COL_GATTN // D_MODEL)] + attn_specs
                 + [modv(), modv(), modv(), full((ATTN_GROUP_COLS, D_MODEL)), full((D_MODEL, D_MODEL)),
                    full((1, D_MODEL)), full((1, D_MODEL)), full((N_EXPERTS, D_MODEL))],
        out_specs=[local(D_MODEL), local(PACKED_COLS),
                   pl.BlockSpec((N_EXPERTS, tm), lambda b, i: (0, b * n_i + i))],
        out_shape=[jax.ShapeDtypeStruct((t, D_MODEL), F32), jax.ShapeDtypeStruct((t, PACKED_COLS), jnp.uint32),
                   jax.ShapeDtypeStruct((N_EXPERTS, t), F32)],
        scratch_shapes=[pltpu.VMEM((2 * ATTN_GROUPS * (ATTN_GROUP_COLS // LANES), tm, LANES), F32)],
        compiler_params=_cparams(("arbitrary", "arbitrary")),
        name="mixer_merge",
    )(x, y_ssd, proj, proj, *attn_args, gate1, scale2, shift2, w_ba, w_out, ln_g, ln_b, w_router_t)


def _route_kernel(logit_ref, bias_ref, idx_ref, w_ref, rank_ref, cnt_ref, run_ref):
    tm = logit_ref.shape[1]

    @pl.when(pl.program_id(0) == 0)
    def _():
        run_ref[...] = jnp.zeros(run_ref.shape, F32)

    scores = _sigmoid(logit_ref[...])
    sel = scores + bias_ref[...]
    iota_g = lax.broadcasted_iota(I32, (EXPERTS_PER_GROUP, tm), 0)
    grp = []
    for g in range(N_EXPERT_GROUPS):
        sg = sel[g * EXPERTS_PER_GROUP:(g + 1) * EXPERTS_PER_GROUP, :]
        top1 = jnp.max(sg, axis=0, keepdims=True)
        first = jnp.min(jnp.where(sg == top1, iota_g, EXPERTS_PER_GROUP), axis=0, keepdims=True)
        top2 = jnp.max(jnp.where(iota_g == first, NEG_INF, sg), axis=0, keepdims=True)
        grp.append(top1 + top2)
    gs = jnp.concatenate(grp, axis=0)
    iota_n = lax.broadcasted_iota(I32, (N_EXPERT_GROUPS, tm), 0)
    keep_g = jnp.zeros((N_EXPERT_GROUPS, tm), F32)
    for _ in range(TOPK_GROUPS):
        best = jnp.max(gs, axis=0, keepdims=True)
        first = jnp.min(jnp.where(gs == best, iota_n, N_EXPERT_GROUPS), axis=0, keepdims=True)
        hit = iota_n == first
        keep_g = jnp.where(hit, 1.0, keep_g)
        gs = jnp.where(hit, NEG_INF, gs)
    cand = jnp.concatenate(
        [jnp.where(keep_g[g:g + 1, :] > 0.0, sel[g * EXPERTS_PER_GROUP:(g + 1) * EXPERTS_PER_GROUP, :], NEG_INF)
         for g in range(N_EXPERT_GROUPS)], axis=0)
    iota_e = lax.broadcasted_iota(I32, (N_EXPERTS, tm), 0)
    chosen = jnp.zeros((N_EXPERTS, tm), F32)
    idxs = []
    ws = []
    for _ in range(TOP_K):
        best = jnp.max(cand, axis=0, keepdims=True)
        first = jnp.min(jnp.where(cand == best, iota_e, N_EXPERTS), axis=0, keepdims=True)
        hit = iota_e == first
        idxs.append(first)
        ws.append(jnp.sum(jnp.where(hit, scores, 0.0), axis=0, keepdims=True))
        chosen = jnp.where(hit, 1.0, chosen)
        cand = jnp.where(hit, NEG_INF, cand)
    top_w = jnp.concatenate(ws, axis=0)
    top_w = top_w / jnp.sum(top_w, axis=0, keepdims=True) * ROUTED_SCALE
    idx_ref[...] = jnp.concatenate(idxs, axis=0)
    w_ref[...] = top_w
    s_i = lax.broadcasted_iota(I32, (tm, tm), 0)
    t_i = lax.broadcasted_iota(I32, (tm, tm), 1)
    before = (s_i < t_i).astype(BF16)
    prior = jnp.dot(chosen.astype(BF16), before, preferred_element_type=F32) + run_ref[...]
    ranks = [jnp.sum(jnp.where(iota_e == idxs[k], prior, 0.0), axis=0, keepdims=True) for k in range(TOP_K)]
    rank_ref[...] = jnp.concatenate(ranks, axis=0).astype(I32)
    run_ref[...] = run_ref[...] + jnp.sum(chosen, axis=1, keepdims=True)
    cnt_ref[...] = run_ref[...].astype(I32)


def _route(logits_t, bias_col, tm=256):
    t = logits_t.shape[1]
    tok = pl.BlockSpec((TOP_K, tm), lambda i: (0, i))
    return pl.pallas_call(
        _route_kernel,
        grid=(t // tm,),
        in_specs=[pl.BlockSpec((N_EXPERTS, tm), lambda i: (0, i)),
                  pl.BlockSpec((N_EXPERTS, 1), lambda i: (0, 0))],
        out_specs=[tok, tok, tok, pl.BlockSpec((N_EXPERTS, 1), lambda i: (0, 0))],
        out_shape=[jax.ShapeDtypeStruct((TOP_K, t), I32), jax.ShapeDtypeStruct((TOP_K, t), F32),
                   jax.ShapeDtypeStruct((TOP_K, t), I32), jax.ShapeDtypeStruct((N_EXPERTS, 1), I32)],
        scratch_shapes=[pltpu.VMEM((N_EXPERTS, 1), F32)],
        compiler_params=_cparams(("arbitrary",)),
        name="route",
    )(logits_t, bias_col)


def _dest_kernel(idx_ref, rank_ref, start_ref, o_ref):
    tm = idx_ref.shape[1]
    iota_e = lax.broadcasted_iota(I32, (N_EXPERTS, tm), 0)
    start = start_ref[...].astype(F32)
    for k in range(TOP_K):
        base = jnp.sum(jnp.where(iota_e == idx_ref[k:k + 1, :], start, 0.0), axis=0, keepdims=True)
        o_ref[0, :, k * tm:(k + 1) * tm] = base.astype(I32) + rank_ref[k:k + 1, :]


def _dest(top_idx, rank, pad_start_col, tm):
    t = top_idx.shape[1]
    tok = pl.BlockSpec((TOP_K, tm), lambda i: (0, i))
    out = pl.pallas_call(
        _dest_kernel,
        grid=(t // tm,),
        in_specs=[tok, tok, pl.BlockSpec((N_EXPERTS, 1), lambda i: (0, 0))],
        out_specs=pl.BlockSpec((1, 1, TOP_K * tm), lambda i: (i, 0, 0)),
        out_shape=jax.ShapeDtypeStruct((t // tm, 1, TOP_K * tm), I32),
        compiler_params=_cparams(("arbitrary",)),
        name="moe_dest",
    )(top_idx, rank, pad_start_col)
    return out.reshape(t // tm, TOP_K * tm)


def _dispatch_kernel(pad_end_ref, dest_hbm, h_ref, xs_hbm, dest_smem, zeros_ref, sem_idx, sem_zero, sem_row,
                     *, tm):
    i = pl.program_id(0)
    idx_copy = pltpu.make_async_copy(dest_hbm.at[i], dest_smem, sem_idx)
    idx_copy.start()

    def zero_copy(e):
        start = pl.multiple_of(jnp.maximum(pad_end_ref[e] - EXPERT_BLOCK, 0), EXPERT_BLOCK)
        return pltpu.make_async_copy(zeros_ref, xs_hbm.at[pl.ds(start, EXPERT_BLOCK)], sem_zero)

    @pl.when(i == 0)
    def _():
        zeros_ref[...] = jnp.zeros(zeros_ref.shape, zeros_ref.dtype)

        def start_zero(e, carry):
            zero_copy(e).start()
            return carry

        def wait_zero(e, carry):
            zero_copy(e).wait()
            return carry

        lax.fori_loop(0, N_EXPERTS, start_zero, 0)
        lax.fori_loop(0, N_EXPERTS, wait_zero, 0)

    idx_copy.wait()

    def row_copy(t, k):
        return pltpu.make_async_copy(h_ref.at[pl.ds(t, 1)],
                                     xs_hbm.at[pl.ds(dest_smem[k * tm + t], 1)], sem_row)

    def start_rows(t, carry):
        for k in range(TOP_K):
            row_copy(t, k).start(priority=k % DMA_PRIORITIES)
        return carry

    def wait_rows(t, carry):
        for k in range(TOP_K):
            row_copy(t, k).wait()
        return carry

    lax.fori_loop(0, tm, start_rows, 0)
    lax.fori_loop(0, tm, wait_rows, 0)


def _dispatch(pad_end, dest_tiles, h2, n_rows, tm):
    n_tiles = dest_tiles.shape[0]
    return pl.pallas_call(
        functools.partial(_dispatch_kernel, tm=tm),
        grid_spec=pltpu.PrefetchScalarGridSpec(
            num_scalar_prefetch=1,
            grid=(n_tiles,),
            in_specs=[pl.BlockSpec(memory_space=pl.ANY),
                      pl.BlockSpec((tm, PACKED_COLS), lambda i, pe: (i, 0))],
            out_specs=pl.BlockSpec(memory_space=pl.ANY),
            scratch_shapes=[pltpu.SMEM((TOP_K * tm,), I32),
                            pltpu.VMEM((EXPERT_BLOCK, PACKED_COLS), jnp.uint32),
                            pltpu.SemaphoreType.DMA, pltpu.SemaphoreType.DMA, pltpu.SemaphoreType.DMA]),
        out_shape=jax.ShapeDtypeStruct((n_rows, PACKED_COLS), jnp.uint32),
        compiler_params=_cparams(("arbitrary",)),
        name="moe_dispatch",
    )(pad_end, dest_tiles, h2)


def _experts_kernel(blk_exp_ref, n_valid_ref, xs_ref, wg_ref, wu_ref, wd_ref, ys_ref, wgu_ref, wdn_ref):
    i = pl.program_id(0)

    @pl.when(i < n_valid_ref[0])
    def _():
        @pl.when((i == 0) | (blk_exp_ref[i] != blk_exp_ref[jnp.maximum(i - 1, 0)]))
        def _():
            wgu_ref[:, :EXPERT_HIDDEN] = wg_ref[0].astype(BF16)
            wgu_ref[:, EXPERT_HIDDEN:] = wu_ref[0].astype(BF16)
            wdn_ref[...] = wd_ref[0].astype(BF16)

        left, right = _unpack_halves(xs_ref[...])
        x = jnp.concatenate([left.astype(BF16), right.astype(BF16)], axis=1)
        gu = jnp.dot(x, wgu_ref[...], preferred_element_type=F32)
        hmid = (_silu(gu[:, :EXPERT_HIDDEN]) * gu[:, EXPERT_HIDDEN:]).astype(BF16)
        ys_ref[...] = _pack_halves(jnp.dot(hmid, wdn_ref[...], preferred_element_type=F32))


def _experts(blk_exp, n_valid, xs, w_gate, w_up, w_down):
    n_blocks = xs.shape[0] // EXPERT_BLOCK

    def row_map(i, be, nv):
        return (jnp.minimum(i, nv[0] - 1), 0)

    def w_map(i, be, nv):
        return (be[i], 0, 0)

    return pl.pallas_call(
        _experts_kernel,
        grid_spec=pltpu.PrefetchScalarGridSpec(
            num_scalar_prefetch=2,
            grid=(n_blocks,),
            in_specs=[pl.BlockSpec((EXPERT_BLOCK, PACKED_COLS), row_map),
                      pl.BlockSpec((1, D_MODEL, EXPERT_HIDDEN), w_map),
                      pl.BlockSpec((1, D_MODEL, EXPERT_HIDDEN), w_map),
                      pl.BlockSpec((1, EXPERT_HIDDEN, D_MODEL), w_map)],
            out_specs=pl.BlockSpec((EXPERT_BLOCK, PACKED_COLS), row_map),
            scratch_shapes=[pltpu.VMEM((D_MODEL, 2 * EXPERT_HIDDEN), BF16),
                            pltpu.VMEM((EXPERT_HIDDEN, D_MODEL), BF16)]),
        out_shape=jax.ShapeDtypeStruct(xs.shape, jnp.uint32),
        compiler_params=_cparams(("arbitrary",)),
        name="moe_experts",
    )(blk_exp, n_valid, xs, w_gate, w_up, w_down)


def _combine_kernel(dest_hbm, ys_hbm, w_ref, x1_ref, h2_ref, g2_ref, wsg_ref, wsu_ref, wsd_ref,
                    lng_ref, lnb_ref, o_ref, dest_smem, rows_ref, sem_idx, sem_row, *, tm):
    i = pl.program_id(0)
    idx_copy = pltpu.make_async_copy(dest_hbm.at[i], dest_smem, sem_idx)
    idx_copy.start()
    idx_copy.wait()

    def row_copy(t, k):
        return pltpu.make_async_copy(ys_hbm.at[pl.ds(dest_smem[k * tm + t], 1)],
                                     rows_ref.at[k, pl.ds(t, 1)], sem_row)

    def start_rows(t, carry):
        for k in range(TOP_K):
            row_copy(t, k).start(priority=k % DMA_PRIORITIES)
        return carry

    def wait_rows(t, carry):
        for k in range(TOP_K):
            row_copy(t, k).wait()
        return carry

    lax.fori_loop(0, tm, start_rows, 0)
    left, right = _unpack_halves(h2_ref[...])
    h2 = jnp.concatenate([left.astype(BF16), right.astype(BF16)], axis=1)
    g = jnp.dot(h2, wsg_ref[...], preferred_element_type=F32)
    u = jnp.dot(h2, wsu_ref[...], preferred_element_type=F32)
    ffn = jnp.dot((_silu(g) * u).astype(BF16), wsd_ref[...], preferred_element_type=F32)
    lax.fori_loop(0, tm, wait_rows, 0)
    w = w_ref[...]
    acc_l = ffn[:, :PACKED_COLS]
    acc_r = ffn[:, PACKED_COLS:]
    for k in range(TOP_K):
        left, right = _unpack_halves(rows_ref[k])
        acc_l = acc_l + w[:, k:k + 1] * left
        acc_r = acc_r + w[:, k:k + 1] * right
    ffn = jnp.concatenate([acc_l, acc_r], axis=1)
    o_ref[...] = _layer_norm(ALPHA * x1_ref[...] + g2_ref[0] * ffn, lng_ref[...], lnb_ref[...])


def _combine(dest_tiles, ys, top_w_rows, x1, h2, gate2, w_sg, w_su, w_sd, ln_g, ln_b, seg, tm):
    t = x1.shape[0]
    per_seg = seg // tm

    def full(shape):
        return pl.BlockSpec(shape, lambda i: tuple(0 for _ in shape))

    tok = pl.BlockSpec((tm, D_MODEL), lambda i: (i, 0))
    return pl.pallas_call(
        functools.partial(_combine_kernel, tm=tm),
        grid=(t // tm,),
        in_specs=[pl.BlockSpec(memory_space=pl.ANY), pl.BlockSpec(memory_space=pl.ANY),
                  pl.BlockSpec((tm, TOP_K), lambda i: (i, 0)), tok,
                  pl.BlockSpec((tm, PACKED_COLS), lambda i: (i, 0)),
                  pl.BlockSpec((1, 1, D_MODEL), lambda i: (i // per_seg, 0, 0)),
                  full((D_MODEL, SHARED_HIDDEN)), full((D_MODEL, SHARED_HIDDEN)), full((SHARED_HIDDEN, D_MODEL)),
                  full((1, D_MODEL)), full((1, D_MODEL))],
        out_specs=tok,
        out_shape=jax.ShapeDtypeStruct((t, D_MODEL), F32),
        scratch_shapes=[pltpu.SMEM((TOP_K * tm,), I32), pltpu.VMEM((TOP_K, tm, PACKED_COLS), jnp.uint32),
                        pltpu.SemaphoreType.DMA, pltpu.SemaphoreType.DMA],
        compiler_params=_cparams(("arbitrary",)),
        name="moe_combine",
    )(dest_tiles, ys, top_w_rows, x1, h2, gate2, w_sg, w_su, w_sd, ln_g, ln_b)


def _rope_tables(seq):
    inv_freq = ROPE_THETA ** (-jnp.arange(ROT_HALF, dtype=F32) * 2.0 / ROT_DIM)
    ang = jnp.arange(seq, dtype=F32)[:, None] * inv_freq[None, :]
    cos, sin = jnp.cos(ang), jnp.sin(ang)
    rest = ATTN_HEAD_DIM - ROT_DIM
    ones = jnp.ones((seq, rest), F32)
    zeros = jnp.zeros((seq, rest), F32)
    zh = jnp.zeros((seq, ROT_HALF), F32)
    reps = LANES // ATTN_HEAD_DIM
    cos_t = jnp.tile(jnp.concatenate([cos, cos, ones], axis=1), (1, reps))
    sa_t = jnp.tile(jnp.concatenate([-sin, zh, zeros], axis=1), (1, reps))
    sb_t = jnp.tile(jnp.concatenate([zh, sin, zeros], axis=1), (1, reps))
    return cos_t, sa_t, sb_t


def _split_in_weights(w_in):
    cuts = (D_INNER, CONV_DIM, SSD_HEADS, SSD_HEADS, ATTN_DIM, ATTN_DIM, ATTN_DIM, D_MODEL, D_MODEL)
    offs = [0]
    for c in cuts:
        offs.append(offs[-1] + c)
    z, xbc, dtf, dtb, q, k, v, gs, ga = (w_in[:, offs[i]:offs[i + 1]] for i in range(len(cuts)))
    w_main = jnp.concatenate([z, xbc, gs, ga, q, k, v], axis=1).astype(BF16)
    assert COL_GSSD % D_MODEL == 0 and COL_Q % ATTN_GROUP_COLS == 0 and w_main.shape[1] == N_MAIN
    pad = jnp.zeros((D_MODEL, DT_COLS - 2 * SSD_HEADS), F32)
    w_dt = jnp.concatenate([dtf, dtb, pad], axis=1).astype(BF16)
    return w_main, w_dt


def kernel(x_prompt, x_sample, c_prompt, c_sample, w_ada, b_ada, w_in, conv_w, conv_b, dt_bias_fwd, dt_bias_bwd, a_log_fwd, a_log_bwd, d_skip, ssd_norm_w, w_branch_ssd, w_branch_attn, w_out, ln1_g, ln1_b, w_router, router_bias, w_exp_gate, w_exp_up, w_exp_down, w_sh_gate, w_sh_up, w_sh_down, ln2_g, ln2_b):
    assert w_ada.shape[0] == DEPTH
    groups = ((x_prompt, c_prompt), (x_sample, c_sample))
    shapes = [(x.shape[0], x.shape[1]) for x, _ in groups]
    seg = math.gcd(*(s for _, s in shapes))
    tokens = [b * s for b, s in shapes]
    t_all = sum(tokens)
    row_offs = [0, tokens[0]]

    c_all = jnp.concatenate([c_prompt, c_sample], axis=0)
    n_req = c_all.shape[0]
    c_pad = jnp.zeros((-(-n_req // SUBLANES) * SUBLANES, D_MODEL), F32).at[:n_req].set(c_all)
    ada = _ada(c_pad, w_ada[0], b_ada)[:n_req]
    reps = jnp.array([s // seg for b, s in shapes for _ in range(b)], dtype=I32)
    n_seg = t_all // seg
    ada_seg = jnp.repeat(ada, reps, axis=0, total_repeat_length=n_seg)
    shift1, scale1, gate1, shift2, scale2, gate2 = (
        a.reshape(n_seg, 1, D_MODEL) for a in jnp.split(ada_seg, 6, axis=-1))

    x_all = jnp.concatenate([x_prompt.reshape(-1, D_MODEL), x_sample.reshape(-1, D_MODEL)], axis=0)
    w_main, w_dt = _split_in_weights(w_in[0])
    proj = _inproj(x_all, scale1, shift1, w_main, seg, tm=1024, tn=1280, out_dtype=F32)
    dt_raw = _inproj(x_all, scale1, shift1, w_dt, seg, tm=1024, tn=DT_COLS, out_dtype=F32)

    dt_pad = jnp.zeros((DT_COLS - 2 * SSD_HEADS,), F32)
    dt_bias = jnp.concatenate([dt_bias_fwd[0], dt_bias_bwd[0], dt_pad]).reshape(1, DT_COLS)
    a_log = jnp.concatenate([a_log_fwd[0], a_log_bwd[0], dt_pad]).reshape(1, DT_COLS)
    d_skip_cols = jnp.repeat(d_skip[0], SSD_HEAD_DIM).reshape(1, D_INNER)
    w_bs = w_branch_ssd[0].astype(BF16)
    w_ba = w_branch_attn[0].astype(BF16)
    w_o = w_out[0].astype(BF16)
    w_router_t = w_router[0].T.astype(BF16)

    x1_parts, h2_parts, logit_parts = [], [], []
    for (batch, seq), row_off in zip(shapes, row_offs):
        xbc = _conv(proj, conv_w[0], conv_b, row_off, batch, seq)
        y = _ssd(xbc, dt_raw, dt_bias, a_log, None, row_off, batch, seq, reverse=False)
        y = _ssd(xbc, dt_raw, dt_bias, a_log, y, row_off, batch, seq, reverse=True)
        y_ssd = _ssd_out(y, xbc, proj, d_skip_cols, ssd_norm_w, w_bs, row_off)
        tables = _rope_tables(seq)
        attn = []
        for gi, (_, dil) in enumerate(ATTN_PATTERNS):
            q, k, v = _prep(proj, tables, gi, dil, row_off, batch, seq)
            attn.append(_band_attn(q, k, v))
        x1, h2, logits_t = _merge(x_all, y_ssd, proj, attn, gate1, scale2, shift2, w_ba, w_o,
                                  ln1_g, ln1_b, w_router_t, row_off, batch, seq, seg)
        x1_parts.append(x1)
        h2_parts.append(h2)
        logit_parts.append(logits_t)
    x1 = jnp.concatenate(x1_parts, axis=0)
    h2 = jnp.concatenate(h2_parts, axis=0)
    logits_t = jnp.concatenate(logit_parts, axis=1)

    top_idx, top_w, rank, counts = _route(logits_t, router_bias[0].reshape(N_EXPERTS, 1))
    counts = counts[:, 0]
    padded = (counts + EXPERT_BLOCK - 1) // EXPERT_BLOCK * EXPERT_BLOCK
    pad_end = jnp.cumsum(padded)
    pad_start = pad_end - padded
    n_blocks = (t_all * TOP_K + N_EXPERTS * (EXPERT_BLOCK - 1)) // EXPERT_BLOCK
    n_rows = n_blocks * EXPERT_BLOCK
    tm_moe = 256
    dest_tiles = _dest(top_idx, rank, pad_start.astype(I32).reshape(N_EXPERTS, 1), tm_moe)
    n_valid = (pad_end[-1] // EXPERT_BLOCK).astype(I32).reshape(1)
    blk_start = jnp.arange(n_blocks, dtype=I32) * EXPERT_BLOCK
    blk_exp = jnp.searchsorted(pad_end, jnp.minimum(blk_start, pad_end[-1] - 1), side="right").astype(I32)
    blk_exp = jnp.minimum(blk_exp, N_EXPERTS - 1)

    xs = _dispatch(pad_end.astype(I32), dest_tiles, h2, n_rows, tm_moe)
    ys = _experts(blk_exp, n_valid, xs, w_exp_gate[0], w_exp_up[0], w_exp_down[0])
    out = _combine(dest_tiles, ys, top_w.T, x1, h2, gate2, w_sh_gate[0].astype(BF16), w_sh_up[0].astype(BF16),
                   w_sh_down[0].astype(BF16), ln2_g, ln2_b, seg, tm_moe)
    y_prompt = out[:tokens[0]].reshape(x_prompt.shape)
    y_sample = out[tokens[0]:].reshape(x_sample.shape)
    return (y_prompt, y_sample)
```

```python
import functools
import math

import jax
import jax.numpy as jnp
from jax import lax
from jax.experimental import pallas as pl
from jax.experimental.pallas import tpu as pltpu

F32 = jnp.float32
BF16 = jnp.bfloat16
I32 = jnp.int32

D_MODEL = 1024
D_INNER = 2048
SSD_HEADS = 32
SSD_HEAD_DIM = 64
SSD_GROUPS = 8
SSD_STATE = 128
SSD_CHUNK = 128
HEADS_PER_SSD_GROUP = SSD_HEADS // SSD_GROUPS
CONV_WIDTH = 5
CONV_PAD = CONV_WIDTH // 2
CONV_DIM = D_INNER + 2 * SSD_GROUPS * SSD_STATE
ATTN_PATTERNS = ((128, 1), (512, 4), (2048, 16))
ATTN_GROUPS = len(ATTN_PATTERNS)
ATTN_HEADS_PER_GROUP = 8
ATTN_HEAD_DIM = 64
ATTN_GROUP_COLS = ATTN_HEADS_PER_GROUP * ATTN_HEAD_DIM
ATTN_DIM = ATTN_GROUPS * ATTN_GROUP_COLS
ROT_DIM = ATTN_HEAD_DIM // 4
ROT_HALF = ROT_DIM // 2
ROPE_THETA = 500000.0
N_EXPERTS = 256
EXPERT_HIDDEN = 256
TOP_K = 8
N_EXPERT_GROUPS = 8
EXPERTS_PER_GROUP = N_EXPERTS // N_EXPERT_GROUPS
TOPK_GROUPS = 4
ROUTED_SCALE = 2.5
SHARED_HIDDEN = 256
EXPERT_BLOCK = 256
PACKED_COLS = D_MODEL // 2
ROW_SUBLANES = PACKED_COLS // 128
DEPTH = 1
ALPHA = (2.0 * DEPTH) ** 0.25
NORM_EPS = 1e-5

COL_Z = 0
COL_XBC = COL_Z + D_INNER
COL_GSSD = COL_XBC + CONV_DIM
COL_GATTN = COL_GSSD + D_MODEL
COL_Q = COL_GATTN + D_MODEL
COL_K = COL_Q + ATTN_DIM
COL_V = COL_K + ATTN_DIM
N_MAIN = COL_V + ATTN_DIM
DT_COLS = 128

LANES = 128
SUBLANES = 8
VMEM_LIMIT = 56 * 1024 * 1024

DMA_PRIORITIES = 2

BAND_RADIUS = 64
ATTN_TQ = 128
NEG_INF = float("-inf")


def _cparams(sem):
    return pltpu.CompilerParams(dimension_semantics=sem, vmem_limit_bytes=VMEM_LIMIT)


def _sigmoid(x):
    return 1.0 / (1.0 + jnp.exp(-x))


def _silu(x):
    return x * _sigmoid(x)


def _softplus(x):
    return jnp.maximum(x, 0.0) + jnp.log(1.0 + jnp.exp(-jnp.abs(x)))


def _pack_halves(x):
    c = x.shape[1] // 2
    hi = lax.bitcast_convert_type(x[:, :c].astype(BF16).astype(F32), jnp.uint32)
    lo = lax.bitcast_convert_type(x[:, c:].astype(BF16).astype(F32), jnp.uint32)
    return hi | (lo >> 16)


def _unpack_halves(u):
    left = lax.bitcast_convert_type(u & jnp.uint32(0xFFFF0000), F32)
    right = lax.bitcast_convert_type(u << 16, F32)
    return left, right


def _store_rows(ref, lead, x):
    rows = x.shape[0]
    packed = _pack_halves(x)
    for j in range(ROW_SUBLANES):
        idx = lead + (pl.ds(j, rows, stride=ROW_SUBLANES), slice(None))
        ref[idx] = packed[:, j * LANES:(j + 1) * LANES]


def _load_row_pieces(ref, lead, rows):
    lefts, rights = [], []
    for j in range(ROW_SUBLANES):
        idx = lead + (pl.ds(j, rows, stride=ROW_SUBLANES), slice(None))
        left, right = _unpack_halves(ref[idx])
        lefts.append(left)
        rights.append(right)
    return lefts + rights


def _ada_kernel(c_ref, w_ref, b_ref, o_ref):
    c = _silu(c_ref[...]).astype(BF16)
    o_ref[...] = jnp.dot(c, w_ref[...].astype(BF16), preferred_element_type=F32) + b_ref[...]


def _ada(c_pad, w_ada, b_ada):
    rows = c_pad.shape[0]
    n = w_ada.shape[1]
    tn = 1536
    return pl.pallas_call(
        _ada_kernel,
        grid=(n // tn,),
        in_specs=[pl.BlockSpec((rows, D_MODEL), lambda j: (0, 0)),
                  pl.BlockSpec((D_MODEL, tn), lambda j: (0, j)),
                  pl.BlockSpec((1, tn), lambda j: (0, j))],
        out_specs=pl.BlockSpec((rows, tn), lambda j: (0, j)),
        out_shape=jax.ShapeDtypeStruct((rows, n), F32),
        compiler_params=_cparams(("arbitrary",)),
        name="ada",
    )(c_pad, w_ada, b_ada)


def _inproj_kernel(x_ref, sc_ref, sh_ref, w_ref, o_ref, h_ref):
    @pl.when(pl.program_id(1) == 0)
    def _():
        h_ref[...] = (x_ref[...] * (1.0 + sc_ref[0]) + sh_ref[0]).astype(BF16)

    o_ref[...] = jnp.dot(h_ref[...], w_ref[...], preferred_element_type=F32).astype(o_ref.dtype)


def _inproj(x, scale, shift, w, seg, tm, tn, out_dtype):
    t = x.shape[0]
    n = w.shape[1]
    per_seg = seg // tm
    return pl.pallas_call(
        _inproj_kernel,
        grid=(t // tm, n // tn),
        in_specs=[pl.BlockSpec((tm, D_MODEL), lambda i, j: (i, 0)),
                  pl.BlockSpec((1, 1, D_MODEL), lambda i, j: (i // per_seg, 0, 0)),
                  pl.BlockSpec((1, 1, D_MODEL), lambda i, j: (i // per_seg, 0, 0)),
                  pl.BlockSpec((D_MODEL, tn), lambda i, j: (0, j))],
        out_specs=pl.BlockSpec((tm, tn), lambda i, j: (i, j)),
        out_shape=jax.ShapeDtypeStruct((t, n), out_dtype),
        scratch_shapes=[pltpu.VMEM((tm, D_MODEL), BF16)],
        compiler_params=_cparams(("arbitrary", "arbitrary")),
        name="inproj",
    )(x, scale, shift, w)


def _conv_kernel(xp_ref, x_ref, xn_ref, w_ref, b_ref, o_ref, *, n_l):
    l = pl.program_id(1)
    tl = x_ref.shape[0]
    xp = jnp.where(l > 0, xp_ref[...], 0.0)
    xn = jnp.where(l < n_l - 1, xn_ref[...], 0.0)
    xx = jnp.concatenate([xp, x_ref[...], xn], axis=0)
    acc = jnp.zeros(x_ref.shape, F32) + b_ref[...]
    for k in range(CONV_WIDTH):
        lo = SUBLANES - CONV_PAD + k
        acc = acc + w_ref[k:k + 1, :] * xx[lo:lo + tl, :]
    o_ref[...] = _silu(acc)


def _conv(proj, conv_w, conv_b, row_off, batch, seq, tl=512, tc=512):
    t = proj.shape[0]
    n_l = seq // tl
    n_c = CONV_DIM // tc
    c0 = COL_XBC // tc
    r8 = tl // SUBLANES
    last8 = t // SUBLANES - 1

    def row_blk(b, l):
        return (row_off + b * seq) // tl + l

    return pl.pallas_call(
        functools.partial(_conv_kernel, n_l=n_l),
        grid=(batch, n_l, n_c),
        in_specs=[pl.BlockSpec((SUBLANES, tc), lambda b, l, c: (jnp.maximum(row_blk(b, l) * r8 - 1, 0), c0 + c)),
                  pl.BlockSpec((tl, tc), lambda b, l, c: (row_blk(b, l), c0 + c)),
                  pl.BlockSpec((SUBLANES, tc), lambda b, l, c: (jnp.minimum((row_blk(b, l) + 1) * r8, last8), c0 + c)),
                  pl.BlockSpec((CONV_WIDTH, tc), lambda b, l, c: (0, c)),
                  pl.BlockSpec((1, tc), lambda b, l, c: (0, c))],
        out_specs=pl.BlockSpec((tl, tc), lambda b, l, c: (b * n_l + l, c)),
        out_shape=jax.ShapeDtypeStruct((batch * seq, CONV_DIM), F32),
        compiler_params=_cparams(("arbitrary", "arbitrary", "arbitrary")),
        name="conv",
    )(proj, proj, proj, conv_w, conv_b)


def _exact_tri_matmul(tri_bf16, x):
    hi = x.astype(BF16)
    r1 = x - hi.astype(F32)
    mid = r1.astype(BF16)
    lo = (r1 - mid.astype(F32)).astype(BF16)
    out = jnp.dot(tri_bf16, hi, preferred_element_type=F32)
    out = out + jnp.dot(tri_bf16, mid, preferred_element_type=F32)
    return out + jnp.dot(tri_bf16, lo, preferred_element_type=F32)


def _ssd_kernel(*refs, reverse, accumulate):
    if accumulate:
        x_ref, b_ref, c_ref, dt_ref, dtb_ref, alog_ref, yin_ref, y_ref, state_ref = refs
    else:
        x_ref, b_ref, c_ref, dt_ref, dtb_ref, alog_ref, y_ref, state_ref = refs
        yin_ref = None
    lane0 = SSD_HEADS if reverse else 0

    @pl.when(pl.program_id(1) == 0)
    def _():
        state_ref[...] = jnp.zeros(state_ref.shape, F32)

    dt = _softplus(dt_ref[...] + dtb_ref[...])
    da = dt * (-jnp.exp(alog_ref[...]))
    row = lax.broadcasted_iota(I32, (SSD_CHUNK, SSD_CHUNK), 0)
    col = lax.broadcasted_iota(I32, (SSD_CHUNK, SSD_CHUNK), 1)
    tri = (col >= row) if reverse else (col <= row)
    cum = _exact_tri_matmul(tri.astype(BF16), da)
    cum_t = cum.T
    dt_t = dt.T
    edge = 0 if reverse else SSD_CHUNK - 1
    total_b = jnp.broadcast_to(cum_t[:, edge:edge + 1], (LANES, SSD_CHUNK))
    to_end_t = dt_t * jnp.exp(total_b - cum_t)
    chunk_decay_b = jnp.exp(total_b)
    src_t = cum_t - jnp.log(dt_t)
    first_head = lax.broadcasted_iota(I32, (1, 2 * SSD_HEAD_DIM), 1) < SSD_HEAD_DIM
    pairs = HEADS_PER_SSD_GROUP // 2

    for g in range(SSD_GROUPS):
        bg = b_ref[:, g * SSD_STATE:(g + 1) * SSD_STATE]
        cg = c_ref[:, g * SSD_STATE:(g + 1) * SSD_STATE]
        cb = lax.dot_general(cg.astype(BF16), bg.astype(BF16), (((1,), (1,)), ((), ())),
                             preferred_element_type=F32)
        bg_t = bg.T
        for pr in range(pairs):
            h0 = g * HEADS_PER_SSD_GROUP + 2 * pr
            cols = slice(h0 * SSD_HEAD_DIM, (h0 + 2) * SSD_HEAD_DIM)
            x_pair = x_ref[:, cols].astype(BF16)
            state = state_ref[g * pairs + pr]
            rhs = jnp.concatenate([x_pair, state.astype(BF16)], axis=0)
            res = []
            new = []
            for e in range(2):
                ln = lane0 + h0 + e
                cum_l = jnp.broadcast_to(cum[:, ln:ln + 1], (SSD_CHUNK, SSD_CHUNK))
                decay = jnp.exp(jnp.where(tri, cum_l - src_t[ln:ln + 1, :], NEG_INF))
                lhs = jnp.concatenate([(cb * decay).astype(BF16), (cg * jnp.exp(cum_l)).astype(BF16)], axis=1)
                res.append(jnp.dot(lhs, rhs, preferred_element_type=F32))
                b_scaled = (bg_t * to_end_t[ln:ln + 1, :]).astype(BF16)
                new.append(jnp.dot(b_scaled, x_pair, preferred_element_type=F32))
            y_pair = jnp.where(first_head, res[0], res[1])
            if accumulate:
                y_pair = y_pair + yin_ref[:, cols]
            y_ref[:, cols] = y_pair
            ln0 = lane0 + h0
            dec = jnp.where(first_head, chunk_decay_b[ln0:ln0 + 1, :], chunk_decay_b[ln0 + 1:ln0 + 2, :])
            state_ref[g * pairs + pr] = state * dec + jnp.where(first_head, new[0], new[1])


def _ssd(xbc, dt_raw, dt_bias, a_log, y_in, row_off, batch, seq, reverse):
    n_c = seq // SSD_CHUNK
    accumulate = y_in is not None

    def loc(b, c):
        return b * n_c + (n_c - 1 - c if reverse else c)

    def glob(b, c):
        return row_off // SSD_CHUNK + loc(b, c)

    in_specs = [pl.BlockSpec((SSD_CHUNK, D_INNER), lambda b, c: (loc(b, c), 0)),
                pl.BlockSpec((SSD_CHUNK, SSD_GROUPS * SSD_STATE), lambda b, c: (loc(b, c), 2)),
                pl.BlockSpec((SSD_CHUNK, SSD_GROUPS * SSD_STATE), lambda b, c: (loc(b, c), 3)),
                pl.BlockSpec((SSD_CHUNK, DT_COLS), lambda b, c: (glob(b, c), 0)),
                pl.BlockSpec((1, DT_COLS), lambda b, c: (0, 0)),
                pl.BlockSpec((1, DT_COLS), lambda b, c: (0, 0))]
    args = [xbc, xbc, xbc, dt_raw, dt_bias, a_log]
    aliases = {}
    if accumulate:
        in_specs.append(pl.BlockSpec((SSD_CHUNK, D_INNER), lambda b, c: (loc(b, c), 0)))
        args.append(y_in)
        aliases = {len(args) - 1: 0}
    return pl.pallas_call(
        functools.partial(_ssd_kernel, reverse=reverse, accumulate=accumulate),
        grid=(batch, n_c),
        in_specs=in_specs,
        out_specs=pl.BlockSpec((SSD_CHUNK, D_INNER), lambda b, c: (loc(b, c), 0)),
        out_shape=jax.ShapeDtypeStruct((batch * seq, D_INNER), F32),
        scratch_shapes=[pltpu.VMEM((SSD_HEADS // 2, SSD_STATE, 2 * SSD_HEAD_DIM), F32)],
        input_output_aliases=aliases,
        compiler_params=_cparams(("arbitrary", "arbitrary")),
        name="ssd_bwd" if reverse else "ssd_fwd",
    )(*args)


def _ssd_out_kernel(y_ref, x_ref, z_ref, dskip_ref, nw_ref, w_ref, o_ref):
    y = y_ref[...] + dskip_ref[...] * x_ref[...]
    y = y * _silu(z_ref[...])
    ms = jnp.mean(y * y, axis=-1, keepdims=True)
    yn = y * lax.rsqrt(ms + NORM_EPS) * nw_ref[...]
    o_ref[...] = jnp.dot(yn.astype(BF16), w_ref[...], preferred_element_type=F32)


def _ssd_out(y, xbc, proj, d_skip_cols, norm_w, w_branch, row_off, tm=256):
    t = y.shape[0]
    return pl.pallas_call(
        _ssd_out_kernel,
        grid=(t // tm,),
        in_specs=[pl.BlockSpec((tm, D_INNER), lambda i: (i, 0)),
                  pl.BlockSpec((tm, D_INNER), lambda i: (i, 0)),
                  pl.BlockSpec((tm, D_INNER), lambda i: (row_off // tm + i, COL_Z // D_INNER)),
                  pl.BlockSpec((1, D_INNER), lambda i: (0, 0)),
                  pl.BlockSpec((1, D_INNER), lambda i: (0, 0)),
                  pl.BlockSpec((D_INNER, D_MODEL), lambda i: (0, 0))],
        out_specs=pl.BlockSpec((tm, D_MODEL), lambda i: (i, 0)),
        out_shape=jax.ShapeDtypeStruct((t, D_MODEL), F32),
        compiler_params=_cparams(("arbitrary",)),
        name="ssd_out",
    )(y, xbc, proj, d_skip_cols, norm_w, w_branch)


def _prep_kernel(q_ref, k_ref, v_ref, cos_ref, sa_ref, sb_ref, qo_ref, ko_ref, vo_ref, *, dil):
    rows = q_ref.shape[0] // dil

    def phase(ref, p):
        if dil == 1:
            return ref[...]
        return ref[pl.ds(p, rows, stride=dil), :]

    for p in range(dil):
        cos = phase(cos_ref, p)
        sa = phase(sa_ref, p)
        sb = phase(sb_ref, p)

        def rot(t):
            return (t * cos + pltpu.roll(t, LANES - ROT_HALF, axis=1) * sa
                    + pltpu.roll(t, ROT_HALF, axis=1) * sb)

        qo_ref[0, p] = rot(phase(q_ref, p)) * (ATTN_HEAD_DIM ** -0.5)
        ko_ref[0, p] = rot(phase(k_ref, p))
        vo_ref[0, p] = phase(v_ref, p)


def _prep(proj, tables, gi, dil, row_off, batch, seq, rows_per_step=2048):
    r = rows_per_step
    n_i = seq // r
    n_c = ATTN_GROUP_COLS // LANES
    m = seq // dil
    cos_t, sa_t, sb_t = tables

    def in_blk(col):
        cb = (col + gi * ATTN_GROUP_COLS) // LANES
        return pl.BlockSpec((r, LANES), lambda b, i, c: ((row_off + b * seq) // r + i, cb + c))

    tab = pl.BlockSpec((r, LANES), lambda b, i, c: (i, 0))
    out = pl.BlockSpec((1, dil, r // dil, LANES), lambda b, i, c: (b, 0, i, c))
    shape = jax.ShapeDtypeStruct((batch, dil, m, ATTN_GROUP_COLS), F32)
    return pl.pallas_call(
        functools.partial(_prep_kernel, dil=dil),
        grid=(batch, n_i, n_c),
        in_specs=[in_blk(COL_Q), in_blk(COL_K), in_blk(COL_V), tab, tab, tab],
        out_specs=[out, out, out],
        out_shape=[shape, shape, shape],
        compiler_params=_cparams(("arbitrary", "arbitrary", "arbitrary")),
        name=f"attn_prep_d{dil}",
    )(proj, proj, proj, cos_t, sa_t, sb_t)


def _band_attn_kernel(q_ref, kp_ref, kc_ref, kn_ref, vp_ref, vc_ref, vn_ref, o_ref, lse_ref, *, m_len):
    i = pl.program_id(1)
    tq = q_ref.shape[1]
    tk = tq + 2 * BAND_RADIUS
    q = q_ref[0].astype(BF16)
    k = jnp.concatenate([kp_ref[0], kc_ref[0], kn_ref[0]], axis=0).astype(BF16)
    v = jnp.concatenate([vp_ref[0], vc_ref[0], vn_ref[0]], axis=0).astype(BF16)
    qpos = i * tq + lax.broadcasted_iota(I32, (tq, tk), 0)
    kpos = i * tq - BAND_RADIUS + lax.broadcasted_iota(I32, (tq, tk), 1)
    valid = (jnp.abs(qpos - kpos) <= BAND_RADIUS) & (kpos >= 0) & (kpos < m_len)
    for h in range(ATTN_HEADS_PER_GROUP):
        cols = slice(h * ATTN_HEAD_DIM, (h + 1) * ATTN_HEAD_DIM)
        s = lax.dot_general(q[:, cols], k[:, cols], (((1,), (1,)), ((), ())), preferred_element_type=F32)
        s = jnp.where(valid, s, NEG_INF)
        mx = jnp.max(s, axis=-1, keepdims=True)
        p = jnp.exp(s - mx)
        den = jnp.sum(p, axis=-1, keepdims=True)
        o = jnp.dot(p.astype(BF16), v[:, cols], preferred_element_type=F32)
        o_ref[0, :, cols] = o / den
        lse_ref[0, :, cols] = jnp.broadcast_to(mx + jnp.log(den), (tq, ATTN_HEAD_DIM))


def _band_attn(q, k, v):
    b, d, m, c = q.shape
    n = b * d
    q, k, v = (a.reshape(n, m, c) for a in (q, k, v))
    tq = ATTN_TQ
    per = tq // BAND_RADIUS
    last = m // BAND_RADIUS - 1
    main = pl.BlockSpec((1, tq, c), lambda s, i: (s, i, 0))
    prev = pl.BlockSpec((1, BAND_RADIUS, c), lambda s, i: (s, jnp.maximum(i * per - 1, 0), 0))
    nxt = pl.BlockSpec((1, BAND_RADIUS, c), lambda s, i: (s, jnp.minimum((i + 1) * per, last), 0))
    shape = jax.ShapeDtypeStruct((n, m, c), F32)
    o, lse = pl.pallas_call(
        functools.partial(_band_attn_kernel, m_len=m),
        grid=(n, m // tq),
        in_specs=[main, prev, main, nxt, prev, main, nxt],
        out_specs=[main, main],
        out_shape=[shape, shape],
        compiler_params=_cparams(("arbitrary", "arbitrary")),
        name=f"band_attn_m{m}",
    )(q, k, k, k, v, v, v)
    return o.reshape(b, d, m, c), lse.reshape(b, d, m, c)


def _layer_norm(x, g, b):
    mu = jnp.mean(x, axis=-1, keepdims=True)
    xc = x - mu
    var = jnp.mean(xc * xc, axis=-1, keepdims=True)
    return xc * lax.rsqrt(var + NORM_EPS) * g + b


def _merge_kernel(x_ref, yssd_ref, gs_ref, ga_ref, o1_ref, l1_ref, o2_ref, l2_ref, o3_ref, l3_ref,
                  g1_ref, sc2_ref, sh2_ref, wba_ref, wout_ref, lng_ref, lnb_ref, wr_ref,
                  x1_ref, h2_ref, logit_ref, scr_ref):
    tm = x_ref.shape[0]

    def interleaved(ref, dil, slot):
        if dil == 1:
            return ref[0, 0]
        rows = tm // dil
        n_c = ATTN_GROUP_COLS // LANES
        for p in range(dil):
            for c in range(n_c):
                scr_ref[slot * n_c + c, pl.ds(p, rows, stride=dil), :] = ref[0, p, :, c * LANES:(c + 1) * LANES]
        return jnp.concatenate([scr_ref[slot * n_c + c] for c in range(n_c)], axis=1)

    outs = []
    lses = []
    slot = 0
    for (o_ref, l_ref), (_, dil) in zip(((o1_ref, l1_ref), (o2_ref, l2_ref), (o3_ref, l3_ref)), ATTN_PATTERNS):
        outs.append(interleaved(o_ref, dil, slot))
        lses.append(interleaved(l_ref, dil, slot + 1))
        slot += 2
    mx = jnp.maximum(jnp.maximum(lses[0], lses[1]), lses[2])
    es = [jnp.exp(l - mx) for l in lses]
    den = es[0] + es[1] + es[2]
    y_att = (es[0] * outs[0] + es[1] * outs[1] + es[2] * outs[2]) / den
    y_attn = jnp.dot(y_att.astype(BF16), wba_ref[...], preferred_element_type=F32)
    merged = _sigmoid(gs_ref[...]) * yssd_ref[...] + _sigmoid(ga_ref[...]) * y_attn
    mix = jnp.dot(merged.astype(BF16), wout_ref[...], preferred_element_type=F32)
    x1 = _layer_norm(ALPHA * x_ref[...] + g1_ref[0] * mix, lng_ref[...], lnb_ref[...])
    x1_ref[...] = x1
    h2 = x1 * (1.0 + sc2_ref[0]) + sh2_ref[0]
    _store_rows(h2_ref, (), h2)
    logit_ref[...] = lax.dot_general(wr_ref[...], h2.astype(BF16), (((1,), (1,)), ((), ())),
                                     preferred_element_type=F32)


def _merge(x, y_ssd, proj, attn, gate1, scale2, shift2, w_ba, w_out, ln_g, ln_b, w_router_t,
           row_off, batch, seq, seg, tm=256):
    t = batch * seq
    n_i = seq // tm
    per_seg = seg // tm
    seg0 = row_off // seg

    def rows(col_blk):
        return pl.BlockSpec((tm, D_MODEL), lambda b, i: (row_off // tm + b * n_i + i, col_blk))

    def local(width):
        return pl.BlockSpec((tm, width), lambda b, i: (b * n_i + i, 0))

    def modv():
        return pl.BlockSpec((1, 1, D_MODEL), lambda b, i: (seg0 + (b * n_i + i) // per_seg, 0, 0))

    def full(shape):
        return pl.BlockSpec(shape, lambda b, i: tuple(0 for _ in shape))

    attn_specs = []
    attn_args = []
    for (o, lse), (_, dil) in zip(attn, ATTN_PATTERNS):
        spec = pl.BlockSpec((1, dil, tm // dil, ATTN_GROUP_COLS), lambda b, i: (b, 0, i, 0))
        attn_specs += [spec, spec]
        attn_args += [o, lse]
    return pl.pallas_call(
        _merge_kernel,
        grid=(batch, n_i),
        in_specs=[rows(0), local(D_MODEL), rows(COL_GSSD // D_MODEL), rows(COL_GATTN // D_MODEL)] + attn_specs
                 + [modv(), modv(), modv(), full((ATTN_GROUP_COLS, D_MODEL)), full((D_MODEL, D_MODEL)),
                    full((1, D_MODEL)), full((1, D_MODEL)), full((N_EXPERTS, D_MODEL))],
        out_specs=[local(D_MODEL), pl.BlockSpec((tm * ROW_SUBLANES, LANES), lambda b, i: (b * n_i + i, 0)),
                   pl.BlockSpec((N_EXPERTS, tm), lambda b, i: (0, b * n_i + i))],
        out_shape=[jax.ShapeDtypeStruct((t, D_MODEL), F32),
                   jax.ShapeDtypeStruct((t * ROW_SUBLANES, LANES), jnp.uint32),
                   jax.ShapeDtypeStruct((N_EXPERTS, t), F32)],
        scratch_shapes=[pltpu.VMEM((2 * ATTN_GROUPS * (ATTN_GROUP_COLS // LANES), tm, LANES), F32)],
        compiler_params=_cparams(("arbitrary", "arbitrary")),
        name="mixer_merge",
    )(x, y_ssd, proj, proj, *attn_args, gate1, scale2, shift2, w_ba, w_out, ln_g, ln_b, w_router_t)


def _route_kernel(logit_ref, bias_ref, idx_ref, w_ref, rank_ref, cnt_ref, run_ref):
    tm = logit_ref.shape[1]

    @pl.when(pl.program_id(0) == 0)
    def _():
        run_ref[...] = jnp.zeros(run_ref.shape, F32)

    scores = _sigmoid(logit_ref[...])
    sel = scores + bias_ref[...]
    iota_g = lax.broadcasted_iota(I32, (EXPERTS_PER_GROUP, tm), 0)
    grp = []
    for g in range(N_EXPERT_GROUPS):
        sg = sel[g * EXPERTS_PER_GROUP:(g + 1) * EXPERTS_PER_GROUP, :]
        top1 = jnp.max(sg, axis=0, keepdims=True)
        first = jnp.min(jnp.where(sg == top1, iota_g, EXPERTS_PER_GROUP), axis=0, keepdims=True)
        top2 = jnp.max(jnp.where(iota_g == first, NEG_INF, sg), axis=0, keepdims=True)
        grp.append(top1 + top2)
    gs = jnp.concatenate(grp, axis=0)
    iota_n = lax.broadcasted_iota(I32, (N_EXPERT_GROUPS, tm), 0)
    keep_g = jnp.zeros((N_EXPERT_GROUPS, tm), F32)
    for _ in range(TOPK_GROUPS):
        best = jnp.max(gs, axis=0, keepdims=True)
        first = jnp.min(jnp.where(gs == best, iota_n, N_EXPERT_GROUPS), axis=0, keepdims=True)
        hit = iota_n == first
        keep_g = jnp.where(hit, 1.0, keep_g)
        gs = jnp.where(hit, NEG_INF, gs)
    cand = jnp.concatenate(
        [jnp.where(keep_g[g:g + 1, :] > 0.0, sel[g * EXPERTS_PER_GROUP:(g + 1) * EXPERTS_PER_GROUP, :], NEG_INF)
         for g in range(N_EXPERT_GROUPS)], axis=0)
    iota_e = lax.broadcasted_iota(I32, (N_EXPERTS, tm), 0)
    chosen = jnp.zeros((N_EXPERTS, tm), F32)
    idxs = []
    ws = []
    for _ in range(TOP_K):
        best = jnp.max(cand, axis=0, keepdims=True)
        first = jnp.min(jnp.where(cand == best, iota_e, N_EXPERTS), axis=0, keepdims=True)
        hit = iota_e == first
        idxs.append(first)
        ws.append(jnp.sum(jnp.where(hit, scores, 0.0), axis=0, keepdims=True))
        chosen = jnp.where(hit, 1.0, chosen)
        cand = jnp.where(hit, NEG_INF, cand)
    top_w = jnp.concatenate(ws, axis=0)
    top_w = top_w / jnp.sum(top_w, axis=0, keepdims=True) * ROUTED_SCALE
    idx_ref[...] = jnp.concatenate(idxs, axis=0)
    w_ref[...] = top_w
    s_i = lax.broadcasted_iota(I32, (tm, tm), 0)
    t_i = lax.broadcasted_iota(I32, (tm, tm), 1)
    before = (s_i < t_i).astype(BF16)
    prior = jnp.dot(chosen.astype(BF16), before, preferred_element_type=F32) + run_ref[...]
    ranks = [jnp.sum(jnp.where(iota_e == idxs[k], prior, 0.0), axis=0, keepdims=True) for k in range(TOP_K)]
    rank_ref[...] = jnp.concatenate(ranks, axis=0).astype(I32)
    run_ref[...] = run_ref[...] + jnp.sum(chosen, axis=1, keepdims=True)
    cnt_ref[...] = run_ref[...].astype(I32)


def _route(logits_t, bias_col, tm=256):
    t = logits_t.shape[1]
    tok = pl.BlockSpec((TOP_K, tm), lambda i: (0, i))
    return pl.pallas_call(
        _route_kernel,
        grid=(t // tm,),
        in_specs=[pl.BlockSpec((N_EXPERTS, tm), lambda i: (0, i)),
                  pl.BlockSpec((N_EXPERTS, 1), lambda i: (0, 0))],
        out_specs=[tok, tok, tok, pl.BlockSpec((N_EXPERTS, 1), lambda i: (0, 0))],
        out_shape=[jax.ShapeDtypeStruct((TOP_K, t), I32), jax.ShapeDtypeStruct((TOP_K, t), F32),
                   jax.ShapeDtypeStruct((TOP_K, t), I32), jax.ShapeDtypeStruct((N_EXPERTS, 1), I32)],
        scratch_shapes=[pltpu.VMEM((N_EXPERTS, 1), F32)],
        compiler_params=_cparams(("arbitrary",)),
        name="route",
    )(logits_t, bias_col)


def _dest_kernel(idx_ref, rank_ref, start_ref, o_ref):
    tm = idx_ref.shape[1]
    iota_e = lax.broadcasted_iota(I32, (N_EXPERTS, tm), 0)
    start = start_ref[...].astype(F32)
    for k in range(TOP_K):
        base = jnp.sum(jnp.where(iota_e == idx_ref[k:k + 1, :], start, 0.0), axis=0, keepdims=True)
        o_ref[0, :, k * tm:(k + 1) * tm] = (base.astype(I32) + rank_ref[k:k + 1, :]) * ROW_SUBLANES


def _dest(top_idx, rank, pad_start_col, tm):
    t = top_idx.shape[1]
    tok = pl.BlockSpec((TOP_K, tm), lambda i: (0, i))
    out = pl.pallas_call(
        _dest_kernel,
        grid=(t // tm,),
        in_specs=[tok, tok, pl.BlockSpec((N_EXPERTS, 1), lambda i: (0, 0))],
        out_specs=pl.BlockSpec((1, 1, TOP_K * tm), lambda i: (i, 0, 0)),
        out_shape=jax.ShapeDtypeStruct((t // tm, 1, TOP_K * tm), I32),
        compiler_params=_cparams(("arbitrary",)),
        name="moe_dest",
    )(top_idx, rank, pad_start_col)
    return out.reshape(t // tm, TOP_K * tm)


def _dispatch_kernel(pad_end_ref, dest_hbm, h_ref, xs_hbm, dest_smem, zeros_ref, sem_idx, sem_zero, sem_row,
                     *, tm):
    i = pl.program_id(0)
    idx_copy = pltpu.make_async_copy(dest_hbm.at[i], dest_smem, sem_idx)
    idx_copy.start()

    blk_rows = EXPERT_BLOCK * ROW_SUBLANES

    def zero_copy(e):
        start = pl.multiple_of(jnp.maximum(pad_end_ref[e] - EXPERT_BLOCK, 0) * ROW_SUBLANES, blk_rows)
        return pltpu.make_async_copy(zeros_ref, xs_hbm.at[pl.ds(start, blk_rows)], sem_zero)

    @pl.when(i == 0)
    def _():
        zeros_ref[...] = jnp.zeros(zeros_ref.shape, zeros_ref.dtype)

        def start_zero(e, carry):
            zero_copy(e).start()
            return carry

        def wait_zero(e, carry):
            zero_copy(e).wait()
            return carry

        lax.fori_loop(0, N_EXPERTS, start_zero, 0)
        lax.fori_loop(0, N_EXPERTS, wait_zero, 0)

    idx_copy.wait()

    def row_copy(t, k):
        dst = pl.multiple_of(dest_smem[k * tm + t], ROW_SUBLANES)
        return pltpu.make_async_copy(h_ref.at[pl.ds(pl.multiple_of(t * ROW_SUBLANES, ROW_SUBLANES), ROW_SUBLANES)],
                                     xs_hbm.at[pl.ds(dst, ROW_SUBLANES)], sem_row)

    def start_rows(t, carry):
        for k in range(TOP_K):
            row_copy(t, k).start(priority=k % DMA_PRIORITIES)
        return carry

    def wait_rows(t, carry):
        for k in range(TOP_K):
            row_copy(t, k).wait()
        return carry

    lax.fori_loop(0, tm, start_rows, 0)
    lax.fori_loop(0, tm, wait_rows, 0)


def _dispatch(pad_end, dest_tiles, h2, n_rows, tm):
    n_tiles = dest_tiles.shape[0]
    return pl.pallas_call(
        functools.partial(_dispatch_kernel, tm=tm),
        grid_spec=pltpu.PrefetchScalarGridSpec(
            num_scalar_prefetch=1,
            grid=(n_tiles,),
            in_specs=[pl.BlockSpec(memory_space=pl.ANY),
                      pl.BlockSpec((tm * ROW_SUBLANES, LANES), lambda i, pe: (i, 0))],
            out_specs=pl.BlockSpec(memory_space=pl.ANY),
            scratch_shapes=[pltpu.SMEM((TOP_K * tm,), I32),
                            pltpu.VMEM((EXPERT_BLOCK * ROW_SUBLANES, LANES), jnp.uint32),
                            pltpu.SemaphoreType.DMA, pltpu.SemaphoreType.DMA, pltpu.SemaphoreType.DMA]),
        out_shape=jax.ShapeDtypeStruct((n_rows * ROW_SUBLANES, LANES), jnp.uint32),
        compiler_params=_cparams(("arbitrary",)),
        name="moe_dispatch",
    )(pad_end, dest_tiles, h2)


def _experts_kernel(blk_exp_ref, n_valid_ref, xs_ref, wg_ref, wu_ref, wd_ref, ys_ref, wgu_ref, wdn_ref):
    i = pl.program_id(0)

    @pl.when(i < n_valid_ref[0])
    def _():
        @pl.when((i == 0) | (blk_exp_ref[i] != blk_exp_ref[jnp.maximum(i - 1, 0)]))
        def _():
            wgu_ref[:, :EXPERT_HIDDEN] = wg_ref[0].astype(BF16)
            wgu_ref[:, EXPERT_HIDDEN:] = wu_ref[0].astype(BF16)
            wdn_ref[...] = wd_ref[0].astype(BF16)

        x = jnp.concatenate([c.astype(BF16) for c in _load_row_pieces(xs_ref, (), EXPERT_BLOCK)], axis=1)
        gu = jnp.dot(x, wgu_ref[...], preferred_element_type=F32)
        hmid = (_silu(gu[:, :EXPERT_HIDDEN]) * gu[:, EXPERT_HIDDEN:]).astype(BF16)
        _store_rows(ys_ref, (), jnp.dot(hmid, wdn_ref[...], preferred_element_type=F32))


def _experts(blk_exp, n_valid, xs, w_gate, w_up, w_down):
    n_blocks = xs.shape[0] // (EXPERT_BLOCK * ROW_SUBLANES)

    def row_map(i, be, nv):
        return (jnp.minimum(i, nv[0] - 1), 0)

    def w_map(i, be, nv):
        return (be[i], 0, 0)

    return pl.pallas_call(
        _experts_kernel,
        grid_spec=pltpu.PrefetchScalarGridSpec(
            num_scalar_prefetch=2,
            grid=(n_blocks,),
            in_specs=[pl.BlockSpec((EXPERT_BLOCK * ROW_SUBLANES, LANES), row_map),
                      pl.BlockSpec((1, D_MODEL, EXPERT_HIDDEN), w_map),
                      pl.BlockSpec((1, D_MODEL, EXPERT_HIDDEN), w_map),
                      pl.BlockSpec((1, EXPERT_HIDDEN, D_MODEL), w_map)],
            out_specs=pl.BlockSpec((EXPERT_BLOCK * ROW_SUBLANES, LANES), row_map),
            scratch_shapes=[pltpu.VMEM((D_MODEL, 2 * EXPERT_HIDDEN), BF16),
                            pltpu.VMEM((EXPERT_HIDDEN, D_MODEL), BF16)]),
        out_shape=jax.ShapeDtypeStruct(xs.shape, jnp.uint32),
        compiler_params=_cparams(("arbitrary",)),
        name="moe_experts",
    )(blk_exp, n_valid, xs, w_gate, w_up, w_down)


def _combine_kernel(dest_hbm, ys_hbm, w_ref, x1_ref, h2_ref, g2_ref, wsg_ref, wsu_ref, wsd_ref,
                    lng_ref, lnb_ref, o_ref, dest_smem, rows_ref, sem_idx, sem_row, *, tm):
    i = pl.program_id(0)
    idx_copy = pltpu.make_async_copy(dest_hbm.at[i], dest_smem, sem_idx)
    idx_copy.start()
    idx_copy.wait()

    def row_copy(t, k):
        src = pl.multiple_of(dest_smem[k * tm + t], ROW_SUBLANES)
        dst = pl.multiple_of(t * ROW_SUBLANES, ROW_SUBLANES)
        return pltpu.make_async_copy(ys_hbm.at[pl.ds(src, ROW_SUBLANES)],
                                     rows_ref.at[k, pl.ds(dst, ROW_SUBLANES)], sem_row)

    def start_rows(t, carry):
        for k in range(TOP_K):
            row_copy(t, k).start(priority=k % DMA_PRIORITIES)
        return carry

    def wait_rows(t, carry):
        for k in range(TOP_K):
            row_copy(t, k).wait()
        return carry

    lax.fori_loop(0, tm, start_rows, 0)
    h2 = jnp.concatenate([c.astype(BF16) for c in _load_row_pieces(h2_ref, (), tm)], axis=1)
    g = jnp.dot(h2, wsg_ref[...], preferred_element_type=F32)
    u = jnp.dot(h2, wsu_ref[...], preferred_element_type=F32)
    ffn = jnp.dot((_silu(g) * u).astype(BF16), wsd_ref[...], preferred_element_type=F32)
    lax.fori_loop(0, tm, wait_rows, 0)
    w = w_ref[...]
    n_pieces = D_MODEL // LANES
    acc = [ffn[:, j * LANES:(j + 1) * LANES] for j in range(n_pieces)]
    for k in range(TOP_K):
        wk = jnp.broadcast_to(w[:, k:k + 1], (tm, LANES))
        pieces = _load_row_pieces(rows_ref, (k,), tm)
        acc = [a + wk * p for a, p in zip(acc, pieces)]
    ffn = jnp.concatenate(acc, axis=1)
    o_ref[...] = _layer_norm(ALPHA * x1_ref[...] + g2_ref[0] * ffn, lng_ref[...], lnb_ref[...])


def _combine(dest_tiles, ys, top_w_rows, x1, h2, gate2, w_sg, w_su, w_sd, ln_g, ln_b, seg, tm):
    t = x1.shape[0]
    per_seg = seg // tm

    def full(shape):
        return pl.BlockSpec(shape, lambda i: tuple(0 for _ in shape))

    tok = pl.BlockSpec((tm, D_MODEL), lambda i: (i, 0))
    return pl.pallas_call(
        functools.partial(_combine_kernel, tm=tm),
        grid=(t // tm,),
        in_specs=[pl.BlockSpec(memory_space=pl.ANY), pl.BlockSpec(memory_space=pl.ANY),
                  pl.BlockSpec((tm, TOP_K), lambda i: (i, 0)), tok,
                  pl.BlockSpec((tm * ROW_SUBLANES, LANES), lambda i: (i, 0)),
                  pl.BlockSpec((1, 1, D_MODEL), lambda i: (i // per_seg, 0, 0)),
                  full((D_MODEL, SHARED_HIDDEN)), full((D_MODEL, SHARED_HIDDEN)), full((SHARED_HIDDEN, D_MODEL)),
                  full((1, D_MODEL)), full((1, D_MODEL))],
        out_specs=tok,
        out_shape=jax.ShapeDtypeStruct((t, D_MODEL), F32),
        scratch_shapes=[pltpu.SMEM((TOP_K * tm,), I32), pltpu.VMEM((TOP_K, tm * ROW_SUBLANES, LANES), jnp.uint32),
                        pltpu.SemaphoreType.DMA, pltpu.SemaphoreType.DMA],
        compiler_params=_cparams(("arbitrary",)),
        name="moe_combine",
    )(dest_tiles, ys, top_w_rows, x1, h2, gate2, w_sg, w_su, w_sd, ln_g, ln_b)


def _rope_tables(seq):
    inv_freq = ROPE_THETA ** (-jnp.arange(ROT_HALF, dtype=F32) * 2.0 / ROT_DIM)
    ang = jnp.arange(seq, dtype=F32)[:, None] * inv_freq[None, :]
    cos, sin = jnp.cos(ang), jnp.sin(ang)
    rest = ATTN_HEAD_DIM - ROT_DIM
    ones = jnp.ones((seq, rest), F32)
    zeros = jnp.zeros((seq, rest), F32)
    zh = jnp.zeros((seq, ROT_HALF), F32)
    reps = LANES // ATTN_HEAD_DIM
    cos_t = jnp.tile(jnp.concatenate([cos, cos, ones], axis=1), (1, reps))
    sa_t = jnp.tile(jnp.concatenate([-sin, zh, zeros], axis=1), (1, reps))
    sb_t = jnp.tile(jnp.concatenate([zh, sin, zeros], axis=1), (1, reps))
    return cos_t, sa_t, sb_t


def _split_in_weights(w_in):
    cuts = (D_INNER, CONV_DIM, SSD_HEADS, SSD_HEADS, ATTN_DIM, ATTN_DIM, ATTN_DIM, D_MODEL, D_MODEL)
    offs = [0]
    for c in cuts:
        offs.append(offs[-1] + c)
    z, xbc, dtf, dtb, q, k, v, gs, ga = (w_in[:, offs[i]:offs[i + 1]] for i in range(len(cuts)))
    w_main = jnp.concatenate([z, xbc, gs, ga, q, k, v], axis=1).astype(BF16)
    assert COL_GSSD % D_MODEL == 0 and COL_Q % ATTN_GROUP_COLS == 0 and w_main.shape[1] == N_MAIN
    pad = jnp.zeros((D_MODEL, DT_COLS - 2 * SSD_HEADS), F32)
    w_dt = jnp.concatenate([dtf, dtb, pad], axis=1).astype(BF16)
    return w_main, w_dt


def kernel(x_prompt, x_sample, c_prompt, c_sample, w_ada, b_ada, w_in, conv_w, conv_b, dt_bias_fwd, dt_bias_bwd, a_log_fwd, a_log_bwd, d_skip, ssd_norm_w, w_branch_ssd, w_branch_attn, w_out, ln1_g, ln1_b, w_router, router_bias, w_exp_gate, w_exp_up, w_exp_down, w_sh_gate, w_sh_up, w_sh_down, ln2_g, ln2_b):
    assert w_ada.shape[0] == DEPTH
    groups = ((x_prompt, c_prompt), (x_sample, c_sample))
    shapes = [(x.shape[0], x.shape[1]) for x, _ in groups]
    seg = math.gcd(*(s for _, s in shapes))
    tokens = [b * s for b, s in shapes]
    t_all = sum(tokens)
    row_offs = [0, tokens[0]]

    c_all = jnp.concatenate([c_prompt, c_sample], axis=0)
    n_req = c_all.shape[0]
    c_pad = jnp.zeros((-(-n_req // SUBLANES) * SUBLANES, D_MODEL), F32).at[:n_req].set(c_all)
    ada = _ada(c_pad, w_ada[0], b_ada)[:n_req]
    reps = jnp.array([s // seg for b, s in shapes for _ in range(b)], dtype=I32)
    n_seg = t_all // seg
    ada_seg = jnp.repeat(ada, reps, axis=0, total_repeat_length=n_seg)
    shift1, scale1, gate1, shift2, scale2, gate2 = (
        a.reshape(n_seg, 1, D_MODEL) for a in jnp.split(ada_seg, 6, axis=-1))

    x_all = jnp.concatenate([x_prompt.reshape(-1, D_MODEL), x_sample.reshape(-1, D_MODEL)], axis=0)
    w_main, w_dt = _split_in_weights(w_in[0])
    proj = _inproj(x_all, scale1, shift1, w_main, seg, tm=1024, tn=1280, out_dtype=F32)
    dt_raw = _inproj(x_all, scale1, shift1, w_dt, seg, tm=1024, tn=DT_COLS, out_dtype=F32)

    dt_pad = jnp.zeros((DT_COLS - 2 * SSD_HEADS,), F32)
    dt_bias = jnp.concatenate([dt_bias_fwd[0], dt_bias_bwd[0], dt_pad]).reshape(1, DT_COLS)
    a_log = jnp.concatenate([a_log_fwd[0], a_log_bwd[0], dt_pad]).reshape(1, DT_COLS)
    d_skip_cols = jnp.repeat(d_skip[0], SSD_HEAD_DIM).reshape(1, D_INNER)
    w_bs = w_branch_ssd[0].astype(BF16)
    w_ba = w_branch_attn[0].astype(BF16)
    w_o = w_out[0].astype(BF16)
    w_router_t = w_router[0].T.astype(BF16)

    x1_parts, h2_parts, logit_parts = [], [], []
    for (batch, seq), row_off in zip(shapes, row_offs):
        xbc = _conv(proj, conv_w[0], conv_b, row_off, batch, seq)
        y = _ssd(xbc, dt_raw, dt_bias, a_log, None, row_off, batch, seq, reverse=False)
        y = _ssd(xbc, dt_raw, dt_bias, a_log, y, row_off, batch, seq, reverse=True)
        y_ssd = _ssd_out(y, xbc, proj, d_skip_cols, ssd_norm_w, w_bs, row_off)
        tables = _rope_tables(seq)
        attn = []
        for gi, (_, dil) in enumerate(ATTN_PATTERNS):
            q, k, v = _prep(proj, tables, gi, dil, row_off, batch, seq)
            attn.append(_band_attn(q, k, v))
        x1, h2, logits_t = _merge(x_all, y_ssd, proj, attn, gate1, scale2, shift2, w_ba, w_o,
                                  ln1_g, ln1_b, w_router_t, row_off, batch, seq, seg)
        x1_parts.append(x1)
        h2_parts.append(h2)
        logit_parts.append(logits_t)
    x1 = jnp.concatenate(x1_parts, axis=0)
    h2 = jnp.concatenate(h2_parts, axis=0)
    logits_t = jnp.concatenate(logit_parts, axis=1)

    top_idx, top_w, rank, counts = _route(logits_t, router_bias[0].reshape(N_EXPERTS, 1))
    counts = counts[:, 0]
    padded = (counts + EXPERT_BLOCK - 1) // EXPERT_BLOCK * EXPERT_BLOCK
    pad_end = jnp.cumsum(padded)
    pad_start = pad_end - padded
    n_blocks = (t_all * TOP_K + N_EXPERTS * (EXPERT_BLOCK - 1)) // EXPERT_BLOCK
    n_rows = n_blocks * EXPERT_BLOCK
    tm_moe = 256
    dest_tiles = _dest(top_idx, rank, pad_start.astype(I32).reshape(N_EXPERTS, 1), tm_moe)
    n_valid = (pad_end[-1] // EXPERT_BLOCK).astype(I32).reshape(1)
    blk_start = jnp.arange(n_blocks, dtype=I32) * EXPERT_BLOCK
    blk_exp = jnp.searchsorted(pad_end, jnp.minimum(blk_start, pad_end[-1] - 1), side="right").astype(I32)
    blk_exp = jnp.minimum(blk_exp, N_EXPERTS - 1)

    xs = _dispatch(pad_end.astype(I32), dest_tiles, h2, n_rows, tm_moe)
    ys = _experts(blk_exp, n_valid, xs, w_exp_gate[0], w_exp_up[0], w_exp_down[0])
    out = _combine(dest_tiles, ys, top_w.T, x1, h2, gate2, w_sh_gate[0].astype(BF16), w_sh_up[0].astype(BF16),
                   w_sh_down[0].astype(BF16), ln2_g, ln2_b, seg, tm_moe)
    y_prompt = out[:tokens[0]].reshape(x_prompt.shape)
    y_sample = out[tokens[0]:].reshape(x_sample.shape)
    return (y_prompt, y_sample)
```

```python
import functools
import math

import jax
import jax.numpy as jnp
from jax import lax
from jax.experimental import pallas as pl
from jax.experimental.pallas import tpu as pltpu

F32 = jnp.float32
BF16 = jnp.bfloat16
I32 = jnp.int32

D_MODEL = 1024
D_INNER = 2048
SSD_HEADS = 32
SSD_HEAD_DIM = 64
SSD_GROUPS = 8
SSD_STATE = 128
SSD_CHUNK = 128
HEADS_PER_SSD_GROUP = SSD_HEADS // SSD_GROUPS
CONV_WIDTH = 5
CONV_PAD = CONV_WIDTH // 2
CONV_DIM = D_INNER + 2 * SSD_GROUPS * SSD_STATE
ATTN_PATTERNS = ((128, 1), (512, 4), (2048, 16))
ATTN_GROUPS = len(ATTN_PATTERNS)
ATTN_HEADS_PER_GROUP = 8
ATTN_HEAD_DIM = 64
ATTN_GROUP_COLS = ATTN_HEADS_PER_GROUP * ATTN_HEAD_DIM
ATTN_DIM = ATTN_GROUPS * ATTN_GROUP_COLS
ROT_DIM = ATTN_HEAD_DIM // 4
ROT_HALF = ROT_DIM // 2
ROPE_THETA = 500000.0
N_EXPERTS = 256
EXPERT_HIDDEN = 256
TOP_K = 8
N_EXPERT_GROUPS = 8
EXPERTS_PER_GROUP = N_EXPERTS // N_EXPERT_GROUPS
TOPK_GROUPS = 4
ROUTED_SCALE = 2.5
SHARED_HIDDEN = 256
EXPERT_BLOCK = 512
PACKED_COLS = D_MODEL // 2
ROW_SUBLANES = PACKED_COLS // 128
DEPTH = 1
ALPHA = (2.0 * DEPTH) ** 0.25
NORM_EPS = 1e-5

COL_Z = 0
COL_XBC = COL_Z + D_INNER
COL_GSSD = COL_XBC + CONV_DIM
COL_GATTN = COL_GSSD + D_MODEL
COL_Q = COL_GATTN + D_MODEL
COL_K = COL_Q + ATTN_DIM
COL_V = COL_K + ATTN_DIM
N_MAIN = COL_V + ATTN_DIM
DT_COLS = 128

LANES = 128
SUBLANES = 8
VMEM_LIMIT = 56 * 1024 * 1024

DMA_PRIORITIES = 2

BAND_RADIUS = 64
ATTN_TQ = 128
NEG_INF = float("-inf")


def _cparams(sem):
    return pltpu.CompilerParams(dimension_semantics=sem, vmem_limit_bytes=VMEM_LIMIT)


def _sigmoid(x):
    return 1.0 / (1.0 + jnp.exp(-x))


def _silu(x):
    return x * _sigmoid(x)


def _softplus(x):
    return jnp.maximum(x, 0.0) + jnp.log(1.0 + jnp.exp(-jnp.abs(x)))


def _pack_halves(x):
    c = x.shape[1] // 2
    hi = lax.bitcast_convert_type(x[:, :c].astype(BF16).astype(F32), jnp.uint32)
    lo = lax.bitcast_convert_type(x[:, c:].astype(BF16).astype(F32), jnp.uint32)
    return hi | (lo >> 16)


def _unpack_halves(u):
    left = lax.bitcast_convert_type(u & jnp.uint32(0xFFFF0000), F32)
    right = lax.bitcast_convert_type(u << 16, F32)
    return left, right


def _store_rows(ref, lead, x):
    rows = x.shape[0]
    packed = _pack_halves(x)
    for j in range(ROW_SUBLANES):
        idx = lead + (pl.ds(j, rows, stride=ROW_SUBLANES), slice(None))
        ref[idx] = packed[:, j * LANES:(j + 1) * LANES]


def _load_row_pieces(ref, lead, rows):
    lefts, rights = [], []
    for j in range(ROW_SUBLANES):
        idx = lead + (pl.ds(j, rows, stride=ROW_SUBLANES), slice(None))
        left, right = _unpack_halves(ref[idx])
        lefts.append(left)
        rights.append(right)
    return lefts + rights


def _ada_kernel(c_ref, w_ref, b_ref, o_ref):
    c = _silu(c_ref[...]).astype(BF16)
    o_ref[...] = jnp.dot(c, w_ref[...].astype(BF16), preferred_element_type=F32) + b_ref[...]


def _ada(c_pad, w_ada, b_ada):
    rows = c_pad.shape[0]
    n = w_ada.shape[1]
    tn = 1536
    return pl.pallas_call(
        _ada_kernel,
        grid=(n // tn,),
        in_specs=[pl.BlockSpec((rows, D_MODEL), lambda j: (0, 0)),
                  pl.BlockSpec((D_MODEL, tn), lambda j: (0, j)),
                  pl.BlockSpec((1, tn), lambda j: (0, j))],
        out_specs=pl.BlockSpec((rows, tn), lambda j: (0, j)),
        out_shape=jax.ShapeDtypeStruct((rows, n), F32),
        compiler_params=_cparams(("arbitrary",)),
        name="ada",
    )(c_pad, w_ada, b_ada)


def _inproj_kernel(x_ref, sc_ref, sh_ref, w_ref, o_ref, h_ref):
    @pl.when(pl.program_id(1) == 0)
    def _():
        h_ref[...] = (x_ref[...] * (1.0 + sc_ref[0]) + sh_ref[0]).astype(BF16)

    o_ref[...] = jnp.dot(h_ref[...], w_ref[...], preferred_element_type=F32).astype(o_ref.dtype)


def _inproj(x, scale, shift, w, seg, tm, tn, out_dtype):
    t = x.shape[0]
    n = w.shape[1]
    per_seg = seg // tm
    return pl.pallas_call(
        _inproj_kernel,
        grid=(t // tm, n // tn),
        in_specs=[pl.BlockSpec((tm, D_MODEL), lambda i, j: (i, 0)),
                  pl.BlockSpec((1, 1, D_MODEL), lambda i, j: (i // per_seg, 0, 0)),
                  pl.BlockSpec((1, 1, D_MODEL), lambda i, j: (i // per_seg, 0, 0)),
                  pl.BlockSpec((D_MODEL, tn), lambda i, j: (0, j))],
        out_specs=pl.BlockSpec((tm, tn), lambda i, j: (i, j)),
        out_shape=jax.ShapeDtypeStruct((t, n), out_dtype),
        scratch_shapes=[pltpu.VMEM((tm, D_MODEL), BF16)],
        compiler_params=_cparams(("arbitrary", "arbitrary")),
        name="inproj",
    )(x, scale, shift, w)


def _conv_kernel(xp_ref, x_ref, xn_ref, w_ref, b_ref, o_ref, *, n_l):
    l = pl.program_id(1)
    tl = x_ref.shape[0]
    xp = jnp.where(l > 0, xp_ref[...], 0.0)
    xn = jnp.where(l < n_l - 1, xn_ref[...], 0.0)
    xx = jnp.concatenate([xp, x_ref[...], xn], axis=0)
    acc = b_ref[...] + w_ref[CONV_PAD:CONV_PAD + 1, :] * x_ref[...]
    for k in range(CONV_WIDTH):
        if k != CONV_PAD:
            shifted = pltpu.roll(xx, (CONV_PAD - k) % xx.shape[0], axis=0)
            acc = acc + w_ref[k:k + 1, :] * shifted[SUBLANES:SUBLANES + tl, :]
    o_ref[...] = _silu(acc)


def _conv(proj, conv_w, conv_b, row_off, batch, seq, tl=512, tc=512):
    t = proj.shape[0]
    n_l = seq // tl
    n_c = CONV_DIM // tc
    c0 = COL_XBC // tc
    r8 = tl // SUBLANES
    last8 = t // SUBLANES - 1

    def row_blk(b, l):
        return (row_off + b * seq) // tl + l

    return pl.pallas_call(
        functools.partial(_conv_kernel, n_l=n_l),
        grid=(batch, n_l, n_c),
        in_specs=[pl.BlockSpec((SUBLANES, tc), lambda b, l, c: (jnp.maximum(row_blk(b, l) * r8 - 1, 0), c0 + c)),
                  pl.BlockSpec((tl, tc), lambda b, l, c: (row_blk(b, l), c0 + c)),
                  pl.BlockSpec((SUBLANES, tc), lambda b, l, c: (jnp.minimum((row_blk(b, l) + 1) * r8, last8), c0 + c)),
                  pl.BlockSpec((CONV_WIDTH, tc), lambda b, l, c: (0, c)),
                  pl.BlockSpec((1, tc), lambda b, l, c: (0, c))],
        out_specs=pl.BlockSpec((tl, tc), lambda b, l, c: (b * n_l + l, c)),
        out_shape=jax.ShapeDtypeStruct((batch * seq, CONV_DIM), F32),
        compiler_params=_cparams(("arbitrary", "arbitrary", "arbitrary")),
        name="conv",
    )(proj, proj, proj, conv_w, conv_b)


def _exact_tri_matmul(tri_bf16, x):
    hi = x.astype(BF16)
    r1 = x - hi.astype(F32)
    mid = r1.astype(BF16)
    lo = (r1 - mid.astype(F32)).astype(BF16)
    out = jnp.dot(tri_bf16, hi, preferred_element_type=F32)
    out = out + jnp.dot(tri_bf16, mid, preferred_element_type=F32)
    return out + jnp.dot(tri_bf16, lo, preferred_element_type=F32)


def _ssd_kernel(*refs, reverse, accumulate):
    if accumulate:
        x_ref, b_ref, c_ref, dt_ref, dtb_ref, alog_ref, yin_ref, y_ref, state_ref = refs
    else:
        x_ref, b_ref, c_ref, dt_ref, dtb_ref, alog_ref, y_ref, state_ref = refs
        yin_ref = None
    lane0 = SSD_HEADS if reverse else 0

    @pl.when(pl.program_id(1) == 0)
    def _():
        state_ref[...] = jnp.zeros(state_ref.shape, F32)

    dt = _softplus(dt_ref[...] + dtb_ref[...])
    da = dt * (-jnp.exp(alog_ref[...]))
    row = lax.broadcasted_iota(I32, (SSD_CHUNK, SSD_CHUNK), 0)
    col = lax.broadcasted_iota(I32, (SSD_CHUNK, SSD_CHUNK), 1)
    tri = (col >= row) if reverse else (col <= row)
    cum = _exact_tri_matmul(tri.astype(BF16), da)
    cum_t = cum.T
    dt_t = dt.T
    edge = 0 if reverse else SSD_CHUNK - 1
    total_b = jnp.broadcast_to(cum_t[:, edge:edge + 1], (LANES, SSD_CHUNK))
    to_end_t = dt_t * jnp.exp(total_b - cum_t)
    chunk_decay_b = jnp.exp(total_b)
    src_t = cum_t - jnp.log(dt_t)
    first_head = lax.broadcasted_iota(I32, (1, 2 * SSD_HEAD_DIM), 1) < SSD_HEAD_DIM
    pairs = HEADS_PER_SSD_GROUP // 2

    for g in range(SSD_GROUPS):
        bg = b_ref[:, g * SSD_STATE:(g + 1) * SSD_STATE]
        cg = c_ref[:, g * SSD_STATE:(g + 1) * SSD_STATE]
        cb = lax.dot_general(cg.astype(BF16), bg.astype(BF16), (((1,), (1,)), ((), ())),
                             preferred_element_type=F32)
        bg_t = bg.T
        for pr in range(pairs):
            h0 = g * HEADS_PER_SSD_GROUP + 2 * pr
            cols = slice(h0 * SSD_HEAD_DIM, (h0 + 2) * SSD_HEAD_DIM)
            x_pair = x_ref[:, cols].astype(BF16)
            state = state_ref[g * pairs + pr]
            rhs = jnp.concatenate([x_pair, state.astype(BF16)], axis=0)
            res = []
            new = []
            for e in range(2):
                ln = lane0 + h0 + e
                cum_l = jnp.broadcast_to(cum[:, ln:ln + 1], (SSD_CHUNK, SSD_CHUNK))
                decay = jnp.exp(jnp.where(tri, cum_l - src_t[ln:ln + 1, :], NEG_INF))
                lhs = jnp.concatenate([(cb * decay).astype(BF16), (cg * jnp.exp(cum_l)).astype(BF16)], axis=1)
                res.append(jnp.dot(lhs, rhs, preferred_element_type=F32))
                b_scaled = (bg_t * to_end_t[ln:ln + 1, :]).astype(BF16)
                new.append(jnp.dot(b_scaled, x_pair, preferred_element_type=F32))
            y_pair = jnp.where(first_head, res[0], res[1])
            if accumulate:
                y_pair = y_pair + yin_ref[:, cols]
            y_ref[:, cols] = y_pair
            ln0 = lane0 + h0
            dec = jnp.where(first_head, chunk_decay_b[ln0:ln0 + 1, :], chunk_decay_b[ln0 + 1:ln0 + 2, :])
            state_ref[g * pairs + pr] = state * dec + jnp.where(first_head, new[0], new[1])


def _ssd(xbc, dt_raw, dt_bias, a_log, y_in, row_off, batch, seq, reverse):
    n_c = seq // SSD_CHUNK
    accumulate = y_in is not None

    def loc(b, c):
        return b * n_c + (n_c - 1 - c if reverse else c)

    def glob(b, c):
        return row_off // SSD_CHUNK + loc(b, c)

    in_specs = [pl.BlockSpec((SSD_CHUNK, D_INNER), lambda b, c: (loc(b, c), 0)),
                pl.BlockSpec((SSD_CHUNK, SSD_GROUPS * SSD_STATE), lambda b, c: (loc(b, c), 2)),
                pl.BlockSpec((SSD_CHUNK, SSD_GROUPS * SSD_STATE), lambda b, c: (loc(b, c), 3)),
                pl.BlockSpec((SSD_CHUNK, DT_COLS), lambda b, c: (glob(b, c), 0)),
                pl.BlockSpec((1, DT_COLS), lambda b, c: (0, 0)),
                pl.BlockSpec((1, DT_COLS), lambda b, c: (0, 0))]
    args = [xbc, xbc, xbc, dt_raw, dt_bias, a_log]
    aliases = {}
    if accumulate:
        in_specs.append(pl.BlockSpec((SSD_CHUNK, D_INNER), lambda b, c: (loc(b, c), 0)))
        args.append(y_in)
        aliases = {len(args) - 1: 0}
    return pl.pallas_call(
        functools.partial(_ssd_kernel, reverse=reverse, accumulate=accumulate),
        grid=(batch, n_c),
        in_specs=in_specs,
        out_specs=pl.BlockSpec((SSD_CHUNK, D_INNER), lambda b, c: (loc(b, c), 0)),
        out_shape=jax.ShapeDtypeStruct((batch * seq, D_INNER), F32),
        scratch_shapes=[pltpu.VMEM((SSD_HEADS // 2, SSD_STATE, 2 * SSD_HEAD_DIM), F32)],
        input_output_aliases=aliases,
        compiler_params=_cparams(("arbitrary", "arbitrary")),
        name="ssd_bwd" if reverse else "ssd_fwd",
    )(*args)


def _ssd_out_kernel(y_ref, x_ref, z_ref, dskip_ref, nw_ref, w_ref, o_ref):
    y = y_ref[...] + dskip_ref[...] * x_ref[...]
    y = y * _silu(z_ref[...])
    ms = jnp.mean(y * y, axis=-1, keepdims=True)
    yn = y * lax.rsqrt(ms + NORM_EPS) * nw_ref[...]
    o_ref[...] = jnp.dot(yn.astype(BF16), w_ref[...], preferred_element_type=F32)


def _ssd_out(y, xbc, proj, d_skip_cols, norm_w, w_branch, row_off, tm=256):
    t = y.shape[0]
    return pl.pallas_call(
        _ssd_out_kernel,
        grid=(t // tm,),
        in_specs=[pl.BlockSpec((tm, D_INNER), lambda i: (i, 0)),
                  pl.BlockSpec((tm, D_INNER), lambda i: (i, 0)),
                  pl.BlockSpec((tm, D_INNER), lambda i: (row_off // tm + i, COL_Z // D_INNER)),
                  pl.BlockSpec((1, D_INNER), lambda i: (0, 0)),
                  pl.BlockSpec((1, D_INNER), lambda i: (0, 0)),
                  pl.BlockSpec((D_INNER, D_MODEL), lambda i: (0, 0))],
        out_specs=pl.BlockSpec((tm, D_MODEL), lambda i: (i, 0)),
        out_shape=jax.ShapeDtypeStruct((t, D_MODEL), F32),
        compiler_params=_cparams(("arbitrary",)),
        name="ssd_out",
    )(y, xbc, proj, d_skip_cols, norm_w, w_branch)


def _prep_kernel(q_ref, k_ref, v_ref, cos_ref, sa_ref, sb_ref, qo_ref, ko_ref, vo_ref, *, dil):
    rows = q_ref.shape[0] // dil

    def phase(ref, p):
        if dil == 1:
            return ref[...]
        return ref[pl.ds(p, rows, stride=dil), :]

    for p in range(dil):
        cos = phase(cos_ref, p)
        sa = phase(sa_ref, p)
        sb = phase(sb_ref, p)

        def rot(t):
            return (t * cos + pltpu.roll(t, LANES - ROT_HALF, axis=1) * sa
                    + pltpu.roll(t, ROT_HALF, axis=1) * sb)

        qo_ref[0, p] = rot(phase(q_ref, p)) * (ATTN_HEAD_DIM ** -0.5)
        ko_ref[0, p] = rot(phase(k_ref, p))
        vo_ref[0, p] = phase(v_ref, p)


def _prep(proj, tables, gi, dil, row_off, batch, seq, rows_per_step=2048):
    r = rows_per_step
    n_i = seq // r
    n_c = ATTN_GROUP_COLS // LANES
    m = seq // dil
    cos_t, sa_t, sb_t = tables

    def in_blk(col):
        cb = (col + gi * ATTN_GROUP_COLS) // LANES
        return pl.BlockSpec((r, LANES), lambda b, i, c: ((row_off + b * seq) // r + i, cb + c))

    tab = pl.BlockSpec((r, LANES), lambda b, i, c: (i, 0))
    out = pl.BlockSpec((1, dil, r // dil, LANES), lambda b, i, c: (b, 0, i, c))
    shape = jax.ShapeDtypeStruct((batch, dil, m, ATTN_GROUP_COLS), F32)
    return pl.pallas_call(
        functools.partial(_prep_kernel, dil=dil),
        grid=(batch, n_i, n_c),
        in_specs=[in_blk(COL_Q), in_blk(COL_K), in_blk(COL_V), tab, tab, tab],
        out_specs=[out, out, out],
        out_shape=[shape, shape, shape],
        compiler_params=_cparams(("arbitrary", "arbitrary", "arbitrary")),
        name=f"attn_prep_d{dil}",
    )(proj, proj, proj, cos_t, sa_t, sb_t)


def _band_attn_kernel(q_ref, kp_ref, kc_ref, kn_ref, vp_ref, vc_ref, vn_ref, o_ref, lse_ref, *, m_len):
    i = pl.program_id(1)
    tq = q_ref.shape[1]
    tk = tq + 2 * BAND_RADIUS
    q = q_ref[0].astype(BF16)
    k = jnp.concatenate([kp_ref[0], kc_ref[0], kn_ref[0]], axis=0).astype(BF16)
    v = jnp.concatenate([vp_ref[0], vc_ref[0], vn_ref[0]], axis=0).astype(BF16)
    qpos = i * tq + lax.broadcasted_iota(I32, (tq, tk), 0)
    kpos = i * tq - BAND_RADIUS + lax.broadcasted_iota(I32, (tq, tk), 1)
    valid = (jnp.abs(qpos - kpos) <= BAND_RADIUS) & (kpos >= 0) & (kpos < m_len)
    for h in range(ATTN_HEADS_PER_GROUP):
        cols = slice(h * ATTN_HEAD_DIM, (h + 1) * ATTN_HEAD_DIM)
        s = lax.dot_general(q[:, cols], k[:, cols], (((1,), (1,)), ((), ())), preferred_element_type=F32)
        s = jnp.where(valid, s, NEG_INF)
        mx = jnp.max(s, axis=-1, keepdims=True)
        p = jnp.exp(s - mx)
        den = jnp.sum(p, axis=-1, keepdims=True)
        o = jnp.dot(p.astype(BF16), v[:, cols], preferred_element_type=F32)
        o_ref[0, :, cols] = o / den
        lse_ref[0, :, cols] = jnp.broadcast_to(mx + jnp.log(den), (tq, ATTN_HEAD_DIM))


def _band_attn(q, k, v):
    b, d, m, c = q.shape
    n = b * d
    q, k, v = (a.reshape(n, m, c) for a in (q, k, v))
    tq = ATTN_TQ
    per = tq // BAND_RADIUS
    last = m // BAND_RADIUS - 1
    main = pl.BlockSpec((1, tq, c), lambda s, i: (s, i, 0))
    prev = pl.BlockSpec((1, BAND_RADIUS, c), lambda s, i: (s, jnp.maximum(i * per - 1, 0), 0))
    nxt = pl.BlockSpec((1, BAND_RADIUS, c), lambda s, i: (s, jnp.minimum((i + 1) * per, last), 0))
    shape = jax.ShapeDtypeStruct((n, m, c), F32)
    o, lse = pl.pallas_call(
        functools.partial(_band_attn_kernel, m_len=m),
        grid=(n, m // tq),
        in_specs=[main, prev, main, nxt, prev, main, nxt],
        out_specs=[main, main],
        out_shape=[shape, shape],
        compiler_params=_cparams(("arbitrary", "arbitrary")),
        name=f"band_attn_m{m}",
    )(q, k, k, k, v, v, v)
    return o.reshape(b, d, m, c), lse.reshape(b, d, m, c)


def _layer_norm(x, g, b):
    mu = jnp.mean(x, axis=-1, keepdims=True)
    xc = x - mu
    var = jnp.mean(xc * xc, axis=-1, keepdims=True)
    return xc * lax.rsqrt(var + NORM_EPS) * g + b


def _merge_kernel(x_ref, yssd_ref, gs_ref, ga_ref, o1_ref, l1_ref, o2_ref, l2_ref, o3_ref, l3_ref,
                  g1_ref, sc2_ref, sh2_ref, wba_ref, wout_ref, lng_ref, lnb_ref, wr_ref, *rest, n_prev):
    x1_ref, h2_ref, logit_ref, scr_ref = rest[n_prev:]
    tm = x_ref.shape[0]

    def interleaved(ref, dil, slot):
        if dil == 1:
            return ref[0, 0]
        rows = tm // dil
        n_c = ATTN_GROUP_COLS // LANES
        for p in range(dil):
            for c in range(n_c):
                scr_ref[slot * n_c + c, pl.ds(p, rows, stride=dil), :] = ref[0, p, :, c * LANES:(c + 1) * LANES]
        return jnp.concatenate([scr_ref[slot * n_c + c] for c in range(n_c)], axis=1)

    outs = []
    lses = []
    slot = 0
    for (o_ref, l_ref), (_, dil) in zip(((o1_ref, l1_ref), (o2_ref, l2_ref), (o3_ref, l3_ref)), ATTN_PATTERNS):
        outs.append(interleaved(o_ref, dil, slot))
        lses.append(interleaved(l_ref, dil, slot + 1))
        slot += 2
    mx = jnp.maximum(jnp.maximum(lses[0], lses[1]), lses[2])
    es = [jnp.exp(l - mx) for l in lses]
    den = es[0] + es[1] + es[2]
    y_att = (es[0] * outs[0] + es[1] * outs[1] + es[2] * outs[2]) / den
    y_attn = jnp.dot(y_att.astype(BF16), wba_ref[...], preferred_element_type=F32)
    merged = _sigmoid(gs_ref[...]) * yssd_ref[...] + _sigmoid(ga_ref[...]) * y_attn
    mix = jnp.dot(merged.astype(BF16), wout_ref[...], preferred_element_type=F32)
    x1 = _layer_norm(ALPHA * x_ref[...] + g1_ref[0] * mix, lng_ref[...], lnb_ref[...])
    x1_ref[...] = x1
    h2 = x1 * (1.0 + sc2_ref[0]) + sh2_ref[0]
    _store_rows(h2_ref, (), h2)
    logit_ref[...] = lax.dot_general(wr_ref[...], h2.astype(BF16), (((1,), (1,)), ((), ())),
                                     preferred_element_type=F32)


def _merge(x, y_ssd, proj, attn, gate1, scale2, shift2, w_ba, w_out, ln_g, ln_b, w_router_t,
           batch, seq, out_row_off, t_all, prev, tm=256):
    n_i = seq // tm
    out0 = out_row_off // tm

    def rows(col_blk):
        return pl.BlockSpec((tm, D_MODEL), lambda b, i: (b * n_i + i, col_blk))

    def local(width):
        return pl.BlockSpec((tm, width), lambda b, i: (b * n_i + i, 0))

    def modv():
        return pl.BlockSpec((1, 1, D_MODEL), lambda b, i: (b, 0, 0))

    def full(shape):
        return pl.BlockSpec(shape, lambda b, i: tuple(0 for _ in shape))

    attn_specs = []
    attn_args = []
    for (o, lse), (_, dil) in zip(attn, ATTN_PATTERNS):
        spec = pl.BlockSpec((1, dil, tm // dil, ATTN_GROUP_COLS), lambda b, i: (b, 0, i, 0))
        attn_specs += [spec, spec]
        attn_args += [o, lse]
    n_in = 4 + len(attn_args) + 8
    return pl.pallas_call(
        functools.partial(_merge_kernel, n_prev=0 if prev is None else 3),
        grid=(batch, n_i),
        in_specs=[rows(0), local(D_MODEL), rows(COL_GSSD // D_MODEL), rows(COL_GATTN // D_MODEL)] + attn_specs
                 + [modv(), modv(), modv(), full((ATTN_GROUP_COLS, D_MODEL)), full((D_MODEL, D_MODEL)),
                    full((1, D_MODEL)), full((1, D_MODEL)), full((N_EXPERTS, D_MODEL))]
                 + ([pl.BlockSpec(memory_space=pl.ANY)] * 3 if prev is not None else []),
        out_specs=[pl.BlockSpec((tm, D_MODEL), lambda b, i: (out0 + b * n_i + i, 0)),
                   pl.BlockSpec((tm * ROW_SUBLANES, LANES), lambda b, i: (out0 + b * n_i + i, 0)),
                   pl.BlockSpec((N_EXPERTS, tm), lambda b, i: (0, out0 + b * n_i + i))],
        out_shape=[jax.ShapeDtypeStruct((t_all, D_MODEL), F32),
                   jax.ShapeDtypeStruct((t_all * ROW_SUBLANES, LANES), jnp.uint32),
                   jax.ShapeDtypeStruct((N_EXPERTS, t_all), F32)],
        scratch_shapes=[pltpu.VMEM((2 * ATTN_GROUPS * (ATTN_GROUP_COLS // LANES), tm, LANES), F32)],
        input_output_aliases={n_in + j: j for j in range(3)} if prev is not None else {},
        compiler_params=_cparams(("arbitrary", "arbitrary")),
        name="mixer_merge",
    )(x, y_ssd, proj, proj, *attn_args, gate1, scale2, shift2, w_ba, w_out, ln_g, ln_b, w_router_t,
      *(prev if prev is not None else ()))


def _route_kernel(logit_ref, bias_ref, idx_ref, w_ref, rank_ref, cnt_ref, run_ref):
    tm = logit_ref.shape[1]

    @pl.when(pl.program_id(0) == 0)
    def _():
        run_ref[...] = jnp.zeros(run_ref.shape, F32)

    scores = _sigmoid(logit_ref[...])
    sel = scores + bias_ref[...]
    iota_g = lax.broadcasted_iota(I32, (EXPERTS_PER_GROUP, tm), 0)
    grp = []
    for g in range(N_EXPERT_GROUPS):
        sg = sel[g * EXPERTS_PER_GROUP:(g + 1) * EXPERTS_PER_GROUP, :]
        top1 = jnp.max(sg, axis=0, keepdims=True)
        first = jnp.min(jnp.where(sg == top1, iota_g, EXPERTS_PER_GROUP), axis=0, keepdims=True)
        top2 = jnp.max(jnp.where(iota_g == first, NEG_INF, sg), axis=0, keepdims=True)
        grp.append(top1 + top2)
    gs = jnp.concatenate(grp, axis=0)
    iota_n = lax.broadcasted_iota(I32, (N_EXPERT_GROUPS, tm), 0)
    keep_g = jnp.zeros((N_EXPERT_GROUPS, tm), F32)
    for _ in range(TOPK_GROUPS):
        best = jnp.max(gs, axis=0, keepdims=True)
        first = jnp.min(jnp.where(gs == best, iota_n, N_EXPERT_GROUPS), axis=0, keepdims=True)
        hit = iota_n == first
        keep_g = jnp.where(hit, 1.0, keep_g)
        gs = jnp.where(hit, NEG_INF, gs)
    cand = jnp.concatenate(
        [jnp.where(keep_g[g:g + 1, :] > 0.0, sel[g * EXPERTS_PER_GROUP:(g + 1) * EXPERTS_PER_GROUP, :], NEG_INF)
         for g in range(N_EXPERT_GROUPS)], axis=0)
    iota_e = lax.broadcasted_iota(I32, (N_EXPERTS, tm), 0)
    chosen = jnp.zeros((N_EXPERTS, tm), F32)
    idxs = []
    ws = []
    for _ in range(TOP_K):
        best = jnp.max(cand, axis=0, keepdims=True)
        first = jnp.min(jnp.where(cand == best, iota_e, N_EXPERTS), axis=0, keepdims=True)
        hit = iota_e == first
        idxs.append(first)
        ws.append(jnp.sum(jnp.where(hit, scores, 0.0), axis=0, keepdims=True))
        chosen = jnp.where(hit, 1.0, chosen)
        cand = jnp.where(hit, NEG_INF, cand)
    top_w = jnp.concatenate(ws, axis=0)
    top_w = top_w / jnp.sum(top_w, axis=0, keepdims=True) * ROUTED_SCALE
    idx_ref[...] = jnp.concatenate(idxs, axis=0)
    w_ref[...] = top_w
    s_i = lax.broadcasted_iota(I32, (tm, tm), 0)
    t_i = lax.broadcasted_iota(I32, (tm, tm), 1)
    before = (s_i < t_i).astype(BF16)
    prior = jnp.dot(chosen.astype(BF16), before, preferred_element_type=F32) + run_ref[...]
    ranks = [jnp.sum(jnp.where(iota_e == idxs[k], prior, 0.0), axis=0, keepdims=True) for k in range(TOP_K)]
    rank_ref[...] = jnp.concatenate(ranks, axis=0).astype(I32)
    run_ref[...] = run_ref[...] + jnp.sum(chosen, axis=1, keepdims=True)
    cnt_ref[...] = run_ref[...].astype(I32)


def _route(logits_t, bias_col, tm=256):
    t = logits_t.shape[1]
    tok = pl.BlockSpec((TOP_K, tm), lambda i: (0, i))
    return pl.pallas_call(
        _route_kernel,
        grid=(t // tm,),
        in_specs=[pl.BlockSpec((N_EXPERTS, tm), lambda i: (0, i)),
                  pl.BlockSpec((N_EXPERTS, 1), lambda i: (0, 0))],
        out_specs=[tok, tok, tok, pl.BlockSpec((N_EXPERTS, 1), lambda i: (0, 0))],
        out_shape=[jax.ShapeDtypeStruct((TOP_K, t), I32), jax.ShapeDtypeStruct((TOP_K, t), F32),
                   jax.ShapeDtypeStruct((TOP_K, t), I32), jax.ShapeDtypeStruct((N_EXPERTS, 1), I32)],
        scratch_shapes=[pltpu.VMEM((N_EXPERTS, 1), F32)],
        compiler_params=_cparams(("arbitrary",)),
        name="route",
    )(logits_t, bias_col)


def _dest_kernel(idx_ref, rank_ref, start_ref, o_ref):
    tm = idx_ref.shape[1]
    iota_e = lax.broadcasted_iota(I32, (N_EXPERTS, tm), 0)
    start = start_ref[...].astype(F32)
    for k in range(TOP_K):
        base = jnp.sum(jnp.where(iota_e == idx_ref[k:k + 1, :], start, 0.0), axis=0, keepdims=True)
        o_ref[0, :, k * tm:(k + 1) * tm] = (base.astype(I32) + rank_ref[k:k + 1, :]) * ROW_SUBLANES


def _dest(top_idx, rank, pad_start_col, tm):
    t = top_idx.shape[1]
    tok = pl.BlockSpec((TOP_K, tm), lambda i: (0, i))
    out = pl.pallas_call(
        _dest_kernel,
        grid=(t // tm,),
        in_specs=[tok, tok, pl.BlockSpec((N_EXPERTS, 1), lambda i: (0, 0))],
        out_specs=pl.BlockSpec((1, 1, TOP_K * tm), lambda i: (i, 0, 0)),
        out_shape=jax.ShapeDtypeStruct((t // tm, 1, TOP_K * tm), I32),
        compiler_params=_cparams(("arbitrary",)),
        name="moe_dest",
    )(top_idx, rank, pad_start_col)
    return out.reshape(t // tm, TOP_K * tm)


def _dispatch_kernel(pad_end_ref, dest_hbm, h_ref, xs_hbm, dest_smem, zeros_ref, sem_idx, sem_zero, sem_row,
                     *, tm):
    i = pl.program_id(0)
    idx_copy = pltpu.make_async_copy(dest_hbm.at[i], dest_smem, sem_idx)
    idx_copy.start()

    blk_rows = EXPERT_BLOCK * ROW_SUBLANES

    def zero_copy(e):
        start = pl.multiple_of(jnp.maximum(pad_end_ref[e] - EXPERT_BLOCK, 0) * ROW_SUBLANES, blk_rows)
        return pltpu.make_async_copy(zeros_ref, xs_hbm.at[pl.ds(start, blk_rows)], sem_zero)

    @pl.when(i == 0)
    def _():
        zeros_ref[...] = jnp.zeros(zeros_ref.shape, zeros_ref.dtype)

        def start_zero(e, carry):
            zero_copy(e).start()
            return carry

        def wait_zero(e, carry):
            zero_copy(e).wait()
            return carry

        lax.fori_loop(0, N_EXPERTS, start_zero, 0)
        lax.fori_loop(0, N_EXPERTS, wait_zero, 0)

    idx_copy.wait()

    def row_copy(t, k):
        dst = pl.multiple_of(dest_smem[k * tm + t], ROW_SUBLANES)
        return pltpu.make_async_copy(h_ref.at[pl.ds(pl.multiple_of(t * ROW_SUBLANES, ROW_SUBLANES), ROW_SUBLANES)],
                                     xs_hbm.at[pl.ds(dst, ROW_SUBLANES)], sem_row)

    def start_rows(t, carry):
        for k in range(TOP_K):
            row_copy(t, k).start(priority=k % DMA_PRIORITIES)
        return carry

    def wait_rows(t, carry):
        for k in range(TOP_K):
            row_copy(t, k).wait()
        return carry

    lax.fori_loop(0, tm, start_rows, 0)
    lax.fori_loop(0, tm, wait_rows, 0)


def _dispatch(pad_end, dest_tiles, h2, n_rows, tm):
    n_tiles = dest_tiles.shape[0]
    return pl.pallas_call(
        functools.partial(_dispatch_kernel, tm=tm),
        grid_spec=pltpu.PrefetchScalarGridSpec(
            num_scalar_prefetch=1,
            grid=(n_tiles,),
            in_specs=[pl.BlockSpec(memory_space=pl.ANY),
                      pl.BlockSpec((tm * ROW_SUBLANES, LANES), lambda i, pe: (i, 0))],
            out_specs=pl.BlockSpec(memory_space=pl.ANY),
            scratch_shapes=[pltpu.SMEM((TOP_K * tm,), I32),
                            pltpu.VMEM((EXPERT_BLOCK * ROW_SUBLANES, LANES), jnp.uint32),
                            pltpu.SemaphoreType.DMA, pltpu.SemaphoreType.DMA, pltpu.SemaphoreType.DMA]),
        out_shape=jax.ShapeDtypeStruct((n_rows * ROW_SUBLANES, LANES), jnp.uint32),
        compiler_params=_cparams(("arbitrary",)),
        name="moe_dispatch",
    )(pad_end, dest_tiles, h2)


def _experts_kernel(blk_exp_ref, n_valid_ref, xs_ref, wg_ref, wu_ref, wd_ref, ys_ref, wgu_ref, wdn_ref):
    i = pl.program_id(0)

    @pl.when(i < n_valid_ref[0])
    def _():
        @pl.when((i == 0) | (blk_exp_ref[i] != blk_exp_ref[jnp.maximum(i - 1, 0)]))
        def _():
            wgu_ref[:, :EXPERT_HIDDEN] = wg_ref[0].astype(BF16)
            wgu_ref[:, EXPERT_HIDDEN:] = wu_ref[0].astype(BF16)
            wdn_ref[...] = wd_ref[0].astype(BF16)

        x = jnp.concatenate([c.astype(BF16) for c in _load_row_pieces(xs_ref, (), EXPERT_BLOCK)], axis=1)
        gu = jnp.dot(x, wgu_ref[...], preferred_element_type=F32)
        hmid = (_silu(gu[:, :EXPERT_HIDDEN]) * gu[:, EXPERT_HIDDEN:]).astype(BF16)
        _store_rows(ys_ref, (), jnp.dot(hmid, wdn_ref[...], preferred_element_type=F32))


def _experts(blk_exp, n_valid, xs, w_gate, w_up, w_down):
    n_blocks = xs.shape[0] // (EXPERT_BLOCK * ROW_SUBLANES)

    def row_map(i, be, nv):
        return (jnp.minimum(i, nv[0] - 1), 0)

    def w_map(i, be, nv):
        return (be[i], 0, 0)

    return pl.pallas_call(
        _experts_kernel,
        grid_spec=pltpu.PrefetchScalarGridSpec(
            num_scalar_prefetch=2,
            grid=(n_blocks,),
            in_specs=[pl.BlockSpec((EXPERT_BLOCK * ROW_SUBLANES, LANES), row_map),
                      pl.BlockSpec((1, D_MODEL, EXPERT_HIDDEN), w_map),
                      pl.BlockSpec((1, D_MODEL, EXPERT_HIDDEN), w_map),
                      pl.BlockSpec((1, EXPERT_HIDDEN, D_MODEL), w_map)],
            out_specs=pl.BlockSpec((EXPERT_BLOCK * ROW_SUBLANES, LANES), row_map),
            scratch_shapes=[pltpu.VMEM((D_MODEL, 2 * EXPERT_HIDDEN), BF16),
                            pltpu.VMEM((EXPERT_HIDDEN, D_MODEL), BF16)]),
        out_shape=jax.ShapeDtypeStruct(xs.shape, jnp.uint32),
        compiler_params=_cparams(("arbitrary",)),
        name="moe_experts",
    )(blk_exp, n_valid, xs, w_gate, w_up, w_down)


def _combine_kernel(dest_hbm, ys_hbm, w_ref, x1_ref, h2_ref, g2_ref, wsg_ref, wsu_ref, wsd_ref,
                    lng_ref, lnb_ref, o_ref, dest_smem, rows_ref, sem_idx, sem_row, *, tm, tile_off):
    i = pl.program_id(0) + tile_off
    idx_copy = pltpu.make_async_copy(dest_hbm.at[i], dest_smem, sem_idx)
    idx_copy.start()
    idx_copy.wait()

    def row_copy(t, k):
        src = pl.multiple_of(dest_smem[k * tm + t], ROW_SUBLANES)
        dst = pl.multiple_of(t * ROW_SUBLANES, ROW_SUBLANES)
        return pltpu.make_async_copy(ys_hbm.at[pl.ds(src, ROW_SUBLANES)],
                                     rows_ref.at[k, pl.ds(dst, ROW_SUBLANES)], sem_row)

    def start_rows(t, carry):
        for k in range(TOP_K):
            row_copy(t, k).start(priority=k % DMA_PRIORITIES)
        return carry

    def wait_rows(t, carry):
        for k in range(TOP_K):
            row_copy(t, k).wait()
        return carry

    lax.fori_loop(0, tm, start_rows, 0)
    h2 = jnp.concatenate([c.astype(BF16) for c in _load_row_pieces(h2_ref, (), tm)], axis=1)
    g = jnp.dot(h2, wsg_ref[...], preferred_element_type=F32)
    u = jnp.dot(h2, wsu_ref[...], preferred_element_type=F32)
    ffn = jnp.dot((_silu(g) * u).astype(BF16), wsd_ref[...], preferred_element_type=F32)
    lax.fori_loop(0, tm, wait_rows, 0)
    w = w_ref[...]
    n_pieces = D_MODEL // LANES
    acc = [ffn[:, j * LANES:(j + 1) * LANES] for j in range(n_pieces)]
    for k in range(TOP_K):
        wk = jnp.broadcast_to(w[:, k:k + 1], (tm, LANES))
        pieces = _load_row_pieces(rows_ref, (k,), tm)
        acc = [a + wk * p for a, p in zip(acc, pieces)]
    ffn = jnp.concatenate(acc, axis=1)
    o_ref[...] = _layer_norm(ALPHA * x1_ref[...] + g2_ref[0] * ffn, lng_ref[...], lnb_ref[...])


def _combine(dest_tiles, ys, top_w_rows, x1, h2, gate2, w_sg, w_su, w_sd, ln_g, ln_b, seg, tm, tile_off, n_tiles):
    per_seg = seg // tm

    def full(shape):
        return pl.BlockSpec(shape, lambda i: tuple(0 for _ in shape))

    tok = pl.BlockSpec((tm, D_MODEL), lambda i: (i + tile_off, 0))
    return pl.pallas_call(
        functools.partial(_combine_kernel, tm=tm, tile_off=tile_off),
        grid=(n_tiles,),
        in_specs=[pl.BlockSpec(memory_space=pl.ANY), pl.BlockSpec(memory_space=pl.ANY),
                  pl.BlockSpec((tm, TOP_K), lambda i: (i + tile_off, 0)), tok,
                  pl.BlockSpec((tm * ROW_SUBLANES, LANES), lambda i: (i + tile_off, 0)),
                  pl.BlockSpec((1, 1, D_MODEL), lambda i: ((i + tile_off) // per_seg, 0, 0)),
                  full((D_MODEL, SHARED_HIDDEN)), full((D_MODEL, SHARED_HIDDEN)), full((SHARED_HIDDEN, D_MODEL)),
                  full((1, D_MODEL)), full((1, D_MODEL))],
        out_specs=pl.BlockSpec((tm, D_MODEL), lambda i: (i, 0)),
        out_shape=jax.ShapeDtypeStruct((n_tiles * tm, D_MODEL), F32),
        scratch_shapes=[pltpu.SMEM((TOP_K * tm,), I32), pltpu.VMEM((TOP_K, tm * ROW_SUBLANES, LANES), jnp.uint32),
                        pltpu.SemaphoreType.DMA, pltpu.SemaphoreType.DMA],
        compiler_params=_cparams(("arbitrary",)),
        name="moe_combine",
    )(dest_tiles, ys, top_w_rows, x1, h2, gate2, w_sg, w_su, w_sd, ln_g, ln_b)


def _rope_tables(seq):
    inv_freq = ROPE_THETA ** (-jnp.arange(ROT_HALF, dtype=F32) * 2.0 / ROT_DIM)
    ang = jnp.arange(seq, dtype=F32)[:, None] * inv_freq[None, :]
    cos, sin = jnp.cos(ang), jnp.sin(ang)
    rest = ATTN_HEAD_DIM - ROT_DIM
    ones = jnp.ones((seq, rest), F32)
    zeros = jnp.zeros((seq, rest), F32)
    zh = jnp.zeros((seq, ROT_HALF), F32)
    reps = LANES // ATTN_HEAD_DIM
    cos_t = jnp.tile(jnp.concatenate([cos, cos, ones], axis=1), (1, reps))
    sa_t = jnp.tile(jnp.concatenate([-sin, zh, zeros], axis=1), (1, reps))
    sb_t = jnp.tile(jnp.concatenate([zh, sin, zeros], axis=1), (1, reps))
    return cos_t, sa_t, sb_t


def _split_in_weights(w_in):
    cuts = (D_INNER, CONV_DIM, SSD_HEADS, SSD_HEADS, ATTN_DIM, ATTN_DIM, ATTN_DIM, D_MODEL, D_MODEL)
    offs = [0]
    for c in cuts:
        offs.append(offs[-1] + c)
    z, xbc, dtf, dtb, q, k, v, gs, ga = (w_in[:, offs[i]:offs[i + 1]] for i in range(len(cuts)))
    w_main = jnp.concatenate([z, xbc, gs, ga, q, k, v], axis=1).astype(BF16)
    assert COL_GSSD % D_MODEL == 0 and COL_Q % ATTN_GROUP_COLS == 0 and w_main.shape[1] == N_MAIN
    pad = jnp.zeros((D_MODEL, DT_COLS - 2 * SSD_HEADS), F32)
    w_dt = jnp.concatenate([dtf, dtb, pad], axis=1).astype(BF16)
    return w_main, w_dt


def kernel(x_prompt, x_sample, c_prompt, c_sample, w_ada, b_ada, w_in, conv_w, conv_b, dt_bias_fwd, dt_bias_bwd, a_log_fwd, a_log_bwd, d_skip, ssd_norm_w, w_branch_ssd, w_branch_attn, w_out, ln1_g, ln1_b, w_router, router_bias, w_exp_gate, w_exp_up, w_exp_down, w_sh_gate, w_sh_up, w_sh_down, ln2_g, ln2_b):
    assert w_ada.shape[0] == DEPTH
    groups = ((x_prompt, c_prompt), (x_sample, c_sample))
    shapes = [(x.shape[0], x.shape[1]) for x, _ in groups]
    seg = math.gcd(*(s for _, s in shapes))
    tokens = [b * s for b, s in shapes]
    t_all = sum(tokens)
    row_offs = [0, tokens[0]]

    c_all = jnp.concatenate([c_prompt, c_sample], axis=0)
    n_req = c_all.shape[0]
    c_pad = jnp.zeros((-(-n_req // SUBLANES) * SUBLANES, D_MODEL), F32).at[:n_req].set(c_all)
    ada = _ada(c_pad, w_ada[0], b_ada)[:n_req]
    req_offs = [0, shapes[0][0]]
    reps = jnp.array([s // seg for b, s in shapes for _ in range(b)], dtype=I32)
    n_seg = t_all // seg
    gate2 = jnp.repeat(ada[:, 5 * D_MODEL:], reps, axis=0, total_repeat_length=n_seg).reshape(n_seg, 1, D_MODEL)

    w_main, w_dt = _split_in_weights(w_in[0])
    dt_pad = jnp.zeros((DT_COLS - 2 * SSD_HEADS,), F32)
    dt_bias = jnp.concatenate([dt_bias_fwd[0], dt_bias_bwd[0], dt_pad]).reshape(1, DT_COLS)
    a_log = jnp.concatenate([a_log_fwd[0], a_log_bwd[0], dt_pad]).reshape(1, DT_COLS)
    d_skip_cols = jnp.repeat(d_skip[0], SSD_HEAD_DIM).reshape(1, D_INNER)
    w_bs = w_branch_ssd[0].astype(BF16)
    w_ba = w_branch_attn[0].astype(BF16)
    w_o = w_out[0].astype(BF16)
    w_router_t = w_router[0].T.astype(BF16)

    merged = None
    for (x, _), (batch, seq), row_off, req_off in zip(groups, shapes, row_offs, req_offs):
        shift1, scale1, gate1, shift2, scale2, _ = (
            a.reshape(batch, 1, D_MODEL) for a in jnp.split(ada[req_off:req_off + batch], 6, axis=-1))
        x2d = x.reshape(batch * seq, D_MODEL)
        proj = _inproj(x2d, scale1, shift1, w_main, seq, tm=1024, tn=1280, out_dtype=F32)
        dt_raw = _inproj(x2d, scale1, shift1, w_dt, seq, tm=1024, tn=DT_COLS, out_dtype=F32)
        xbc = _conv(proj, conv_w[0], conv_b, 0, batch, seq)
        y = _ssd(xbc, dt_raw, dt_bias, a_log, None, 0, batch, seq, reverse=False)
        y = _ssd(xbc, dt_raw, dt_bias, a_log, y, 0, batch, seq, reverse=True)
        y_ssd = _ssd_out(y, xbc, proj, d_skip_cols, ssd_norm_w, w_bs, 0)
        tables = _rope_tables(seq)
        attn = []
        for gi, (_, dil) in enumerate(ATTN_PATTERNS):
            q, k, v = _prep(proj, tables, gi, dil, 0, batch, seq)
            attn.append(_band_attn(q, k, v))
        merged = _merge(x2d, y_ssd, proj, attn, gate1, scale2, shift2, w_ba, w_o, ln1_g, ln1_b, w_router_t,
                        batch, seq, row_off, t_all, merged)
    x1, h2, logits_t = merged

    top_idx, top_w, rank, counts = _route(logits_t, router_bias[0].reshape(N_EXPERTS, 1))
    counts = counts[:, 0]
    padded = (counts + EXPERT_BLOCK - 1) // EXPERT_BLOCK * EXPERT_BLOCK
    pad_end = jnp.cumsum(padded)
    pad_start = pad_end - padded
    n_blocks = (t_all * TOP_K + N_EXPERTS * (EXPERT_BLOCK - 1)) // EXPERT_BLOCK
    n_rows = n_blocks * EXPERT_BLOCK
    tm_moe = 256
    dest_tiles = _dest(top_idx, rank, pad_start.astype(I32).reshape(N_EXPERTS, 1), tm_moe)
    n_valid = (pad_end[-1] // EXPERT_BLOCK).astype(I32).reshape(1)
    blk_start = jnp.minimum(jnp.arange(n_blocks, dtype=I32) * EXPERT_BLOCK, pad_end[-1] - 1)
    blk_exp = jnp.sum((pad_end[None, :] <= blk_start[:, None]).astype(I32), axis=1)
    blk_exp = jnp.minimum(blk_exp, N_EXPERTS - 1)

    xs = _dispatch(pad_end.astype(I32), dest_tiles, h2, n_rows, tm_moe)
    ys = _experts(blk_exp, n_valid, xs, w_exp_gate[0], w_exp_up[0], w_exp_down[0])
    outs = []
    for (x, _), n_tok, row_off in zip(groups, tokens, row_offs):
        out = _combine(dest_tiles, ys, top_w.T, x1, h2, gate2, w_sh_gate[0].astype(BF16), w_sh_up[0].astype(BF16),
                       w_sh_down[0].astype(BF16), ln2_g, ln2_b, seg, tm_moe, row_off // tm_moe, n_tok // tm_moe)
        outs.append(out.reshape(x.shape))
    return tuple(outs)
```

```python
import functools
import math

import jax
import jax.numpy as jnp
from jax import lax
from jax.experimental import pallas as pl
from jax.experimental.pallas import tpu as pltpu

F32 = jnp.float32
BF16 = jnp.bfloat16
I32 = jnp.int32

D_MODEL = 1024
D_INNER = 2048
SSD_HEADS = 32
SSD_HEAD_DIM = 64
SSD_GROUPS = 8
SSD_STATE = 128
SSD_CHUNK = 128
HEADS_PER_SSD_GROUP = SSD_HEADS // SSD_GROUPS
CONV_WIDTH = 5
CONV_PAD = CONV_WIDTH // 2
CONV_DIM = D_INNER + 2 * SSD_GROUPS * SSD_STATE
ATTN_PATTERNS = ((128, 1), (512, 4), (2048, 16))
ATTN_GROUPS = len(ATTN_PATTERNS)
ATTN_HEADS_PER_GROUP = 8
ATTN_HEAD_DIM = 64
ATTN_GROUP_COLS = ATTN_HEADS_PER_GROUP * ATTN_HEAD_DIM
ATTN_DIM = ATTN_GROUPS * ATTN_GROUP_COLS
ROT_DIM = ATTN_HEAD_DIM // 4
ROT_HALF = ROT_DIM // 2
ROPE_THETA = 500000.0
N_EXPERTS = 256
EXPERT_HIDDEN = 256
TOP_K = 8
N_EXPERT_GROUPS = 8
EXPERTS_PER_GROUP = N_EXPERTS // N_EXPERT_GROUPS
TOPK_GROUPS = 4
ROUTED_SCALE = 2.5
SHARED_HIDDEN = 256
EXPERT_BLOCK = 512
PACKED_COLS = D_MODEL // 2
ROW_SUBLANES = PACKED_COLS // 128
DEPTH = 1
ALPHA = (2.0 * DEPTH) ** 0.25
NORM_EPS = 1e-5

COL_Z = 0
COL_XBC = COL_Z + D_INNER
COL_GSSD = COL_XBC + CONV_DIM
COL_GATTN = COL_GSSD + D_MODEL
COL_Q = COL_GATTN + D_MODEL
COL_K = COL_Q + ATTN_DIM
COL_V = COL_K + ATTN_DIM
N_MAIN = COL_V + ATTN_DIM
DT_COLS = 128

LANES = 128
SUBLANES = 8
VMEM_LIMIT = 56 * 1024 * 1024

DMA_PRIORITIES = 2

BAND_RADIUS = 64
ATTN_TQ = 128
NEG_INF = float("-inf")


def _cparams(sem):
    return pltpu.CompilerParams(dimension_semantics=sem, vmem_limit_bytes=VMEM_LIMIT)


def _sigmoid(x):
    return 1.0 / (1.0 + jnp.exp(-x))


def _silu(x):
    return x * _sigmoid(x)


def _softplus(x):
    return jnp.maximum(x, 0.0) + jnp.log(1.0 + jnp.exp(-jnp.abs(x)))


def _pack_halves(x):
    c = x.shape[1] // 2
    hi = lax.bitcast_convert_type(x[:, :c].astype(BF16).astype(F32), jnp.uint32)
    lo = lax.bitcast_convert_type(x[:, c:].astype(BF16).astype(F32), jnp.uint32)
    return hi | (lo >> 16)


def _unpack_halves(u):
    left = lax.bitcast_convert_type(u & jnp.uint32(0xFFFF0000), F32)
    right = lax.bitcast_convert_type(u << 16, F32)
    return left, right


def _store_rows(ref, lead, x):
    rows = x.shape[0]
    packed = _pack_halves(x)
    for j in range(ROW_SUBLANES):
        idx = lead + (pl.ds(j, rows, stride=ROW_SUBLANES), slice(None))
        ref[idx] = packed[:, j * LANES:(j + 1) * LANES]


def _load_row_pieces(ref, lead, rows):
    lefts, rights = [], []
    for j in range(ROW_SUBLANES):
        idx = lead + (pl.ds(j, rows, stride=ROW_SUBLANES), slice(None))
        left, right = _unpack_halves(ref[idx])
        lefts.append(left)
        rights.append(right)
    return lefts + rights


def _ada_kernel(c_ref, w_ref, b_ref, o_ref):
    c = _silu(c_ref[...]).astype(BF16)
    o_ref[...] = jnp.dot(c, w_ref[...].astype(BF16), preferred_element_type=F32) + b_ref[...]


def _ada(c_pad, w_ada, b_ada):
    rows = c_pad.shape[0]
    n = w_ada.shape[1]
    tn = 1536
    return pl.pallas_call(
        _ada_kernel,
        grid=(n // tn,),
        in_specs=[pl.BlockSpec((rows, D_MODEL), lambda j: (0, 0)),
                  pl.BlockSpec((D_MODEL, tn), lambda j: (0, j)),
                  pl.BlockSpec((1, tn), lambda j: (0, j))],
        out_specs=pl.BlockSpec((rows, tn), lambda j: (0, j)),
        out_shape=jax.ShapeDtypeStruct((rows, n), F32),
        compiler_params=_cparams(("arbitrary",)),
        name="ada",
    )(c_pad, w_ada, b_ada)


def _inproj_kernel(x_ref, sc_ref, sh_ref, w_ref, o_ref, h_ref):
    @pl.when(pl.program_id(1) == 0)
    def _():
        h_ref[...] = (x_ref[...] * (1.0 + sc_ref[0]) + sh_ref[0]).astype(BF16)

    o_ref[...] = jnp.dot(h_ref[...], w_ref[...], preferred_element_type=F32).astype(o_ref.dtype)


def _inproj(x, scale, shift, w, seg, tm, tn, out_dtype):
    t = x.shape[0]
    n = w.shape[1]
    per_seg = seg // tm
    return pl.pallas_call(
        _inproj_kernel,
        grid=(t // tm, n // tn),
        in_specs=[pl.BlockSpec((tm, D_MODEL), lambda i, j: (i, 0)),
                  pl.BlockSpec((1, 1, D_MODEL), lambda i, j: (i // per_seg, 0, 0)),
                  pl.BlockSpec((1, 1, D_MODEL), lambda i, j: (i // per_seg, 0, 0)),
                  pl.BlockSpec((D_MODEL, tn), lambda i, j: (0, j))],
        out_specs=pl.BlockSpec((tm, tn), lambda i, j: (i, j)),
        out_shape=jax.ShapeDtypeStruct((t, n), out_dtype),
        scratch_shapes=[pltpu.VMEM((tm, D_MODEL), BF16)],
        compiler_params=_cparams(("arbitrary", "arbitrary")),
        name="inproj",
    )(x, scale, shift, w)


def _conv_kernel(xp_ref, x_ref, xn_ref, w_ref, b_ref, o_ref, *, n_l):
    l = pl.program_id(1)
    tl = x_ref.shape[0]
    xp = jnp.where(l > 0, xp_ref[...], 0.0)
    xn = jnp.where(l < n_l - 1, xn_ref[...], 0.0)
    xx = jnp.concatenate([xp, x_ref[...], xn], axis=0)
    acc = b_ref[...] + w_ref[CONV_PAD:CONV_PAD + 1, :] * x_ref[...]
    for k in range(CONV_WIDTH):
        if k != CONV_PAD:
            shifted = pltpu.roll(xx, (CONV_PAD - k) % xx.shape[0], axis=0)
            acc = acc + w_ref[k:k + 1, :] * shifted[SUBLANES:SUBLANES + tl, :]
    o_ref[...] = _silu(acc)


def _conv(proj, conv_w, conv_b, row_off, batch, seq, tl=512, tc=512):
    t = proj.shape[0]
    n_l = seq // tl
    n_c = CONV_DIM // tc
    c0 = COL_XBC // tc
    r8 = tl // SUBLANES
    last8 = t // SUBLANES - 1

    def row_blk(b, l):
        return (row_off + b * seq) // tl + l

    return pl.pallas_call(
        functools.partial(_conv_kernel, n_l=n_l),
        grid=(batch, n_l, n_c),
        in_specs=[pl.BlockSpec((SUBLANES, tc), lambda b, l, c: (jnp.maximum(row_blk(b, l) * r8 - 1, 0), c0 + c)),
                  pl.BlockSpec((tl, tc), lambda b, l, c: (row_blk(b, l), c0 + c)),
                  pl.BlockSpec((SUBLANES, tc), lambda b, l, c: (jnp.minimum((row_blk(b, l) + 1) * r8, last8), c0 + c)),
                  pl.BlockSpec((CONV_WIDTH, tc), lambda b, l, c: (0, c)),
                  pl.BlockSpec((1, tc), lambda b, l, c: (0, c))],
        out_specs=pl.BlockSpec((tl, tc), lambda b, l, c: (b * n_l + l, c)),
        out_shape=jax.ShapeDtypeStruct((batch * seq, CONV_DIM), F32),
        compiler_params=_cparams(("arbitrary", "arbitrary", "arbitrary")),
        name="conv",
    )(proj, proj, proj, conv_w, conv_b)


def _exact_tri_matmul(tri_bf16, x):
    hi = x.astype(BF16)
    r1 = x - hi.astype(F32)
    mid = r1.astype(BF16)
    lo = (r1 - mid.astype(F32)).astype(BF16)
    out = jnp.dot(tri_bf16, hi, preferred_element_type=F32)
    out = out + jnp.dot(tri_bf16, mid, preferred_element_type=F32)
    return out + jnp.dot(tri_bf16, lo, preferred_element_type=F32)


def _ssd_kernel(*refs, reverse, accumulate):
    if accumulate:
        x_ref, b_ref, c_ref, dt_ref, dtb_ref, alog_ref, yin_ref, y_ref, state_ref = refs
    else:
        x_ref, b_ref, c_ref, dt_ref, dtb_ref, alog_ref, y_ref, state_ref = refs
        yin_ref = None
    lane0 = SSD_HEADS if reverse else 0

    @pl.when(pl.program_id(1) == 0)
    def _():
        state_ref[...] = jnp.zeros(state_ref.shape, F32)

    dt = _softplus(dt_ref[...] + dtb_ref[...])
    da = dt * (-jnp.exp(alog_ref[...]))
    row = lax.broadcasted_iota(I32, (SSD_CHUNK, SSD_CHUNK), 0)
    col = lax.broadcasted_iota(I32, (SSD_CHUNK, SSD_CHUNK), 1)
    tri = (col >= row) if reverse else (col <= row)
    cum = _exact_tri_matmul(tri.astype(BF16), da)
    cum_t = cum.T
    dt_t = dt.T
    edge = 0 if reverse else SSD_CHUNK - 1
    total_b = jnp.broadcast_to(cum_t[:, edge:edge + 1], (LANES, SSD_CHUNK))
    to_end_t = dt_t * jnp.exp(total_b - cum_t)
    chunk_decay_b = jnp.exp(total_b)
    src_t = cum_t - jnp.log(dt_t)
    first_head = lax.broadcasted_iota(I32, (1, 2 * SSD_HEAD_DIM), 1) < SSD_HEAD_DIM
    pairs = HEADS_PER_SSD_GROUP // 2

    for g in range(SSD_GROUPS):
        bg = b_ref[:, g * SSD_STATE:(g + 1) * SSD_STATE]
        cg = c_ref[:, g * SSD_STATE:(g + 1) * SSD_STATE]
        cb = lax.dot_general(cg.astype(BF16), bg.astype(BF16), (((1,), (1,)), ((), ())),
                             preferred_element_type=F32)
        bg_t = bg.T
        for pr in range(pairs):
            h0 = g * HEADS_PER_SSD_GROUP + 2 * pr
            cols = slice(h0 * SSD_HEAD_DIM, (h0 + 2) * SSD_HEAD_DIM)
            x_pair = x_ref[:, cols].astype(BF16)
            state = state_ref[g * pairs + pr]
            rhs = jnp.concatenate([x_pair, state.astype(BF16)], axis=0)
            res = []
            new = []
            for e in range(2):
                ln = lane0 + h0 + e
                cum_l = jnp.broadcast_to(cum[:, ln:ln + 1], (SSD_CHUNK, SSD_CHUNK))
                decay = jnp.exp(jnp.where(tri, cum_l - src_t[ln:ln + 1, :], NEG_INF))
                lhs = jnp.concatenate([(cb * decay).astype(BF16), (cg * jnp.exp(cum_l)).astype(BF16)], axis=1)
                res.append(jnp.dot(lhs, rhs, preferred_element_type=F32))
                b_scaled = (bg_t * to_end_t[ln:ln + 1, :]).astype(BF16)
                new.append(jnp.dot(b_scaled, x_pair, preferred_element_type=F32))
            y_pair = jnp.where(first_head, res[0], res[1])
            if accumulate:
                y_pair = y_pair + yin_ref[:, cols]
            y_ref[:, cols] = y_pair
            ln0 = lane0 + h0
            dec = jnp.where(first_head, chunk_decay_b[ln0:ln0 + 1, :], chunk_decay_b[ln0 + 1:ln0 + 2, :])
            state_ref[g * pairs + pr] = state * dec + jnp.where(first_head, new[0], new[1])


def _ssd(xbc, dt_raw, dt_bias, a_log, y_in, row_off, batch, seq, reverse):
    n_c = seq // SSD_CHUNK
    accumulate = y_in is not None

    def loc(b, c):
        return b * n_c + (n_c - 1 - c if reverse else c)

    def glob(b, c):
        return row_off // SSD_CHUNK + loc(b, c)

    in_specs = [pl.BlockSpec((SSD_CHUNK, D_INNER), lambda b, c: (loc(b, c), 0)),
                pl.BlockSpec((SSD_CHUNK, SSD_GROUPS * SSD_STATE), lambda b, c: (loc(b, c), 2)),
                pl.BlockSpec((SSD_CHUNK, SSD_GROUPS * SSD_STATE), lambda b, c: (loc(b, c), 3)),
                pl.BlockSpec((SSD_CHUNK, DT_COLS), lambda b, c: (glob(b, c), 0)),
                pl.BlockSpec((1, DT_COLS), lambda b, c: (0, 0)),
                pl.BlockSpec((1, DT_COLS), lambda b, c: (0, 0))]
    args = [xbc, xbc, xbc, dt_raw, dt_bias, a_log]
    aliases = {}
    if accumulate:
        in_specs.append(pl.BlockSpec((SSD_CHUNK, D_INNER), lambda b, c: (loc(b, c), 0)))
        args.append(y_in)
        aliases = {len(args) - 1: 0}
    return pl.pallas_call(
        functools.partial(_ssd_kernel, reverse=reverse, accumulate=accumulate),
        grid=(batch, n_c),
        in_specs=in_specs,
        out_specs=pl.BlockSpec((SSD_CHUNK, D_INNER), lambda b, c: (loc(b, c), 0)),
        out_shape=jax.ShapeDtypeStruct((batch * seq, D_INNER), F32),
        scratch_shapes=[pltpu.VMEM((SSD_HEADS // 2, SSD_STATE, 2 * SSD_HEAD_DIM), F32)],
        input_output_aliases=aliases,
        compiler_params=_cparams(("arbitrary", "arbitrary")),
        name="ssd_bwd" if reverse else "ssd_fwd",
    )(*args)


def _ssd_out_kernel(y_ref, x_ref, z_ref, dskip_ref, nw_ref, w_ref, o_ref):
    y = y_ref[...] + dskip_ref[...] * x_ref[...]
    y = y * _silu(z_ref[...])
    ms = jnp.mean(y * y, axis=-1, keepdims=True)
    yn = y * lax.rsqrt(ms + NORM_EPS) * nw_ref[...]
    o_ref[...] = jnp.dot(yn.astype(BF16), w_ref[...], preferred_element_type=F32)


def _ssd_out(y, xbc, proj, d_skip_cols, norm_w, w_branch, row_off, tm=256):
    t = y.shape[0]
    return pl.pallas_call(
        _ssd_out_kernel,
        grid=(t // tm,),
        in_specs=[pl.BlockSpec((tm, D_INNER), lambda i: (i, 0)),
                  pl.BlockSpec((tm, D_INNER), lambda i: (i, 0)),
                  pl.BlockSpec((tm, D_INNER), lambda i: (row_off // tm + i, COL_Z // D_INNER)),
                  pl.BlockSpec((1, D_INNER), lambda i: (0, 0)),
                  pl.BlockSpec((1, D_INNER), lambda i: (0, 0)),
                  pl.BlockSpec((D_INNER, D_MODEL), lambda i: (0, 0))],
        out_specs=pl.BlockSpec((tm, D_MODEL), lambda i: (i, 0)),
        out_shape=jax.ShapeDtypeStruct((t, D_MODEL), F32),
        compiler_params=_cparams(("arbitrary",)),
        name="ssd_out",
    )(y, xbc, proj, d_skip_cols, norm_w, w_branch)


def _prep_kernel(q_ref, k_ref, v_ref, cos_ref, sa_ref, sb_ref, qo_ref, ko_ref, vo_ref, *, dil):
    rows = q_ref.shape[0] // dil

    def phase(ref, p):
        if dil == 1:
            return ref[...]
        return ref[pl.ds(p, rows, stride=dil), :]

    for p in range(dil):
        cos = phase(cos_ref, p)
        sa = phase(sa_ref, p)
        sb = phase(sb_ref, p)

        def rot(t):
            return (t * cos + pltpu.roll(t, LANES - ROT_HALF, axis=1) * sa
                    + pltpu.roll(t, ROT_HALF, axis=1) * sb)

        qo_ref[0, p] = rot(phase(q_ref, p)) * (ATTN_HEAD_DIM ** -0.5)
        ko_ref[0, p] = rot(phase(k_ref, p))
        vo_ref[0, p] = phase(v_ref, p)


def _prep(proj, tables, gi, dil, row_off, batch, seq, rows_per_step=2048):
    r = rows_per_step
    n_i = seq // r
    n_c = ATTN_GROUP_COLS // LANES
    m = seq // dil
    cos_t, sa_t, sb_t = tables

    def in_blk(col):
        cb = (col + gi * ATTN_GROUP_COLS) // LANES
        return pl.BlockSpec((r, LANES), lambda b, i, c: ((row_off + b * seq) // r + i, cb + c))

    tab = pl.BlockSpec((r, LANES), lambda b, i, c: (i, 0))
    out = pl.BlockSpec((1, dil, r // dil, LANES), lambda b, i, c: (b, 0, i, c))
    shape = jax.ShapeDtypeStruct((batch, dil, m, ATTN_GROUP_COLS), F32)
    return pl.pallas_call(
        functools.partial(_prep_kernel, dil=dil),
        grid=(batch, n_i, n_c),
        in_specs=[in_blk(COL_Q), in_blk(COL_K), in_blk(COL_V), tab, tab, tab],
        out_specs=[out, out, out],
        out_shape=[shape, shape, shape],
        compiler_params=_cparams(("arbitrary", "arbitrary", "arbitrary")),
        name=f"attn_prep_d{dil}",
    )(proj, proj, proj, cos_t, sa_t, sb_t)


def _band_attn_kernel(q_ref, kp_ref, kc_ref, kn_ref, vp_ref, vc_ref, vn_ref, o_ref, lse_ref, *, m_len):
    i = pl.program_id(1)
    tq = q_ref.shape[1]
    tk = tq + 2 * BAND_RADIUS
    q = q_ref[0].astype(BF16)
    k = jnp.concatenate([kp_ref[0], kc_ref[0], kn_ref[0]], axis=0).astype(BF16)
    v = jnp.concatenate([vp_ref[0], vc_ref[0], vn_ref[0]], axis=0).astype(BF16)
    qpos = i * tq + lax.broadcasted_iota(I32, (tq, tk), 0)
    kpos = i * tq - BAND_RADIUS + lax.broadcasted_iota(I32, (tq, tk), 1)
    valid = (jnp.abs(qpos - kpos) <= BAND_RADIUS) & (kpos >= 0) & (kpos < m_len)
    first_head = lax.broadcasted_iota(I32, (1, LANES), 1) < ATTN_HEAD_DIM
    ones = jnp.ones((tk, LANES), BF16)
    zero = jnp.zeros((), BF16)
    for pr in range(ATTN_GROUP_COLS // LANES):
        cols = slice(pr * LANES, (pr + 1) * LANES)
        q_pair, k_pair = q[:, cols], k[:, cols]
        v_ext = jnp.concatenate([v[:, cols], ones], axis=1)
        outs, lses = [], []
        for sel in (first_head, jnp.logical_not(first_head)):
            s = lax.dot_general(jnp.where(sel, q_pair, zero), k_pair, (((1,), (1,)), ((), ())),
                                preferred_element_type=F32)
            s = jnp.where(valid, s, NEG_INF)
            mx = jnp.max(s, axis=-1, keepdims=True)
            p = jnp.exp(s - mx).astype(BF16)
            ov = jnp.dot(p, v_ext, preferred_element_type=F32)
            den = ov[:, LANES:]
            outs.append(ov[:, :LANES] / den)
            lses.append(mx + jnp.log(den))
        o_ref[0, :, cols] = jnp.where(first_head, outs[0], outs[1])
        lse_ref[0, :, cols] = jnp.where(first_head, lses[0], lses[1])


def _band_attn(q, k, v):
    b, d, m, c = q.shape
    n = b * d
    q, k, v = (a.reshape(n, m, c) for a in (q, k, v))
    tq = ATTN_TQ
    per = tq // BAND_RADIUS
    last = m // BAND_RADIUS - 1
    main = pl.BlockSpec((1, tq, c), lambda s, i: (s, i, 0))
    prev = pl.BlockSpec((1, BAND_RADIUS, c), lambda s, i: (s, jnp.maximum(i * per - 1, 0), 0))
    nxt = pl.BlockSpec((1, BAND_RADIUS, c), lambda s, i: (s, jnp.minimum((i + 1) * per, last), 0))
    shape = jax.ShapeDtypeStruct((n, m, c), F32)
    o, lse = pl.pallas_call(
        functools.partial(_band_attn_kernel, m_len=m),
        grid=(n, m // tq),
        in_specs=[main, prev, main, nxt, prev, main, nxt],
        out_specs=[main, main],
        out_shape=[shape, shape],
        compiler_params=_cparams(("arbitrary", "arbitrary")),
        name=f"band_attn_m{m}",
    )(q, k, k, k, v, v, v)
    return o.reshape(b, d, m, c), lse.reshape(b, d, m, c)


def _layer_norm(x, g, b):
    mu = jnp.mean(x, axis=-1, keepdims=True)
    xc = x - mu
    var = jnp.mean(xc * xc, axis=-1, keepdims=True)
    return xc * lax.rsqrt(var + NORM_EPS) * g + b


def _merge_kernel(x_ref, yssd_ref, gs_ref, ga_ref, o1_ref, l1_ref, o2_ref, l2_ref, o3_ref, l3_ref,
                  g1_ref, sc2_ref, sh2_ref, wba_ref, wout_ref, lng_ref, lnb_ref, wr_ref, *rest, n_prev):
    x1_ref, h2_ref, logit_ref, scr_ref = rest[n_prev:]
    tm = x_ref.shape[0]

    def interleaved(ref, dil, slot):
        if dil == 1:
            return ref[0, 0]
        rows = tm // dil
        n_c = ATTN_GROUP_COLS // LANES
        for p in range(dil):
            for c in range(n_c):
                scr_ref[slot * n_c + c, pl.ds(p, rows, stride=dil), :] = ref[0, p, :, c * LANES:(c + 1) * LANES]
        return jnp.concatenate([scr_ref[slot * n_c + c] for c in range(n_c)], axis=1)

    outs = []
    lses = []
    slot = 0
    for (o_ref, l_ref), (_, dil) in zip(((o1_ref, l1_ref), (o2_ref, l2_ref), (o3_ref, l3_ref)), ATTN_PATTERNS):
        outs.append(interleaved(o_ref, dil, slot))
        lses.append(interleaved(l_ref, dil, slot + 1))
        slot += 2
    mx = jnp.maximum(jnp.maximum(lses[0], lses[1]), lses[2])
    es = [jnp.exp(l - mx) for l in lses]
    den = es[0] + es[1] + es[2]
    y_att = (es[0] * outs[0] + es[1] * outs[1] + es[2] * outs[2]) / den
    y_attn = jnp.dot(y_att.astype(BF16), wba_ref[...], preferred_element_type=F32)
    merged = _sigmoid(gs_ref[...]) * yssd_ref[...] + _sigmoid(ga_ref[...]) * y_attn
    mix = jnp.dot(merged.astype(BF16), wout_ref[...], preferred_element_type=F32)
    x1 = _layer_norm(ALPHA * x_ref[...] + g1_ref[0] * mix, lng_ref[...], lnb_ref[...])
    x1_ref[...] = x1
    h2 = x1 * (1.0 + sc2_ref[0]) + sh2_ref[0]
    _store_rows(h2_ref, (), h2)
    logit_ref[...] = lax.dot_general(wr_ref[...], h2.astype(BF16), (((1,), (1,)), ((), ())),
                                     preferred_element_type=F32)


def _merge(x, y_ssd, proj, attn, gate1, scale2, shift2, w_ba, w_out, ln_g, ln_b, w_router_t,
           batch, seq, out_row_off, t_all, prev, tm=256):
    n_i = seq // tm
    out0 = out_row_off // tm

    def rows(col_blk):
        return pl.BlockSpec((tm, D_MODEL), lambda b, i: (b * n_i + i, col_blk))

    def local(width):
        return pl.BlockSpec((tm, width), lambda b, i: (b * n_i + i, 0))

    def modv():
        return pl.BlockSpec((1, 1, D_MODEL), lambda b, i: (b, 0, 0))

    def full(shape):
        return pl.BlockSpec(shape, lambda b, i: tuple(0 for _ in shape))

    attn_specs = []
    attn_args = []
    for (o, lse), (_, dil) in zip(attn, ATTN_PATTERNS):
        spec = pl.BlockSpec((1, dil, tm // dil, ATTN_GROUP_COLS), lambda b, i: (b, 0, i, 0))
        attn_specs += [spec, spec]
        attn_args += [o, lse]
    n_in = 4 + len(attn_args) + 8
    return pl.pallas_call(
        functools.partial(_merge_kernel, n_prev=0 if prev is None else 3),
        grid=(batch, n_i),
        in_specs=[rows(0), local(D_MODEL), rows(COL_GSSD // D_MODEL), rows(COL_GATTN // D_MODEL)] + attn_specs
                 + [modv(), modv(), modv(), full((ATTN_GROUP_COLS, D_MODEL)), full((D_MODEL, D_MODEL)),
                    full((1, D_MODEL)), full((1, D_MODEL)), full((N_EXPERTS, D_MODEL))]
                 + ([pl.BlockSpec(memory_space=pl.ANY)] * 3 if prev is not None else []),
        out_specs=[pl.BlockSpec((tm, D_MODEL), lambda b, i: (out0 + b * n_i + i, 0)),
                   pl.BlockSpec((tm * ROW_SUBLANES, LANES), lambda b, i: (out0 + b * n_i + i, 0)),
                   pl.BlockSpec((N_EXPERTS, tm), lambda b, i: (0, out0 + b * n_i + i))],
        out_shape=[jax.ShapeDtypeStruct((t_all, D_MODEL), F32),
                   jax.ShapeDtypeStruct((t_all * ROW_SUBLANES, LANES), jnp.uint32),
                   jax.ShapeDtypeStruct((N_EXPERTS, t_all), F32)],
        scratch_shapes=[pltpu.VMEM((2 * ATTN_GROUPS * (ATTN_GROUP_COLS // LANES), tm, LANES), F32)],
        input_output_aliases={n_in + j: j for j in range(3)} if prev is not None else {},
        compiler_params=_cparams(("arbitrary", "arbitrary")),
        name="mixer_merge",
    )(x, y_ssd, proj, proj, *attn_args, gate1, scale2, shift2, w_ba, w_out, ln_g, ln_b, w_router_t,
      *(prev if prev is not None else ()))


def _route_kernel(logit_ref, bias_ref, idx_ref, w_ref, rank_ref, cnt_ref, run_ref):
    tm = logit_ref.shape[1]

    @pl.when(pl.program_id(0) == 0)
    def _():
        run_ref[...] = jnp.zeros(run_ref.shape, F32)

    scores = _sigmoid(logit_ref[...])
    sel = scores + bias_ref[...]
    iota_g = lax.broadcasted_iota(I32, (EXPERTS_PER_GROUP, tm), 0)
    grp = []
    for g in range(N_EXPERT_GROUPS):
        sg = sel[g * EXPERTS_PER_GROUP:(g + 1) * EXPERTS_PER_GROUP, :]
        top1 = jnp.max(sg, axis=0, keepdims=True)
        first = jnp.min(jnp.where(sg == top1, iota_g, EXPERTS_PER_GROUP), axis=0, keepdims=True)
        top2 = jnp.max(jnp.where(iota_g == first, NEG_INF, sg), axis=0, keepdims=True)
        grp.append(top1 + top2)
    gs = jnp.concatenate(grp, axis=0)
    iota_n = lax.broadcasted_iota(I32, (N_EXPERT_GROUPS, tm), 0)
    keep_g = jnp.zeros((N_EXPERT_GROUPS, tm), F32)
    for _ in range(TOPK_GROUPS):
        best = jnp.max(gs, axis=0, keepdims=True)
        first = jnp.min(jnp.where(gs == best, iota_n, N_EXPERT_GROUPS), axis=0, keepdims=True)
        hit = iota_n == first
        keep_g = jnp.where(hit, 1.0, keep_g)
        gs = jnp.where(hit, NEG_INF, gs)
    cand = jnp.concatenate(
        [jnp.where(keep_g[g:g + 1, :] > 0.0, sel[g * EXPERTS_PER_GROUP:(g + 1) * EXPERTS_PER_GROUP, :], NEG_INF)
         for g in range(N_EXPERT_GROUPS)], axis=0)
    iota_e = lax.broadcasted_iota(I32, (N_EXPERTS, tm), 0)
    chosen = jnp.zeros((N_EXPERTS, tm), F32)
    idxs = []
    ws = []
    for _ in range(TOP_K):
        best = jnp.max(cand, axis=0, keepdims=True)
        first = jnp.min(jnp.where(cand == best, iota_e, N_EXPERTS), axis=0, keepdims=True)
        hit = iota_e == first
        idxs.append(first)
        ws.append(jnp.sum(jnp.where(hit, scores, 0.0), axis=0, keepdims=True))
        chosen = jnp.where(hit, 1.0, chosen)
        cand = jnp.where(hit, NEG_INF, cand)
    top_w = jnp.concatenate(ws, axis=0)
    top_w = top_w / jnp.sum(top_w, axis=0, keepdims=True) * ROUTED_SCALE
    idx_ref[...] = jnp.concatenate(idxs, axis=0)
    w_ref[...] = top_w
    s_i = lax.broadcasted_iota(I32, (tm, tm), 0)
    t_i = lax.broadcasted_iota(I32, (tm, tm), 1)
    before = (s_i < t_i).astype(BF16)
    prior = jnp.dot(chosen.astype(BF16), before, preferred_element_type=F32) + run_ref[...]
    ranks = [jnp.sum(jnp.where(iota_e == idxs[k], prior, 0.0), axis=0, keepdims=True) for k in range(TOP_K)]
    rank_ref[...] = jnp.concatenate(ranks, axis=0).astype(I32)
    run_ref[...] = run_ref[...] + jnp.sum(chosen, axis=1, keepdims=True)
    cnt_ref[...] = run_ref[...].astype(I32)


def _route(logits_t, bias_col, tm=256):
    t = logits_t.shape[1]
    tok = pl.BlockSpec((TOP_K, tm), lambda i: (0, i))
    return pl.pallas_call(
        _route_kernel,
        grid=(t // tm,),
        in_specs=[pl.BlockSpec((N_EXPERTS, tm), lambda i: (0, i)),
                  pl.BlockSpec((N_EXPERTS, 1), lambda i: (0, 0))],
        out_specs=[tok, tok, tok, pl.BlockSpec((N_EXPERTS, 1), lambda i: (0, 0))],
        out_shape=[jax.ShapeDtypeStruct((TOP_K, t), I32), jax.ShapeDtypeStruct((TOP_K, t), F32),
                   jax.ShapeDtypeStruct((TOP_K, t), I32), jax.ShapeDtypeStruct((N_EXPERTS, 1), I32)],
        scratch_shapes=[pltpu.VMEM((N_EXPERTS, 1), F32)],
        compiler_params=_cparams(("arbitrary",)),
        name="route",
    )(logits_t, bias_col)


def _dest_kernel(idx_ref, rank_ref, start_ref, o_ref):
    tm = idx_ref.shape[1]
    iota_e = lax.broadcasted_iota(I32, (N_EXPERTS, tm), 0)
    start = start_ref[...].astype(F32)
    for k in range(TOP_K):
        base = jnp.sum(jnp.where(iota_e == idx_ref[k:k + 1, :], start, 0.0), axis=0, keepdims=True)
        o_ref[0, :, k * tm:(k + 1) * tm] = (base.astype(I32) + rank_ref[k:k + 1, :]) * ROW_SUBLANES


def _dest(top_idx, rank, pad_start_col, tm):
    t = top_idx.shape[1]
    tok = pl.BlockSpec((TOP_K, tm), lambda i: (0, i))
    out = pl.pallas_call(
        _dest_kernel,
        grid=(t // tm,),
        in_specs=[tok, tok, pl.BlockSpec((N_EXPERTS, 1), lambda i: (0, 0))],
        out_specs=pl.BlockSpec((1, 1, TOP_K * tm), lambda i: (i, 0, 0)),
        out_shape=jax.ShapeDtypeStruct((t // tm, 1, TOP_K * tm), I32),
        compiler_params=_cparams(("arbitrary",)),
        name="moe_dest",
    )(top_idx, rank, pad_start_col)
    return out.reshape(t // tm, TOP_K * tm)


def _dispatch_kernel(pad_end_ref, dest_hbm, h_ref, xs_hbm, dest_smem, zeros_ref, sem_idx, sem_zero, sem_row,
                     *, tm):
    i = pl.program_id(0)
    idx_copy = pltpu.make_async_copy(dest_hbm.at[i], dest_smem, sem_idx)
    idx_copy.start()

    blk_rows = EXPERT_BLOCK * ROW_SUBLANES

    def zero_copy(e):
        start = pl.multiple_of(jnp.maximum(pad_end_ref[e] - EXPERT_BLOCK, 0) * ROW_SUBLANES, blk_rows)
        return pltpu.make_async_copy(zeros_ref, xs_hbm.at[pl.ds(start, blk_rows)], sem_zero)

    @pl.when(i == 0)
    def _():
        zeros_ref[...] = jnp.zeros(zeros_ref.shape, zeros_ref.dtype)

        def start_zero(e, carry):
            zero_copy(e).start()
            return carry

        def wait_zero(e, carry):
            zero_copy(e).wait()
            return carry

        lax.fori_loop(0, N_EXPERTS, start_zero, 0)
        lax.fori_loop(0, N_EXPERTS, wait_zero, 0)

    idx_copy.wait()

    def row_copy(t, k):
        dst = pl.multiple_of(dest_smem[k * tm + t], ROW_SUBLANES)
        return pltpu.make_async_copy(h_ref.at[pl.ds(pl.multiple_of(t * ROW_SUBLANES, ROW_SUBLANES), ROW_SUBLANES)],
                                     xs_hbm.at[pl.ds(dst, ROW_SUBLANES)], sem_row)

    def start_rows(t, carry):
        for k in range(TOP_K):
            row_copy(t, k).start(priority=k % DMA_PRIORITIES)
        return carry

    def wait_rows(t, carry):
        for k in range(TOP_K):
            row_copy(t, k).wait()
        return carry

    lax.fori_loop(0, tm, start_rows, 0)
    lax.fori_loop(0, tm, wait_rows, 0)


def _dispatch(pad_end, dest_tiles, h2, n_rows, tm):
    n_tiles = dest_tiles.shape[0]
    return pl.pallas_call(
        functools.partial(_dispatch_kernel, tm=tm),
        grid_spec=pltpu.PrefetchScalarGridSpec(
            num_scalar_prefetch=1,
            grid=(n_tiles,),
            in_specs=[pl.BlockSpec(memory_space=pl.ANY),
                      pl.BlockSpec((tm * ROW_SUBLANES, LANES), lambda i, pe: (i, 0))],
            out_specs=pl.BlockSpec(memory_space=pl.ANY),
            scratch_shapes=[pltpu.SMEM((TOP_K * tm,), I32),
                            pltpu.VMEM((EXPERT_BLOCK * ROW_SUBLANES, LANES), jnp.uint32),
                            pltpu.SemaphoreType.DMA, pltpu.SemaphoreType.DMA, pltpu.SemaphoreType.DMA]),
        out_shape=jax.ShapeDtypeStruct((n_rows * ROW_SUBLANES, LANES), jnp.uint32),
        compiler_params=_cparams(("arbitrary",)),
        name="moe_dispatch",
    )(pad_end, dest_tiles, h2)


def _experts_kernel(blk_exp_ref, n_valid_ref, xs_ref, wg_ref, wu_ref, wd_ref, ys_ref, wgu_ref, wdn_ref):
    i = pl.program_id(0)

    @pl.when(i < n_valid_ref[0])
    def _():
        @pl.when((i == 0) | (blk_exp_ref[i] != blk_exp_ref[jnp.maximum(i - 1, 0)]))
        def _():
            wgu_ref[:, :EXPERT_HIDDEN] = wg_ref[0].astype(BF16)
            wgu_ref[:, EXPERT_HIDDEN:] = wu_ref[0].astype(BF16)
            wdn_ref[...] = wd_ref[0].astype(BF16)

        x = jnp.concatenate([c.astype(BF16) for c in _load_row_pieces(xs_ref, (), EXPERT_BLOCK)], axis=1)
        gu = jnp.dot(x, wgu_ref[...], preferred_element_type=F32)
        hmid = (_silu(gu[:, :EXPERT_HIDDEN]) * gu[:, EXPERT_HIDDEN:]).astype(BF16)
        _store_rows(ys_ref, (), jnp.dot(hmid, wdn_ref[...], preferred_element_type=F32))


def _experts(blk_exp, n_valid, xs, w_gate, w_up, w_down):
    n_blocks = xs.shape[0] // (EXPERT_BLOCK * ROW_SUBLANES)

    def row_map(i, be, nv):
        return (jnp.minimum(i, nv[0] - 1), 0)

    def w_map(i, be, nv):
        return (be[i], 0, 0)

    return pl.pallas_call(
        _experts_kernel,
        grid_spec=pltpu.PrefetchScalarGridSpec(
            num_scalar_prefetch=2,
            grid=(n_blocks,),
            in_specs=[pl.BlockSpec((EXPERT_BLOCK * ROW_SUBLANES, LANES), row_map),
                      pl.BlockSpec((1, D_MODEL, EXPERT_HIDDEN), w_map),
                      pl.BlockSpec((1, D_MODEL, EXPERT_HIDDEN), w_map),
                      pl.BlockSpec((1, EXPERT_HIDDEN, D_MODEL), w_map)],
            out_specs=pl.BlockSpec((EXPERT_BLOCK * ROW_SUBLANES, LANES), row_map),
            scratch_shapes=[pltpu.VMEM((D_MODEL, 2 * EXPERT_HIDDEN), BF16),
                            pltpu.VMEM((EXPERT_HIDDEN, D_MODEL), BF16)]),
        out_shape=jax.ShapeDtypeStruct(xs.shape, jnp.uint32),
        compiler_params=_cparams(("arbitrary",)),
        name="moe_experts",
    )(blk_exp, n_valid, xs, w_gate, w_up, w_down)


def _combine_kernel(dest_hbm, ys_hbm, w_ref, x1_ref, h2_ref, g2_ref, wsg_ref, wsu_ref, wsd_ref,
                    lng_ref, lnb_ref, o_ref, dest_smem, rows_ref, sem_idx, sem_row, *, tm, tile_off):
    i = pl.program_id(0)
    n_i = pl.num_programs(0)
    slot = lax.rem(i, 2)

    def row_copy(s, t, k):
        src = pl.multiple_of(dest_smem[s, k * tm + t], ROW_SUBLANES)
        dst = pl.multiple_of(t * ROW_SUBLANES, ROW_SUBLANES)
        return pltpu.make_async_copy(ys_hbm.at[pl.ds(src, ROW_SUBLANES)],
                                     rows_ref.at[s, k, pl.ds(dst, ROW_SUBLANES)], sem_row.at[s])

    def start_gather(tile, s):
        idx_copy = pltpu.make_async_copy(dest_hbm.at[tile + tile_off], dest_smem.at[s], sem_idx)
        idx_copy.start()
        idx_copy.wait()

        def start_rows(t, carry):
            for k in range(TOP_K):
                row_copy(s, t, k).start(priority=k % DMA_PRIORITIES)
            return carry

        lax.fori_loop(0, tm, start_rows, 0)

    @pl.when(i == 0)
    def _():
        start_gather(i, slot)

    @pl.when(i + 1 < n_i)
    def _():
        start_gather(i + 1, 1 - slot)

    h2 = jnp.concatenate([c.astype(BF16) for c in _load_row_pieces(h2_ref, (), tm)], axis=1)
    g = jnp.dot(h2, wsg_ref[...], preferred_element_type=F32)
    u = jnp.dot(h2, wsu_ref[...], preferred_element_type=F32)
    ffn = jnp.dot((_silu(g) * u).astype(BF16), wsd_ref[...], preferred_element_type=F32)

    def wait_rows(t, carry):
        for k in range(TOP_K):
            row_copy(slot, t, k).wait()
        return carry

    lax.fori_loop(0, tm, wait_rows, 0)
    w = w_ref[...]
    n_pieces = D_MODEL // LANES
    acc = [ffn[:, j * LANES:(j + 1) * LANES] for j in range(n_pieces)]
    for k in range(TOP_K):
        wk = jnp.broadcast_to(w[:, k:k + 1], (tm, LANES))
        pieces = _load_row_pieces(rows_ref, (slot, k), tm)
        acc = [a + wk * p for a, p in zip(acc, pieces)]
    ffn = jnp.concatenate(acc, axis=1)
    o_ref[...] = _layer_norm(ALPHA * x1_ref[...] + g2_ref[0] * ffn, lng_ref[...], lnb_ref[...])


def _combine(dest_tiles, ys, top_w_rows, x1, h2, gate2, w_sg, w_su, w_sd, ln_g, ln_b, seg, tm, tile_off, n_tiles):
    per_seg = seg // tm

    def full(shape):
        return pl.BlockSpec(shape, lambda i: tuple(0 for _ in shape))

    tok = pl.BlockSpec((tm, D_MODEL), lambda i: (i + tile_off, 0))
    return pl.pallas_call(
        functools.partial(_combine_kernel, tm=tm, tile_off=tile_off),
        grid=(n_tiles,),
        in_specs=[pl.BlockSpec(memory_space=pl.ANY), pl.BlockSpec(memory_space=pl.ANY),
                  pl.BlockSpec((tm, TOP_K), lambda i: (i + tile_off, 0)), tok,
                  pl.BlockSpec((tm * ROW_SUBLANES, LANES), lambda i: (i + tile_off, 0)),
                  pl.BlockSpec((1, 1, D_MODEL), lambda i: ((i + tile_off) // per_seg, 0, 0)),
                  full((D_MODEL, SHARED_HIDDEN)), full((D_MODEL, SHARED_HIDDEN)), full((SHARED_HIDDEN, D_MODEL)),
                  full((1, D_MODEL)), full((1, D_MODEL))],
        out_specs=pl.BlockSpec((tm, D_MODEL), lambda i: (i, 0)),
        out_shape=jax.ShapeDtypeStruct((n_tiles * tm, D_MODEL), F32),
        scratch_shapes=[pltpu.SMEM((2, TOP_K * tm), I32),
                        pltpu.VMEM((2, TOP_K, tm * ROW_SUBLANES, LANES), jnp.uint32),
                        pltpu.SemaphoreType.DMA, pltpu.SemaphoreType.DMA((2,))],
        compiler_params=_cparams(("arbitrary",)),
        name="moe_combine",
    )(dest_tiles, ys, top_w_rows, x1, h2, gate2, w_sg, w_su, w_sd, ln_g, ln_b)


def _rope_tables(seq):
    inv_freq = ROPE_THETA ** (-jnp.arange(ROT_HALF, dtype=F32) * 2.0 / ROT_DIM)
    ang = jnp.arange(seq, dtype=F32)[:, None] * inv_freq[None, :]
    cos, sin = jnp.cos(ang), jnp.sin(ang)
    rest = ATTN_HEAD_DIM - ROT_DIM
    ones = jnp.ones((seq, rest), F32)
    zeros = jnp.zeros((seq, rest), F32)
    zh = jnp.zeros((seq, ROT_HALF), F32)
    reps = LANES // ATTN_HEAD_DIM
    cos_t = jnp.tile(jnp.concatenate([cos, cos, ones], axis=1), (1, reps))
    sa_t = jnp.tile(jnp.concatenate([-sin, zh, zeros], axis=1), (1, reps))
    sb_t = jnp.tile(jnp.concatenate([zh, sin, zeros], axis=1), (1, reps))
    return cos_t, sa_t, sb_t


def _split_in_weights(w_in):
    cuts = (D_INNER, CONV_DIM, SSD_HEADS, SSD_HEADS, ATTN_DIM, ATTN_DIM, ATTN_DIM, D_MODEL, D_MODEL)
    offs = [0]
    for c in cuts:
        offs.append(offs[-1] + c)
    z, xbc, dtf, dtb, q, k, v, gs, ga = (w_in[:, offs[i]:offs[i + 1]] for i in range(len(cuts)))
    w_main = jnp.concatenate([z, xbc, gs, ga, q, k, v], axis=1).astype(BF16)
    assert COL_GSSD % D_MODEL == 0 and COL_Q % ATTN_GROUP_COLS == 0 and w_main.shape[1] == N_MAIN
    pad = jnp.zeros((D_MODEL, DT_COLS - 2 * SSD_HEADS), F32)
    w_dt = jnp.concatenate([dtf, dtb, pad], axis=1).astype(BF16)
    return w_main, w_dt


def kernel(x_prompt, x_sample, c_prompt, c_sample, w_ada, b_ada, w_in, conv_w, conv_b, dt_bias_fwd, dt_bias_bwd, a_log_fwd, a_log_bwd, d_skip, ssd_norm_w, w_branch_ssd, w_branch_attn, w_out, ln1_g, ln1_b, w_router, router_bias, w_exp_gate, w_exp_up, w_exp_down, w_sh_gate, w_sh_up, w_sh_down, ln2_g, ln2_b):
    assert w_ada.shape[0] == DEPTH
    groups = ((x_prompt, c_prompt), (x_sample, c_sample))
    shapes = [(x.shape[0], x.shape[1]) for x, _ in groups]
    seg = math.gcd(*(s for _, s in shapes))
    tokens = [b * s for b, s in shapes]
    t_all = sum(tokens)
    row_offs = [0, tokens[0]]

    c_all = jnp.concatenate([c_prompt, c_sample], axis=0)
    n_req = c_all.shape[0]
    c_pad = jnp.zeros((-(-n_req // SUBLANES) * SUBLANES, D_MODEL), F32).at[:n_req].set(c_all)
    ada = _ada(c_pad, w_ada[0], b_ada)[:n_req]
    req_offs = [0, shapes[0][0]]
    reps = jnp.array([s // seg for b, s in shapes for _ in range(b)], dtype=I32)
    n_seg = t_all // seg
    gate2 = jnp.repeat(ada[:, 5 * D_MODEL:], reps, axis=0, total_repeat_length=n_seg).reshape(n_seg, 1, D_MODEL)

    w_main, w_dt = _split_in_weights(w_in[0])
    dt_pad = jnp.zeros((DT_COLS - 2 * SSD_HEADS,), F32)
    dt_bias = jnp.concatenate([dt_bias_fwd[0], dt_bias_bwd[0], dt_pad]).reshape(1, DT_COLS)
    a_log = jnp.concatenate([a_log_fwd[0], a_log_bwd[0], dt_pad]).reshape(1, DT_COLS)
    d_skip_cols = jnp.repeat(d_skip[0], SSD_HEAD_DIM).reshape(1, D_INNER)
    w_bs = w_branch_ssd[0].astype(BF16)
    w_ba = w_branch_attn[0].astype(BF16)
    w_o = w_out[0].astype(BF16)
    w_router_t = w_router[0].T.astype(BF16)

    merged = None
    for (x, _), (batch, seq), row_off, req_off in zip(groups, shapes, row_offs, req_offs):
        shift1, scale1, gate1, shift2, scale2, _ = (
            a.reshape(batch, 1, D_MODEL) for a in jnp.split(ada[req_off:req_off + batch], 6, axis=-1))
        x2d = x.reshape(batch * seq, D_MODEL)
        proj = _inproj(x2d, scale1, shift1, w_main, seq, tm=1024, tn=1280, out_dtype=F32)
        dt_raw = _inproj(x2d, scale1, shift1, w_dt, seq, tm=1024, tn=DT_COLS, out_dtype=F32)
        xbc = _conv(proj, conv_w[0], conv_b, 0, batch, seq)
        y = _ssd(xbc, dt_raw, dt_bias, a_log, None, 0, batch, seq, reverse=False)
        y = _ssd(xbc, dt_raw, dt_bias, a_log, y, 0, batch, seq, reverse=True)
        y_ssd = _ssd_out(y, xbc, proj, d_skip_cols, ssd_norm_w, w_bs, 0)
        tables = _rope_tables(seq)
        attn = []
        for gi, (_, dil) in enumerate(ATTN_PATTERNS):
            q, k, v = _prep(proj, tables, gi, dil, 0, batch, seq)
            attn.append(_band_attn(q, k, v))
        merged = _merge(x2d, y_ssd, proj, attn, gate1, scale2, shift2, w_ba, w_o, ln1_g, ln1_b, w_router_t,
                        batch, seq, row_off, t_all, merged)
    x1, h2, logits_t = merged

    top_idx, top_w, rank, counts = _route(logits_t, router_bias[0].reshape(N_EXPERTS, 1))
    counts = counts[:, 0]
    padded = (counts + EXPERT_BLOCK - 1) // EXPERT_BLOCK * EXPERT_BLOCK
    pad_end = jnp.cumsum(padded)
    pad_start = pad_end - padded
    n_blocks = (t_all * TOP_K + N_EXPERTS * (EXPERT_BLOCK - 1)) // EXPERT_BLOCK
    n_rows = n_blocks * EXPERT_BLOCK
    tm_moe = 256
    dest_tiles = _dest(top_idx, rank, pad_start.astype(I32).reshape(N_EXPERTS, 1), tm_moe)
    n_valid = (pad_end[-1] // EXPERT_BLOCK).astype(I32).reshape(1)
    blk_start = jnp.minimum(jnp.arange(n_blocks, dtype=I32) * EXPERT_BLOCK, pad_end[-1] - 1)
    blk_exp = jnp.sum((pad_end[None, :] <= blk_start[:, None]).astype(I32), axis=1)
    blk_exp = jnp.minimum(blk_exp, N_EXPERTS - 1)

    xs = _dispatch(pad_end.astype(I32), dest_tiles, h2, n_rows, tm_moe)
    ys = _experts(blk_exp, n_valid, xs, w_exp_gate[0], w_exp_up[0], w_exp_down[0])
    outs = []
    for (x, _), n_tok, row_off in zip(groups, tokens, row_offs):
        out = _combine(dest_tiles, ys, top_w.T, x1, h2, gate2, w_sh_gate[0].astype(BF16), w_sh_up[0].astype(BF16),
                       w_sh_down[0].astype(BF16), ln2_g, ln2_b, seg, tm_moe, row_off // tm_moe, n_tok // tm_moe)
        outs.append(out.reshape(x.shape))
    return tuple(outs)
```

```python
import functools
import math

import jax
import jax.numpy as jnp
from jax import lax
from jax.experimental import pallas as pl
from jax.experimental.pallas import tpu as pltpu

F32 = jnp.float32
BF16 = jnp.bfloat16
I32 = jnp.int32

D_MODEL = 1024
D_INNER = 2048
SSD_HEADS = 32
SSD_HEAD_DIM = 64
SSD_GROUPS = 8
SSD_STATE = 128
SSD_CHUNK = 128
HEADS_PER_SSD_GROUP = SSD_HEADS // SSD_GROUPS
CONV_WIDTH = 5
CONV_PAD = CONV_WIDTH // 2
CONV_DIM = D_INNER + 2 * SSD_GROUPS * SSD_STATE
ATTN_PATTERNS = ((128, 1), (512, 4), (2048, 16))
ATTN_GROUPS = len(ATTN_PATTERNS)
ATTN_HEADS_PER_GROUP = 8
ATTN_HEAD_DIM = 64
ATTN_GROUP_COLS = ATTN_HEADS_PER_GROUP * ATTN_HEAD_DIM
ATTN_DIM = ATTN_GROUPS * ATTN_GROUP_COLS
ROT_DIM = ATTN_HEAD_DIM // 4
ROT_HALF = ROT_DIM // 2
ROPE_THETA = 500000.0
N_EXPERTS = 256
EXPERT_HIDDEN = 256
TOP_K = 8
N_EXPERT_GROUPS = 8
EXPERTS_PER_GROUP = N_EXPERTS // N_EXPERT_GROUPS
TOPK_GROUPS = 4
ROUTED_SCALE = 2.5
SHARED_HIDDEN = 256
EXPERT_BLOCK = 512
PACKED_COLS = D_MODEL // 2
ROW_SUBLANES = PACKED_COLS // 128
DEPTH = 1
ALPHA = (2.0 * DEPTH) ** 0.25
NORM_EPS = 1e-5

COL_Z = 0
COL_XBC = COL_Z + D_INNER
COL_GSSD = COL_XBC + CONV_DIM
COL_GATTN = COL_GSSD + D_MODEL
COL_Q = COL_GATTN + D_MODEL
COL_K = COL_Q + ATTN_DIM
COL_V = COL_K + ATTN_DIM
N_MAIN = COL_V + ATTN_DIM
DT_COLS = 128

LANES = 128
SUBLANES = 8
VMEM_LIMIT = 56 * 1024 * 1024

DMA_PRIORITIES = 2

BAND_RADIUS = 64
ATTN_TQ = 128
NEG_INF = float("-inf")


def _cparams(sem):
    return pltpu.CompilerParams(dimension_semantics=sem, vmem_limit_bytes=VMEM_LIMIT)


def _sigmoid(x):
    return 1.0 / (1.0 + jnp.exp(-x))


def _silu(x):
    return x * _sigmoid(x)


def _softplus(x):
    return jnp.maximum(x, 0.0) + jnp.log(1.0 + jnp.exp(-jnp.abs(x)))


def _pack_halves(x):
    c = x.shape[1] // 2
    hi = lax.bitcast_convert_type(x[:, :c].astype(BF16).astype(F32), jnp.uint32)
    lo = lax.bitcast_convert_type(x[:, c:].astype(BF16).astype(F32), jnp.uint32)
    return hi | (lo >> 16)


def _unpack_halves(u):
    left = lax.bitcast_convert_type(u & jnp.uint32(0xFFFF0000), F32)
    right = lax.bitcast_convert_type(u << 16, F32)
    return left, right


def _store_rows(ref, lead, x):
    rows = x.shape[0]
    packed = _pack_halves(x)
    for j in range(ROW_SUBLANES):
        idx = lead + (pl.ds(j, rows, stride=ROW_SUBLANES), slice(None))
        ref[idx] = packed[:, j * LANES:(j + 1) * LANES]


def _load_row_pieces(ref, lead, rows, base=0):
    lefts, rights = [], []
    for j in range(ROW_SUBLANES):
        idx = lead + (pl.ds(base + j, rows, stride=ROW_SUBLANES), slice(None))
        left, right = _unpack_halves(ref[idx])
        lefts.append(left)
        rights.append(right)
    return lefts + rights


def _ada_kernel(c_ref, w_ref, b_ref, o_ref):
    c = _silu(c_ref[...]).astype(BF16)
    o_ref[...] = jnp.dot(c, w_ref[...].astype(BF16), preferred_element_type=F32) + b_ref[...]


def _ada(c_pad, w_ada, b_ada):
    rows = c_pad.shape[0]
    n = w_ada.shape[1]
    tn = 1536
    return pl.pallas_call(
        _ada_kernel,
        grid=(n // tn,),
        in_specs=[pl.BlockSpec((rows, D_MODEL), lambda j: (0, 0)),
                  pl.BlockSpec((D_MODEL, tn), lambda j: (0, j)),
                  pl.BlockSpec((1, tn), lambda j: (0, j))],
        out_specs=pl.BlockSpec((rows, tn), lambda j: (0, j)),
        out_shape=jax.ShapeDtypeStruct((rows, n), F32),
        compiler_params=_cparams(("arbitrary",)),
        name="ada",
    )(c_pad, w_ada, b_ada)


def _inproj_kernel(x_ref, sc_ref, sh_ref, w_ref, o_ref, h_ref):
    @pl.when(pl.program_id(1) == 0)
    def _():
        h_ref[...] = (x_ref[...] * (1.0 + sc_ref[0]) + sh_ref[0]).astype(BF16)

    o_ref[...] = jnp.dot(h_ref[...], w_ref[...], preferred_element_type=F32).astype(o_ref.dtype)


def _inproj(x, scale, shift, w, seg, tm, tn, out_dtype):
    t = x.shape[0]
    n = w.shape[1]
    per_seg = seg // tm
    return pl.pallas_call(
        _inproj_kernel,
        grid=(t // tm, n // tn),
        in_specs=[pl.BlockSpec((tm, D_MODEL), lambda i, j: (i, 0)),
                  pl.BlockSpec((1, 1, D_MODEL), lambda i, j: (i // per_seg, 0, 0)),
                  pl.BlockSpec((1, 1, D_MODEL), lambda i, j: (i // per_seg, 0, 0)),
                  pl.BlockSpec((D_MODEL, tn), lambda i, j: (0, j))],
        out_specs=pl.BlockSpec((tm, tn), lambda i, j: (i, j)),
        out_shape=jax.ShapeDtypeStruct((t, n), out_dtype),
        scratch_shapes=[pltpu.VMEM((tm, D_MODEL), BF16)],
        compiler_params=_cparams(("arbitrary", "arbitrary")),
        name="inproj",
    )(x, scale, shift, w)


def _conv_kernel(xp_ref, x_ref, xn_ref, w_ref, b_ref, o_ref, *, n_l):
    l = pl.program_id(1)
    tl = x_ref.shape[0]
    xp = jnp.where(l > 0, xp_ref[...], 0.0)
    xn = jnp.where(l < n_l - 1, xn_ref[...], 0.0)
    xx = jnp.concatenate([xp, x_ref[...], xn], axis=0)
    acc = b_ref[...] + w_ref[CONV_PAD:CONV_PAD + 1, :] * x_ref[...]
    for k in range(CONV_WIDTH):
        if k != CONV_PAD:
            shifted = pltpu.roll(xx, (CONV_PAD - k) % xx.shape[0], axis=0)
            acc = acc + w_ref[k:k + 1, :] * shifted[SUBLANES:SUBLANES + tl, :]
    o_ref[...] = _silu(acc)


def _conv(proj, conv_w, conv_b, row_off, batch, seq, tl=512, tc=512):
    t = proj.shape[0]
    n_l = seq // tl
    n_c = CONV_DIM // tc
    c0 = COL_XBC // tc
    r8 = tl // SUBLANES
    last8 = t // SUBLANES - 1

    def row_blk(b, l):
        return (row_off + b * seq) // tl + l

    return pl.pallas_call(
        functools.partial(_conv_kernel, n_l=n_l),
        grid=(batch, n_l, n_c),
        in_specs=[pl.BlockSpec((SUBLANES, tc), lambda b, l, c: (jnp.maximum(row_blk(b, l) * r8 - 1, 0), c0 + c)),
                  pl.BlockSpec((tl, tc), lambda b, l, c: (row_blk(b, l), c0 + c)),
                  pl.BlockSpec((SUBLANES, tc), lambda b, l, c: (jnp.minimum((row_blk(b, l) + 1) * r8, last8), c0 + c)),
                  pl.BlockSpec((CONV_WIDTH, tc), lambda b, l, c: (0, c)),
                  pl.BlockSpec((1, tc), lambda b, l, c: (0, c))],
        out_specs=pl.BlockSpec((tl, tc), lambda b, l, c: (b * n_l + l, c)),
        out_shape=jax.ShapeDtypeStruct((batch * seq, CONV_DIM), F32),
        compiler_params=_cparams(("arbitrary", "arbitrary", "arbitrary")),
        name="conv",
    )(proj, proj, proj, conv_w, conv_b)


def _exact_tri_matmul(tri_bf16, x):
    hi = x.astype(BF16)
    r1 = x - hi.astype(F32)
    mid = r1.astype(BF16)
    lo = (r1 - mid.astype(F32)).astype(BF16)
    out = jnp.dot(tri_bf16, hi, preferred_element_type=F32)
    out = out + jnp.dot(tri_bf16, mid, preferred_element_type=F32)
    return out + jnp.dot(tri_bf16, lo, preferred_element_type=F32)


def _ssd_kernel(*refs, reverse, accumulate):
    if accumulate:
        x_ref, b_ref, c_ref, dt_ref, dtb_ref, alog_ref, yin_ref, y_ref, state_ref = refs
    else:
        x_ref, b_ref, c_ref, dt_ref, dtb_ref, alog_ref, y_ref, state_ref = refs
        yin_ref = None
    lane0 = SSD_HEADS if reverse else 0

    @pl.when(pl.program_id(1) == 0)
    def _():
        state_ref[...] = jnp.zeros(state_ref.shape, F32)

    dt = _softplus(dt_ref[...] + dtb_ref[...])
    da = dt * (-jnp.exp(alog_ref[...]))
    row = lax.broadcasted_iota(I32, (SSD_CHUNK, SSD_CHUNK), 0)
    col = lax.broadcasted_iota(I32, (SSD_CHUNK, SSD_CHUNK), 1)
    tri = (col >= row) if reverse else (col <= row)
    cum = _exact_tri_matmul(tri.astype(BF16), da)
    cum_t = cum.T
    dt_t = dt.T
    edge = 0 if reverse else SSD_CHUNK - 1
    total_b = jnp.broadcast_to(cum_t[:, edge:edge + 1], (LANES, SSD_CHUNK))
    to_end_t = dt_t * jnp.exp(total_b - cum_t)
    chunk_decay_b = jnp.exp(total_b)
    src_t = cum_t - jnp.log(dt_t)
    first_head = lax.broadcasted_iota(I32, (1, 2 * SSD_HEAD_DIM), 1) < SSD_HEAD_DIM
    pairs = HEADS_PER_SSD_GROUP // 2

    for g in range(SSD_GROUPS):
        bg = b_ref[:, g * SSD_STATE:(g + 1) * SSD_STATE]
        cg = c_ref[:, g * SSD_STATE:(g + 1) * SSD_STATE]
        cb = lax.dot_general(cg.astype(BF16), bg.astype(BF16), (((1,), (1,)), ((), ())),
                             preferred_element_type=F32)
        bg_t = bg.T
        for pr in range(pairs):
            h0 = g * HEADS_PER_SSD_GROUP + 2 * pr
            cols = slice(h0 * SSD_HEAD_DIM, (h0 + 2) * SSD_HEAD_DIM)
            x_pair = x_ref[:, cols].astype(BF16)
            state = state_ref[g * pairs + pr]
            rhs = jnp.concatenate([x_pair, state.astype(BF16)], axis=0)
            res = []
            new = []
            for e in range(2):
                ln = lane0 + h0 + e
                cum_l = jnp.broadcast_to(cum[:, ln:ln + 1], (SSD_CHUNK, SSD_CHUNK))
                decay = jnp.exp(jnp.where(tri, cum_l - src_t[ln:ln + 1, :], NEG_INF))
                lhs = jnp.concatenate([(cb * decay).astype(BF16), (cg * jnp.exp(cum_l)).astype(BF16)], axis=1)
                res.append(jnp.dot(lhs, rhs, preferred_element_type=F32))
                b_scaled = (bg_t * to_end_t[ln:ln + 1, :]).astype(BF16)
                new.append(jnp.dot(b_scaled, x_pair, preferred_element_type=F32))
            y_pair = jnp.where(first_head, res[0], res[1])
            if accumulate:
                y_pair = y_pair + yin_ref[:, cols]
            y_ref[:, cols] = y_pair
            ln0 = lane0 + h0
            dec = jnp.where(first_head, chunk_decay_b[ln0:ln0 + 1, :], chunk_decay_b[ln0 + 1:ln0 + 2, :])
            state_ref[g * pairs + pr] = state * dec + jnp.where(first_head, new[0], new[1])


def _ssd(xbc, dt_raw, dt_bias, a_log, y_in, row_off, batch, seq, reverse):
    n_c = seq // SSD_CHUNK
    accumulate = y_in is not None

    def loc(b, c):
        return b * n_c + (n_c - 1 - c if reverse else c)

    def glob(b, c):
        return row_off // SSD_CHUNK + loc(b, c)

    in_specs = [pl.BlockSpec((SSD_CHUNK, D_INNER), lambda b, c: (loc(b, c), 0)),
                pl.BlockSpec((SSD_CHUNK, SSD_GROUPS * SSD_STATE), lambda b, c: (loc(b, c), 2)),
                pl.BlockSpec((SSD_CHUNK, SSD_GROUPS * SSD_STATE), lambda b, c: (loc(b, c), 3)),
                pl.BlockSpec((SSD_CHUNK, DT_COLS), lambda b, c: (glob(b, c), 0)),
                pl.BlockSpec((1, DT_COLS), lambda b, c: (0, 0)),
                pl.BlockSpec((1, DT_COLS), lambda b, c: (0, 0))]
    args = [xbc, xbc, xbc, dt_raw, dt_bias, a_log]
    aliases = {}
    if accumulate:
        in_specs.append(pl.BlockSpec((SSD_CHUNK, D_INNER), lambda b, c: (loc(b, c), 0)))
        args.append(y_in)
        aliases = {len(args) - 1: 0}
    return pl.pallas_call(
        functools.partial(_ssd_kernel, reverse=reverse, accumulate=accumulate),
        grid=(batch, n_c),
        in_specs=in_specs,
        out_specs=pl.BlockSpec((SSD_CHUNK, D_INNER), lambda b, c: (loc(b, c), 0)),
        out_shape=jax.ShapeDtypeStruct((batch * seq, D_INNER), F32),
        scratch_shapes=[pltpu.VMEM((SSD_HEADS // 2, SSD_STATE, 2 * SSD_HEAD_DIM), F32)],
        input_output_aliases=aliases,
        compiler_params=_cparams(("arbitrary", "arbitrary")),
        name="ssd_bwd" if reverse else "ssd_fwd",
    )(*args)


def _ssd_out_kernel(y_ref, x_ref, z_ref, dskip_ref, nw_ref, w_ref, o_ref):
    y = y_ref[...] + dskip_ref[...] * x_ref[...]
    y = y * _silu(z_ref[...])
    ms = jnp.mean(y * y, axis=-1, keepdims=True)
    yn = y * lax.rsqrt(ms + NORM_EPS) * nw_ref[...]
    o_ref[...] = jnp.dot(yn.astype(BF16), w_ref[...], preferred_element_type=F32)


def _ssd_out(y, xbc, proj, d_skip_cols, norm_w, w_branch, row_off, tm=256):
    t = y.shape[0]
    return pl.pallas_call(
        _ssd_out_kernel,
        grid=(t // tm,),
        in_specs=[pl.BlockSpec((tm, D_INNER), lambda i: (i, 0)),
                  pl.BlockSpec((tm, D_INNER), lambda i: (i, 0)),
                  pl.BlockSpec((tm, D_INNER), lambda i: (row_off // tm + i, COL_Z // D_INNER)),
                  pl.BlockSpec((1, D_INNER), lambda i: (0, 0)),
                  pl.BlockSpec((1, D_INNER), lambda i: (0, 0)),
                  pl.BlockSpec((D_INNER, D_MODEL), lambda i: (0, 0))],
        out_specs=pl.BlockSpec((tm, D_MODEL), lambda i: (i, 0)),
        out_shape=jax.ShapeDtypeStruct((t, D_MODEL), F32),
        compiler_params=_cparams(("arbitrary",)),
        name="ssd_out",
    )(y, xbc, proj, d_skip_cols, norm_w, w_branch)


def _prep_kernel(q_ref, k_ref, v_ref, cos_ref, sa_ref, sb_ref, qo_ref, ko_ref, vo_ref, *, dil):
    rows = q_ref.shape[0] // dil

    def phase(ref, p):
        if dil == 1:
            return ref[...]
        return ref[pl.ds(p, rows, stride=dil), :]

    for p in range(dil):
        cos = phase(cos_ref, p)
        sa = phase(sa_ref, p)
        sb = phase(sb_ref, p)

        def rot(t):
            return (t * cos + pltpu.roll(t, LANES - ROT_HALF, axis=1) * sa
                    + pltpu.roll(t, ROT_HALF, axis=1) * sb)

        qo_ref[0, p] = (rot(phase(q_ref, p)) * (ATTN_HEAD_DIM ** -0.5)).astype(qo_ref.dtype)
        ko_ref[0, p] = rot(phase(k_ref, p)).astype(ko_ref.dtype)
        vo_ref[0, p] = phase(v_ref, p).astype(vo_ref.dtype)


def _prep(proj, tables, gi, dil, row_off, batch, seq, rows_per_step=2048):
    r = rows_per_step
    n_i = seq // r
    n_c = ATTN_GROUP_COLS // LANES
    m = seq // dil
    cos_t, sa_t, sb_t = tables

    def in_blk(col):
        cb = (col + gi * ATTN_GROUP_COLS) // LANES
        return pl.BlockSpec((r, LANES), lambda b, i, c: ((row_off + b * seq) // r + i, cb + c))

    tab = pl.BlockSpec((r, LANES), lambda b, i, c: (i, 0))
    out = pl.BlockSpec((1, dil, r // dil, LANES), lambda b, i, c: (b, 0, i, c))
    shape = jax.ShapeDtypeStruct((batch, dil, m, ATTN_GROUP_COLS), BF16)
    return pl.pallas_call(
        functools.partial(_prep_kernel, dil=dil),
        grid=(batch, n_i, n_c),
        in_specs=[in_blk(COL_Q), in_blk(COL_K), in_blk(COL_V), tab, tab, tab],
        out_specs=[out, out, out],
        out_shape=[shape, shape, shape],
        compiler_params=_cparams(("arbitrary", "arbitrary", "arbitrary")),
        name=f"attn_prep_d{dil}",
    )(proj, proj, proj, cos_t, sa_t, sb_t)


def _band_attn_kernel(q_ref, kp_ref, kc_ref, kn_ref, vp_ref, vc_ref, vn_ref, o_ref, lse_ref, *, m_len):
    i = pl.program_id(1)
    tq = q_ref.shape[1]
    tk = tq + 2 * BAND_RADIUS
    q = q_ref[0].astype(BF16)
    k = jnp.concatenate([kp_ref[0], kc_ref[0], kn_ref[0]], axis=0).astype(BF16)
    v = jnp.concatenate([vp_ref[0], vc_ref[0], vn_ref[0]], axis=0).astype(BF16)
    qpos = i * tq + lax.broadcasted_iota(I32, (tq, tk), 0)
    kpos = i * tq - BAND_RADIUS + lax.broadcasted_iota(I32, (tq, tk), 1)
    valid = (jnp.abs(qpos - kpos) <= BAND_RADIUS) & (kpos >= 0) & (kpos < m_len)
    first_head = lax.broadcasted_iota(I32, (1, LANES), 1) < ATTN_HEAD_DIM
    ones = jnp.ones((tk, LANES), BF16)
    zero = jnp.zeros((), BF16)
    for pr in range(ATTN_GROUP_COLS // LANES):
        cols = slice(pr * LANES, (pr + 1) * LANES)
        q_pair, k_pair = q[:, cols], k[:, cols]
        v_ext = jnp.concatenate([v[:, cols], ones], axis=1)
        outs, lses = [], []
        for sel in (first_head, jnp.logical_not(first_head)):
            s = lax.dot_general(jnp.where(sel, q_pair, zero), k_pair, (((1,), (1,)), ((), ())),
                                preferred_element_type=F32)
            s = jnp.where(valid, s, NEG_INF)
            mx = jnp.max(s, axis=-1, keepdims=True)
            p = jnp.exp(s - mx).astype(BF16)
            ov = jnp.dot(p, v_ext, preferred_element_type=F32)
            den = ov[:, LANES:]
            outs.append(ov[:, :LANES] / den)
            lses.append(mx + jnp.log(den))
        o_ref[0, :, cols] = jnp.where(first_head, outs[0], outs[1])
        lse_ref[0, :, cols] = jnp.where(first_head, lses[0], lses[1])


def _band_attn(q, k, v):
    b, d, m, c = q.shape
    n = b * d
    q, k, v = (a.reshape(n, m, c) for a in (q, k, v))
    tq = ATTN_TQ
    per = tq // BAND_RADIUS
    last = m // BAND_RADIUS - 1
    main = pl.BlockSpec((1, tq, c), lambda s, i: (s, i, 0))
    prev = pl.BlockSpec((1, BAND_RADIUS, c), lambda s, i: (s, jnp.maximum(i * per - 1, 0), 0))
    nxt = pl.BlockSpec((1, BAND_RADIUS, c), lambda s, i: (s, jnp.minimum((i + 1) * per, last), 0))
    shape = jax.ShapeDtypeStruct((n, m, c), F32)
    o, lse = pl.pallas_call(
        functools.partial(_band_attn_kernel, m_len=m),
        grid=(n, m // tq),
        in_specs=[main, prev, main, nxt, prev, main, nxt],
        out_specs=[main, main],
        out_shape=[shape, shape],
        compiler_params=_cparams(("arbitrary", "arbitrary")),
        name=f"band_attn_m{m}",
    )(q, k, k, k, v, v, v)
    return o.reshape(b, d, m, c), lse.reshape(b, d, m, c)


def _layer_norm(x, g, b):
    mu = jnp.mean(x, axis=-1, keepdims=True)
    xc = x - mu
    var = jnp.mean(xc * xc, axis=-1, keepdims=True)
    return xc * lax.rsqrt(var + NORM_EPS) * g + b


def _merge_kernel(x_ref, yssd_ref, gs_ref, ga_ref, o1_ref, l1_ref, o2_ref, l2_ref, o3_ref, l3_ref,
                  g1_ref, sc2_ref, sh2_ref, wba_ref, wout_ref, lng_ref, lnb_ref, wr_ref, *rest, n_prev):
    x1_ref, h2_ref, logit_ref, scr_ref = rest[n_prev:]
    tm = x_ref.shape[0]

    def interleaved(ref, dil, slot):
        if dil == 1:
            return ref[0, 0]
        rows = tm // dil
        n_c = ATTN_GROUP_COLS // LANES
        for p in range(dil):
            for c in range(n_c):
                scr_ref[slot * n_c + c, pl.ds(p, rows, stride=dil), :] = ref[0, p, :, c * LANES:(c + 1) * LANES]
        return jnp.concatenate([scr_ref[slot * n_c + c] for c in range(n_c)], axis=1)

    outs = []
    lses = []
    slot = 0
    for (o_ref, l_ref), (_, dil) in zip(((o1_ref, l1_ref), (o2_ref, l2_ref), (o3_ref, l3_ref)), ATTN_PATTERNS):
        outs.append(interleaved(o_ref, dil, slot))
        lses.append(interleaved(l_ref, dil, slot + 1))
        slot += 2
    mx = jnp.maximum(jnp.maximum(lses[0], lses[1]), lses[2])
    es = [jnp.exp(l - mx) for l in lses]
    den = es[0] + es[1] + es[2]
    y_att = (es[0] * outs[0] + es[1] * outs[1] + es[2] * outs[2]) / den
    y_attn = jnp.dot(y_att.astype(BF16), wba_ref[...], preferred_element_type=F32)
    merged = _sigmoid(gs_ref[...]) * yssd_ref[...] + _sigmoid(ga_ref[...]) * y_attn
    mix = jnp.dot(merged.astype(BF16), wout_ref[...], preferred_element_type=F32)
    x1 = _layer_norm(ALPHA * x_ref[...] + g1_ref[0] * mix, lng_ref[...], lnb_ref[...])
    x1_ref[...] = x1
    h2 = x1 * (1.0 + sc2_ref[0]) + sh2_ref[0]
    _store_rows(h2_ref, (), h2)
    logit_ref[...] = lax.dot_general(wr_ref[...], h2.astype(BF16), (((1,), (1,)), ((), ())),
                                     preferred_element_type=F32)


def _merge(x, y_ssd, proj, attn, gate1, scale2, shift2, w_ba, w_out, ln_g, ln_b, w_router_t,
           batch, seq, out_row_off, t_all, prev, tm=256):
    n_i = seq // tm
    out0 = out_row_off // tm

    def rows(col_blk):
        return pl.BlockSpec((tm, D_MODEL), lambda b, i: (b * n_i + i, col_blk))

    def local(width):
        return pl.BlockSpec((tm, width), lambda b, i: (b * n_i + i, 0))

    def modv():
        return pl.BlockSpec((1, 1, D_MODEL), lambda b, i: (b, 0, 0))

    def full(shape):
        return pl.BlockSpec(shape, lambda b, i: tuple(0 for _ in shape))

    attn_specs = []
    attn_args = []
    for (o, lse), (_, dil) in zip(attn, ATTN_PATTERNS):
        spec = pl.BlockSpec((1, dil, tm // dil, ATTN_GROUP_COLS), lambda b, i: (b, 0, i, 0))
        attn_specs += [spec, spec]
        attn_args += [o, lse]
    n_in = 4 + len(attn_args) + 8
    return pl.pallas_call(
        functools.partial(_merge_kernel, n_prev=0 if prev is None else 3),
        grid=(batch, n_i),
        in_specs=[rows(0), local(D_MODEL), rows(COL_GSSD // D_MODEL), rows(COL_GATTN // D_MODEL)] + attn_specs
                 + [modv(), modv(), modv(), full((ATTN_GROUP_COLS, D_MODEL)), full((D_MODEL, D_MODEL)),
                    full((1, D_MODEL)), full((1, D_MODEL)), full((N_EXPERTS, D_MODEL))]
                 + ([pl.BlockSpec(memory_space=pl.ANY)] * 3 if prev is not None else []),
        out_specs=[pl.BlockSpec((tm, D_MODEL), lambda b, i: (out0 + b * n_i + i, 0)),
                   pl.BlockSpec((tm * ROW_SUBLANES, LANES), lambda b, i: (out0 + b * n_i + i, 0)),
                   pl.BlockSpec((N_EXPERTS, tm), lambda b, i: (0, out0 + b * n_i + i))],
        out_shape=[jax.ShapeDtypeStruct((t_all, D_MODEL), F32),
                   jax.ShapeDtypeStruct((t_all * ROW_SUBLANES, LANES), jnp.uint32),
                   jax.ShapeDtypeStruct((N_EXPERTS, t_all), F32)],
        scratch_shapes=[pltpu.VMEM((2 * ATTN_GROUPS * (ATTN_GROUP_COLS // LANES), tm, LANES), F32)],
        input_output_aliases={n_in + j: j for j in range(3)} if prev is not None else {},
        compiler_params=_cparams(("arbitrary", "arbitrary")),
        name="mixer_merge",
    )(x, y_ssd, proj, proj, *attn_args, gate1, scale2, shift2, w_ba, w_out, ln_g, ln_b, w_router_t,
      *(prev if prev is not None else ()))


def _route_kernel(logit_ref, bias_ref, idx_ref, w_ref, rank_ref, cnt_ref, run_ref):
    tm = logit_ref.shape[1]

    @pl.when(pl.program_id(0) == 0)
    def _():
        run_ref[...] = jnp.zeros(run_ref.shape, F32)

    scores = _sigmoid(logit_ref[...])
    sel = scores + bias_ref[...]
    iota_g = lax.broadcasted_iota(I32, (EXPERTS_PER_GROUP, tm), 0)
    grp = []
    for g in range(N_EXPERT_GROUPS):
        sg = sel[g * EXPERTS_PER_GROUP:(g + 1) * EXPERTS_PER_GROUP, :]
        top1 = jnp.max(sg, axis=0, keepdims=True)
        first = jnp.min(jnp.where(sg == top1, iota_g, EXPERTS_PER_GROUP), axis=0, keepdims=True)
        top2 = jnp.max(jnp.where(iota_g == first, NEG_INF, sg), axis=0, keepdims=True)
        grp.append(top1 + top2)
    gs = jnp.concatenate(grp, axis=0)
    iota_n = lax.broadcasted_iota(I32, (N_EXPERT_GROUPS, tm), 0)
    keep_g = jnp.zeros((N_EXPERT_GROUPS, tm), F32)
    for _ in range(TOPK_GROUPS):
        best = jnp.max(gs, axis=0, keepdims=True)
        first = jnp.min(jnp.where(gs == best, iota_n, N_EXPERT_GROUPS), axis=0, keepdims=True)
        hit = iota_n == first
        keep_g = jnp.where(hit, 1.0, keep_g)
        gs = jnp.where(hit, NEG_INF, gs)
    cand = jnp.concatenate(
        [jnp.where(keep_g[g:g + 1, :] > 0.0, sel[g * EXPERTS_PER_GROUP:(g + 1) * EXPERTS_PER_GROUP, :], NEG_INF)
         for g in range(N_EXPERT_GROUPS)], axis=0)
    iota_e = lax.broadcasted_iota(I32, (N_EXPERTS, tm), 0)
    chosen = jnp.zeros((N_EXPERTS, tm), F32)
    idxs = []
    ws = []
    for _ in range(TOP_K):
        best = jnp.max(cand, axis=0, keepdims=True)
        first = jnp.min(jnp.where(cand == best, iota_e, N_EXPERTS), axis=0, keepdims=True)
        hit = iota_e == first
        idxs.append(first)
        ws.append(jnp.sum(jnp.where(hit, scores, 0.0), axis=0, keepdims=True))
        chosen = jnp.where(hit, 1.0, chosen)
        cand = jnp.where(hit, NEG_INF, cand)
    top_w = jnp.concatenate(ws, axis=0)
    top_w = top_w / jnp.sum(top_w, axis=0, keepdims=True) * ROUTED_SCALE
    idx_ref[...] = jnp.concatenate(idxs, axis=0)
    w_ref[...] = top_w
    s_i = lax.broadcasted_iota(I32, (tm, tm), 0)
    t_i = lax.broadcasted_iota(I32, (tm, tm), 1)
    before = (s_i < t_i).astype(BF16)
    prior = jnp.dot(chosen.astype(BF16), before, preferred_element_type=F32) + run_ref[...]
    ranks = [jnp.sum(jnp.where(iota_e == idxs[k], prior, 0.0), axis=0, keepdims=True) for k in range(TOP_K)]
    rank_ref[...] = jnp.concatenate(ranks, axis=0).astype(I32)
    run_ref[...] = run_ref[...] + jnp.sum(chosen, axis=1, keepdims=True)
    cnt_ref[...] = run_ref[...].astype(I32)


def _route(logits_t, bias_col, tm=256):
    t = logits_t.shape[1]
    tok = pl.BlockSpec((TOP_K, tm), lambda i: (0, i))
    return pl.pallas_call(
        _route_kernel,
        grid=(t // tm,),
        in_specs=[pl.BlockSpec((N_EXPERTS, tm), lambda i: (0, i)),
                  pl.BlockSpec((N_EXPERTS, 1), lambda i: (0, 0))],
        out_specs=[tok, tok, tok, pl.BlockSpec((N_EXPERTS, 1), lambda i: (0, 0))],
        out_shape=[jax.ShapeDtypeStruct((TOP_K, t), I32), jax.ShapeDtypeStruct((TOP_K, t), F32),
                   jax.ShapeDtypeStruct((TOP_K, t), I32), jax.ShapeDtypeStruct((N_EXPERTS, 1), I32)],
        scratch_shapes=[pltpu.VMEM((N_EXPERTS, 1), F32)],
        compiler_params=_cparams(("arbitrary",)),
        name="route",
    )(logits_t, bias_col)


def _dest_kernel(idx_ref, rank_ref, start_ref, o_ref):
    tm = idx_ref.shape[1]
    iota_e = lax.broadcasted_iota(I32, (N_EXPERTS, tm), 0)
    start = start_ref[...].astype(F32)
    for k in range(TOP_K):
        base = jnp.sum(jnp.where(iota_e == idx_ref[k:k + 1, :], start, 0.0), axis=0, keepdims=True)
        o_ref[0, :, k * tm:(k + 1) * tm] = (base.astype(I32) + rank_ref[k:k + 1, :]) * ROW_SUBLANES


def _dest(top_idx, rank, pad_start_col, tm):
    t = top_idx.shape[1]
    tok = pl.BlockSpec((TOP_K, tm), lambda i: (0, i))
    out = pl.pallas_call(
        _dest_kernel,
        grid=(t // tm,),
        in_specs=[tok, tok, pl.BlockSpec((N_EXPERTS, 1), lambda i: (0, 0))],
        out_specs=pl.BlockSpec((1, 1, TOP_K * tm), lambda i: (i, 0, 0)),
        out_shape=jax.ShapeDtypeStruct((t // tm, 1, TOP_K * tm), I32),
        compiler_params=_cparams(("arbitrary",)),
        name="moe_dest",
    )(top_idx, rank, pad_start_col)
    return out.reshape(t // tm, TOP_K * tm)


def _dispatch_kernel(pad_end_ref, dest_hbm, h_ref, xs_hbm, dest_smem, zeros_ref, sem_idx, sem_zero, sem_row,
                     *, tm):
    i = pl.program_id(0)
    idx_copy = pltpu.make_async_copy(dest_hbm.at[i], dest_smem, sem_idx)
    idx_copy.start()

    blk_rows = EXPERT_BLOCK * ROW_SUBLANES

    def zero_copy(e):
        start = pl.multiple_of(jnp.maximum(pad_end_ref[e] - EXPERT_BLOCK, 0) * ROW_SUBLANES, blk_rows)
        return pltpu.make_async_copy(zeros_ref, xs_hbm.at[pl.ds(start, blk_rows)], sem_zero)

    @pl.when(i == 0)
    def _():
        zeros_ref[...] = jnp.zeros(zeros_ref.shape, zeros_ref.dtype)

        def start_zero(e, carry):
            zero_copy(e).start()
            return carry

        def wait_zero(e, carry):
            zero_copy(e).wait()
            return carry

        lax.fori_loop(0, N_EXPERTS, start_zero, 0)
        lax.fori_loop(0, N_EXPERTS, wait_zero, 0)

    idx_copy.wait()

    def row_copy(t, k):
        dst = pl.multiple_of(dest_smem[k * tm + t], ROW_SUBLANES)
        return pltpu.make_async_copy(h_ref.at[pl.ds(pl.multiple_of(t * ROW_SUBLANES, ROW_SUBLANES), ROW_SUBLANES)],
                                     xs_hbm.at[pl.ds(dst, ROW_SUBLANES)], sem_row)

    def start_rows(t, carry):
        for k in range(TOP_K):
            row_copy(t, k).start(priority=k % DMA_PRIORITIES)
        return carry

    def wait_rows(t, carry):
        for k in range(TOP_K):
            row_copy(t, k).wait()
        return carry

    lax.fori_loop(0, tm, start_rows, 0)
    lax.fori_loop(0, tm, wait_rows, 0)


def _dispatch(pad_end, dest_tiles, h2, n_rows, tm):
    n_tiles = dest_tiles.shape[0]
    return pl.pallas_call(
        functools.partial(_dispatch_kernel, tm=tm),
        grid_spec=pltpu.PrefetchScalarGridSpec(
            num_scalar_prefetch=1,
            grid=(n_tiles,),
            in_specs=[pl.BlockSpec(memory_space=pl.ANY),
                      pl.BlockSpec((tm * ROW_SUBLANES, LANES), lambda i, pe: (i, 0))],
            out_specs=pl.BlockSpec(memory_space=pl.ANY),
            scratch_shapes=[pltpu.SMEM((TOP_K * tm,), I32),
                            pltpu.VMEM((EXPERT_BLOCK * ROW_SUBLANES, LANES), jnp.uint32),
                            pltpu.SemaphoreType.DMA, pltpu.SemaphoreType.DMA, pltpu.SemaphoreType.DMA]),
        out_shape=jax.ShapeDtypeStruct((n_rows * ROW_SUBLANES, LANES), jnp.uint32),
        compiler_params=_cparams(("arbitrary",)),
        name="moe_dispatch",
    )(pad_end, dest_tiles, h2)


def _experts_kernel(blk_exp_ref, n_valid_ref, xs_ref, wg_ref, wu_ref, wd_ref, ys_ref, wgu_ref, wdn_ref):
    i = pl.program_id(0)

    @pl.when(i < n_valid_ref[0])
    def _():
        @pl.when((i == 0) | (blk_exp_ref[i] != blk_exp_ref[jnp.maximum(i - 1, 0)]))
        def _():
            wgu_ref[:, :EXPERT_HIDDEN] = wg_ref[0].astype(BF16)
            wgu_ref[:, EXPERT_HIDDEN:] = wu_ref[0].astype(BF16)
            wdn_ref[...] = wd_ref[0].astype(BF16)

        x = jnp.concatenate([c.astype(BF16) for c in _load_row_pieces(xs_ref, (), EXPERT_BLOCK)], axis=1)
        gu = jnp.dot(x, wgu_ref[...], preferred_element_type=F32)
        hmid = (_silu(gu[:, :EXPERT_HIDDEN]) * gu[:, EXPERT_HIDDEN:]).astype(BF16)
        _store_rows(ys_ref, (), jnp.dot(hmid, wdn_ref[...], preferred_element_type=F32))


def _experts(blk_exp, n_valid, xs, w_gate, w_up, w_down):
    n_blocks = xs.shape[0] // (EXPERT_BLOCK * ROW_SUBLANES)

    def row_map(i, be, nv):
        return (jnp.minimum(i, nv[0] - 1), 0)

    def w_map(i, be, nv):
        return (be[i], 0, 0)

    return pl.pallas_call(
        _experts_kernel,
        grid_spec=pltpu.PrefetchScalarGridSpec(
            num_scalar_prefetch=2,
            grid=(n_blocks,),
            in_specs=[pl.BlockSpec((EXPERT_BLOCK * ROW_SUBLANES, LANES), row_map),
                      pl.BlockSpec((1, D_MODEL, EXPERT_HIDDEN), w_map),
                      pl.BlockSpec((1, D_MODEL, EXPERT_HIDDEN), w_map),
                      pl.BlockSpec((1, EXPERT_HIDDEN, D_MODEL), w_map)],
            out_specs=pl.BlockSpec((EXPERT_BLOCK * ROW_SUBLANES, LANES), row_map),
            scratch_shapes=[pltpu.VMEM((D_MODEL, 2 * EXPERT_HIDDEN), BF16),
                            pltpu.VMEM((EXPERT_HIDDEN, D_MODEL), BF16)]),
        out_shape=jax.ShapeDtypeStruct(xs.shape, jnp.uint32),
        compiler_params=_cparams(("arbitrary",)),
        name="moe_experts",
    )(blk_exp, n_valid, xs, w_gate, w_up, w_down)


def _combine_kernel(dest_hbm, ys_hbm, w_ref, x1_ref, h2_ref, g2_ref, wsg_ref, wsu_ref, wsd_ref,
                    lng_ref, lnb_ref, o_ref, dest0, dest1, rows0, rows1, sem_idx, sem0, sem1, *, tm, tile_off):
    i = pl.program_id(0)
    n_i = pl.num_programs(0)
    dest_smem, rows_ref, sem_row = (dest0, dest1), (rows0, rows1), (sem0, sem1)

    def row_copy(s, t, k):
        src = pl.multiple_of(dest_smem[s][k * tm + t], ROW_SUBLANES)
        dst = pl.multiple_of(t * ROW_SUBLANES, ROW_SUBLANES)
        return pltpu.make_async_copy(ys_hbm.at[pl.ds(src, ROW_SUBLANES)],
                                     rows_ref[s].at[k, pl.ds(dst, ROW_SUBLANES)], sem_row[s])

    def start_gather(tile, s):
        idx_copy = pltpu.make_async_copy(dest_hbm.at[tile + tile_off], dest_smem[s], sem_idx)
        idx_copy.start()
        idx_copy.wait()

        def start_rows(t, carry):
            for k in range(TOP_K):
                row_copy(s, t, k).start(priority=k % DMA_PRIORITIES)
            return carry

        lax.fori_loop(0, tm, start_rows, 0)

    @pl.when(i == 0)
    def _():
        start_gather(0, 0)

    for s in range(2):
        if s == 0:
            start_gather(2 * i + 1, 1)
        else:
            @pl.when(i + 1 < n_i)
            def _():
                start_gather(2 * i + 2, 0)

        rows = slice(s * tm, (s + 1) * tm)
        h2 = jnp.concatenate(
            [c.astype(BF16) for c in _load_row_pieces(h2_ref, (), tm, s * tm * ROW_SUBLANES)], axis=1)
        g = jnp.dot(h2, wsg_ref[...], preferred_element_type=F32)
        u = jnp.dot(h2, wsu_ref[...], preferred_element_type=F32)
        ffn = jnp.dot((_silu(g) * u).astype(BF16), wsd_ref[...], preferred_element_type=F32)

        def wait_rows(t, carry):
            for k in range(TOP_K):
                row_copy(s, t, k).wait()
            return carry

        lax.fori_loop(0, tm, wait_rows, 0)
        w = w_ref[rows, :]
        n_pieces = D_MODEL // LANES
        acc = [ffn[:, j * LANES:(j + 1) * LANES] for j in range(n_pieces)]
        for k in range(TOP_K):
            wk = jnp.broadcast_to(w[:, k:k + 1], (tm, LANES))
            pieces = _load_row_pieces(rows_ref[s], (k,), tm)
            acc = [a + wk * p for a, p in zip(acc, pieces)]
        ffn = jnp.concatenate(acc, axis=1)
        o_ref[rows, :] = _layer_norm(ALPHA * x1_ref[rows, :] + g2_ref[0] * ffn, lng_ref[...], lnb_ref[...])


def _combine(dest_tiles, ys, top_w_rows, x1, h2, gate2, w_sg, w_su, w_sd, ln_g, ln_b, seg, tm, tile_off, n_tiles):
    assert tile_off % 2 == 0 and n_tiles % 2 == 0 and seg % (2 * tm) == 0
    per_seg = seg // (2 * tm)
    off = tile_off // 2

    def full(shape):
        return pl.BlockSpec(shape, lambda i: tuple(0 for _ in shape))

    return pl.pallas_call(
        functools.partial(_combine_kernel, tm=tm, tile_off=tile_off),
        grid=(n_tiles // 2,),
        in_specs=[pl.BlockSpec(memory_space=pl.ANY), pl.BlockSpec(memory_space=pl.ANY),
                  pl.BlockSpec((2 * tm, TOP_K), lambda i: (i + off, 0)),
                  pl.BlockSpec((2 * tm, D_MODEL), lambda i: (i + off, 0)),
                  pl.BlockSpec((2 * tm * ROW_SUBLANES, LANES), lambda i: (i + off, 0)),
                  pl.BlockSpec((1, 1, D_MODEL), lambda i: ((i + off) // per_seg, 0, 0)),
                  full((D_MODEL, SHARED_HIDDEN)), full((D_MODEL, SHARED_HIDDEN)), full((SHARED_HIDDEN, D_MODEL)),
                  full((1, D_MODEL)), full((1, D_MODEL))],
        out_specs=pl.BlockSpec((2 * tm, D_MODEL), lambda i: (i, 0)),
        out_shape=jax.ShapeDtypeStruct((n_tiles * tm, D_MODEL), F32),
        scratch_shapes=[pltpu.SMEM((TOP_K * tm,), I32), pltpu.SMEM((TOP_K * tm,), I32),
                        pltpu.VMEM((TOP_K, tm * ROW_SUBLANES, LANES), jnp.uint32),
                        pltpu.VMEM((TOP_K, tm * ROW_SUBLANES, LANES), jnp.uint32),
                        pltpu.SemaphoreType.DMA, pltpu.SemaphoreType.DMA, pltpu.SemaphoreType.DMA],
        compiler_params=_cparams(("arbitrary",)),
        name="moe_combine",
    )(dest_tiles, ys, top_w_rows, x1, h2, gate2, w_sg, w_su, w_sd, ln_g, ln_b)


def _rope_tables(seq):
    inv_freq = ROPE_THETA ** (-jnp.arange(ROT_HALF, dtype=F32) * 2.0 / ROT_DIM)
    ang = jnp.arange(seq, dtype=F32)[:, None] * inv_freq[None, :]
    cos, sin = jnp.cos(ang), jnp.sin(ang)
    rest = ATTN_HEAD_DIM - ROT_DIM
    ones = jnp.ones((seq, rest), F32)
    zeros = jnp.zeros((seq, rest), F32)
    zh = jnp.zeros((seq, ROT_HALF), F32)
    reps = LANES // ATTN_HEAD_DIM
    cos_t = jnp.tile(jnp.concatenate([cos, cos, ones], axis=1), (1, reps))
    sa_t = jnp.tile(jnp.concatenate([-sin, zh, zeros], axis=1), (1, reps))
    sb_t = jnp.tile(jnp.concatenate([zh, sin, zeros], axis=1), (1, reps))
    return cos_t, sa_t, sb_t


def _split_in_weights(w_in):
    cuts = (D_INNER, CONV_DIM, SSD_HEADS, SSD_HEADS, ATTN_DIM, ATTN_DIM, ATTN_DIM, D_MODEL, D_MODEL)
    offs = [0]
    for c in cuts:
        offs.append(offs[-1] + c)
    z, xbc, dtf, dtb, q, k, v, gs, ga = (w_in[:, offs[i]:offs[i + 1]] for i in range(len(cuts)))
    w_main = jnp.concatenate([z, xbc, gs, ga, q, k, v], axis=1).astype(BF16)
    assert COL_GSSD % D_MODEL == 0 and COL_Q % ATTN_GROUP_COLS == 0 and w_main.shape[1] == N_MAIN
    pad = jnp.zeros((D_MODEL, DT_COLS - 2 * SSD_HEADS), F32)
    w_dt = jnp.concatenate([dtf, dtb, pad], axis=1).astype(BF16)
    return w_main, w_dt


def kernel(x_prompt, x_sample, c_prompt, c_sample, w_ada, b_ada, w_in, conv_w, conv_b, dt_bias_fwd, dt_bias_bwd, a_log_fwd, a_log_bwd, d_skip, ssd_norm_w, w_branch_ssd, w_branch_attn, w_out, ln1_g, ln1_b, w_router, router_bias, w_exp_gate, w_exp_up, w_exp_down, w_sh_gate, w_sh_up, w_sh_down, ln2_g, ln2_b):
    assert w_ada.shape[0] == DEPTH
    groups = ((x_prompt, c_prompt), (x_sample, c_sample))
    shapes = [(x.shape[0], x.shape[1]) for x, _ in groups]
    seg = math.gcd(*(s for _, s in shapes))
    tokens = [b * s for b, s in shapes]
    t_all = sum(tokens)
    row_offs = [0, tokens[0]]

    c_all = jnp.concatenate([c_prompt, c_sample], axis=0)
    n_req = c_all.shape[0]
    c_pad = jnp.zeros((-(-n_req // SUBLANES) * SUBLANES, D_MODEL), F32).at[:n_req].set(c_all)
    ada = _ada(c_pad, w_ada[0], b_ada)[:n_req]
    req_offs = [0, shapes[0][0]]
    reps = jnp.array([s // seg for b, s in shapes for _ in range(b)], dtype=I32)
    n_seg = t_all // seg
    gate2 = jnp.repeat(ada[:, 5 * D_MODEL:], reps, axis=0, total_repeat_length=n_seg).reshape(n_seg, 1, D_MODEL)

    w_main, w_dt = _split_in_weights(w_in[0])
    dt_pad = jnp.zeros((DT_COLS - 2 * SSD_HEADS,), F32)
    dt_bias = jnp.concatenate([dt_bias_fwd[0], dt_bias_bwd[0], dt_pad]).reshape(1, DT_COLS)
    a_log = jnp.concatenate([a_log_fwd[0], a_log_bwd[0], dt_pad]).reshape(1, DT_COLS)
    d_skip_cols = jnp.repeat(d_skip[0], SSD_HEAD_DIM).reshape(1, D_INNER)
    w_bs = w_branch_ssd[0].astype(BF16)
    w_ba = w_branch_attn[0].astype(BF16)
    w_o = w_out[0].astype(BF16)
    w_router_t = w_router[0].T.astype(BF16)

    merged = None
    for (x, _), (batch, seq), row_off, req_off in zip(groups, shapes, row_offs, req_offs):
        shift1, scale1, gate1, shift2, scale2, _ = (
            a.reshape(batch, 1, D_MODEL) for a in jnp.split(ada[req_off:req_off + batch], 6, axis=-1))
        x2d = x.reshape(batch * seq, D_MODEL)
        proj = _inproj(x2d, scale1, shift1, w_main, seq, tm=1024, tn=1280, out_dtype=F32)
        dt_raw = _inproj(x2d, scale1, shift1, w_dt, seq, tm=1024, tn=DT_COLS, out_dtype=F32)
        xbc = _conv(proj, conv_w[0], conv_b, 0, batch, seq)
        y = _ssd(xbc, dt_raw, dt_bias, a_log, None, 0, batch, seq, reverse=False)
        y = _ssd(xbc, dt_raw, dt_bias, a_log, y, 0, batch, seq, reverse=True)
        y_ssd = _ssd_out(y, xbc, proj, d_skip_cols, ssd_norm_w, w_bs, 0)
        tables = _rope_tables(seq)
        attn = []
        for gi, (_, dil) in enumerate(ATTN_PATTERNS):
            q, k, v = _prep(proj, tables, gi, dil, 0, batch, seq)
            attn.append(_band_attn(q, k, v))
        merged = _merge(x2d, y_ssd, proj, attn, gate1, scale2, shift2, w_ba, w_o, ln1_g, ln1_b, w_router_t,
                        batch, seq, row_off, t_all, merged)
    x1, h2, logits_t = merged

    top_idx, top_w, rank, counts = _route(logits_t, router_bias[0].reshape(N_EXPERTS, 1))
    counts = counts[:, 0]
    padded = (counts + EXPERT_BLOCK - 1) // EXPERT_BLOCK * EXPERT_BLOCK
    pad_end = jnp.cumsum(padded)
    pad_start = pad_end - padded
    n_blocks = (t_all * TOP_K + N_EXPERTS * (EXPERT_BLOCK - 1)) // EXPERT_BLOCK
    n_rows = n_blocks * EXPERT_BLOCK
    tm_moe = 256
    dest_tiles = _dest(top_idx, rank, pad_start.astype(I32).reshape(N_EXPERTS, 1), tm_moe)
    n_valid = (pad_end[-1] // EXPERT_BLOCK).astype(I32).reshape(1)
    blk_start = jnp.minimum(jnp.arange(n_blocks, dtype=I32) * EXPERT_BLOCK, pad_end[-1] - 1)
    blk_exp = jnp.sum((pad_end[None, :] <= blk_start[:, None]).astype(I32), axis=1)
    blk_exp = jnp.minimum(blk_exp, N_EXPERTS - 1)

    xs = _dispatch(pad_end.astype(I32), dest_tiles, h2, n_rows, tm_moe)
    ys = _experts(blk_exp, n_valid, xs, w_exp_gate[0], w_exp_up[0], w_exp_down[0])
    outs = []
    for (x, _), n_tok, row_off in zip(groups, tokens, row_offs):
        out = _combine(dest_tiles, ys, top_w.T, x1, h2, gate2, w_sh_gate[0].astype(BF16), w_sh_up[0].astype(BF16),
                       w_sh_down[0].astype(BF16), ln2_g, ln2_b, seg, tm_moe, row_off // tm_moe, n_tok // tm_moe)
        outs.append(out.reshape(x.shape))
    return tuple(outs)
```

```python
import functools
import math

import jax
import jax.numpy as jnp
from jax import lax
from jax.experimental import pallas as pl
from jax.experimental.pallas import tpu as pltpu

F32 = jnp.float32
BF16 = jnp.bfloat16
I32 = jnp.int32

D_MODEL = 1024
D_INNER = 2048
SSD_HEADS = 32
SSD_HEAD_DIM = 64
SSD_GROUPS = 8
SSD_STATE = 128
SSD_CHUNK = 128
HEADS_PER_SSD_GROUP = SSD_HEADS // SSD_GROUPS
CONV_WIDTH = 5
CONV_PAD = CONV_WIDTH // 2
CONV_DIM = D_INNER + 2 * SSD_GROUPS * SSD_STATE
ATTN_PATTERNS = ((128, 1), (512, 4), (2048, 16))
ATTN_GROUPS = len(ATTN_PATTERNS)
ATTN_HEADS_PER_GROUP = 8
ATTN_HEAD_DIM = 64
ATTN_GROUP_COLS = ATTN_HEADS_PER_GROUP * ATTN_HEAD_DIM
ATTN_DIM = ATTN_GROUPS * ATTN_GROUP_COLS
ROT_DIM = ATTN_HEAD_DIM // 4
ROT_HALF = ROT_DIM // 2
ROPE_THETA = 500000.0
N_EXPERTS = 256
EXPERT_HIDDEN = 256
TOP_K = 8
N_EXPERT_GROUPS = 8
EXPERTS_PER_GROUP = N_EXPERTS // N_EXPERT_GROUPS
TOPK_GROUPS = 4
ROUTED_SCALE = 2.5
SHARED_HIDDEN = 256
EXPERT_BLOCK = 512
PACKED_COLS = D_MODEL // 2
ROW_SUBLANES = PACKED_COLS // 128
DEPTH = 1
ALPHA = (2.0 * DEPTH) ** 0.25
NORM_EPS = 1e-5

COL_Z = 0
COL_XBC = COL_Z + D_INNER
COL_GSSD = COL_XBC + CONV_DIM
COL_GATTN = COL_GSSD + D_MODEL
COL_Q = COL_GATTN + D_MODEL
COL_K = COL_Q + ATTN_DIM
COL_V = COL_K + ATTN_DIM
N_MAIN = COL_V + ATTN_DIM
DT_COLS = 128

LANES = 128
SUBLANES = 8
VMEM_LIMIT = 56 * 1024 * 1024

DMA_PRIORITIES = 2

BAND_RADIUS = 64
ATTN_TQ = 128
NEG_INF = float("-inf")


def _cparams(sem):
    return pltpu.CompilerParams(dimension_semantics=sem, vmem_limit_bytes=VMEM_LIMIT)


def _sigmoid(x):
    return 1.0 / (1.0 + jnp.exp(-x))


def _silu(x):
    return x * _sigmoid(x)


def _softplus(x):
    return jnp.maximum(x, 0.0) + jnp.log(1.0 + jnp.exp(-jnp.abs(x)))


def _pack_halves(x):
    c = x.shape[1] // 2
    hi = lax.bitcast_convert_type(x[:, :c].astype(BF16).astype(F32), jnp.uint32)
    lo = lax.bitcast_convert_type(x[:, c:].astype(BF16).astype(F32), jnp.uint32)
    return hi | (lo >> 16)


def _unpack_halves(u):
    left = lax.bitcast_convert_type(u & jnp.uint32(0xFFFF0000), F32)
    right = lax.bitcast_convert_type(u << 16, F32)
    return left, right


def _store_rows(ref, lead, x):
    rows = x.shape[0]
    packed = _pack_halves(x)
    for j in range(ROW_SUBLANES):
        idx = lead + (pl.ds(j, rows, stride=ROW_SUBLANES), slice(None))
        ref[idx] = packed[:, j * LANES:(j + 1) * LANES]


def _load_row_pieces(ref, lead, rows, base=0):
    lefts, rights = [], []
    for j in range(ROW_SUBLANES):
        idx = lead + (pl.ds(base + j, rows, stride=ROW_SUBLANES), slice(None))
        left, right = _unpack_halves(ref[idx])
        lefts.append(left)
        rights.append(right)
    return lefts + rights


def _ada_kernel(c_ref, w_ref, b_ref, o_ref):
    c = _silu(c_ref[...]).astype(BF16)
    o_ref[...] = jnp.dot(c, w_ref[...].astype(BF16), preferred_element_type=F32) + b_ref[...]


def _ada(c_pad, w_ada, b_ada):
    rows = c_pad.shape[0]
    n = w_ada.shape[1]
    tn = 1536
    return pl.pallas_call(
        _ada_kernel,
        grid=(n // tn,),
        in_specs=[pl.BlockSpec((rows, D_MODEL), lambda j: (0, 0)),
                  pl.BlockSpec((D_MODEL, tn), lambda j: (0, j)),
                  pl.BlockSpec((1, tn), lambda j: (0, j))],
        out_specs=pl.BlockSpec((rows, tn), lambda j: (0, j)),
        out_shape=jax.ShapeDtypeStruct((rows, n), F32),
        compiler_params=_cparams(("arbitrary",)),
        name="ada",
    )(c_pad, w_ada, b_ada)


def _inproj_kernel(x_ref, sc_ref, sh_ref, w_ref, o_ref, h_ref):
    @pl.when(pl.program_id(1) == 0)
    def _():
        h_ref[...] = (x_ref[...] * (1.0 + sc_ref[0]) + sh_ref[0]).astype(BF16)

    o_ref[...] = jnp.dot(h_ref[...], w_ref[...], preferred_element_type=F32).astype(o_ref.dtype)


def _inproj(x, scale, shift, w, seg, tm, tn, out_dtype):
    t = x.shape[0]
    n = w.shape[1]
    per_seg = seg // tm
    return pl.pallas_call(
        _inproj_kernel,
        grid=(t // tm, n // tn),
        in_specs=[pl.BlockSpec((tm, D_MODEL), lambda i, j: (i, 0)),
                  pl.BlockSpec((1, 1, D_MODEL), lambda i, j: (i // per_seg, 0, 0)),
                  pl.BlockSpec((1, 1, D_MODEL), lambda i, j: (i // per_seg, 0, 0)),
                  pl.BlockSpec((D_MODEL, tn), lambda i, j: (0, j))],
        out_specs=pl.BlockSpec((tm, tn), lambda i, j: (i, j)),
        out_shape=jax.ShapeDtypeStruct((t, n), out_dtype),
        scratch_shapes=[pltpu.VMEM((tm, D_MODEL), BF16)],
        compiler_params=_cparams(("arbitrary", "arbitrary")),
        name="inproj",
    )(x, scale, shift, w)


def _conv_kernel(xp_ref, x_ref, xn_ref, w_ref, b_ref, o_ref, *, n_l):
    l = pl.program_id(1)
    tl = x_ref.shape[0]
    xp = jnp.where(l > 0, xp_ref[...], 0.0)
    xn = jnp.where(l < n_l - 1, xn_ref[...], 0.0)
    xx = jnp.concatenate([xp, x_ref[...], xn], axis=0)
    acc = b_ref[...] + w_ref[CONV_PAD:CONV_PAD + 1, :] * x_ref[...]
    for k in range(CONV_WIDTH):
        if k != CONV_PAD:
            shifted = pltpu.roll(xx, (CONV_PAD - k) % xx.shape[0], axis=0)
            acc = acc + w_ref[k:k + 1, :] * shifted[SUBLANES:SUBLANES + tl, :]
    o_ref[...] = _silu(acc)


def _conv(proj, conv_w, conv_b, row_off, batch, seq, tl=512, tc=512):
    t = proj.shape[0]
    n_l = seq // tl
    n_c = CONV_DIM // tc
    c0 = COL_XBC // tc
    r8 = tl // SUBLANES
    last8 = t // SUBLANES - 1

    def row_blk(b, l):
        return (row_off + b * seq) // tl + l

    return pl.pallas_call(
        functools.partial(_conv_kernel, n_l=n_l),
        grid=(batch, n_l, n_c),
        in_specs=[pl.BlockSpec((SUBLANES, tc), lambda b, l, c: (jnp.maximum(row_blk(b, l) * r8 - 1, 0), c0 + c)),
                  pl.BlockSpec((tl, tc), lambda b, l, c: (row_blk(b, l), c0 + c)),
                  pl.BlockSpec((SUBLANES, tc), lambda b, l, c: (jnp.minimum((row_blk(b, l) + 1) * r8, last8), c0 + c)),
                  pl.BlockSpec((CONV_WIDTH, tc), lambda b, l, c: (0, c)),
                  pl.BlockSpec((1, tc), lambda b, l, c: (0, c))],
        out_specs=pl.BlockSpec((tl, tc), lambda b, l, c: (b * n_l + l, c)),
        out_shape=jax.ShapeDtypeStruct((batch * seq, CONV_DIM), F32),
        compiler_params=_cparams(("arbitrary", "arbitrary", "arbitrary")),
        name="conv",
    )(proj, proj, proj, conv_w, conv_b)


def _exact_tri_matmul(tri_bf16, x):
    hi = x.astype(BF16)
    r1 = x - hi.astype(F32)
    mid = r1.astype(BF16)
    lo = (r1 - mid.astype(F32)).astype(BF16)
    out = jnp.dot(tri_bf16, hi, preferred_element_type=F32)
    out = out + jnp.dot(tri_bf16, mid, preferred_element_type=F32)
    return out + jnp.dot(tri_bf16, lo, preferred_element_type=F32)


def _ssd_kernel(*refs, reverse, accumulate):
    if accumulate:
        x_ref, b_ref, c_ref, dt_ref, dtb_ref, alog_ref, yin_ref, y_ref, state_ref = refs
    else:
        x_ref, b_ref, c_ref, dt_ref, dtb_ref, alog_ref, y_ref, state_ref = refs
        yin_ref = None
    lane0 = SSD_HEADS if reverse else 0

    @pl.when(pl.program_id(1) == 0)
    def _():
        state_ref[...] = jnp.zeros(state_ref.shape, F32)

    dt = _softplus(dt_ref[...] + dtb_ref[...])
    da = dt * (-jnp.exp(alog_ref[...]))
    row = lax.broadcasted_iota(I32, (SSD_CHUNK, SSD_CHUNK), 0)
    col = lax.broadcasted_iota(I32, (SSD_CHUNK, SSD_CHUNK), 1)
    tri = (col >= row) if reverse else (col <= row)
    cum = _exact_tri_matmul(tri.astype(BF16), da)
    cum_t = cum.T
    dt_t = dt.T
    edge = 0 if reverse else SSD_CHUNK - 1
    total_b = jnp.broadcast_to(cum_t[:, edge:edge + 1], (LANES, SSD_CHUNK))
    to_end_t = dt_t * jnp.exp(total_b - cum_t)
    chunk_decay_b = jnp.exp(total_b)
    src_t = cum_t - jnp.log(dt_t)
    first_head = lax.broadcasted_iota(I32, (1, 2 * SSD_HEAD_DIM), 1) < SSD_HEAD_DIM
    pairs = HEADS_PER_SSD_GROUP // 2

    for g in range(SSD_GROUPS):
        bg = b_ref[:, g * SSD_STATE:(g + 1) * SSD_STATE]
        cg = c_ref[:, g * SSD_STATE:(g + 1) * SSD_STATE]
        cb = lax.dot_general(cg.astype(BF16), bg.astype(BF16), (((1,), (1,)), ((), ())),
                             preferred_element_type=F32)
        bg_t = bg.T
        for pr in range(pairs):
            h0 = g * HEADS_PER_SSD_GROUP + 2 * pr
            cols = slice(h0 * SSD_HEAD_DIM, (h0 + 2) * SSD_HEAD_DIM)
            x_pair = x_ref[:, cols].astype(BF16)
            state = state_ref[g * pairs + pr]
            rhs = jnp.concatenate([x_pair, state.astype(BF16)], axis=0)
            res = []
            new = []
            for e in range(2):
                ln = lane0 + h0 + e
                cum_l = jnp.broadcast_to(cum[:, ln:ln + 1], (SSD_CHUNK, SSD_CHUNK))
                decay = jnp.exp(jnp.where(tri, cum_l - src_t[ln:ln + 1, :], NEG_INF))
                lhs = jnp.concatenate([(cb * decay).astype(BF16), (cg * jnp.exp(cum_l)).astype(BF16)], axis=1)
                res.append(jnp.dot(lhs, rhs, preferred_element_type=F32))
                b_scaled = (bg_t * to_end_t[ln:ln + 1, :]).astype(BF16)
                new.append(jnp.dot(b_scaled, x_pair, preferred_element_type=F32))
            y_pair = jnp.where(first_head, res[0], res[1])
            if accumulate:
                y_pair = y_pair + yin_ref[:, cols]
            y_ref[:, cols] = y_pair
            ln0 = lane0 + h0
            dec = jnp.where(first_head, chunk_decay_b[ln0:ln0 + 1, :], chunk_decay_b[ln0 + 1:ln0 + 2, :])
            state_ref[g * pairs + pr] = state * dec + jnp.where(first_head, new[0], new[1])


def _ssd(xbc, dt_raw, dt_bias, a_log, y_in, row_off, batch, seq, reverse):
    n_c = seq // SSD_CHUNK
    accumulate = y_in is not None

    def loc(b, c):
        return b * n_c + (n_c - 1 - c if reverse else c)

    def glob(b, c):
        return row_off // SSD_CHUNK + loc(b, c)

    in_specs = [pl.BlockSpec((SSD_CHUNK, D_INNER), lambda b, c: (loc(b, c), 0)),
                pl.BlockSpec((SSD_CHUNK, SSD_GROUPS * SSD_STATE), lambda b, c: (loc(b, c), 2)),
                pl.BlockSpec((SSD_CHUNK, SSD_GROUPS * SSD_STATE), lambda b, c: (loc(b, c), 3)),
                pl.BlockSpec((SSD_CHUNK, DT_COLS), lambda b, c: (glob(b, c), 0)),
                pl.BlockSpec((1, DT_COLS), lambda b, c: (0, 0)),
                pl.BlockSpec((1, DT_COLS), lambda b, c: (0, 0))]
    args = [xbc, xbc, xbc, dt_raw, dt_bias, a_log]
    aliases = {}
    if accumulate:
        in_specs.append(pl.BlockSpec((SSD_CHUNK, D_INNER), lambda b, c: (loc(b, c), 0)))
        args.append(y_in)
        aliases = {len(args) - 1: 0}
    return pl.pallas_call(
        functools.partial(_ssd_kernel, reverse=reverse, accumulate=accumulate),
        grid=(batch, n_c),
        in_specs=in_specs,
        out_specs=pl.BlockSpec((SSD_CHUNK, D_INNER), lambda b, c: (loc(b, c), 0)),
        out_shape=jax.ShapeDtypeStruct((batch * seq, D_INNER), F32),
        scratch_shapes=[pltpu.VMEM((SSD_HEADS // 2, SSD_STATE, 2 * SSD_HEAD_DIM), F32)],
        input_output_aliases=aliases,
        compiler_params=_cparams(("arbitrary", "arbitrary")),
        name="ssd_bwd" if reverse else "ssd_fwd",
    )(*args)


def _ssd_out_kernel(y_ref, x_ref, z_ref, dskip_ref, nw_ref, w_ref, o_ref):
    y = y_ref[...] + dskip_ref[...] * x_ref[...]
    y = y * _silu(z_ref[...])
    ms = jnp.mean(y * y, axis=-1, keepdims=True)
    yn = y * lax.rsqrt(ms + NORM_EPS) * nw_ref[...]
    o_ref[...] = jnp.dot(yn.astype(BF16), w_ref[...], preferred_element_type=F32)


def _ssd_out(y, xbc, proj, d_skip_cols, norm_w, w_branch, row_off, tm=512):
    t = y.shape[0]
    return pl.pallas_call(
        _ssd_out_kernel,
        grid=(t // tm,),
        in_specs=[pl.BlockSpec((tm, D_INNER), lambda i: (i, 0)),
                  pl.BlockSpec((tm, D_INNER), lambda i: (i, 0)),
                  pl.BlockSpec((tm, D_INNER), lambda i: (row_off // tm + i, COL_Z // D_INNER)),
                  pl.BlockSpec((1, D_INNER), lambda i: (0, 0)),
                  pl.BlockSpec((1, D_INNER), lambda i: (0, 0)),
                  pl.BlockSpec((D_INNER, D_MODEL), lambda i: (0, 0))],
        out_specs=pl.BlockSpec((tm, D_MODEL), lambda i: (i, 0)),
        out_shape=jax.ShapeDtypeStruct((t, D_MODEL), F32),
        compiler_params=_cparams(("arbitrary",)),
        name="ssd_out",
    )(y, xbc, proj, d_skip_cols, norm_w, w_branch)


def _prep_kernel(q_ref, k_ref, v_ref, cos_ref, sa_ref, sb_ref, qo_ref, ko_ref, vo_ref, *, dil):
    rows = q_ref.shape[0] // dil

    def phase(ref, p):
        if dil == 1:
            return ref[...]
        return ref[pl.ds(p, rows, stride=dil), :]

    for p in range(dil):
        cos = phase(cos_ref, p)
        sa = phase(sa_ref, p)
        sb = phase(sb_ref, p)

        def rot(t):
            return (t * cos + pltpu.roll(t, LANES - ROT_HALF, axis=1) * sa
                    + pltpu.roll(t, ROT_HALF, axis=1) * sb)

        qo_ref[0, p] = (rot(phase(q_ref, p)) * (ATTN_HEAD_DIM ** -0.5)).astype(qo_ref.dtype)
        ko_ref[0, p] = rot(phase(k_ref, p)).astype(ko_ref.dtype)
        vo_ref[0, p] = phase(v_ref, p).astype(vo_ref.dtype)


def _prep(proj, tables, gi, dil, row_off, batch, seq, rows_per_step=2048):
    r = rows_per_step
    n_i = seq // r
    n_c = ATTN_GROUP_COLS // LANES
    m = seq // dil
    cos_t, sa_t, sb_t = tables

    def in_blk(col):
        cb = (col + gi * ATTN_GROUP_COLS) // LANES
        return pl.BlockSpec((r, LANES), lambda b, i, c: ((row_off + b * seq) // r + i, cb + c))

    tab = pl.BlockSpec((r, LANES), lambda b, i, c: (i, 0))
    out = pl.BlockSpec((1, dil, r // dil, LANES), lambda b, i, c: (b, 0, i, c))
    shape = jax.ShapeDtypeStruct((batch, dil, m, ATTN_GROUP_COLS), BF16)
    return pl.pallas_call(
        functools.partial(_prep_kernel, dil=dil),
        grid=(batch, n_i, n_c),
        in_specs=[in_blk(COL_Q), in_blk(COL_K), in_blk(COL_V), tab, tab, tab],
        out_specs=[out, out, out],
        out_shape=[shape, shape, shape],
        compiler_params=_cparams(("arbitrary", "arbitrary", "arbitrary")),
        name=f"attn_prep_d{dil}",
    )(proj, proj, proj, cos_t, sa_t, sb_t)


def _band_attn_kernel(q_ref, kp_ref, kc_ref, kn_ref, vp_ref, vc_ref, vn_ref, o_ref, lse_ref, *, m_len):
    i = pl.program_id(1)
    tq = ATTN_TQ
    n_sub = q_ref.shape[1] // tq
    tk = tq + 2 * BAND_RADIUS
    q_all = q_ref[0].astype(BF16)
    k_all = jnp.concatenate([kp_ref[0], kc_ref[0], kn_ref[0]], axis=0).astype(BF16)
    v_all = jnp.concatenate([vp_ref[0], vc_ref[0], vn_ref[0]], axis=0).astype(BF16)
    first_head = lax.broadcasted_iota(I32, (1, LANES), 1) < ATTN_HEAD_DIM
    ones = jnp.ones((tk, LANES), BF16)
    zero = jnp.zeros((), BF16)
    for sub in range(n_sub):
        start = (i * n_sub + sub) * tq
        rows = slice(sub * tq, (sub + 1) * tq)
        q = q_all[rows]
        k = k_all[sub * tq:sub * tq + tk]
        v = v_all[sub * tq:sub * tq + tk]
        qpos = start + lax.broadcasted_iota(I32, (tq, tk), 0)
        kpos = start - BAND_RADIUS + lax.broadcasted_iota(I32, (tq, tk), 1)
        valid = (jnp.abs(qpos - kpos) <= BAND_RADIUS) & (kpos >= 0) & (kpos < m_len)
        for pr in range(ATTN_GROUP_COLS // LANES):
            cols = slice(pr * LANES, (pr + 1) * LANES)
            q_pair, k_pair = q[:, cols], k[:, cols]
            v_ext = jnp.concatenate([v[:, cols], ones], axis=1)
            outs, lses = [], []
            for sel in (first_head, jnp.logical_not(first_head)):
                s = lax.dot_general(jnp.where(sel, q_pair, zero), k_pair, (((1,), (1,)), ((), ())),
                                    preferred_element_type=F32)
                s = jnp.where(valid, s, NEG_INF)
                mx = jnp.max(s, axis=-1, keepdims=True)
                p = jnp.exp(s - mx).astype(BF16)
                ov = jnp.dot(p, v_ext, preferred_element_type=F32)
                den = ov[:, LANES:]
                outs.append(ov[:, :LANES] / den)
                lses.append(mx + jnp.log(den))
            o_ref[0, rows, cols] = jnp.where(first_head, outs[0], outs[1])
            lse_ref[0, rows, cols] = jnp.where(first_head, lses[0], lses[1])


def _band_attn(q, k, v):
    b, d, m, c = q.shape
    n = b * d
    q, k, v = (a.reshape(n, m, c) for a in (q, k, v))
    tq = ATTN_TQ * (2 if m % (2 * ATTN_TQ) == 0 else 1)
    per = tq // BAND_RADIUS
    last = m // BAND_RADIUS - 1
    main = pl.BlockSpec((1, tq, c), lambda s, i: (s, i, 0))
    prev = pl.BlockSpec((1, BAND_RADIUS, c), lambda s, i: (s, jnp.maximum(i * per - 1, 0), 0))
    nxt = pl.BlockSpec((1, BAND_RADIUS, c), lambda s, i: (s, jnp.minimum((i + 1) * per, last), 0))
    shape = jax.ShapeDtypeStruct((n, m, c), F32)
    o, lse = pl.pallas_call(
        functools.partial(_band_attn_kernel, m_len=m),
        grid=(n, m // tq),
        in_specs=[main, prev, main, nxt, prev, main, nxt],
        out_specs=[main, main],
        out_shape=[shape, shape],
        compiler_params=_cparams(("arbitrary", "arbitrary")),
        name=f"band_attn_m{m}",
    )(q, k, k, k, v, v, v)
    return o.reshape(b, d, m, c), lse.reshape(b, d, m, c)


def _layer_norm(x, g, b):
    mu = jnp.mean(x, axis=-1, keepdims=True)
    xc = x - mu
    var = jnp.mean(xc * xc, axis=-1, keepdims=True)
    return xc * lax.rsqrt(var + NORM_EPS) * g + b


def _merge_kernel(x_ref, yssd_ref, gs_ref, ga_ref, o1_ref, l1_ref, o2_ref, l2_ref, o3_ref, l3_ref,
                  g1_ref, sc2_ref, sh2_ref, wba_ref, wout_ref, lng_ref, lnb_ref, wr_ref, *rest, n_prev):
    x1_ref, h2_ref, logit_ref, scr_ref = rest[n_prev:]
    tm = x_ref.shape[0]

    def interleaved(ref, dil, slot):
        if dil == 1:
            return ref[0, 0]
        rows = tm // dil
        n_c = ATTN_GROUP_COLS // LANES
        for p in range(dil):
            for c in range(n_c):
                scr_ref[slot * n_c + c, pl.ds(p, rows, stride=dil), :] = ref[0, p, :, c * LANES:(c + 1) * LANES]
        return jnp.concatenate([scr_ref[slot * n_c + c] for c in range(n_c)], axis=1)

    outs = []
    lses = []
    slot = 0
    for (o_ref, l_ref), (_, dil) in zip(((o1_ref, l1_ref), (o2_ref, l2_ref), (o3_ref, l3_ref)), ATTN_PATTERNS):
        outs.append(interleaved(o_ref, dil, slot))
        lses.append(interleaved(l_ref, dil, slot + 1))
        slot += 2
    mx = jnp.maximum(jnp.maximum(lses[0], lses[1]), lses[2])
    es = [jnp.exp(l - mx) for l in lses]
    den = es[0] + es[1] + es[2]
    y_att = (es[0] * outs[0] + es[1] * outs[1] + es[2] * outs[2]) / den
    y_attn = jnp.dot(y_att.astype(BF16), wba_ref[...], preferred_element_type=F32)
    merged = _sigmoid(gs_ref[...]) * yssd_ref[...] + _sigmoid(ga_ref[...]) * y_attn
    mix = jnp.dot(merged.astype(BF16), wout_ref[...], preferred_element_type=F32)
    x1 = _layer_norm(ALPHA * x_ref[...] + g1_ref[0] * mix, lng_ref[...], lnb_ref[...])
    x1_ref[...] = x1
    h2 = x1 * (1.0 + sc2_ref[0]) + sh2_ref[0]
    _store_rows(h2_ref, (), h2)
    logit_ref[...] = lax.dot_general(wr_ref[...], h2.astype(BF16), (((1,), (1,)), ((), ())),
                                     preferred_element_type=F32)


def _merge(x, y_ssd, proj, attn, gate1, scale2, shift2, w_ba, w_out, ln_g, ln_b, w_router_t,
           batch, seq, out_row_off, t_all, prev, tm=256):
    n_i = seq // tm
    out0 = out_row_off // tm

    def rows(col_blk):
        return pl.BlockSpec((tm, D_MODEL), lambda b, i: (b * n_i + i, col_blk))

    def local(width):
        return pl.BlockSpec((tm, width), lambda b, i: (b * n_i + i, 0))

    def modv():
        return pl.BlockSpec((1, 1, D_MODEL), lambda b, i: (b, 0, 0))

    def full(shape):
        return pl.BlockSpec(shape, lambda b, i: tuple(0 for _ in shape))

    attn_specs = []
    attn_args = []
    for (o, lse), (_, dil) in zip(attn, ATTN_PATTERNS):
        spec = pl.BlockSpec((1, dil, tm // dil, ATTN_GROUP_COLS), lambda b, i: (b, 0, i, 0))
        attn_specs += [spec, spec]
        attn_args += [o, lse]
    n_in = 4 + len(attn_args) + 8
    return pl.pallas_call(
        functools.partial(_merge_kernel, n_prev=0 if prev is None else 3),
        grid=(batch, n_i),
        in_specs=[rows(0), local(D_MODEL), rows(COL_GSSD // D_MODEL), rows(COL_GATTN // D_MODEL)] + attn_specs
                 + [modv(), modv(), modv(), full((ATTN_GROUP_COLS, D_MODEL)), full((D_MODEL, D_MODEL)),
                    full((1, D_MODEL)), full((1, D_MODEL)), full((N_EXPERTS, D_MODEL))]
                 + ([pl.BlockSpec(memory_space=pl.ANY)] * 3 if prev is not None else []),
        out_specs=[pl.BlockSpec((tm, D_MODEL), lambda b, i: (out0 + b * n_i + i, 0)),
                   pl.BlockSpec((tm * ROW_SUBLANES, LANES), lambda b, i: (out0 + b * n_i + i, 0)),
                   pl.BlockSpec((N_EXPERTS, tm), lambda b, i: (0, out0 + b * n_i + i))],
        out_shape=[jax.ShapeDtypeStruct((t_all, D_MODEL), F32),
                   jax.ShapeDtypeStruct((t_all * ROW_SUBLANES, LANES), jnp.uint32),
                   jax.ShapeDtypeStruct((N_EXPERTS, t_all), F32)],
        scratch_shapes=[pltpu.VMEM((2 * ATTN_GROUPS * (ATTN_GROUP_COLS // LANES), tm, LANES), F32)],
        input_output_aliases={n_in + j: j for j in range(3)} if prev is not None else {},
        compiler_params=_cparams(("arbitrary", "arbitrary")),
        name="mixer_merge",
    )(x, y_ssd, proj, proj, *attn_args, gate1, scale2, shift2, w_ba, w_out, ln_g, ln_b, w_router_t,
      *(prev if prev is not None else ()))


def _route_kernel(logit_ref, bias_ref, idx_ref, w_ref, rank_ref, cnt_ref, run_ref):
    tm = logit_ref.shape[1]

    @pl.when(pl.program_id(0) == 0)
    def _():
        run_ref[...] = jnp.zeros(run_ref.shape, F32)

    scores = _sigmoid(logit_ref[...])
    sel = scores + bias_ref[...]
    iota_g = lax.broadcasted_iota(I32, (EXPERTS_PER_GROUP, tm), 0)
    grp = []
    for g in range(N_EXPERT_GROUPS):
        sg = sel[g * EXPERTS_PER_GROUP:(g + 1) * EXPERTS_PER_GROUP, :]
        top1 = jnp.max(sg, axis=0, keepdims=True)
        first = jnp.min(jnp.where(sg == top1, iota_g, EXPERTS_PER_GROUP), axis=0, keepdims=True)
        top2 = jnp.max(jnp.where(iota_g == first, NEG_INF, sg), axis=0, keepdims=True)
        grp.append(top1 + top2)
    gs = jnp.concatenate(grp, axis=0)
    iota_n = lax.broadcasted_iota(I32, (N_EXPERT_GROUPS, tm), 0)
    keep_g = jnp.zeros((N_EXPERT_GROUPS, tm), F32)
    for _ in range(TOPK_GROUPS):
        best = jnp.max(gs, axis=0, keepdims=True)
        first = jnp.min(jnp.where(gs == best, iota_n, N_EXPERT_GROUPS), axis=0, keepdims=True)
        hit = iota_n == first
        keep_g = jnp.where(hit, 1.0, keep_g)
        gs = jnp.where(hit, NEG_INF, gs)
    cand = jnp.concatenate(
        [jnp.where(keep_g[g:g + 1, :] > 0.0, sel[g * EXPERTS_PER_GROUP:(g + 1) * EXPERTS_PER_GROUP, :], NEG_INF)
         for g in range(N_EXPERT_GROUPS)], axis=0)
    iota_e = lax.broadcasted_iota(I32, (N_EXPERTS, tm), 0)
    chosen = jnp.zeros((N_EXPERTS, tm), F32)
    idxs = []
    ws = []
    for _ in range(TOP_K):
        best = jnp.max(cand, axis=0, keepdims=True)
        first = jnp.min(jnp.where(cand == best, iota_e, N_EXPERTS), axis=0, keepdims=True)
        hit = iota_e == first
        idxs.append(first)
        ws.append(jnp.sum(jnp.where(hit, scores, 0.0), axis=0, keepdims=True))
        chosen = jnp.where(hit, 1.0, chosen)
        cand = jnp.where(hit, NEG_INF, cand)
    top_w = jnp.concatenate(ws, axis=0)
    top_w = top_w / jnp.sum(top_w, axis=0, keepdims=True) * ROUTED_SCALE
    idx_ref[...] = jnp.concatenate(idxs, axis=0)
    w_ref[...] = top_w
    s_i = lax.broadcasted_iota(I32, (tm, tm), 0)
    t_i = lax.broadcasted_iota(I32, (tm, tm), 1)
    before = (s_i < t_i).astype(BF16)
    prior = jnp.dot(chosen.astype(BF16), before, preferred_element_type=F32) + run_ref[...]
    ranks = [jnp.sum(jnp.where(iota_e == idxs[k], prior, 0.0), axis=0, keepdims=True) for k in range(TOP_K)]
    rank_ref[...] = jnp.concatenate(ranks, axis=0).astype(I32)
    run_ref[...] = run_ref[...] + jnp.sum(chosen, axis=1, keepdims=True)
    cnt_ref[...] = run_ref[...].astype(I32)


def _route(logits_t, bias_col, tm=256):
    t = logits_t.shape[1]
    tok = pl.BlockSpec((TOP_K, tm), lambda i: (0, i))
    return pl.pallas_call(
        _route_kernel,
        grid=(t // tm,),
        in_specs=[pl.BlockSpec((N_EXPERTS, tm), lambda i: (0, i)),
                  pl.BlockSpec((N_EXPERTS, 1), lambda i: (0, 0))],
        out_specs=[tok, tok, tok, pl.BlockSpec((N_EXPERTS, 1), lambda i: (0, 0))],
        out_shape=[jax.ShapeDtypeStruct((TOP_K, t), I32), jax.ShapeDtypeStruct((TOP_K, t), F32),
                   jax.ShapeDtypeStruct((TOP_K, t), I32), jax.ShapeDtypeStruct((N_EXPERTS, 1), I32)],
        scratch_shapes=[pltpu.VMEM((N_EXPERTS, 1), F32)],
        compiler_params=_cparams(("arbitrary",)),
        name="route",
    )(logits_t, bias_col)


def _dest_kernel(idx_ref, rank_ref, start_ref, o_ref):
    tm = idx_ref.shape[1]
    iota_e = lax.broadcasted_iota(I32, (N_EXPERTS, tm), 0)
    start = start_ref[...].astype(F32)
    for k in range(TOP_K):
        base = jnp.sum(jnp.where(iota_e == idx_ref[k:k + 1, :], start, 0.0), axis=0, keepdims=True)
        o_ref[0, :, k * tm:(k + 1) * tm] = (base.astype(I32) + rank_ref[k:k + 1, :]) * ROW_SUBLANES


def _dest(top_idx, rank, pad_start_col, tm):
    t = top_idx.shape[1]
    tok = pl.BlockSpec((TOP_K, tm), lambda i: (0, i))
    out = pl.pallas_call(
        _dest_kernel,
        grid=(t // tm,),
        in_specs=[tok, tok, pl.BlockSpec((N_EXPERTS, 1), lambda i: (0, 0))],
        out_specs=pl.BlockSpec((1, 1, TOP_K * tm), lambda i: (i, 0, 0)),
        out_shape=jax.ShapeDtypeStruct((t // tm, 1, TOP_K * tm), I32),
        compiler_params=_cparams(("arbitrary",)),
        name="moe_dest",
    )(top_idx, rank, pad_start_col)
    return out.reshape(t // tm, TOP_K * tm)


def _dispatch_kernel(pad_end_ref, dest_hbm, h_ref, xs_hbm, dest_smem, zeros_ref, sem_idx, sem_zero, sem_row,
                     *, tm):
    i = pl.program_id(0)
    idx_copy = pltpu.make_async_copy(dest_hbm.at[i], dest_smem, sem_idx)
    idx_copy.start()

    blk_rows = EXPERT_BLOCK * ROW_SUBLANES

    def zero_copy(e):
        start = pl.multiple_of(jnp.maximum(pad_end_ref[e] - EXPERT_BLOCK, 0) * ROW_SUBLANES, blk_rows)
        return pltpu.make_async_copy(zeros_ref, xs_hbm.at[pl.ds(start, blk_rows)], sem_zero)

    @pl.when(i == 0)
    def _():
        zeros_ref[...] = jnp.zeros(zeros_ref.shape, zeros_ref.dtype)

        def start_zero(e, carry):
            zero_copy(e).start()
            return carry

        def wait_zero(e, carry):
            zero_copy(e).wait()
            return carry

        lax.fori_loop(0, N_EXPERTS, start_zero, 0)
        lax.fori_loop(0, N_EXPERTS, wait_zero, 0)

    idx_copy.wait()

    def row_copy(t, k):
        dst = pl.multiple_of(dest_smem[k * tm + t], ROW_SUBLANES)
        return pltpu.make_async_copy(h_ref.at[pl.ds(pl.multiple_of(t * ROW_SUBLANES, ROW_SUBLANES), ROW_SUBLANES)],
                                     xs_hbm.at[pl.ds(dst, ROW_SUBLANES)], sem_row)

    def start_rows(t, carry):
        for k in range(TOP_K):
            row_copy(t, k).start(priority=k % DMA_PRIORITIES)
        return carry

    def wait_rows(t, carry):
        for k in range(TOP_K):
            row_copy(t, k).wait()
        return carry

    lax.fori_loop(0, tm, start_rows, 0)
    lax.fori_loop(0, tm, wait_rows, 0)


def _dispatch(pad_end, dest_tiles, h2, n_rows, tm):
    n_tiles = dest_tiles.shape[0]
    return pl.pallas_call(
        functools.partial(_dispatch_kernel, tm=tm),
        grid_spec=pltpu.PrefetchScalarGridSpec(
            num_scalar_prefetch=1,
            grid=(n_tiles,),
            in_specs=[pl.BlockSpec(memory_space=pl.ANY),
                      pl.BlockSpec((tm * ROW_SUBLANES, LANES), lambda i, pe: (i, 0))],
            out_specs=pl.BlockSpec(memory_space=pl.ANY),
            scratch_shapes=[pltpu.SMEM((TOP_K * tm,), I32),
                            pltpu.VMEM((EXPERT_BLOCK * ROW_SUBLANES, LANES), jnp.uint32),
                            pltpu.SemaphoreType.DMA, pltpu.SemaphoreType.DMA, pltpu.SemaphoreType.DMA]),
        out_shape=jax.ShapeDtypeStruct((n_rows * ROW_SUBLANES, LANES), jnp.uint32),
        compiler_params=_cparams(("arbitrary",)),
        name="moe_dispatch",
    )(pad_end, dest_tiles, h2)


def _experts_kernel(blk_exp_ref, n_valid_ref, xs_ref, wg_ref, wu_ref, wd_ref, ys_ref, wgu_ref, wdn_ref):
    i = pl.program_id(0)

    @pl.when(i < n_valid_ref[0])
    def _():
        @pl.when((i == 0) | (blk_exp_ref[i] != blk_exp_ref[jnp.maximum(i - 1, 0)]))
        def _():
            wgu_ref[:, :EXPERT_HIDDEN] = wg_ref[0].astype(BF16)
            wgu_ref[:, EXPERT_HIDDEN:] = wu_ref[0].astype(BF16)
            wdn_ref[...] = wd_ref[0].astype(BF16)

        x = jnp.concatenate([c.astype(BF16) for c in _load_row_pieces(xs_ref, (), EXPERT_BLOCK)], axis=1)
        gu = jnp.dot(x, wgu_ref[...], preferred_element_type=F32)
        hmid = (_silu(gu[:, :EXPERT_HIDDEN]) * gu[:, EXPERT_HIDDEN:]).astype(BF16)
        _store_rows(ys_ref, (), jnp.dot(hmid, wdn_ref[...], preferred_element_type=F32))


def _experts(blk_exp, n_valid, xs, w_gate, w_up, w_down):
    n_blocks = xs.shape[0] // (EXPERT_BLOCK * ROW_SUBLANES)

    def row_map(i, be, nv):
        return (jnp.minimum(i, nv[0] - 1), 0)

    def w_map(i, be, nv):
        return (be[i], 0, 0)

    return pl.pallas_call(
        _experts_kernel,
        grid_spec=pltpu.PrefetchScalarGridSpec(
            num_scalar_prefetch=2,
            grid=(n_blocks,),
            in_specs=[pl.BlockSpec((EXPERT_BLOCK * ROW_SUBLANES, LANES), row_map),
                      pl.BlockSpec((1, D_MODEL, EXPERT_HIDDEN), w_map),
                      pl.BlockSpec((1, D_MODEL, EXPERT_HIDDEN), w_map),
                      pl.BlockSpec((1, EXPERT_HIDDEN, D_MODEL), w_map)],
            out_specs=pl.BlockSpec((EXPERT_BLOCK * ROW_SUBLANES, LANES), row_map),
            scratch_shapes=[pltpu.VMEM((D_MODEL, 2 * EXPERT_HIDDEN), BF16),
                            pltpu.VMEM((EXPERT_HIDDEN, D_MODEL), BF16)]),
        out_shape=jax.ShapeDtypeStruct(xs.shape, jnp.uint32),
        compiler_params=_cparams(("arbitrary",)),
        name="moe_experts",
    )(blk_exp, n_valid, xs, w_gate, w_up, w_down)


def _combine_kernel(dest_hbm, ys_hbm, w_ref, x1_ref, h2_ref, g2_ref, wsg_ref, wsu_ref, wsd_ref,
                    lng_ref, lnb_ref, o_ref, dest0, dest1, rows0, rows1, sem_idx, sem0, sem1, *, tm, tile_off):
    i = pl.program_id(0)
    n_i = pl.num_programs(0)
    dest_smem, rows_ref, sem_row = (dest0, dest1), (rows0, rows1), (sem0, sem1)

    def row_copy(s, t, k):
        src = pl.multiple_of(dest_smem[s][k * tm + t], ROW_SUBLANES)
        dst = pl.multiple_of(t * ROW_SUBLANES, ROW_SUBLANES)
        return pltpu.make_async_copy(ys_hbm.at[pl.ds(src, ROW_SUBLANES)],
                                     rows_ref[s].at[k, pl.ds(dst, ROW_SUBLANES)], sem_row[s])

    def start_gather(tile, s):
        idx_copy = pltpu.make_async_copy(dest_hbm.at[tile + tile_off], dest_smem[s], sem_idx)
        idx_copy.start()
        idx_copy.wait()

        def start_rows(t, carry):
            for k in range(TOP_K):
                row_copy(s, t, k).start(priority=k % DMA_PRIORITIES)
            return carry

        lax.fori_loop(0, tm, start_rows, 0)

    @pl.when(i == 0)
    def _():
        start_gather(0, 0)

    for s in range(2):
        if s == 0:
            start_gather(2 * i + 1, 1)
        else:
            @pl.when(i + 1 < n_i)
            def _():
                start_gather(2 * i + 2, 0)

        rows = slice(s * tm, (s + 1) * tm)
        h2 = jnp.concatenate(
            [c.astype(BF16) for c in _load_row_pieces(h2_ref, (), tm, s * tm * ROW_SUBLANES)], axis=1)
        g = jnp.dot(h2, wsg_ref[...], preferred_element_type=F32)
        u = jnp.dot(h2, wsu_ref[...], preferred_element_type=F32)
        ffn = jnp.dot((_silu(g) * u).astype(BF16), wsd_ref[...], preferred_element_type=F32)

        def wait_rows(t, carry):
            for k in range(TOP_K):
                row_copy(s, t, k).wait()
            return carry

        lax.fori_loop(0, tm, wait_rows, 0)
        w = w_ref[rows, :]
        n_pieces = D_MODEL // LANES
        acc = [ffn[:, j * LANES:(j + 1) * LANES] for j in range(n_pieces)]
        for k in range(TOP_K):
            wk = jnp.broadcast_to(w[:, k:k + 1], (tm, LANES))
            pieces = _load_row_pieces(rows_ref[s], (k,), tm)
            acc = [a + wk * p for a, p in zip(acc, pieces)]
        ffn = jnp.concatenate(acc, axis=1)
        o_ref[rows, :] = _layer_norm(ALPHA * x1_ref[rows, :] + g2_ref[0] * ffn, lng_ref[...], lnb_ref[...])


def _combine(dest_tiles, ys, top_w_rows, x1, h2, gate2, w_sg, w_su, w_sd, ln_g, ln_b, seg, tm, tile_off, n_tiles):
    assert tile_off % 2 == 0 and n_tiles % 2 == 0 and seg % (2 * tm) == 0
    per_seg = seg // (2 * tm)
    off = tile_off // 2

    def full(shape):
        return pl.BlockSpec(shape, lambda i: tuple(0 for _ in shape))

    return pl.pallas_call(
        functools.partial(_combine_kernel, tm=tm, tile_off=tile_off),
        grid=(n_tiles // 2,),
        in_specs=[pl.BlockSpec(memory_space=pl.ANY), pl.BlockSpec(memory_space=pl.ANY),
                  pl.BlockSpec((2 * tm, TOP_K), lambda i: (i + off, 0)),
                  pl.BlockSpec((2 * tm, D_MODEL), lambda i: (i + off, 0)),
                  pl.BlockSpec((2 * tm * ROW_SUBLANES, LANES), lambda i: (i + off, 0)),
                  pl.BlockSpec((1, 1, D_MODEL), lambda i: ((i + off) // per_seg, 0, 0)),
                  full((D_MODEL, SHARED_HIDDEN)), full((D_MODEL, SHARED_HIDDEN)), full((SHARED_HIDDEN, D_MODEL)),
                  full((1, D_MODEL)), full((1, D_MODEL))],
        out_specs=pl.BlockSpec((2 * tm, D_MODEL), lambda i: (i, 0)),
        out_shape=jax.ShapeDtypeStruct((n_tiles * tm, D_MODEL), F32),
        scratch_shapes=[pltpu.SMEM((TOP_K * tm,), I32), pltpu.SMEM((TOP_K * tm,), I32),
                        pltpu.VMEM((TOP_K, tm * ROW_SUBLANES, LANES), jnp.uint32),
                        pltpu.VMEM((TOP_K, tm * ROW_SUBLANES, LANES), jnp.uint32),
                        pltpu.SemaphoreType.DMA, pltpu.SemaphoreType.DMA, pltpu.SemaphoreType.DMA],
        compiler_params=_cparams(("arbitrary",)),
        name="moe_combine",
    )(dest_tiles, ys, top_w_rows, x1, h2, gate2, w_sg, w_su, w_sd, ln_g, ln_b)


def _rope_tables(seq):
    inv_freq = ROPE_THETA ** (-jnp.arange(ROT_HALF, dtype=F32) * 2.0 / ROT_DIM)
    ang = jnp.arange(seq, dtype=F32)[:, None] * inv_freq[None, :]
    cos, sin = jnp.cos(ang), jnp.sin(ang)
    rest = ATTN_HEAD_DIM - ROT_DIM
    ones = jnp.ones((seq, rest), F32)
    zeros = jnp.zeros((seq, rest), F32)
    zh = jnp.zeros((seq, ROT_HALF), F32)
    reps = LANES // ATTN_HEAD_DIM
    cos_t = jnp.tile(jnp.concatenate([cos, cos, ones], axis=1), (1, reps))
    sa_t = jnp.tile(jnp.concatenate([-sin, zh, zeros], axis=1), (1, reps))
    sb_t = jnp.tile(jnp.concatenate([zh, sin, zeros], axis=1), (1, reps))
    return cos_t, sa_t, sb_t


def _split_in_weights(w_in):
    cuts = (D_INNER, CONV_DIM, SSD_HEADS, SSD_HEADS, ATTN_DIM, ATTN_DIM, ATTN_DIM, D_MODEL, D_MODEL)
    offs = [0]
    for c in cuts:
        offs.append(offs[-1] + c)
    z, xbc, dtf, dtb, q, k, v, gs, ga = (w_in[:, offs[i]:offs[i + 1]] for i in range(len(cuts)))
    w_main = jnp.concatenate([z, xbc, gs, ga, q, k, v], axis=1).astype(BF16)
    assert COL_GSSD % D_MODEL == 0 and COL_Q % ATTN_GROUP_COLS == 0 and w_main.shape[1] == N_MAIN
    pad = jnp.zeros((D_MODEL, DT_COLS - 2 * SSD_HEADS), F32)
    w_dt = jnp.concatenate([dtf, dtb, pad], axis=1).astype(BF16)
    return w_main, w_dt


def kernel(x_prompt, x_sample, c_prompt, c_sample, w_ada, b_ada, w_in, conv_w, conv_b, dt_bias_fwd, dt_bias_bwd, a_log_fwd, a_log_bwd, d_skip, ssd_norm_w, w_branch_ssd, w_branch_attn, w_out, ln1_g, ln1_b, w_router, router_bias, w_exp_gate, w_exp_up, w_exp_down, w_sh_gate, w_sh_up, w_sh_down, ln2_g, ln2_b):
    assert w_ada.shape[0] == DEPTH
    groups = ((x_prompt, c_prompt), (x_sample, c_sample))
    shapes = [(x.shape[0], x.shape[1]) for x, _ in groups]
    seg = math.gcd(*(s for _, s in shapes))
    tokens = [b * s for b, s in shapes]
    t_all = sum(tokens)
    row_offs = [0, tokens[0]]

    c_all = jnp.concatenate([c_prompt, c_sample], axis=0)
    n_req = c_all.shape[0]
    c_pad = jnp.zeros((-(-n_req // SUBLANES) * SUBLANES, D_MODEL), F32).at[:n_req].set(c_all)
    ada = _ada(c_pad, w_ada[0], b_ada)[:n_req]
    req_offs = [0, shapes[0][0]]
    reps = jnp.array([s // seg for b, s in shapes for _ in range(b)], dtype=I32)
    n_seg = t_all // seg
    gate2 = jnp.repeat(ada[:, 5 * D_MODEL:], reps, axis=0, total_repeat_length=n_seg).reshape(n_seg, 1, D_MODEL)

    w_main, w_dt = _split_in_weights(w_in[0])
    dt_pad = jnp.zeros((DT_COLS - 2 * SSD_HEADS,), F32)
    dt_bias = jnp.concatenate([dt_bias_fwd[0], dt_bias_bwd[0], dt_pad]).reshape(1, DT_COLS)
    a_log = jnp.concatenate([a_log_fwd[0], a_log_bwd[0], dt_pad]).reshape(1, DT_COLS)
    d_skip_cols = jnp.repeat(d_skip[0], SSD_HEAD_DIM).reshape(1, D_INNER)
    w_bs = w_branch_ssd[0].astype(BF16)
    w_ba = w_branch_attn[0].astype(BF16)
    w_o = w_out[0].astype(BF16)
    w_router_t = w_router[0].T.astype(BF16)

    merged = None
    for (x, _), (batch, seq), row_off, req_off in zip(groups, shapes, row_offs, req_offs):
        shift1, scale1, gate1, shift2, scale2, _ = (
            a.reshape(batch, 1, D_MODEL) for a in jnp.split(ada[req_off:req_off + batch], 6, axis=-1))
        x2d = x.reshape(batch * seq, D_MODEL)
        proj = _inproj(x2d, scale1, shift1, w_main, seq, tm=1024, tn=2560, out_dtype=F32)
        dt_raw = _inproj(x2d, scale1, shift1, w_dt, seq, tm=1024, tn=DT_COLS, out_dtype=F32)
        xbc = _conv(proj, conv_w[0], conv_b, 0, batch, seq)
        y = _ssd(xbc, dt_raw, dt_bias, a_log, None, 0, batch, seq, reverse=False)
        y = _ssd(xbc, dt_raw, dt_bias, a_log, y, 0, batch, seq, reverse=True)
        y_ssd = _ssd_out(y, xbc, proj, d_skip_cols, ssd_norm_w, w_bs, 0)
        tables = _rope_tables(seq)
        attn = []
        for gi, (_, dil) in enumerate(ATTN_PATTERNS):
            q, k, v = _prep(proj, tables, gi, dil, 0, batch, seq)
            attn.append(_band_attn(q, k, v))
        merged = _merge(x2d, y_ssd, proj, attn, gate1, scale2, shift2, w_ba, w_o, ln1_g, ln1_b, w_router_t,
                        batch, seq, row_off, t_all, merged)
    x1, h2, logits_t = merged

    top_idx, top_w, rank, counts = _route(logits_t, router_bias[0].reshape(N_EXPERTS, 1))
    counts = counts[:, 0]
    padded = (counts + EXPERT_BLOCK - 1) // EXPERT_BLOCK * EXPERT_BLOCK
    pad_end = jnp.cumsum(padded)
    pad_start = pad_end - padded
    n_blocks = (t_all * TOP_K + N_EXPERTS * (EXPERT_BLOCK - 1)) // EXPERT_BLOCK
    n_rows = n_blocks * EXPERT_BLOCK
    tm_moe = 256
    dest_tiles = _dest(top_idx, rank, pad_start.astype(I32).reshape(N_EXPERTS, 1), tm_moe)
    n_valid = (pad_end[-1] // EXPERT_BLOCK).astype(I32).reshape(1)
    blk_start = jnp.minimum(jnp.arange(n_blocks, dtype=I32) * EXPERT_BLOCK, pad_end[-1] - 1)
    blk_exp = jnp.sum((pad_end[None, :] <= blk_start[:, None]).astype(I32), axis=1)
    blk_exp = jnp.minimum(blk_exp, N_EXPERTS - 1)

    xs = _dispatch(pad_end.astype(I32), dest_tiles, h2, n_rows, tm_moe)
    ys = _experts(blk_exp, n_valid, xs, w_exp_gate[0], w_exp_up[0], w_exp_down[0])
    outs = []
    for (x, _), n_tok, row_off in zip(groups, tokens, row_offs):
        out = _combine(dest_tiles, ys, top_w.T, x1, h2, gate2, w_sh_gate[0].astype(BF16), w_sh_up[0].astype(BF16),
                       w_sh_down[0].astype(BF16), ln2_g, ln2_b, seg, tm_moe, row_off // tm_moe, n_tok // tm_moe)
        outs.append(out.reshape(x.shape))
    return tuple(outs)
```

```python
import functools
import math

import jax
import jax.numpy as jnp
from jax import lax
from jax.experimental import pallas as pl
from jax.experimental.pallas import tpu as pltpu

F32 = jnp.float32
BF16 = jnp.bfloat16
I32 = jnp.int32

D_MODEL = 1024
D_INNER = 2048
SSD_HEADS = 32
SSD_HEAD_DIM = 64
SSD_GROUPS = 8
SSD_STATE = 128
SSD_CHUNK = 128
SSD_CHUNKS_PER_STEP = 2
HEADS_PER_SSD_GROUP = SSD_HEADS // SSD_GROUPS
CONV_WIDTH = 5
CONV_PAD = CONV_WIDTH // 2
CONV_DIM = D_INNER + 2 * SSD_GROUPS * SSD_STATE
ATTN_PATTERNS = ((128, 1), (512, 4), (2048, 16))
ATTN_GROUPS = len(ATTN_PATTERNS)
ATTN_HEADS_PER_GROUP = 8
ATTN_HEAD_DIM = 64
ATTN_GROUP_COLS = ATTN_HEADS_PER_GROUP * ATTN_HEAD_DIM
ATTN_DIM = ATTN_GROUPS * ATTN_GROUP_COLS
ROT_DIM = ATTN_HEAD_DIM // 4
ROT_HALF = ROT_DIM // 2
ROPE_THETA = 500000.0
N_EXPERTS = 256
EXPERT_HIDDEN = 256
TOP_K = 8
N_EXPERT_GROUPS = 8
EXPERTS_PER_GROUP = N_EXPERTS // N_EXPERT_GROUPS
TOPK_GROUPS = 4
ROUTED_SCALE = 2.5
SHARED_HIDDEN = 256
EXPERT_BLOCK = 512
PACKED_COLS = D_MODEL // 2
ROW_SUBLANES = PACKED_COLS // 128
DEPTH = 1
ALPHA = (2.0 * DEPTH) ** 0.25
NORM_EPS = 1e-5

COL_Z = 0
COL_XBC = COL_Z + D_INNER
COL_GSSD = COL_XBC + CONV_DIM
COL_GATTN = COL_GSSD + D_MODEL
COL_Q = COL_GATTN + D_MODEL
COL_K = COL_Q + ATTN_DIM
COL_V = COL_K + ATTN_DIM
N_MAIN = COL_V + ATTN_DIM
DT_COLS = 128

LANES = 128
SUBLANES = 8
VMEM_LIMIT = 56 * 1024 * 1024

DMA_PRIORITIES = 2

BAND_RADIUS = 64
ATTN_TQ = 128
NEG_INF = float("-inf")


def _cparams(sem):
    return pltpu.CompilerParams(dimension_semantics=sem, vmem_limit_bytes=VMEM_LIMIT)


def _sigmoid(x):
    return 1.0 / (1.0 + jnp.exp(-x))


def _silu(x):
    return x * _sigmoid(x)


def _softplus(x):
    return jnp.maximum(x, 0.0) + jnp.log(1.0 + jnp.exp(-jnp.abs(x)))


def _pack_halves(x):
    c = x.shape[1] // 2
    hi = lax.bitcast_convert_type(x[:, :c].astype(BF16).astype(F32), jnp.uint32)
    lo = lax.bitcast_convert_type(x[:, c:].astype(BF16).astype(F32), jnp.uint32)
    return hi | (lo >> 16)


def _unpack_halves(u):
    left = lax.bitcast_convert_type(u & jnp.uint32(0xFFFF0000), F32)
    right = lax.bitcast_convert_type(u << 16, F32)
    return left, right


def _store_rows(ref, lead, x):
    rows = x.shape[0]
    packed = _pack_halves(x)
    for j in range(ROW_SUBLANES):
        idx = lead + (pl.ds(j, rows, stride=ROW_SUBLANES), slice(None))
        ref[idx] = packed[:, j * LANES:(j + 1) * LANES]


def _load_row_pieces(ref, lead, rows, base=0):
    lefts, rights = [], []
    for j in range(ROW_SUBLANES):
        idx = lead + (pl.ds(base + j, rows, stride=ROW_SUBLANES), slice(None))
        left, right = _unpack_halves(ref[idx])
        lefts.append(left)
        rights.append(right)
    return lefts + rights


def _ada_kernel(c_ref, w_ref, b_ref, o_ref):
    c = _silu(c_ref[...]).astype(BF16)
    o_ref[...] = jnp.dot(c, w_ref[...].astype(BF16), preferred_element_type=F32) + b_ref[...]


def _ada(c_pad, w_ada, b_ada):
    rows = c_pad.shape[0]
    n = w_ada.shape[1]
    tn = 1536
    return pl.pallas_call(
        _ada_kernel,
        grid=(n // tn,),
        in_specs=[pl.BlockSpec((rows, D_MODEL), lambda j: (0, 0)),
                  pl.BlockSpec((D_MODEL, tn), lambda j: (0, j)),
                  pl.BlockSpec((1, tn), lambda j: (0, j))],
        out_specs=pl.BlockSpec((rows, tn), lambda j: (0, j)),
        out_shape=jax.ShapeDtypeStruct((rows, n), F32),
        compiler_params=_cparams(("arbitrary",)),
        name="ada",
    )(c_pad, w_ada, b_ada)


def _inproj_kernel(x_ref, sc_ref, sh_ref, w_ref, o_ref, h_ref):
    @pl.when(pl.program_id(1) == 0)
    def _():
        h_ref[...] = (x_ref[...] * (1.0 + sc_ref[0]) + sh_ref[0]).astype(BF16)

    o_ref[...] = jnp.dot(h_ref[...], w_ref[...], preferred_element_type=F32).astype(o_ref.dtype)


def _inproj(x, scale, shift, w, seg, tm, tn, out_dtype):
    t = x.shape[0]
    n = w.shape[1]
    per_seg = seg // tm
    return pl.pallas_call(
        _inproj_kernel,
        grid=(t // tm, n // tn),
        in_specs=[pl.BlockSpec((tm, D_MODEL), lambda i, j: (i, 0)),
                  pl.BlockSpec((1, 1, D_MODEL), lambda i, j: (i // per_seg, 0, 0)),
                  pl.BlockSpec((1, 1, D_MODEL), lambda i, j: (i // per_seg, 0, 0)),
                  pl.BlockSpec((D_MODEL, tn), lambda i, j: (0, j))],
        out_specs=pl.BlockSpec((tm, tn), lambda i, j: (i, j)),
        out_shape=jax.ShapeDtypeStruct((t, n), out_dtype),
        scratch_shapes=[pltpu.VMEM((tm, D_MODEL), BF16)],
        compiler_params=_cparams(("arbitrary", "arbitrary")),
        name="inproj",
    )(x, scale, shift, w)


def _conv_kernel(xp_ref, x_ref, xn_ref, w_ref, b_ref, o_ref, *, n_l):
    l = pl.program_id(1)
    tl = x_ref.shape[0]
    xp = jnp.where(l > 0, xp_ref[...], 0.0)
    xn = jnp.where(l < n_l - 1, xn_ref[...], 0.0)
    xx = jnp.concatenate([xp, x_ref[...], xn], axis=0)
    acc = b_ref[...] + w_ref[CONV_PAD:CONV_PAD + 1, :] * x_ref[...]
    for k in range(CONV_WIDTH):
        if k != CONV_PAD:
            shifted = pltpu.roll(xx, (CONV_PAD - k) % xx.shape[0], axis=0)
            acc = acc + w_ref[k:k + 1, :] * shifted[SUBLANES:SUBLANES + tl, :]
    o_ref[...] = _silu(acc)


def _conv(proj, conv_w, conv_b, row_off, batch, seq, tl=1024, tc=512):
    t = proj.shape[0]
    n_l = seq // tl
    n_c = CONV_DIM // tc
    c0 = COL_XBC // tc
    r8 = tl // SUBLANES
    last8 = t // SUBLANES - 1

    def row_blk(b, l):
        return (row_off + b * seq) // tl + l

    return pl.pallas_call(
        functools.partial(_conv_kernel, n_l=n_l),
        grid=(batch, n_l, n_c),
        in_specs=[pl.BlockSpec((SUBLANES, tc), lambda b, l, c: (jnp.maximum(row_blk(b, l) * r8 - 1, 0), c0 + c)),
                  pl.BlockSpec((tl, tc), lambda b, l, c: (row_blk(b, l), c0 + c)),
                  pl.BlockSpec((SUBLANES, tc), lambda b, l, c: (jnp.minimum((row_blk(b, l) + 1) * r8, last8), c0 + c)),
                  pl.BlockSpec((CONV_WIDTH, tc), lambda b, l, c: (0, c)),
                  pl.BlockSpec((1, tc), lambda b, l, c: (0, c))],
        out_specs=pl.BlockSpec((tl, tc), lambda b, l, c: (b * n_l + l, c)),
        out_shape=jax.ShapeDtypeStruct((batch * seq, CONV_DIM), F32),
        compiler_params=_cparams(("arbitrary", "arbitrary", "arbitrary")),
        name="conv",
    )(proj, proj, proj, conv_w, conv_b)


def _exact_tri_matmul(tri_bf16, x):
    hi = x.astype(BF16)
    r1 = x - hi.astype(F32)
    mid = r1.astype(BF16)
    lo = (r1 - mid.astype(F32)).astype(BF16)
    out = jnp.dot(tri_bf16, hi, preferred_element_type=F32)
    out = out + jnp.dot(tri_bf16, mid, preferred_element_type=F32)
    return out + jnp.dot(tri_bf16, lo, preferred_element_type=F32)


def _ssd_kernel(*refs, reverse, accumulate):
    if accumulate:
        x_ref, b_ref, c_ref, dt_ref, dtb_ref, alog_ref, yin_ref, y_ref, state_ref = refs
    else:
        x_ref, b_ref, c_ref, dt_ref, dtb_ref, alog_ref, y_ref, state_ref = refs
        yin_ref = None

    @pl.when(pl.program_id(1) == 0)
    def _():
        state_ref[...] = jnp.zeros(state_ref.shape, F32)

    for sub in (range(SSD_CHUNKS_PER_STEP - 1, -1, -1) if reverse else range(SSD_CHUNKS_PER_STEP)):
        _ssd_chunk(x_ref, b_ref, c_ref, dt_ref, dtb_ref, alog_ref, yin_ref, y_ref, state_ref,
                   slice(sub * SSD_CHUNK, (sub + 1) * SSD_CHUNK), reverse)


def _ssd_chunk(x_ref, b_ref, c_ref, dt_ref, dtb_ref, alog_ref, yin_ref, y_ref, state_ref, rows, reverse):
    lane0 = SSD_HEADS if reverse else 0
    accumulate = yin_ref is not None
    dt = _softplus(dt_ref[rows, :] + dtb_ref[...])
    da = dt * (-jnp.exp(alog_ref[...]))
    row = lax.broadcasted_iota(I32, (SSD_CHUNK, SSD_CHUNK), 0)
    col = lax.broadcasted_iota(I32, (SSD_CHUNK, SSD_CHUNK), 1)
    tri = (col >= row) if reverse else (col <= row)
    cum = _exact_tri_matmul(tri.astype(BF16), da)
    cum_t = cum.T
    dt_t = dt.T
    edge = 0 if reverse else SSD_CHUNK - 1
    total_b = jnp.broadcast_to(cum_t[:, edge:edge + 1], (LANES, SSD_CHUNK))
    to_end_t = dt_t * jnp.exp(total_b - cum_t)
    chunk_decay_b = jnp.exp(total_b)
    src_t = cum_t - jnp.log(dt_t)
    first_head = lax.broadcasted_iota(I32, (1, 2 * SSD_HEAD_DIM), 1) < SSD_HEAD_DIM
    pairs = HEADS_PER_SSD_GROUP // 2

    for g in range(SSD_GROUPS):
        bg = b_ref[rows, g * SSD_STATE:(g + 1) * SSD_STATE]
        cg = c_ref[rows, g * SSD_STATE:(g + 1) * SSD_STATE]
        cb = lax.dot_general(cg.astype(BF16), bg.astype(BF16), (((1,), (1,)), ((), ())),
                             preferred_element_type=F32)
        bg_t = bg.T
        for pr in range(pairs):
            h0 = g * HEADS_PER_SSD_GROUP + 2 * pr
            cols = slice(h0 * SSD_HEAD_DIM, (h0 + 2) * SSD_HEAD_DIM)
            x_pair = x_ref[rows, cols].astype(BF16)
            state = state_ref[g * pairs + pr]
            rhs = jnp.concatenate([x_pair, state.astype(BF16)], axis=0)
            res = []
            new = []
            for e in range(2):
                ln = lane0 + h0 + e
                cum_l = jnp.broadcast_to(cum[:, ln:ln + 1], (SSD_CHUNK, SSD_CHUNK))
                decay = jnp.exp(jnp.where(tri, cum_l - src_t[ln:ln + 1, :], NEG_INF))
                lhs = jnp.concatenate([(cb * decay).astype(BF16), (cg * jnp.exp(cum_l)).astype(BF16)], axis=1)
                res.append(jnp.dot(lhs, rhs, preferred_element_type=F32))
                b_scaled = (bg_t * to_end_t[ln:ln + 1, :]).astype(BF16)
                new.append(jnp.dot(b_scaled, x_pair, preferred_element_type=F32))
            y_pair = jnp.where(first_head, res[0], res[1])
            if accumulate:
                y_pair = y_pair + yin_ref[rows, cols]
            y_ref[rows, cols] = y_pair
            ln0 = lane0 + h0
            dec = jnp.where(first_head, chunk_decay_b[ln0:ln0 + 1, :], chunk_decay_b[ln0 + 1:ln0 + 2, :])
            state_ref[g * pairs + pr] = state * dec + jnp.where(first_head, new[0], new[1])


def _ssd(xbc, dt_raw, dt_bias, a_log, y_in, row_off, batch, seq, reverse):
    rows = SSD_CHUNK * SSD_CHUNKS_PER_STEP
    n_c = seq // rows
    accumulate = y_in is not None

    def loc(b, c):
        return b * n_c + (n_c - 1 - c if reverse else c)

    def glob(b, c):
        return row_off // rows + loc(b, c)

    in_specs = [pl.BlockSpec((rows, D_INNER), lambda b, c: (loc(b, c), 0)),
                pl.BlockSpec((rows, SSD_GROUPS * SSD_STATE), lambda b, c: (loc(b, c), 2)),
                pl.BlockSpec((rows, SSD_GROUPS * SSD_STATE), lambda b, c: (loc(b, c), 3)),
                pl.BlockSpec((rows, DT_COLS), lambda b, c: (glob(b, c), 0)),
                pl.BlockSpec((1, DT_COLS), lambda b, c: (0, 0)),
                pl.BlockSpec((1, DT_COLS), lambda b, c: (0, 0))]
    args = [xbc, xbc, xbc, dt_raw, dt_bias, a_log]
    aliases = {}
    if accumulate:
        in_specs.append(pl.BlockSpec((rows, D_INNER), lambda b, c: (loc(b, c), 0)))
        args.append(y_in)
        aliases = {len(args) - 1: 0}
    return pl.pallas_call(
        functools.partial(_ssd_kernel, reverse=reverse, accumulate=accumulate),
        grid=(batch, n_c),
        in_specs=in_specs,
        out_specs=pl.BlockSpec((rows, D_INNER), lambda b, c: (loc(b, c), 0)),
        out_shape=jax.ShapeDtypeStruct((batch * seq, D_INNER), F32),
        scratch_shapes=[pltpu.VMEM((SSD_HEADS // 2, SSD_STATE, 2 * SSD_HEAD_DIM), F32)],
        input_output_aliases=aliases,
        compiler_params=_cparams(("arbitrary", "arbitrary")),
        name="ssd_bwd" if reverse else "ssd_fwd",
    )(*args)


def _ssd_out_kernel(y_ref, x_ref, z_ref, dskip_ref, nw_ref, w_ref, o_ref):
    y = y_ref[...] + dskip_ref[...] * x_ref[...]
    y = y * _silu(z_ref[...])
    ms = jnp.mean(y * y, axis=-1, keepdims=True)
    yn = y * lax.rsqrt(ms + NORM_EPS) * nw_ref[...]
    o_ref[...] = jnp.dot(yn.astype(BF16), w_ref[...], preferred_element_type=F32)


def _ssd_out(y, xbc, proj, d_skip_cols, norm_w, w_branch, row_off, tm=512):
    t = y.shape[0]
    return pl.pallas_call(
        _ssd_out_kernel,
        grid=(t // tm,),
        in_specs=[pl.BlockSpec((tm, D_INNER), lambda i: (i, 0)),
                  pl.BlockSpec((tm, D_INNER), lambda i: (i, 0)),
                  pl.BlockSpec((tm, D_INNER), lambda i: (row_off // tm + i, COL_Z // D_INNER)),
                  pl.BlockSpec((1, D_INNER), lambda i: (0, 0)),
                  pl.BlockSpec((1, D_INNER), lambda i: (0, 0)),
                  pl.BlockSpec((D_INNER, D_MODEL), lambda i: (0, 0))],
        out_specs=pl.BlockSpec((tm, D_MODEL), lambda i: (i, 0)),
        out_shape=jax.ShapeDtypeStruct((t, D_MODEL), F32),
        compiler_params=_cparams(("arbitrary",)),
        name="ssd_out",
    )(y, xbc, proj, d_skip_cols, norm_w, w_branch)


def _prep_kernel(q_ref, k_ref, v_ref, cos_ref, sa_ref, sb_ref, qo_ref, ko_ref, vo_ref, *, dil):
    rows = q_ref.shape[0] // dil

    def phase(ref, p):
        if dil == 1:
            return ref[...]
        return ref[pl.ds(p, rows, stride=dil), :]

    for p in range(dil):
        cos = phase(cos_ref, p)
        sa = phase(sa_ref, p)
        sb = phase(sb_ref, p)

        def rot(t):
            return (t * cos + pltpu.roll(t, LANES - ROT_HALF, axis=1) * sa
                    + pltpu.roll(t, ROT_HALF, axis=1) * sb)

        qo_ref[0, p] = (rot(phase(q_ref, p)) * (ATTN_HEAD_DIM ** -0.5)).astype(qo_ref.dtype)
        ko_ref[0, p] = rot(phase(k_ref, p)).astype(ko_ref.dtype)
        vo_ref[0, p] = phase(v_ref, p).astype(vo_ref.dtype)


def _prep(proj, tables, gi, dil, row_off, batch, seq, rows_per_step=2048):
    r = rows_per_step
    n_i = seq // r
    n_c = ATTN_GROUP_COLS // LANES
    m = seq // dil
    cos_t, sa_t, sb_t = tables

    def in_blk(col):
        cb = (col + gi * ATTN_GROUP_COLS) // LANES
        return pl.BlockSpec((r, LANES), lambda b, i, c: ((row_off + b * seq) // r + i, cb + c))

    tab = pl.BlockSpec((r, LANES), lambda b, i, c: (i, 0))
    out = pl.BlockSpec((1, dil, r // dil, LANES), lambda b, i, c: (b, 0, i, c))
    shape = jax.ShapeDtypeStruct((batch, dil, m, ATTN_GROUP_COLS), BF16)
    return pl.pallas_call(
        functools.partial(_prep_kernel, dil=dil),
        grid=(batch, n_i, n_c),
        in_specs=[in_blk(COL_Q), in_blk(COL_K), in_blk(COL_V), tab, tab, tab],
        out_specs=[out, out, out],
        out_shape=[shape, shape, shape],
        compiler_params=_cparams(("arbitrary", "arbitrary", "arbitrary")),
        name=f"attn_prep_d{dil}",
    )(proj, proj, proj, cos_t, sa_t, sb_t)


def _band_attn_kernel(q_ref, kp_ref, kc_ref, kn_ref, vp_ref, vc_ref, vn_ref, o_ref, lse_ref, *, m_len):
    i = pl.program_id(1)
    tq = ATTN_TQ
    n_sub = q_ref.shape[1] // tq
    tk = tq + 2 * BAND_RADIUS
    q_all = q_ref[0].astype(BF16)
    k_all = jnp.concatenate([kp_ref[0], kc_ref[0], kn_ref[0]], axis=0).astype(BF16)
    v_all = jnp.concatenate([vp_ref[0], vc_ref[0], vn_ref[0]], axis=0).astype(BF16)
    first_head = lax.broadcasted_iota(I32, (1, LANES), 1) < ATTN_HEAD_DIM
    ones = jnp.ones((tk, LANES), BF16)
    zero = jnp.zeros((), BF16)
    for sub in range(n_sub):
        start = (i * n_sub + sub) * tq
        rows = slice(sub * tq, (sub + 1) * tq)
        q = q_all[rows]
        k = k_all[sub * tq:sub * tq + tk]
        v = v_all[sub * tq:sub * tq + tk]
        qpos = start + lax.broadcasted_iota(I32, (tq, tk), 0)
        kpos = start - BAND_RADIUS + lax.broadcasted_iota(I32, (tq, tk), 1)
        valid = (jnp.abs(qpos - kpos) <= BAND_RADIUS) & (kpos >= 0) & (kpos < m_len)
        for pr in range(ATTN_GROUP_COLS // LANES):
            cols = slice(pr * LANES, (pr + 1) * LANES)
            q_pair, k_pair = q[:, cols], k[:, cols]
            v_ext = jnp.concatenate([v[:, cols], ones], axis=1)
            outs, lses = [], []
            for sel in (first_head, jnp.logical_not(first_head)):
                s = lax.dot_general(jnp.where(sel, q_pair, zero), k_pair, (((1,), (1,)), ((), ())),
                                    preferred_element_type=F32)
                s = jnp.where(valid, s, NEG_INF)
                mx = jnp.max(s, axis=-1, keepdims=True)
                p = jnp.exp(s - mx).astype(BF16)
                ov = jnp.dot(p, v_ext, preferred_element_type=F32)
                den = ov[:, LANES:]
                outs.append(ov[:, :LANES] / den)
                lses.append(mx + jnp.log(den))
            o_ref[0, rows, cols] = jnp.where(first_head, outs[0], outs[1])
            lse_ref[0, rows, cols] = jnp.where(first_head, lses[0], lses[1])


def _band_attn(q, k, v):
    b, d, m, c = q.shape
    n = b * d
    q, k, v = (a.reshape(n, m, c) for a in (q, k, v))
    tq = ATTN_TQ * (2 if m % (2 * ATTN_TQ) == 0 else 1)
    per = tq // BAND_RADIUS
    last = m // BAND_RADIUS - 1
    main = pl.BlockSpec((1, tq, c), lambda s, i: (s, i, 0))
    prev = pl.BlockSpec((1, BAND_RADIUS, c), lambda s, i: (s, jnp.maximum(i * per - 1, 0), 0))
    nxt = pl.BlockSpec((1, BAND_RADIUS, c), lambda s, i: (s, jnp.minimum((i + 1) * per, last), 0))
    shape = jax.ShapeDtypeStruct((n, m, c), F32)
    o, lse = pl.pallas_call(
        functools.partial(_band_attn_kernel, m_len=m),
        grid=(n, m // tq),
        in_specs=[main, prev, main, nxt, prev, main, nxt],
        out_specs=[main, main],
        out_shape=[shape, shape],
        compiler_params=_cparams(("arbitrary", "arbitrary")),
        name=f"band_attn_m{m}",
    )(q, k, k, k, v, v, v)
    return o.reshape(b, d, m, c), lse.reshape(b, d, m, c)


def _layer_norm(x, g, b):
    mu = jnp.mean(x, axis=-1, keepdims=True)
    xc = x - mu
    var = jnp.mean(xc * xc, axis=-1, keepdims=True)
    return xc * lax.rsqrt(var + NORM_EPS) * g + b


def _merge_kernel(x_ref, yssd_ref, gs_ref, ga_ref, o1_ref, l1_ref, o2_ref, l2_ref, o3_ref, l3_ref,
                  g1_ref, sc2_ref, sh2_ref, wba_ref, wout_ref, lng_ref, lnb_ref, wr_ref, *rest, n_prev):
    x1_ref, h2_ref, logit_ref, scr_ref = rest[n_prev:]
    tm = x_ref.shape[0]

    def interleaved(ref, dil, slot):
        if dil == 1:
            return ref[0, 0]
        rows = tm // dil
        n_c = ATTN_GROUP_COLS // LANES
        for p in range(dil):
            for c in range(n_c):
                scr_ref[slot * n_c + c, pl.ds(p, rows, stride=dil), :] = ref[0, p, :, c * LANES:(c + 1) * LANES]
        return jnp.concatenate([scr_ref[slot * n_c + c] for c in range(n_c)], axis=1)

    outs = []
    lses = []
    slot = 0
    for (o_ref, l_ref), (_, dil) in zip(((o1_ref, l1_ref), (o2_ref, l2_ref), (o3_ref, l3_ref)), ATTN_PATTERNS):
        outs.append(interleaved(o_ref, dil, slot))
        lses.append(interleaved(l_ref, dil, slot + 1))
        slot += 2
    mx = jnp.maximum(jnp.maximum(lses[0], lses[1]), lses[2])
    es = [jnp.exp(l - mx) for l in lses]
    den = es[0] + es[1] + es[2]
    y_att = (es[0] * outs[0] + es[1] * outs[1] + es[2] * outs[2]) / den
    y_attn = jnp.dot(y_att.astype(BF16), wba_ref[...], preferred_element_type=F32)
    merged = _sigmoid(gs_ref[...]) * yssd_ref[...] + _sigmoid(ga_ref[...]) * y_attn
    mix = jnp.dot(merged.astype(BF16), wout_ref[...], preferred_element_type=F32)
    x1 = _layer_norm(ALPHA * x_ref[...] + g1_ref[0] * mix, lng_ref[...], lnb_ref[...])
    x1_ref[...] = x1
    h2 = x1 * (1.0 + sc2_ref[0]) + sh2_ref[0]
    _store_rows(h2_ref, (), h2)
    logit_ref[...] = lax.dot_general(wr_ref[...], h2.astype(BF16), (((1,), (1,)), ((), ())),
                                     preferred_element_type=F32)


def _merge(x, y_ssd, proj, attn, gate1, scale2, shift2, w_ba, w_out, ln_g, ln_b, w_router_t,
           batch, seq, out_row_off, t_all, prev, tm=256):
    n_i = seq // tm
    out0 = out_row_off // tm

    def rows(col_blk):
        return pl.BlockSpec((tm, D_MODEL), lambda b, i: (b * n_i + i, col_blk))

    def local(width):
        return pl.BlockSpec((tm, width), lambda b, i: (b * n_i + i, 0))

    def modv():
        return pl.BlockSpec((1, 1, D_MODEL), lambda b, i: (b, 0, 0))

    def full(shape):
        return pl.BlockSpec(shape, lambda b, i: tuple(0 for _ in shape))

    attn_specs = []
    attn_args = []
    for (o, lse), (_, dil) in zip(attn, ATTN_PATTERNS):
        spec = pl.BlockSpec((1, dil, tm // dil, ATTN_GROUP_COLS), lambda b, i: (b, 0, i, 0))
        attn_specs += [spec, spec]
        attn_args += [o, lse]
    n_in = 4 + len(attn_args) + 8
    return pl.pallas_call(
        functools.partial(_merge_kernel, n_prev=0 if prev is None else 3),
        grid=(batch, n_i),
        in_specs=[rows(0), local(D_MODEL), rows(COL_GSSD // D_MODEL), rows(COL_GATTN // D_MODEL)] + attn_specs
                 + [modv(), modv(), modv(), full((ATTN_GROUP_COLS, D_MODEL)), full((D_MODEL, D_MODEL)),
                    full((1, D_MODEL)), full((1, D_MODEL)), full((N_EXPERTS, D_MODEL))]
                 + ([pl.BlockSpec(memory_space=pl.ANY)] * 3 if prev is not None else []),
        out_specs=[pl.BlockSpec((tm, D_MODEL), lambda b, i: (out0 + b * n_i + i, 0)),
                   pl.BlockSpec((tm * ROW_SUBLANES, LANES), lambda b, i: (out0 + b * n_i + i, 0)),
                   pl.BlockSpec((N_EXPERTS, tm), lambda b, i: (0, out0 + b * n_i + i))],
        out_shape=[jax.ShapeDtypeStruct((t_all, D_MODEL), F32),
                   jax.ShapeDtypeStruct((t_all * ROW_SUBLANES, LANES), jnp.uint32),
                   jax.ShapeDtypeStruct((N_EXPERTS, t_all), F32)],
        scratch_shapes=[pltpu.VMEM((2 * ATTN_GROUPS * (ATTN_GROUP_COLS // LANES), tm, LANES), F32)],
        input_output_aliases={n_in + j: j for j in range(3)} if prev is not None else {},
        compiler_params=_cparams(("arbitrary", "arbitrary")),
        name="mixer_merge",
    )(x, y_ssd, proj, proj, *attn_args, gate1, scale2, shift2, w_ba, w_out, ln_g, ln_b, w_router_t,
      *(prev if prev is not None else ()))


def _route_kernel(logit_ref, bias_ref, idx_ref, w_ref, rank_ref, cnt_ref, run_ref):
    tm = logit_ref.shape[1]

    @pl.when(pl.program_id(0) == 0)
    def _():
        run_ref[...] = jnp.zeros(run_ref.shape, F32)

    scores = _sigmoid(logit_ref[...])
    sel = scores + bias_ref[...]
    iota_g = lax.broadcasted_iota(I32, (EXPERTS_PER_GROUP, tm), 0)
    grp = []
    for g in range(N_EXPERT_GROUPS):
        sg = sel[g * EXPERTS_PER_GROUP:(g + 1) * EXPERTS_PER_GROUP, :]
        top1 = jnp.max(sg, axis=0, keepdims=True)
        first = jnp.min(jnp.where(sg == top1, iota_g, EXPERTS_PER_GROUP), axis=0, keepdims=True)
        top2 = jnp.max(jnp.where(iota_g == first, NEG_INF, sg), axis=0, keepdims=True)
        grp.append(top1 + top2)
    gs = jnp.concatenate(grp, axis=0)
    iota_n = lax.broadcasted_iota(I32, (N_EXPERT_GROUPS, tm), 0)
    keep_g = jnp.zeros((N_EXPERT_GROUPS, tm), F32)
    for _ in range(TOPK_GROUPS):
        best = jnp.max(gs, axis=0, keepdims=True)
        first = jnp.min(jnp.where(gs == best, iota_n, N_EXPERT_GROUPS), axis=0, keepdims=True)
        hit = iota_n == first
        keep_g = jnp.where(hit, 1.0, keep_g)
        gs = jnp.where(hit, NEG_INF, gs)
    cand = jnp.concatenate(
        [jnp.where(keep_g[g:g + 1, :] > 0.0, sel[g * EXPERTS_PER_GROUP:(g + 1) * EXPERTS_PER_GROUP, :], NEG_INF)
         for g in range(N_EXPERT_GROUPS)], axis=0)
    iota_e = lax.broadcasted_iota(I32, (N_EXPERTS, tm), 0)
    chosen = jnp.zeros((N_EXPERTS, tm), F32)
    idxs = []
    ws = []
    for _ in range(TOP_K):
        best = jnp.max(cand, axis=0, keepdims=True)
        first = jnp.min(jnp.where(cand == best, iota_e, N_EXPERTS), axis=0, keepdims=True)
        hit = iota_e == first
        idxs.append(first)
        ws.append(jnp.sum(jnp.where(hit, scores, 0.0), axis=0, keepdims=True))
        chosen = jnp.where(hit, 1.0, chosen)
        cand = jnp.where(hit, NEG_INF, cand)
    top_w = jnp.concatenate(ws, axis=0)
    top_w = top_w / jnp.sum(top_w, axis=0, keepdims=True) * ROUTED_SCALE
    idx_ref[...] = jnp.concatenate(idxs, axis=0)
    w_ref[...] = top_w
    s_i = lax.broadcasted_iota(I32, (tm, tm), 0)
    t_i = lax.broadcasted_iota(I32, (tm, tm), 1)
    before = (s_i < t_i).astype(BF16)
    prior = jnp.dot(chosen.astype(BF16), before, preferred_element_type=F32) + run_ref[...]
    ranks = [jnp.sum(jnp.where(iota_e == idxs[k], prior, 0.0), axis=0, keepdims=True) for k in range(TOP_K)]
    rank_ref[...] = jnp.concatenate(ranks, axis=0).astype(I32)
    run_ref[...] = run_ref[...] + jnp.sum(chosen, axis=1, keepdims=True)
    cnt_ref[...] = run_ref[...].astype(I32)


def _route(logits_t, bias_col, tm=256):
    t = logits_t.shape[1]
    tok = pl.BlockSpec((TOP_K, tm), lambda i: (0, i))
    return pl.pallas_call(
        _route_kernel,
        grid=(t // tm,),
        in_specs=[pl.BlockSpec((N_EXPERTS, tm), lambda i: (0, i)),
                  pl.BlockSpec((N_EXPERTS, 1), lambda i: (0, 0))],
        out_specs=[tok, tok, tok, pl.BlockSpec((N_EXPERTS, 1), lambda i: (0, 0))],
        out_shape=[jax.ShapeDtypeStruct((TOP_K, t), I32), jax.ShapeDtypeStruct((TOP_K, t), F32),
                   jax.ShapeDtypeStruct((TOP_K, t), I32), jax.ShapeDtypeStruct((N_EXPERTS, 1), I32)],
        scratch_shapes=[pltpu.VMEM((N_EXPERTS, 1), F32)],
        compiler_params=_cparams(("arbitrary",)),
        name="route",
    )(logits_t, bias_col)


def _dest_kernel(idx_ref, rank_ref, start_ref, o_ref):
    tm = idx_ref.shape[1]
    iota_e = lax.broadcasted_iota(I32, (N_EXPERTS, tm), 0)
    start = start_ref[...].astype(F32)
    for k in range(TOP_K):
        base = jnp.sum(jnp.where(iota_e == idx_ref[k:k + 1, :], start, 0.0), axis=0, keepdims=True)
        o_ref[0, :, k * tm:(k + 1) * tm] = (base.astype(I32) + rank_ref[k:k + 1, :]) * ROW_SUBLANES


def _dest(top_idx, rank, pad_start_col, tm):
    t = top_idx.shape[1]
    tok = pl.BlockSpec((TOP_K, tm), lambda i: (0, i))
    out = pl.pallas_call(
        _dest_kernel,
        grid=(t // tm,),
        in_specs=[tok, tok, pl.BlockSpec((N_EXPERTS, 1), lambda i: (0, 0))],
        out_specs=pl.BlockSpec((1, 1, TOP_K * tm), lambda i: (i, 0, 0)),
        out_shape=jax.ShapeDtypeStruct((t // tm, 1, TOP_K * tm), I32),
        compiler_params=_cparams(("arbitrary",)),
        name="moe_dest",
    )(top_idx, rank, pad_start_col)
    return out.reshape(t // tm, TOP_K * tm)


def _dispatch_kernel(pad_end_ref, dest_hbm, h_ref, xs_hbm, dest_smem, zeros_ref, sem_idx, sem_zero, sem_row,
                     *, tm):
    i = pl.program_id(0)
    idx_copy = pltpu.make_async_copy(dest_hbm.at[i], dest_smem, sem_idx)
    idx_copy.start()

    blk_rows = EXPERT_BLOCK * ROW_SUBLANES

    def zero_copy(e):
        start = pl.multiple_of(jnp.maximum(pad_end_ref[e] - EXPERT_BLOCK, 0) * ROW_SUBLANES, blk_rows)
        return pltpu.make_async_copy(zeros_ref, xs_hbm.at[pl.ds(start, blk_rows)], sem_zero)

    @pl.when(i == 0)
    def _():
        zeros_ref[...] = jnp.zeros(zeros_ref.shape, zeros_ref.dtype)

        def start_zero(e, carry):
            zero_copy(e).start()
            return carry

        def wait_zero(e, carry):
            zero_copy(e).wait()
            return carry

        lax.fori_loop(0, N_EXPERTS, start_zero, 0)
        lax.fori_loop(0, N_EXPERTS, wait_zero, 0)

    idx_copy.wait()

    def row_copy(t, k):
        dst = pl.multiple_of(dest_smem[k * tm + t], ROW_SUBLANES)
        return pltpu.make_async_copy(h_ref.at[pl.ds(pl.multiple_of(t * ROW_SUBLANES, ROW_SUBLANES), ROW_SUBLANES)],
                                     xs_hbm.at[pl.ds(dst, ROW_SUBLANES)], sem_row)

    def start_rows(t, carry):
        for k in range(TOP_K):
            row_copy(t, k).start(priority=k % DMA_PRIORITIES)
        return carry

    def wait_rows(t, carry):
        for k in range(TOP_K):
            row_copy(t, k).wait()
        return carry

    lax.fori_loop(0, tm, start_rows, 0)
    lax.fori_loop(0, tm, wait_rows, 0)


def _dispatch(pad_end, dest_tiles, h2, n_rows, tm):
    n_tiles = dest_tiles.shape[0]
    return pl.pallas_call(
        functools.partial(_dispatch_kernel, tm=tm),
        grid_spec=pltpu.PrefetchScalarGridSpec(
            num_scalar_prefetch=1,
            grid=(n_tiles,),
            in_specs=[pl.BlockSpec(memory_space=pl.ANY),
                      pl.BlockSpec((tm * ROW_SUBLANES, LANES), lambda i, pe: (i, 0))],
            out_specs=pl.BlockSpec(memory_space=pl.ANY),
            scratch_shapes=[pltpu.SMEM((TOP_K * tm,), I32),
                            pltpu.VMEM((EXPERT_BLOCK * ROW_SUBLANES, LANES), jnp.uint32),
                            pltpu.SemaphoreType.DMA, pltpu.SemaphoreType.DMA, pltpu.SemaphoreType.DMA]),
        out_shape=jax.ShapeDtypeStruct((n_rows * ROW_SUBLANES, LANES), jnp.uint32),
        compiler_params=_cparams(("arbitrary",)),
        name="moe_dispatch",
    )(pad_end, dest_tiles, h2)


def _experts_kernel(blk_exp_ref, n_valid_ref, xs_ref, wg_ref, wu_ref, wd_ref, ys_ref, wgu_ref, wdn_ref):
    i = pl.program_id(0)

    @pl.when(i < n_valid_ref[0])
    def _():
        @pl.when((i == 0) | (blk_exp_ref[i] != blk_exp_ref[jnp.maximum(i - 1, 0)]))
        def _():
            wgu_ref[:, :EXPERT_HIDDEN] = wg_ref[0].astype(BF16)
            wgu_ref[:, EXPERT_HIDDEN:] = wu_ref[0].astype(BF16)
            wdn_ref[...] = wd_ref[0].astype(BF16)

        x = jnp.concatenate([c.astype(BF16) for c in _load_row_pieces(xs_ref, (), EXPERT_BLOCK)], axis=1)
        gu = jnp.dot(x, wgu_ref[...], preferred_element_type=F32)
        hmid = (_silu(gu[:, :EXPERT_HIDDEN]) * gu[:, EXPERT_HIDDEN:]).astype(BF16)
        _store_rows(ys_ref, (), jnp.dot(hmid, wdn_ref[...], preferred_element_type=F32))


def _experts(blk_exp, n_valid, xs, w_gate, w_up, w_down):
    n_blocks = xs.shape[0] // (EXPERT_BLOCK * ROW_SUBLANES)

    def row_map(i, be, nv):
        return (jnp.minimum(i, nv[0] - 1), 0)

    def w_map(i, be, nv):
        return (be[i], 0, 0)

    return pl.pallas_call(
        _experts_kernel,
        grid_spec=pltpu.PrefetchScalarGridSpec(
            num_scalar_prefetch=2,
            grid=(n_blocks,),
            in_specs=[pl.BlockSpec((EXPERT_BLOCK * ROW_SUBLANES, LANES), row_map),
                      pl.BlockSpec((1, D_MODEL, EXPERT_HIDDEN), w_map),
                      pl.BlockSpec((1, D_MODEL, EXPERT_HIDDEN), w_map),
                      pl.BlockSpec((1, EXPERT_HIDDEN, D_MODEL), w_map)],
            out_specs=pl.BlockSpec((EXPERT_BLOCK * ROW_SUBLANES, LANES), row_map),
            scratch_shapes=[pltpu.VMEM((D_MODEL, 2 * EXPERT_HIDDEN), BF16),
                            pltpu.VMEM((EXPERT_HIDDEN, D_MODEL), BF16)]),
        out_shape=jax.ShapeDtypeStruct(xs.shape, jnp.uint32),
        compiler_params=_cparams(("arbitrary",)),
        name="moe_experts",
    )(blk_exp, n_valid, xs, w_gate, w_up, w_down)


def _combine_kernel(dest_hbm, ys_hbm, w_ref, x1_ref, h2_ref, g2_ref, wsg_ref, wsu_ref, wsd_ref,
                    lng_ref, lnb_ref, o_ref, dest0, dest1, rows0, rows1, sem_idx, sem0, sem1, *, tm, tile_off):
    i = pl.program_id(0)
    n_i = pl.num_programs(0)
    dest_smem, rows_ref, sem_row = (dest0, dest1), (rows0, rows1), (sem0, sem1)

    def row_copy(s, t, k):
        src = pl.multiple_of(dest_smem[s][k * tm + t], ROW_SUBLANES)
        dst = pl.multiple_of(t * ROW_SUBLANES, ROW_SUBLANES)
        return pltpu.make_async_copy(ys_hbm.at[pl.ds(src, ROW_SUBLANES)],
                                     rows_ref[s].at[k, pl.ds(dst, ROW_SUBLANES)], sem_row[s])

    def start_gather(tile, s):
        idx_copy = pltpu.make_async_copy(dest_hbm.at[tile + tile_off], dest_smem[s], sem_idx)
        idx_copy.start()
        idx_copy.wait()

        def start_rows(t, carry):
            for k in range(TOP_K):
                row_copy(s, t, k).start(priority=k % DMA_PRIORITIES)
            return carry

        lax.fori_loop(0, tm, start_rows, 0)

    @pl.when(i == 0)
    def _():
        start_gather(0, 0)

    for s in range(2):
        if s == 0:
            start_gather(2 * i + 1, 1)
        else:
            @pl.when(i + 1 < n_i)
            def _():
                start_gather(2 * i + 2, 0)

        rows = slice(s * tm, (s + 1) * tm)
        h2 = jnp.concatenate(
            [c.astype(BF16) for c in _load_row_pieces(h2_ref, (), tm, s * tm * ROW_SUBLANES)], axis=1)
        g = jnp.dot(h2, wsg_ref[...], preferred_element_type=F32)
        u = jnp.dot(h2, wsu_ref[...], preferred_element_type=F32)
        ffn = jnp.dot((_silu(g) * u).astype(BF16), wsd_ref[...], preferred_element_type=F32)

        def wait_rows(t, carry):
            for k in range(TOP_K):
                row_copy(s, t, k).wait()
            return carry

        lax.fori_loop(0, tm, wait_rows, 0)
        w = w_ref[rows, :]
        n_pieces = D_MODEL // LANES
        acc = [ffn[:, j * LANES:(j + 1) * LANES] for j in range(n_pieces)]
        for k in range(TOP_K):
            wk = jnp.broadcast_to(w[:, k:k + 1], (tm, LANES))
            pieces = _load_row_pieces(rows_ref[s], (k,), tm)
            acc = [a + wk * p for a, p in zip(acc, pieces)]
        ffn = jnp.concatenate(acc, axis=1)
        o_ref[rows, :] = _layer_norm(ALPHA * x1_ref[rows, :] + g2_ref[0] * ffn, lng_ref[...], lnb_ref[...])


def _combine(dest_tiles, ys, top_w_rows, x1, h2, gate2, w_sg, w_su, w_sd, ln_g, ln_b, seg, tm, tile_off, n_tiles):
    assert tile_off % 2 == 0 and n_tiles % 2 == 0 and seg % (2 * tm) == 0
    per_seg = seg // (2 * tm)
    off = tile_off // 2

    def full(shape):
        return pl.BlockSpec(shape, lambda i: tuple(0 for _ in shape))

    return pl.pallas_call(
        functools.partial(_combine_kernel, tm=tm, tile_off=tile_off),
        grid=(n_tiles // 2,),
        in_specs=[pl.BlockSpec(memory_space=pl.ANY), pl.BlockSpec(memory_space=pl.ANY),
                  pl.BlockSpec((2 * tm, TOP_K), lambda i: (i + off, 0)),
                  pl.BlockSpec((2 * tm, D_MODEL), lambda i: (i + off, 0)),
                  pl.BlockSpec((2 * tm * ROW_SUBLANES, LANES), lambda i: (i + off, 0)),
                  pl.BlockSpec((1, 1, D_MODEL), lambda i: ((i + off) // per_seg, 0, 0)),
                  full((D_MODEL, SHARED_HIDDEN)), full((D_MODEL, SHARED_HIDDEN)), full((SHARED_HIDDEN, D_MODEL)),
                  full((1, D_MODEL)), full((1, D_MODEL))],
        out_specs=pl.BlockSpec((2 * tm, D_MODEL), lambda i: (i, 0)),
        out_shape=jax.ShapeDtypeStruct((n_tiles * tm, D_MODEL), F32),
        scratch_shapes=[pltpu.SMEM((TOP_K * tm,), I32), pltpu.SMEM((TOP_K * tm,), I32),
                        pltpu.VMEM((TOP_K, tm * ROW_SUBLANES, LANES), jnp.uint32),
                        pltpu.VMEM((TOP_K, tm * ROW_SUBLANES, LANES), jnp.uint32),
                        pltpu.SemaphoreType.DMA, pltpu.SemaphoreType.DMA, pltpu.SemaphoreType.DMA],
        compiler_params=_cparams(("arbitrary",)),
        name="moe_combine",
    )(dest_tiles, ys, top_w_rows, x1, h2, gate2, w_sg, w_su, w_sd, ln_g, ln_b)


def _rope_tables(seq):
    inv_freq = ROPE_THETA ** (-jnp.arange(ROT_HALF, dtype=F32) * 2.0 / ROT_DIM)
    ang = jnp.arange(seq, dtype=F32)[:, None] * inv_freq[None, :]
    cos, sin = jnp.cos(ang), jnp.sin(ang)
    rest = ATTN_HEAD_DIM - ROT_DIM
    ones = jnp.ones((seq, rest), F32)
    zeros = jnp.zeros((seq, rest), F32)
    zh = jnp.zeros((seq, ROT_HALF), F32)
    reps = LANES // ATTN_HEAD_DIM
    cos_t = jnp.tile(jnp.concatenate([cos, cos, ones], axis=1), (1, reps))
    sa_t = jnp.tile(jnp.concatenate([-sin, zh, zeros], axis=1), (1, reps))
    sb_t = jnp.tile(jnp.concatenate([zh, sin, zeros], axis=1), (1, reps))
    return cos_t, sa_t, sb_t


def _split_in_weights(w_in):
    cuts = (D_INNER, CONV_DIM, SSD_HEADS, SSD_HEADS, ATTN_DIM, ATTN_DIM, ATTN_DIM, D_MODEL, D_MODEL)
    offs = [0]
    for c in cuts:
        offs.append(offs[-1] + c)
    z, xbc, dtf, dtb, q, k, v, gs, ga = (w_in[:, offs[i]:offs[i + 1]] for i in range(len(cuts)))
    w_main = jnp.concatenate([z, xbc, gs, ga, q, k, v], axis=1).astype(BF16)
    assert COL_GSSD % D_MODEL == 0 and COL_Q % ATTN_GROUP_COLS == 0 and w_main.shape[1] == N_MAIN
    pad = jnp.zeros((D_MODEL, DT_COLS - 2 * SSD_HEADS), F32)
    w_dt = jnp.concatenate([dtf, dtb, pad], axis=1).astype(BF16)
    return w_main, w_dt


def kernel(x_prompt, x_sample, c_prompt, c_sample, w_ada, b_ada, w_in, conv_w, conv_b, dt_bias_fwd, dt_bias_bwd, a_log_fwd, a_log_bwd, d_skip, ssd_norm_w, w_branch_ssd, w_branch_attn, w_out, ln1_g, ln1_b, w_router, router_bias, w_exp_gate, w_exp_up, w_exp_down, w_sh_gate, w_sh_up, w_sh_down, ln2_g, ln2_b):
    assert w_ada.shape[0] == DEPTH
    groups = ((x_prompt, c_prompt), (x_sample, c_sample))
    shapes = [(x.shape[0], x.shape[1]) for x, _ in groups]
    seg = math.gcd(*(s for _, s in shapes))
    tokens = [b * s for b, s in shapes]
    t_all = sum(tokens)
    row_offs = [0, tokens[0]]

    c_all = jnp.concatenate([c_prompt, c_sample], axis=0)
    n_req = c_all.shape[0]
    c_pad = jnp.zeros((-(-n_req // SUBLANES) * SUBLANES, D_MODEL), F32).at[:n_req].set(c_all)
    ada = _ada(c_pad, w_ada[0], b_ada)[:n_req]
    req_offs = [0, shapes[0][0]]
    reps = jnp.array([s // seg for b, s in shapes for _ in range(b)], dtype=I32)
    n_seg = t_all // seg
    gate2 = jnp.repeat(ada[:, 5 * D_MODEL:], reps, axis=0, total_repeat_length=n_seg).reshape(n_seg, 1, D_MODEL)

    w_main, w_dt = _split_in_weights(w_in[0])
    dt_pad = jnp.zeros((DT_COLS - 2 * SSD_HEADS,), F32)
    dt_bias = jnp.concatenate([dt_bias_fwd[0], dt_bias_bwd[0], dt_pad]).reshape(1, DT_COLS)
    a_log = jnp.concatenate([a_log_fwd[0], a_log_bwd[0], dt_pad]).reshape(1, DT_COLS)
    d_skip_cols = jnp.repeat(d_skip[0], SSD_HEAD_DIM).reshape(1, D_INNER)
    w_bs = w_branch_ssd[0].astype(BF16)
    w_ba = w_branch_attn[0].astype(BF16)
    w_o = w_out[0].astype(BF16)
    w_router_t = w_router[0].T.astype(BF16)

    merged = None
    for (x, _), (batch, seq), row_off, req_off in zip(groups, shapes, row_offs, req_offs):
        shift1, scale1, gate1, shift2, scale2, _ = (
            a.reshape(batch, 1, D_MODEL) for a in jnp.split(ada[req_off:req_off + batch], 6, axis=-1))
        x2d = x.reshape(batch * seq, D_MODEL)
        proj = _inproj(x2d, scale1, shift1, w_main, seq, tm=1024, tn=3200, out_dtype=F32)
        dt_raw = _inproj(x2d, scale1, shift1, w_dt, seq, tm=1024, tn=DT_COLS, out_dtype=F32)
        xbc = _conv(proj, conv_w[0], conv_b, 0, batch, seq)
        y = _ssd(xbc, dt_raw, dt_bias, a_log, None, 0, batch, seq, reverse=False)
        y = _ssd(xbc, dt_raw, dt_bias, a_log, y, 0, batch, seq, reverse=True)
        y_ssd = _ssd_out(y, xbc, proj, d_skip_cols, ssd_norm_w, w_bs, 0)
        tables = _rope_tables(seq)
        attn = []
        for gi, (_, dil) in enumerate(ATTN_PATTERNS):
            q, k, v = _prep(proj, tables, gi, dil, 0, batch, seq)
            attn.append(_band_attn(q, k, v))
        merged = _merge(x2d, y_ssd, proj, attn, gate1, scale2, shift2, w_ba, w_o, ln1_g, ln1_b, w_router_t,
                        batch, seq, row_off, t_all, merged)
    x1, h2, logits_t = merged

    top_idx, top_w, rank, counts = _route(logits_t, router_bias[0].reshape(N_EXPERTS, 1))
    counts = counts[:, 0]
    padded = (counts + EXPERT_BLOCK - 1) // EXPERT_BLOCK * EXPERT_BLOCK
    pad_end = jnp.cumsum(padded)
    pad_start = pad_end - padded
    n_blocks = (t_all * TOP_K + N_EXPERTS * (EXPERT_BLOCK - 1)) // EXPERT_BLOCK
    n_rows = n_blocks * EXPERT_BLOCK
    tm_moe = 256
    dest_tiles = _dest(top_idx, rank, pad_start.astype(I32).reshape(N_EXPERTS, 1), tm_moe)
    n_valid = (pad_end[-1] // EXPERT_BLOCK).astype(I32).reshape(1)
    blk_start = jnp.minimum(jnp.arange(n_blocks, dtype=I32) * EXPERT_BLOCK, pad_end[-1] - 1)
    blk_exp = jnp.sum((pad_end[None, :] <= blk_start[:, None]).astype(I32), axis=1)
    blk_exp = jnp.minimum(blk_exp, N_EXPERTS - 1)

    xs = _dispatch(pad_end.astype(I32), dest_tiles, h2, n_rows, tm_moe)
    ys = _experts(blk_exp, n_valid, xs, w_exp_gate[0], w_exp_up[0], w_exp_down[0])
    outs = []
    for (x, _), n_tok, row_off in zip(groups, tokens, row_offs):
        out = _combine(dest_tiles, ys, top_w.T, x1, h2, gate2, w_sh_gate[0].astype(BF16), w_sh_up[0].astype(BF16),
                       w_sh_down[0].astype(BF16), ln2_g, ln2_b, seg, tm_moe, row_off // tm_moe, n_tok // tm_moe)
        outs.append(out.reshape(x.shape))
    return tuple(outs)
```

```python
import functools
import math

import jax
import jax.numpy as jnp
from jax import lax
from jax.experimental import pallas as pl
from jax.experimental.pallas import tpu as pltpu

F32 = jnp.float32
BF16 = jnp.bfloat16
I32 = jnp.int32

D_MODEL = 1024
D_INNER = 2048
SSD_HEADS = 32
SSD_HEAD_DIM = 64
SSD_GROUPS = 8
SSD_STATE = 128
SSD_CHUNK = 128
SSD_CHUNKS_PER_STEP = 2
HEADS_PER_SSD_GROUP = SSD_HEADS // SSD_GROUPS
CONV_WIDTH = 5
CONV_PAD = CONV_WIDTH // 2
CONV_DIM = D_INNER + 2 * SSD_GROUPS * SSD_STATE
ATTN_PATTERNS = ((128, 1), (512, 4), (2048, 16))
ATTN_GROUPS = len(ATTN_PATTERNS)
ATTN_HEADS_PER_GROUP = 8
ATTN_HEAD_DIM = 64
ATTN_GROUP_COLS = ATTN_HEADS_PER_GROUP * ATTN_HEAD_DIM
ATTN_DIM = ATTN_GROUPS * ATTN_GROUP_COLS
ROT_DIM = ATTN_HEAD_DIM // 4
ROT_HALF = ROT_DIM // 2
ROPE_THETA = 500000.0
N_EXPERTS = 256
EXPERT_HIDDEN = 256
TOP_K = 8
N_EXPERT_GROUPS = 8
EXPERTS_PER_GROUP = N_EXPERTS // N_EXPERT_GROUPS
TOPK_GROUPS = 4
ROUTED_SCALE = 2.5
SHARED_HIDDEN = 256
EXPERT_BLOCK = 512
PACKED_COLS = D_MODEL // 2
ROW_SUBLANES = PACKED_COLS // 128
DEPTH = 1
ALPHA = (2.0 * DEPTH) ** 0.25
NORM_EPS = 1e-5

COL_Z = 0
COL_XBC = COL_Z + D_INNER
COL_GSSD = COL_XBC + CONV_DIM
COL_GATTN = COL_GSSD + D_MODEL
COL_Q = COL_GATTN + D_MODEL
COL_K = COL_Q + ATTN_DIM
COL_V = COL_K + ATTN_DIM
N_MAIN = COL_V + ATTN_DIM
DT_COLS = 128

LANES = 128
SUBLANES = 8
VMEM_LIMIT = 56 * 1024 * 1024

DMA_PRIORITIES = 2

BAND_RADIUS = 64
ATTN_TQ = 128
NEG_INF = float("-inf")


def _cparams(sem):
    return pltpu.CompilerParams(dimension_semantics=sem, vmem_limit_bytes=VMEM_LIMIT)


def _sigmoid(x):
    return 1.0 / (1.0 + jnp.exp(-x))


def _silu(x):
    return x * _sigmoid(x)


def _softplus(x):
    return jnp.maximum(x, 0.0) + jnp.log(1.0 + jnp.exp(-jnp.abs(x)))


def _pack_halves(x):
    c = x.shape[1] // 2
    hi = lax.bitcast_convert_type(x[:, :c].astype(BF16).astype(F32), jnp.uint32)
    lo = lax.bitcast_convert_type(x[:, c:].astype(BF16).astype(F32), jnp.uint32)
    return hi | (lo >> 16)


def _unpack_halves(u):
    left = lax.bitcast_convert_type(u & jnp.uint32(0xFFFF0000), F32)
    right = lax.bitcast_convert_type(u << 16, F32)
    return left, right


def _store_rows(ref, lead, x):
    rows = x.shape[0]
    packed = _pack_halves(x)
    for j in range(ROW_SUBLANES):
        idx = lead + (pl.ds(j, rows, stride=ROW_SUBLANES), slice(None))
        ref[idx] = packed[:, j * LANES:(j + 1) * LANES]


def _load_row_pieces(ref, lead, rows, base=0):
    lefts, rights = [], []
    for j in range(ROW_SUBLANES):
        idx = lead + (pl.ds(base + j, rows, stride=ROW_SUBLANES), slice(None))
        left, right = _unpack_halves(ref[idx])
        lefts.append(left)
        rights.append(right)
    return lefts + rights


def _ada_kernel(c_ref, w_ref, b_ref, o_ref):
    c = _silu(c_ref[...]).astype(BF16)
    o_ref[...] = jnp.dot(c, w_ref[...].astype(BF16), preferred_element_type=F32) + b_ref[...]


def _ada(c_pad, w_ada, b_ada):
    rows = c_pad.shape[0]
    n = w_ada.shape[1]
    tn = 1536
    return pl.pallas_call(
        _ada_kernel,
        grid=(n // tn,),
        in_specs=[pl.BlockSpec((rows, D_MODEL), lambda j: (0, 0)),
                  pl.BlockSpec((D_MODEL, tn), lambda j: (0, j)),
                  pl.BlockSpec((1, tn), lambda j: (0, j))],
        out_specs=pl.BlockSpec((rows, tn), lambda j: (0, j)),
        out_shape=jax.ShapeDtypeStruct((rows, n), F32),
        compiler_params=_cparams(("arbitrary",)),
        name="ada",
    )(c_pad, w_ada, b_ada)


def _inproj_kernel(x_ref, sc_ref, sh_ref, w_ref, o_ref, h_ref):
    @pl.when(pl.program_id(1) == 0)
    def _():
        h_ref[...] = (x_ref[...] * (1.0 + sc_ref[0]) + sh_ref[0]).astype(BF16)

    o_ref[...] = jnp.dot(h_ref[...], w_ref[...], preferred_element_type=F32).astype(o_ref.dtype)


def _inproj(x, scale, shift, w, seg, tm, tn, out_dtype):
    t = x.shape[0]
    n = w.shape[1]
    per_seg = seg // tm
    return pl.pallas_call(
        _inproj_kernel,
        grid=(t // tm, n // tn),
        in_specs=[pl.BlockSpec((tm, D_MODEL), lambda i, j: (i, 0)),
                  pl.BlockSpec((1, 1, D_MODEL), lambda i, j: (i // per_seg, 0, 0)),
                  pl.BlockSpec((1, 1, D_MODEL), lambda i, j: (i // per_seg, 0, 0)),
                  pl.BlockSpec((D_MODEL, tn), lambda i, j: (0, j))],
        out_specs=pl.BlockSpec((tm, tn), lambda i, j: (i, j)),
        out_shape=jax.ShapeDtypeStruct((t, n), out_dtype),
        scratch_shapes=[pltpu.VMEM((tm, D_MODEL), BF16)],
        compiler_params=_cparams(("arbitrary", "arbitrary")),
        name="inproj",
    )(x, scale, shift, w)


def _conv_kernel(xp_ref, x_ref, xn_ref, w_ref, b_ref, o_ref, *, n_l):
    l = pl.program_id(1)
    tl = x_ref.shape[0]
    xp = jnp.where(l > 0, xp_ref[...], 0.0)
    xn = jnp.where(l < n_l - 1, xn_ref[...], 0.0)
    xx = jnp.concatenate([xp, x_ref[...], xn], axis=0)
    acc = b_ref[...] + w_ref[CONV_PAD:CONV_PAD + 1, :] * x_ref[...]
    for k in range(CONV_WIDTH):
        if k != CONV_PAD:
            shifted = pltpu.roll(xx, (CONV_PAD - k) % xx.shape[0], axis=0)
            acc = acc + w_ref[k:k + 1, :] * shifted[SUBLANES:SUBLANES + tl, :]
    o_ref[...] = _silu(acc)


def _conv(proj, conv_w, conv_b, row_off, batch, seq, tl=2048, tc=512):
    t = proj.shape[0]
    n_l = seq // tl
    n_c = CONV_DIM // tc
    c0 = COL_XBC // tc
    r8 = tl // SUBLANES
    last8 = t // SUBLANES - 1

    def row_blk(b, l):
        return (row_off + b * seq) // tl + l

    return pl.pallas_call(
        functools.partial(_conv_kernel, n_l=n_l),
        grid=(batch, n_l, n_c),
        in_specs=[pl.BlockSpec((SUBLANES, tc), lambda b, l, c: (jnp.maximum(row_blk(b, l) * r8 - 1, 0), c0 + c)),
                  pl.BlockSpec((tl, tc), lambda b, l, c: (row_blk(b, l), c0 + c)),
                  pl.BlockSpec((SUBLANES, tc), lambda b, l, c: (jnp.minimum((row_blk(b, l) + 1) * r8, last8), c0 + c)),
                  pl.BlockSpec((CONV_WIDTH, tc), lambda b, l, c: (0, c)),
                  pl.BlockSpec((1, tc), lambda b, l, c: (0, c))],
        out_specs=pl.BlockSpec((tl, tc), lambda b, l, c: (b * n_l + l, c)),
        out_shape=jax.ShapeDtypeStruct((batch * seq, CONV_DIM), F32),
        compiler_params=_cparams(("arbitrary", "arbitrary", "arbitrary")),
        name="conv",
    )(proj, proj, proj, conv_w, conv_b)


def _exact_tri_matmul(tri_bf16, x):
    hi = x.astype(BF16)
    r1 = x - hi.astype(F32)
    mid = r1.astype(BF16)
    lo = (r1 - mid.astype(F32)).astype(BF16)
    out = jnp.dot(tri_bf16, hi, preferred_element_type=F32)
    out = out + jnp.dot(tri_bf16, mid, preferred_element_type=F32)
    return out + jnp.dot(tri_bf16, lo, preferred_element_type=F32)


def _ssd_kernel(*refs, reverse, accumulate):
    if accumulate:
        x_ref, b_ref, c_ref, dt_ref, dtb_ref, alog_ref, yin_ref, y_ref, state_ref = refs
    else:
        x_ref, b_ref, c_ref, dt_ref, dtb_ref, alog_ref, y_ref, state_ref = refs
        yin_ref = None

    @pl.when(pl.program_id(1) == 0)
    def _():
        state_ref[...] = jnp.zeros(state_ref.shape, F32)

    for sub in (range(SSD_CHUNKS_PER_STEP - 1, -1, -1) if reverse else range(SSD_CHUNKS_PER_STEP)):
        _ssd_chunk(x_ref, b_ref, c_ref, dt_ref, dtb_ref, alog_ref, yin_ref, y_ref, state_ref,
                   slice(sub * SSD_CHUNK, (sub + 1) * SSD_CHUNK), reverse)


def _ssd_chunk(x_ref, b_ref, c_ref, dt_ref, dtb_ref, alog_ref, yin_ref, y_ref, state_ref, rows, reverse):
    lane0 = SSD_HEADS if reverse else 0
    accumulate = yin_ref is not None
    dt = _softplus(dt_ref[rows, :] + dtb_ref[...])
    da = dt * (-jnp.exp(alog_ref[...]))
    row = lax.broadcasted_iota(I32, (SSD_CHUNK, SSD_CHUNK), 0)
    col = lax.broadcasted_iota(I32, (SSD_CHUNK, SSD_CHUNK), 1)
    tri = (col >= row) if reverse else (col <= row)
    cum = _exact_tri_matmul(tri.astype(BF16), da)
    cum_t = cum.T
    dt_t = dt.T
    edge = 0 if reverse else SSD_CHUNK - 1
    total_b = jnp.broadcast_to(cum_t[:, edge:edge + 1], (LANES, SSD_CHUNK))
    to_end_t = dt_t * jnp.exp(total_b - cum_t)
    chunk_decay_b = jnp.exp(total_b)
    src_t = cum_t - jnp.log(dt_t)
    first_head = lax.broadcasted_iota(I32, (1, 2 * SSD_HEAD_DIM), 1) < SSD_HEAD_DIM
    pairs = HEADS_PER_SSD_GROUP // 2

    for g in range(SSD_GROUPS):
        bg = b_ref[rows, g * SSD_STATE:(g + 1) * SSD_STATE]
        cg = c_ref[rows, g * SSD_STATE:(g + 1) * SSD_STATE]
        cb = lax.dot_general(cg.astype(BF16), bg.astype(BF16), (((1,), (1,)), ((), ())),
                             preferred_element_type=F32)
        bg_t = bg.T
        for pr in range(pairs):
            h0 = g * HEADS_PER_SSD_GROUP + 2 * pr
            cols = slice(h0 * SSD_HEAD_DIM, (h0 + 2) * SSD_HEAD_DIM)
            x_pair = x_ref[rows, cols].astype(BF16)
            state = state_ref[g * pairs + pr]
            rhs = jnp.concatenate([x_pair, state.astype(BF16)], axis=0)
            res = []
            new = []
            for e in range(2):
                ln = lane0 + h0 + e
                cum_l = jnp.broadcast_to(cum[:, ln:ln + 1], (SSD_CHUNK, SSD_CHUNK))
                decay = jnp.exp(jnp.where(tri, cum_l - src_t[ln:ln + 1, :], NEG_INF))
                lhs = jnp.concatenate([(cb * decay).astype(BF16), (cg * jnp.exp(cum_l)).astype(BF16)], axis=1)
                res.append(jnp.dot(lhs, rhs, preferred_element_type=F32))
                b_scaled = (bg_t * to_end_t[ln:ln + 1, :]).astype(BF16)
                new.append(jnp.dot(b_scaled, x_pair, preferred_element_type=F32))
            y_pair = jnp.where(first_head, res[0], res[1])
            if accumulate:
                y_pair = y_pair + yin_ref[rows, cols]
            y_ref[rows, cols] = y_pair
            ln0 = lane0 + h0
            dec = jnp.where(first_head, chunk_decay_b[ln0:ln0 + 1, :], chunk_decay_b[ln0 + 1:ln0 + 2, :])
            state_ref[g * pairs + pr] = state * dec + jnp.where(first_head, new[0], new[1])


def _ssd(xbc, dt_raw, dt_bias, a_log, y_in, row_off, batch, seq, reverse):
    rows = SSD_CHUNK * SSD_CHUNKS_PER_STEP
    n_c = seq // rows
    accumulate = y_in is not None

    def loc(b, c):
        return b * n_c + (n_c - 1 - c if reverse else c)

    def glob(b, c):
        return row_off // rows + loc(b, c)

    in_specs = [pl.BlockSpec((rows, D_INNER), lambda b, c: (loc(b, c), 0)),
                pl.BlockSpec((rows, SSD_GROUPS * SSD_STATE), lambda b, c: (loc(b, c), 2)),
                pl.BlockSpec((rows, SSD_GROUPS * SSD_STATE), lambda b, c: (loc(b, c), 3)),
                pl.BlockSpec((rows, DT_COLS), lambda b, c: (glob(b, c), 0)),
                pl.BlockSpec((1, DT_COLS), lambda b, c: (0, 0)),
                pl.BlockSpec((1, DT_COLS), lambda b, c: (0, 0))]
    args = [xbc, xbc, xbc, dt_raw, dt_bias, a_log]
    aliases = {}
    if accumulate:
        in_specs.append(pl.BlockSpec((rows, D_INNER), lambda b, c: (loc(b, c), 0)))
        args.append(y_in)
        aliases = {len(args) - 1: 0}
    return pl.pallas_call(
        functools.partial(_ssd_kernel, reverse=reverse, accumulate=accumulate),
        grid=(batch, n_c),
        in_specs=in_specs,
        out_specs=pl.BlockSpec((rows, D_INNER), lambda b, c: (loc(b, c), 0)),
        out_shape=jax.ShapeDtypeStruct((batch * seq, D_INNER), F32),
        scratch_shapes=[pltpu.VMEM((SSD_HEADS // 2, SSD_STATE, 2 * SSD_HEAD_DIM), F32)],
        input_output_aliases=aliases,
        compiler_params=_cparams(("arbitrary", "arbitrary")),
        name="ssd_bwd" if reverse else "ssd_fwd",
    )(*args)


def _ssd_out_kernel(y_ref, x_ref, z_ref, dskip_ref, nw_ref, w_ref, o_ref):
    y = y_ref[...] + dskip_ref[...] * x_ref[...]
    y = y * _silu(z_ref[...])
    ms = jnp.mean(y * y, axis=-1, keepdims=True)
    yn = y * lax.rsqrt(ms + NORM_EPS) * nw_ref[...]
    o_ref[...] = jnp.dot(yn.astype(BF16), w_ref[...], preferred_element_type=F32)


def _ssd_out(y, xbc, proj, d_skip_cols, norm_w, w_branch, row_off, tm=512):
    t = y.shape[0]
    return pl.pallas_call(
        _ssd_out_kernel,
        grid=(t // tm,),
        in_specs=[pl.BlockSpec((tm, D_INNER), lambda i: (i, 0)),
                  pl.BlockSpec((tm, D_INNER), lambda i: (i, 0)),
                  pl.BlockSpec((tm, D_INNER), lambda i: (row_off // tm + i, COL_Z // D_INNER)),
                  pl.BlockSpec((1, D_INNER), lambda i: (0, 0)),
                  pl.BlockSpec((1, D_INNER), lambda i: (0, 0)),
                  pl.BlockSpec((D_INNER, D_MODEL), lambda i: (0, 0))],
        out_specs=pl.BlockSpec((tm, D_MODEL), lambda i: (i, 0)),
        out_shape=jax.ShapeDtypeStruct((t, D_MODEL), F32),
        compiler_params=_cparams(("arbitrary",)),
        name="ssd_out",
    )(y, xbc, proj, d_skip_cols, norm_w, w_branch)


def _prep_kernel(q_ref, k_ref, v_ref, cos_ref, sa_ref, sb_ref, qo_ref, ko_ref, vo_ref, *, dil):
    rows = q_ref.shape[0] // dil

    def phase(ref, p):
        if dil == 1:
            return ref[...]
        return ref[pl.ds(p, rows, stride=dil), :]

    for p in range(dil):
        cos = phase(cos_ref, p)
        sa = phase(sa_ref, p)
        sb = phase(sb_ref, p)

        def rot(t):
            return (t * cos + pltpu.roll(t, LANES - ROT_HALF, axis=1) * sa
                    + pltpu.roll(t, ROT_HALF, axis=1) * sb)

        qo_ref[0, p] = (rot(phase(q_ref, p)) * (ATTN_HEAD_DIM ** -0.5)).astype(qo_ref.dtype)
        ko_ref[0, p] = rot(phase(k_ref, p)).astype(ko_ref.dtype)
        vo_ref[0, p] = phase(v_ref, p).astype(vo_ref.dtype)


def _prep(proj, tables, gi, dil, row_off, batch, seq, rows_per_step=2048):
    r = rows_per_step
    n_i = seq // r
    n_c = ATTN_GROUP_COLS // LANES
    m = seq // dil
    cos_t, sa_t, sb_t = tables

    def in_blk(col):
        cb = (col + gi * ATTN_GROUP_COLS) // LANES
        return pl.BlockSpec((r, LANES), lambda b, i, c: ((row_off + b * seq) // r + i, cb + c))

    tab = pl.BlockSpec((r, LANES), lambda b, i, c: (i, 0))
    out = pl.BlockSpec((1, dil, r // dil, LANES), lambda b, i, c: (b, 0, i, c))
    shape = jax.ShapeDtypeStruct((batch, dil, m, ATTN_GROUP_COLS), BF16)
    return pl.pallas_call(
        functools.partial(_prep_kernel, dil=dil),
        grid=(batch, n_i, n_c),
        in_specs=[in_blk(COL_Q), in_blk(COL_K), in_blk(COL_V), tab, tab, tab],
        out_specs=[out, out, out],
        out_shape=[shape, shape, shape],
        compiler_params=_cparams(("arbitrary", "arbitrary", "arbitrary")),
        name=f"attn_prep_d{dil}",
    )(proj, proj, proj, cos_t, sa_t, sb_t)


def _band_attn_kernel(q_ref, kp_ref, kc_ref, kn_ref, vp_ref, vc_ref, vn_ref, o_ref, lse_ref, *, m_len):
    i = pl.program_id(1)
    tq = ATTN_TQ
    n_sub = q_ref.shape[1] // tq
    tk = tq + 2 * BAND_RADIUS
    q_all = q_ref[0].astype(BF16)
    k_all = jnp.concatenate([kp_ref[0], kc_ref[0], kn_ref[0]], axis=0).astype(BF16)
    v_all = jnp.concatenate([vp_ref[0], vc_ref[0], vn_ref[0]], axis=0).astype(BF16)
    first_head = lax.broadcasted_iota(I32, (1, LANES), 1) < ATTN_HEAD_DIM
    ones = jnp.ones((tk, LANES), BF16)
    zero = jnp.zeros((), BF16)
    for sub in range(n_sub):
        start = (i * n_sub + sub) * tq
        rows = slice(sub * tq, (sub + 1) * tq)
        q = q_all[rows]
        k = k_all[sub * tq:sub * tq + tk]
        v = v_all[sub * tq:sub * tq + tk]
        qpos = start + lax.broadcasted_iota(I32, (tq, tk), 0)
        kpos = start - BAND_RADIUS + lax.broadcasted_iota(I32, (tq, tk), 1)
        valid = (jnp.abs(qpos - kpos) <= BAND_RADIUS) & (kpos >= 0) & (kpos < m_len)
        for pr in range(ATTN_GROUP_COLS // LANES):
            cols = slice(pr * LANES, (pr + 1) * LANES)
            q_pair, k_pair = q[:, cols], k[:, cols]
            v_ext = jnp.concatenate([v[:, cols], ones], axis=1)
            outs, lses = [], []
            for sel in (first_head, jnp.logical_not(first_head)):
                s = lax.dot_general(jnp.where(sel, q_pair, zero), k_pair, (((1,), (1,)), ((), ())),
                                    preferred_element_type=F32)
                s = jnp.where(valid, s, NEG_INF)
                mx = jnp.max(s, axis=-1, keepdims=True)
                p = jnp.exp(s - mx).astype(BF16)
                ov = jnp.dot(p, v_ext, preferred_element_type=F32)
                den = ov[:, LANES:]
                outs.append(ov[:, :LANES] / den)
                lses.append(mx + jnp.log(den))
            o_ref[0, rows, cols] = jnp.where(first_head, outs[0], outs[1])
            lse_ref[0, rows, cols] = jnp.where(first_head, lses[0], lses[1])


def _band_attn(q, k, v):
    b, d, m, c = q.shape
    n = b * d
    q, k, v = (a.reshape(n, m, c) for a in (q, k, v))
    tq = ATTN_TQ * (2 if m % (2 * ATTN_TQ) == 0 else 1)
    per = tq // BAND_RADIUS
    last = m // BAND_RADIUS - 1
    main = pl.BlockSpec((1, tq, c), lambda s, i: (s, i, 0))
    prev = pl.BlockSpec((1, BAND_RADIUS, c), lambda s, i: (s, jnp.maximum(i * per - 1, 0), 0))
    nxt = pl.BlockSpec((1, BAND_RADIUS, c), lambda s, i: (s, jnp.minimum((i + 1) * per, last), 0))
    shape = jax.ShapeDtypeStruct((n, m, c), F32)
    o, lse = pl.pallas_call(
        functools.partial(_band_attn_kernel, m_len=m),
        grid=(n, m // tq),
        in_specs=[main, prev, main, nxt, prev, main, nxt],
        out_specs=[main, main],
        out_shape=[shape, shape],
        compiler_params=_cparams(("arbitrary", "arbitrary")),
        name=f"band_attn_m{m}",
    )(q, k, k, k, v, v, v)
    return o.reshape(b, d, m, c), lse.reshape(b, d, m, c)


def _layer_norm(x, g, b):
    mu = jnp.mean(x, axis=-1, keepdims=True)
    xc = x - mu
    var = jnp.mean(xc * xc, axis=-1, keepdims=True)
    return xc * lax.rsqrt(var + NORM_EPS) * g + b


def _merge_kernel(x_ref, yssd_ref, gs_ref, ga_ref, o1_ref, l1_ref, o2_ref, l2_ref, o3_ref, l3_ref,
                  g1_ref, sc2_ref, sh2_ref, wba_ref, wout_ref, lng_ref, lnb_ref, wr_ref, *rest, n_prev):
    x1_ref, h2_ref, logit_ref, scr_ref = rest[n_prev:]
    tm = x_ref.shape[0]

    def interleaved(ref, dil, slot):
        if dil == 1:
            return ref[0, 0]
        rows = tm // dil
        n_c = ATTN_GROUP_COLS // LANES
        for p in range(dil):
            for c in range(n_c):
                scr_ref[slot * n_c + c, pl.ds(p, rows, stride=dil), :] = ref[0, p, :, c * LANES:(c + 1) * LANES]
        return jnp.concatenate([scr_ref[slot * n_c + c] for c in range(n_c)], axis=1)

    outs = []
    lses = []
    slot = 0
    for (o_ref, l_ref), (_, dil) in zip(((o1_ref, l1_ref), (o2_ref, l2_ref), (o3_ref, l3_ref)), ATTN_PATTERNS):
        outs.append(interleaved(o_ref, dil, slot))
        lses.append(interleaved(l_ref, dil, slot + 1))
        slot += 2
    mx = jnp.maximum(jnp.maximum(lses[0], lses[1]), lses[2])
    es = [jnp.exp(l - mx) for l in lses]
    den = es[0] + es[1] + es[2]
    y_att = (es[0] * outs[0] + es[1] * outs[1] + es[2] * outs[2]) / den
    y_attn = jnp.dot(y_att.astype(BF16), wba_ref[...], preferred_element_type=F32)
    merged = _sigmoid(gs_ref[...]) * yssd_ref[...] + _sigmoid(ga_ref[...]) * y_attn
    mix = jnp.dot(merged.astype(BF16), wout_ref[...], preferred_element_type=F32)
    x1 = _layer_norm(ALPHA * x_ref[...] + g1_ref[0] * mix, lng_ref[...], lnb_ref[...])
    x1_ref[...] = x1
    h2 = x1 * (1.0 + sc2_ref[0]) + sh2_ref[0]
    _store_rows(h2_ref, (), h2)
    logit_ref[...] = lax.dot_general(wr_ref[...], h2.astype(BF16), (((1,), (1,)), ((), ())),
                                     preferred_element_type=F32)


def _merge(x, y_ssd, proj, attn, gate1, scale2, shift2, w_ba, w_out, ln_g, ln_b, w_router_t,
           batch, seq, out_row_off, t_all, prev, tm=512):
    n_i = seq // tm
    out0 = out_row_off // tm

    def rows(col_blk):
        return pl.BlockSpec((tm, D_MODEL), lambda b, i: (b * n_i + i, col_blk))

    def local(width):
        return pl.BlockSpec((tm, width), lambda b, i: (b * n_i + i, 0))

    def modv():
        return pl.BlockSpec((1, 1, D_MODEL), lambda b, i: (b, 0, 0))

    def full(shape):
        return pl.BlockSpec(shape, lambda b, i: tuple(0 for _ in shape))

    attn_specs = []
    attn_args = []
    for (o, lse), (_, dil) in zip(attn, ATTN_PATTERNS):
        spec = pl.BlockSpec((1, dil, tm // dil, ATTN_GROUP_COLS), lambda b, i: (b, 0, i, 0))
        attn_specs += [spec, spec]
        attn_args += [o, lse]
    n_in = 4 + len(attn_args) + 8
    return pl.pallas_call(
        functools.partial(_merge_kernel, n_prev=0 if prev is None else 3),
        grid=(batch, n_i),
        in_specs=[rows(0), local(D_MODEL), rows(COL_GSSD // D_MODEL), rows(COL_GATTN // D_MODEL)] + attn_specs
                 + [modv(), modv(), modv(), full((ATTN_GROUP_COLS, D_MODEL)), full((D_MODEL, D_MODEL)),
                    full((1, D_MODEL)), full((1, D_MODEL)), full((N_EXPERTS, D_MODEL))]
                 + ([pl.BlockSpec(memory_space=pl.ANY)] * 3 if prev is not None else []),
        out_specs=[pl.BlockSpec((tm, D_MODEL), lambda b, i: (out0 + b * n_i + i, 0)),
                   pl.BlockSpec((tm * ROW_SUBLANES, LANES), lambda b, i: (out0 + b * n_i + i, 0)),
                   pl.BlockSpec((N_EXPERTS, tm), lambda b, i: (0, out0 + b * n_i + i))],
        out_shape=[jax.ShapeDtypeStruct((t_all, D_MODEL), F32),
                   jax.ShapeDtypeStruct((t_all * ROW_SUBLANES, LANES), jnp.uint32),
                   jax.ShapeDtypeStruct((N_EXPERTS, t_all), F32)],
        scratch_shapes=[pltpu.VMEM((2 * ATTN_GROUPS * (ATTN_GROUP_COLS // LANES), tm, LANES), F32)],
        input_output_aliases={n_in + j: j for j in range(3)} if prev is not None else {},
        compiler_params=_cparams(("arbitrary", "arbitrary")),
        name="mixer_merge",
    )(x, y_ssd, proj, proj, *attn_args, gate1, scale2, shift2, w_ba, w_out, ln_g, ln_b, w_router_t,
      *(prev if prev is not None else ()))


def _route_kernel(logit_ref, bias_ref, idx_ref, w_ref, rank_ref, cnt_ref, run_ref):
    tm = logit_ref.shape[1]

    @pl.when(pl.program_id(0) == 0)
    def _():
        run_ref[...] = jnp.zeros(run_ref.shape, F32)

    scores = _sigmoid(logit_ref[...])
    sel = scores + bias_ref[...]
    iota_g = lax.broadcasted_iota(I32, (EXPERTS_PER_GROUP, tm), 0)
    grp = []
    for g in range(N_EXPERT_GROUPS):
        sg = sel[g * EXPERTS_PER_GROUP:(g + 1) * EXPERTS_PER_GROUP, :]
        top1 = jnp.max(sg, axis=0, keepdims=True)
        first = jnp.min(jnp.where(sg == top1, iota_g, EXPERTS_PER_GROUP), axis=0, keepdims=True)
        top2 = jnp.max(jnp.where(iota_g == first, NEG_INF, sg), axis=0, keepdims=True)
        grp.append(top1 + top2)
    gs = jnp.concatenate(grp, axis=0)
    iota_n = lax.broadcasted_iota(I32, (N_EXPERT_GROUPS, tm), 0)
    keep_g = jnp.zeros((N_EXPERT_GROUPS, tm), F32)
    for _ in range(TOPK_GROUPS):
        best = jnp.max(gs, axis=0, keepdims=True)
        first = jnp.min(jnp.where(gs == best, iota_n, N_EXPERT_GROUPS), axis=0, keepdims=True)
        hit = iota_n == first
        keep_g = jnp.where(hit, 1.0, keep_g)
        gs = jnp.where(hit, NEG_INF, gs)
    cand = jnp.concatenate(
        [jnp.where(keep_g[g:g + 1, :] > 0.0, sel[g * EXPERTS_PER_GROUP:(g + 1) * EXPERTS_PER_GROUP, :], NEG_INF)
         for g in range(N_EXPERT_GROUPS)], axis=0)
    iota_e = lax.broadcasted_iota(I32, (N_EXPERTS, tm), 0)
    chosen = jnp.zeros((N_EXPERTS, tm), F32)
    idxs = []
    ws = []
    for _ in range(TOP_K):
        best = jnp.max(cand, axis=0, keepdims=True)
        first = jnp.min(jnp.where(cand == best, iota_e, N_EXPERTS), axis=0, keepdims=True)
        hit = iota_e == first
        idxs.append(first)
        ws.append(jnp.sum(jnp.where(hit, scores, 0.0), axis=0, keepdims=True))
        chosen = jnp.where(hit, 1.0, chosen)
        cand = jnp.where(hit, NEG_INF, cand)
    top_w = jnp.concatenate(ws, axis=0)
    top_w = top_w / jnp.sum(top_w, axis=0, keepdims=True) * ROUTED_SCALE
    idx_ref[...] = jnp.concatenate(idxs, axis=0)
    w_ref[...] = top_w
    s_i = lax.broadcasted_iota(I32, (tm, tm), 0)
    t_i = lax.broadcasted_iota(I32, (tm, tm), 1)
    before = (s_i < t_i).astype(BF16)
    prior = jnp.dot(chosen.astype(BF16), before, preferred_element_type=F32) + run_ref[...]
    ranks = [jnp.sum(jnp.where(iota_e == idxs[k], prior, 0.0), axis=0, keepdims=True) for k in range(TOP_K)]
    rank_ref[...] = jnp.concatenate(ranks, axis=0).astype(I32)
    run_ref[...] = run_ref[...] + jnp.sum(chosen, axis=1, keepdims=True)
    cnt_ref[...] = run_ref[...].astype(I32)


def _route(logits_t, bias_col, tm=512):
    t = logits_t.shape[1]
    tok = pl.BlockSpec((TOP_K, tm), lambda i: (0, i))
    return pl.pallas_call(
        _route_kernel,
        grid=(t // tm,),
        in_specs=[pl.BlockSpec((N_EXPERTS, tm), lambda i: (0, i)),
                  pl.BlockSpec((N_EXPERTS, 1), lambda i: (0, 0))],
        out_specs=[tok, tok, tok, pl.BlockSpec((N_EXPERTS, 1), lambda i: (0, 0))],
        out_shape=[jax.ShapeDtypeStruct((TOP_K, t), I32), jax.ShapeDtypeStruct((TOP_K, t), F32),
                   jax.ShapeDtypeStruct((TOP_K, t), I32), jax.ShapeDtypeStruct((N_EXPERTS, 1), I32)],
        scratch_shapes=[pltpu.VMEM((N_EXPERTS, 1), F32)],
        compiler_params=_cparams(("arbitrary",)),
        name="route",
    )(logits_t, bias_col)


def _dest_kernel(idx_ref, rank_ref, start_ref, o_ref):
    tm = idx_ref.shape[1]
    iota_e = lax.broadcasted_iota(I32, (N_EXPERTS, tm), 0)
    start = start_ref[...].astype(F32)
    for k in range(TOP_K):
        base = jnp.sum(jnp.where(iota_e == idx_ref[k:k + 1, :], start, 0.0), axis=0, keepdims=True)
        o_ref[0, :, k * tm:(k + 1) * tm] = (base.astype(I32) + rank_ref[k:k + 1, :]) * ROW_SUBLANES


def _dest(top_idx, rank, pad_start_col, tm):
    t = top_idx.shape[1]
    tok = pl.BlockSpec((TOP_K, tm), lambda i: (0, i))
    out = pl.pallas_call(
        _dest_kernel,
        grid=(t // tm,),
        in_specs=[tok, tok, pl.BlockSpec((N_EXPERTS, 1), lambda i: (0, 0))],
        out_specs=pl.BlockSpec((1, 1, TOP_K * tm), lambda i: (i, 0, 0)),
        out_shape=jax.ShapeDtypeStruct((t // tm, 1, TOP_K * tm), I32),
        compiler_params=_cparams(("arbitrary",)),
        name="moe_dest",
    )(top_idx, rank, pad_start_col)
    return out.reshape(t // tm, TOP_K * tm)


def _dispatch_kernel(pad_end_ref, dest_hbm, h_ref, xs_hbm, *scratch, tm, n_sub):
    dest_smem = scratch[:n_sub]
    zeros_ref, sem_idx, sem_zero, sem_row = scratch[n_sub:]
    i = pl.program_id(0)
    idx_copies = [pltpu.make_async_copy(dest_hbm.at[i * n_sub + j], dest_smem[j], sem_idx) for j in range(n_sub)]
    for c in idx_copies:
        c.start()

    blk_rows = EXPERT_BLOCK * ROW_SUBLANES

    def zero_copy(e):
        start = pl.multiple_of(jnp.maximum(pad_end_ref[e] - EXPERT_BLOCK, 0) * ROW_SUBLANES, blk_rows)
        return pltpu.make_async_copy(zeros_ref, xs_hbm.at[pl.ds(start, blk_rows)], sem_zero)

    @pl.when(i == 0)
    def _():
        zeros_ref[...] = jnp.zeros(zeros_ref.shape, zeros_ref.dtype)

        def start_zero(e, carry):
            zero_copy(e).start()
            return carry

        def wait_zero(e, carry):
            zero_copy(e).wait()
            return carry

        lax.fori_loop(0, N_EXPERTS, start_zero, 0)
        lax.fori_loop(0, N_EXPERTS, wait_zero, 0)

    for c in idx_copies:
        c.wait()

    def row_copy(j, t, k):
        dst = pl.multiple_of(dest_smem[j][k * tm + t], ROW_SUBLANES)
        src = pl.multiple_of((j * tm + t) * ROW_SUBLANES, ROW_SUBLANES)
        return pltpu.make_async_copy(h_ref.at[pl.ds(src, ROW_SUBLANES)],
                                     xs_hbm.at[pl.ds(dst, ROW_SUBLANES)], sem_row)

    for j in range(n_sub):
        def start_rows(t, carry, j=j):
            for k in range(TOP_K):
                row_copy(j, t, k).start(priority=k % DMA_PRIORITIES)
            return carry

        lax.fori_loop(0, tm, start_rows, 0)

    for j in range(n_sub):
        def wait_rows(t, carry, j=j):
            for k in range(TOP_K):
                row_copy(j, t, k).wait()
            return carry

        lax.fori_loop(0, tm, wait_rows, 0)


def _dispatch(pad_end, dest_tiles, h2, n_rows, tm, n_sub=4):
    n_tiles = dest_tiles.shape[0]
    assert n_tiles % n_sub == 0
    return pl.pallas_call(
        functools.partial(_dispatch_kernel, tm=tm, n_sub=n_sub),
        grid_spec=pltpu.PrefetchScalarGridSpec(
            num_scalar_prefetch=1,
            grid=(n_tiles // n_sub,),
            in_specs=[pl.BlockSpec(memory_space=pl.ANY),
                      pl.BlockSpec((n_sub * tm * ROW_SUBLANES, LANES), lambda i, pe: (i, 0))],
            out_specs=pl.BlockSpec(memory_space=pl.ANY),
            scratch_shapes=[pltpu.SMEM((TOP_K * tm,), I32) for _ in range(n_sub)] + [
                            pltpu.VMEM((EXPERT_BLOCK * ROW_SUBLANES, LANES), jnp.uint32),
                            pltpu.SemaphoreType.DMA, pltpu.SemaphoreType.DMA, pltpu.SemaphoreType.DMA]),
        out_shape=jax.ShapeDtypeStruct((n_rows * ROW_SUBLANES, LANES), jnp.uint32),
        compiler_params=_cparams(("arbitrary",)),
        name="moe_dispatch",
    )(pad_end, dest_tiles, h2)


def _experts_kernel(blk_exp_ref, n_valid_ref, xs_ref, wg_ref, wu_ref, wd_ref, ys_ref, wgu_ref, wdn_ref):
    i = pl.program_id(0)

    @pl.when(i < n_valid_ref[0])
    def _():
        @pl.when((i == 0) | (blk_exp_ref[i] != blk_exp_ref[jnp.maximum(i - 1, 0)]))
        def _():
            wgu_ref[:, :EXPERT_HIDDEN] = wg_ref[0].astype(BF16)
            wgu_ref[:, EXPERT_HIDDEN:] = wu_ref[0].astype(BF16)
            wdn_ref[...] = wd_ref[0].astype(BF16)

        x = jnp.concatenate([c.astype(BF16) for c in _load_row_pieces(xs_ref, (), EXPERT_BLOCK)], axis=1)
        gu = jnp.dot(x, wgu_ref[...], preferred_element_type=F32)
        hmid = (_silu(gu[:, :EXPERT_HIDDEN]) * gu[:, EXPERT_HIDDEN:]).astype(BF16)
        _store_rows(ys_ref, (), jnp.dot(hmid, wdn_ref[...], preferred_element_type=F32))


def _experts(blk_exp, n_valid, xs, w_gate, w_up, w_down):
    n_blocks = xs.shape[0] // (EXPERT_BLOCK * ROW_SUBLANES)

    def row_map(i, be, nv):
        return (jnp.minimum(i, nv[0] - 1), 0)

    def w_map(i, be, nv):
        return (be[i], 0, 0)

    return pl.pallas_call(
        _experts_kernel,
        grid_spec=pltpu.PrefetchScalarGridSpec(
            num_scalar_prefetch=2,
            grid=(n_blocks,),
            in_specs=[pl.BlockSpec((EXPERT_BLOCK * ROW_SUBLANES, LANES), row_map),
                      pl.BlockSpec((1, D_MODEL, EXPERT_HIDDEN), w_map),
                      pl.BlockSpec((1, D_MODEL, EXPERT_HIDDEN), w_map),
                      pl.BlockSpec((1, EXPERT_HIDDEN, D_MODEL), w_map)],
            out_specs=pl.BlockSpec((EXPERT_BLOCK * ROW_SUBLANES, LANES), row_map),
            scratch_shapes=[pltpu.VMEM((D_MODEL, 2 * EXPERT_HIDDEN), BF16),
                            pltpu.VMEM((EXPERT_HIDDEN, D_MODEL), BF16)]),
        out_shape=jax.ShapeDtypeStruct(xs.shape, jnp.uint32),
        compiler_params=_cparams(("arbitrary",)),
        name="moe_experts",
    )(blk_exp, n_valid, xs, w_gate, w_up, w_down)


def _combine_kernel(dest_hbm, ys_hbm, w_ref, x1_ref, h2_ref, g2_ref, wsg_ref, wsu_ref, wsd_ref,
                    lng_ref, lnb_ref, o_ref, dest0, dest1, rows0, rows1, sem_idx, sem0, sem1, *, tm, tile_off):
    i = pl.program_id(0)
    n_i = pl.num_programs(0)
    dest_smem, rows_ref, sem_row = (dest0, dest1), (rows0, rows1), (sem0, sem1)

    def row_copy(s, t, k):
        src = pl.multiple_of(dest_smem[s][k * tm + t], ROW_SUBLANES)
        dst = pl.multiple_of(t * ROW_SUBLANES, ROW_SUBLANES)
        return pltpu.make_async_copy(ys_hbm.at[pl.ds(src, ROW_SUBLANES)],
                                     rows_ref[s].at[k, pl.ds(dst, ROW_SUBLANES)], sem_row[s])

    def start_gather(tile, s):
        idx_copy = pltpu.make_async_copy(dest_hbm.at[tile + tile_off], dest_smem[s], sem_idx)
        idx_copy.start()
        idx_copy.wait()

        def start_rows(t, carry):
            for k in range(TOP_K):
                row_copy(s, t, k).start(priority=k % DMA_PRIORITIES)
            return carry

        lax.fori_loop(0, tm, start_rows, 0)

    @pl.when(i == 0)
    def _():
        start_gather(0, 0)

    for s in range(2):
        if s == 0:
            start_gather(2 * i + 1, 1)
        else:
            @pl.when(i + 1 < n_i)
            def _():
                start_gather(2 * i + 2, 0)

        rows = slice(s * tm, (s + 1) * tm)
        h2 = jnp.concatenate(
            [c.astype(BF16) for c in _load_row_pieces(h2_ref, (), tm, s * tm * ROW_SUBLANES)], axis=1)
        g = jnp.dot(h2, wsg_ref[...], preferred_element_type=F32)
        u = jnp.dot(h2, wsu_ref[...], preferred_element_type=F32)
        ffn = jnp.dot((_silu(g) * u).astype(BF16), wsd_ref[...], preferred_element_type=F32)

        def wait_rows(t, carry):
            for k in range(TOP_K):
                row_copy(s, t, k).wait()
            return carry

        lax.fori_loop(0, tm, wait_rows, 0)
        w = w_ref[rows, :]
        n_pieces = D_MODEL // LANES
        acc = [ffn[:, j * LANES:(j + 1) * LANES] for j in range(n_pieces)]
        for k in range(TOP_K):
            wk = jnp.broadcast_to(w[:, k:k + 1], (tm, LANES))
            pieces = _load_row_pieces(rows_ref[s], (k,), tm)
            acc = [a + wk * p for a, p in zip(acc, pieces)]
        ffn = jnp.concatenate(acc, axis=1)
        o_ref[rows, :] = _layer_norm(ALPHA * x1_ref[rows, :] + g2_ref[0] * ffn, lng_ref[...], lnb_ref[...])


def _combine(dest_tiles, ys, top_w_rows, x1, h2, gate2, w_sg, w_su, w_sd, ln_g, ln_b, seg, tm, tile_off, n_tiles):
    assert tile_off % 2 == 0 and n_tiles % 2 == 0 and seg % (2 * tm) == 0
    per_seg = seg // (2 * tm)
    off = tile_off // 2

    def full(shape):
        return pl.BlockSpec(shape, lambda i: tuple(0 for _ in shape))

    return pl.pallas_call(
        functools.partial(_combine_kernel, tm=tm, tile_off=tile_off),
        grid=(n_tiles // 2,),
        in_specs=[pl.BlockSpec(memory_space=pl.ANY), pl.BlockSpec(memory_space=pl.ANY),
                  pl.BlockSpec((2 * tm, TOP_K), lambda i: (i + off, 0)),
                  pl.BlockSpec((2 * tm, D_MODEL), lambda i: (i + off, 0)),
                  pl.BlockSpec((2 * tm * ROW_SUBLANES, LANES), lambda i: (i + off, 0)),
                  pl.BlockSpec((1, 1, D_MODEL), lambda i: ((i + off) // per_seg, 0, 0)),
                  full((D_MODEL, SHARED_HIDDEN)), full((D_MODEL, SHARED_HIDDEN)), full((SHARED_HIDDEN, D_MODEL)),
                  full((1, D_MODEL)), full((1, D_MODEL))],
        out_specs=pl.BlockSpec((2 * tm, D_MODEL), lambda i: (i, 0)),
        out_shape=jax.ShapeDtypeStruct((n_tiles * tm, D_MODEL), F32),
        scratch_shapes=[pltpu.SMEM((TOP_K * tm,), I32), pltpu.SMEM((TOP_K * tm,), I32),
                        pltpu.VMEM((TOP_K, tm * ROW_SUBLANES, LANES), jnp.uint32),
                        pltpu.VMEM((TOP_K, tm * ROW_SUBLANES, LANES), jnp.uint32),
                        pltpu.SemaphoreType.DMA, pltpu.SemaphoreType.DMA, pltpu.SemaphoreType.DMA],
        compiler_params=_cparams(("arbitrary",)),
        name="moe_combine",
    )(dest_tiles, ys, top_w_rows, x1, h2, gate2, w_sg, w_su, w_sd, ln_g, ln_b)


def _rope_tables(seq):
    inv_freq = ROPE_THETA ** (-jnp.arange(ROT_HALF, dtype=F32) * 2.0 / ROT_DIM)
    ang = jnp.arange(seq, dtype=F32)[:, None] * inv_freq[None, :]
    cos, sin = jnp.cos(ang), jnp.sin(ang)
    rest = ATTN_HEAD_DIM - ROT_DIM
    ones = jnp.ones((seq, rest), F32)
    zeros = jnp.zeros((seq, rest), F32)
    zh = jnp.zeros((seq, ROT_HALF), F32)
    reps = LANES // ATTN_HEAD_DIM
    cos_t = jnp.tile(jnp.concatenate([cos, cos, ones], axis=1), (1, reps))
    sa_t = jnp.tile(jnp.concatenate([-sin, zh, zeros], axis=1), (1, reps))
    sb_t = jnp.tile(jnp.concatenate([zh, sin, zeros], axis=1), (1, reps))
    return cos_t, sa_t, sb_t


def _split_in_weights(w_in):
    cuts = (D_INNER, CONV_DIM, SSD_HEADS, SSD_HEADS, ATTN_DIM, ATTN_DIM, ATTN_DIM, D_MODEL, D_MODEL)
    offs = [0]
    for c in cuts:
        offs.append(offs[-1] + c)
    z, xbc, dtf, dtb, q, k, v, gs, ga = (w_in[:, offs[i]:offs[i + 1]] for i in range(len(cuts)))
    w_main = jnp.concatenate([z, xbc, gs, ga, q, k, v], axis=1).astype(BF16)
    assert COL_GSSD % D_MODEL == 0 and COL_Q % ATTN_GROUP_COLS == 0 and w_main.shape[1] == N_MAIN
    pad = jnp.zeros((D_MODEL, DT_COLS - 2 * SSD_HEADS), F32)
    w_dt = jnp.concatenate([dtf, dtb, pad], axis=1).astype(BF16)
    return w_main, w_dt


def kernel(x_prompt, x_sample, c_prompt, c_sample, w_ada, b_ada, w_in, conv_w, conv_b, dt_bias_fwd, dt_bias_bwd, a_log_fwd, a_log_bwd, d_skip, ssd_norm_w, w_branch_ssd, w_branch_attn, w_out, ln1_g, ln1_b, w_router, router_bias, w_exp_gate, w_exp_up, w_exp_down, w_sh_gate, w_sh_up, w_sh_down, ln2_g, ln2_b):
    assert w_ada.shape[0] == DEPTH
    groups = ((x_prompt, c_prompt), (x_sample, c_sample))
    shapes = [(x.shape[0], x.shape[1]) for x, _ in groups]
    seg = math.gcd(*(s for _, s in shapes))
    tokens = [b * s for b, s in shapes]
    t_all = sum(tokens)
    row_offs = [0, tokens[0]]

    c_all = jnp.concatenate([c_prompt, c_sample], axis=0)
    n_req = c_all.shape[0]
    c_pad = jnp.zeros((-(-n_req // SUBLANES) * SUBLANES, D_MODEL), F32).at[:n_req].set(c_all)
    ada = _ada(c_pad, w_ada[0], b_ada)[:n_req]
    req_offs = [0, shapes[0][0]]
    reps = jnp.array([s // seg for b, s in shapes for _ in range(b)], dtype=I32)
    n_seg = t_all // seg
    gate2 = jnp.repeat(ada[:, 5 * D_MODEL:], reps, axis=0, total_repeat_length=n_seg).reshape(n_seg, 1, D_MODEL)

    w_main, w_dt = _split_in_weights(w_in[0])
    dt_pad = jnp.zeros((DT_COLS - 2 * SSD_HEADS,), F32)
    dt_bias = jnp.concatenate([dt_bias_fwd[0], dt_bias_bwd[0], dt_pad]).reshape(1, DT_COLS)
    a_log = jnp.concatenate([a_log_fwd[0], a_log_bwd[0], dt_pad]).reshape(1, DT_COLS)
    d_skip_cols = jnp.repeat(d_skip[0], SSD_HEAD_DIM).reshape(1, D_INNER)
    w_bs = w_branch_ssd[0].astype(BF16)
    w_ba = w_branch_attn[0].astype(BF16)
    w_o = w_out[0].astype(BF16)
    w_router_t = w_router[0].T.astype(BF16)

    merged = None
    for (x, _), (batch, seq), row_off, req_off in zip(groups, shapes, row_offs, req_offs):
        shift1, scale1, gate1, shift2, scale2, _ = (
            a.reshape(batch, 1, D_MODEL) for a in jnp.split(ada[req_off:req_off + batch], 6, axis=-1))
        x2d = x.reshape(batch * seq, D_MODEL)
        proj = _inproj(x2d, scale1, shift1, w_main, seq, tm=1024, tn=2560, out_dtype=F32)
        dt_raw = _inproj(x2d, scale1, shift1, w_dt, seq, tm=1024, tn=DT_COLS, out_dtype=F32)
        xbc = _conv(proj, conv_w[0], conv_b, 0, batch, seq)
        y = _ssd(xbc, dt_raw, dt_bias, a_log, None, 0, batch, seq, reverse=False)
        y = _ssd(xbc, dt_raw, dt_bias, a_log, y, 0, batch, seq, reverse=True)
        y_ssd = _ssd_out(y, xbc, proj, d_skip_cols, ssd_norm_w, w_bs, 0)
        tables = _rope_tables(seq)
        attn = []
        for gi, (_, dil) in enumerate(ATTN_PATTERNS):
            q, k, v = _prep(proj, tables, gi, dil, 0, batch, seq)
            attn.append(_band_attn(q, k, v))
        merged = _merge(x2d, y_ssd, proj, attn, gate1, scale2, shift2, w_ba, w_o, ln1_g, ln1_b, w_router_t,
                        batch, seq, row_off, t_all, merged)
    x1, h2, logits_t = merged

    top_idx, top_w, rank, counts = _route(logits_t, router_bias[0].reshape(N_EXPERTS, 1))
    counts = counts[:, 0]
    padded = (counts + EXPERT_BLOCK - 1) // EXPERT_BLOCK * EXPERT_BLOCK
    pad_end = jnp.cumsum(padded)
    pad_start = pad_end - padded
    n_blocks = (t_all * TOP_K + N_EXPERTS * (EXPERT_BLOCK - 1)) // EXPERT_BLOCK
    n_rows = n_blocks * EXPERT_BLOCK
    tm_moe = 256
    dest_tiles = _dest(top_idx, rank, pad_start.astype(I32).reshape(N_EXPERTS, 1), tm_moe)
    n_valid = (pad_end[-1] // EXPERT_BLOCK).astype(I32).reshape(1)
    blk_start = jnp.minimum(jnp.arange(n_blocks, dtype=I32) * EXPERT_BLOCK, pad_end[-1] - 1)
    blk_exp = jnp.sum((pad_end[None, :] <= blk_start[:, None]).astype(I32), axis=1)
    blk_exp = jnp.minimum(blk_exp, N_EXPERTS - 1)

    xs = _dispatch(pad_end.astype(I32), dest_tiles, h2, n_rows, tm_moe)
    ys = _experts(blk_exp, n_valid, xs, w_exp_gate[0], w_exp_up[0], w_exp_down[0])
    outs = []
    for (x, _), n_tok, row_off in zip(groups, tokens, row_offs):
        out = _combine(dest_tiles, ys, top_w.T, x1, h2, gate2, w_sh_gate[0].astype(BF16), w_sh_up[0].astype(BF16),
                       w_sh_down[0].astype(BF16), ln2_g, ln2_b, seg, tm_moe, row_off // tm_moe, n_tok // tm_moe)
        outs.append(out.reshape(x.shape))
    return tuple(outs)
```

```python
import functools
import math

import jax
import jax.numpy as jnp
from jax import lax
from jax.experimental import pallas as pl
from jax.experimental.pallas import tpu as pltpu

F32 = jnp.float32
BF16 = jnp.bfloat16
I32 = jnp.int32

D_MODEL = 1024
D_INNER = 2048
SSD_HEADS = 32
SSD_HEAD_DIM = 64
SSD_GROUPS = 8
SSD_STATE = 128
SSD_CHUNK = 128
SSD_CHUNKS_PER_STEP = 4
HEADS_PER_SSD_GROUP = SSD_HEADS // SSD_GROUPS
CONV_WIDTH = 5
CONV_PAD = CONV_WIDTH // 2
CONV_DIM = D_INNER + 2 * SSD_GROUPS * SSD_STATE
ATTN_PATTERNS = ((128, 1), (512, 4), (2048, 16))
ATTN_GROUPS = len(ATTN_PATTERNS)
ATTN_HEADS_PER_GROUP = 8
ATTN_HEAD_DIM = 64
ATTN_GROUP_COLS = ATTN_HEADS_PER_GROUP * ATTN_HEAD_DIM
ATTN_DIM = ATTN_GROUPS * ATTN_GROUP_COLS
ROT_DIM = ATTN_HEAD_DIM // 4
ROT_HALF = ROT_DIM // 2
ROPE_THETA = 500000.0
N_EXPERTS = 256
EXPERT_HIDDEN = 256
TOP_K = 8
N_EXPERT_GROUPS = 8
EXPERTS_PER_GROUP = N_EXPERTS // N_EXPERT_GROUPS
TOPK_GROUPS = 4
ROUTED_SCALE = 2.5
SHARED_HIDDEN = 256
EXPERT_BLOCK = 512
PACKED_COLS = D_MODEL // 2
ROW_SUBLANES = PACKED_COLS // 128
DEPTH = 1
ALPHA = (2.0 * DEPTH) ** 0.25
NORM_EPS = 1e-5

COL_Z = 0
COL_XBC = COL_Z + D_INNER
COL_GSSD = COL_XBC + CONV_DIM
COL_GATTN = COL_GSSD + D_MODEL
COL_Q = COL_GATTN + D_MODEL
COL_K = COL_Q + ATTN_DIM
COL_V = COL_K + ATTN_DIM
N_MAIN = COL_V + ATTN_DIM
DT_COLS = 128

LANES = 128
SUBLANES = 8
VMEM_LIMIT = 56 * 1024 * 1024

DMA_PRIORITIES = 2

BAND_RADIUS = 64
ATTN_TQ = 128
NEG_INF = float("-inf")


def _cparams(sem):
    return pltpu.CompilerParams(dimension_semantics=sem, vmem_limit_bytes=VMEM_LIMIT)


def _sigmoid(x):
    return 1.0 / (1.0 + jnp.exp(-x))


def _silu(x):
    return x * _sigmoid(x)


def _softplus(x):
    return jnp.maximum(x, 0.0) + jnp.log(1.0 + jnp.exp(-jnp.abs(x)))


def _pack_halves(x):
    c = x.shape[1] // 2
    hi = lax.bitcast_convert_type(x[:, :c].astype(BF16).astype(F32), jnp.uint32)
    lo = lax.bitcast_convert_type(x[:, c:].astype(BF16).astype(F32), jnp.uint32)
    return hi | (lo >> 16)


def _unpack_halves(u):
    left = lax.bitcast_convert_type(u & jnp.uint32(0xFFFF0000), F32)
    right = lax.bitcast_convert_type(u << 16, F32)
    return left, right


def _store_rows(ref, lead, x):
    rows = x.shape[0]
    packed = _pack_halves(x)
    for j in range(ROW_SUBLANES):
        idx = lead + (pl.ds(j, rows, stride=ROW_SUBLANES), slice(None))
        ref[idx] = packed[:, j * LANES:(j + 1) * LANES]


def _load_row_pieces(ref, lead, rows, base=0):
    lefts, rights = [], []
    for j in range(ROW_SUBLANES):
        idx = lead + (pl.ds(base + j, rows, stride=ROW_SUBLANES), slice(None))
        left, right = _unpack_halves(ref[idx])
        lefts.append(left)
        rights.append(right)
    return lefts + rights


def _ada_kernel(c_ref, w_ref, b_ref, o_ref):
    c = _silu(c_ref[...]).astype(BF16)
    o_ref[...] = jnp.dot(c, w_ref[...].astype(BF16), preferred_element_type=F32) + b_ref[...]


def _ada(c_pad, w_ada, b_ada):
    rows = c_pad.shape[0]
    n = w_ada.shape[1]
    tn = 1536
    return pl.pallas_call(
        _ada_kernel,
        grid=(n // tn,),
        in_specs=[pl.BlockSpec((rows, D_MODEL), lambda j: (0, 0)),
                  pl.BlockSpec((D_MODEL, tn), lambda j: (0, j)),
                  pl.BlockSpec((1, tn), lambda j: (0, j))],
        out_specs=pl.BlockSpec((rows, tn), lambda j: (0, j)),
        out_shape=jax.ShapeDtypeStruct((rows, n), F32),
        compiler_params=_cparams(("arbitrary",)),
        name="ada",
    )(c_pad, w_ada, b_ada)


def _inproj_kernel(x_ref, sc_ref, sh_ref, w_ref, o_ref, h_ref):
    @pl.when(pl.program_id(1) == 0)
    def _():
        h_ref[...] = (x_ref[...] * (1.0 + sc_ref[0]) + sh_ref[0]).astype(BF16)

    o_ref[...] = jnp.dot(h_ref[...], w_ref[...], preferred_element_type=F32).astype(o_ref.dtype)


def _inproj(x, scale, shift, w, seg, tm, tn, out_dtype):
    t = x.shape[0]
    n = w.shape[1]
    per_seg = seg // tm
    return pl.pallas_call(
        _inproj_kernel,
        grid=(t // tm, n // tn),
        in_specs=[pl.BlockSpec((tm, D_MODEL), lambda i, j: (i, 0)),
                  pl.BlockSpec((1, 1, D_MODEL), lambda i, j: (i // per_seg, 0, 0)),
                  pl.BlockSpec((1, 1, D_MODEL), lambda i, j: (i // per_seg, 0, 0)),
                  pl.BlockSpec((D_MODEL, tn), lambda i, j: (0, j))],
        out_specs=pl.BlockSpec((tm, tn), lambda i, j: (i, j)),
        out_shape=jax.ShapeDtypeStruct((t, n), out_dtype),
        scratch_shapes=[pltpu.VMEM((tm, D_MODEL), BF16)],
        compiler_params=_cparams(("arbitrary", "arbitrary")),
        name="inproj",
    )(x, scale, shift, w)


def _conv_kernel(xp_ref, x_ref, xn_ref, w_ref, b_ref, o_ref, *, n_l):
    l = pl.program_id(1)
    tl = x_ref.shape[0]
    xp = jnp.where(l > 0, xp_ref[...], 0.0)
    xn = jnp.where(l < n_l - 1, xn_ref[...], 0.0)
    xx = jnp.concatenate([xp, x_ref[...], xn], axis=0)
    acc = b_ref[...] + w_ref[CONV_PAD:CONV_PAD + 1, :] * x_ref[...]
    for k in range(CONV_WIDTH):
        if k != CONV_PAD:
            shifted = pltpu.roll(xx, (CONV_PAD - k) % xx.shape[0], axis=0)
            acc = acc + w_ref[k:k + 1, :] * shifted[SUBLANES:SUBLANES + tl, :]
    o_ref[...] = _silu(acc)


def _conv(proj, conv_w, conv_b, row_off, batch, seq, tl=2048, tc=512):
    t = proj.shape[0]
    n_l = seq // tl
    n_c = CONV_DIM // tc
    c0 = COL_XBC // tc
    r8 = tl // SUBLANES
    last8 = t // SUBLANES - 1

    def row_blk(b, l):
        return (row_off + b * seq) // tl + l

    return pl.pallas_call(
        functools.partial(_conv_kernel, n_l=n_l),
        grid=(batch, n_l, n_c),
        in_specs=[pl.BlockSpec((SUBLANES, tc), lambda b, l, c: (jnp.maximum(row_blk(b, l) * r8 - 1, 0), c0 + c)),
                  pl.BlockSpec((tl, tc), lambda b, l, c: (row_blk(b, l), c0 + c)),
                  pl.BlockSpec((SUBLANES, tc), lambda b, l, c: (jnp.minimum((row_blk(b, l) + 1) * r8, last8), c0 + c)),
                  pl.BlockSpec((CONV_WIDTH, tc), lambda b, l, c: (0, c)),
                  pl.BlockSpec((1, tc), lambda b, l, c: (0, c))],
        out_specs=pl.BlockSpec((tl, tc), lambda b, l, c: (b * n_l + l, c)),
        out_shape=jax.ShapeDtypeStruct((batch * seq, CONV_DIM), F32),
        compiler_params=_cparams(("arbitrary", "arbitrary", "arbitrary")),
        name="conv",
    )(proj, proj, proj, conv_w, conv_b)


def _exact_tri_matmul(tri_bf16, x):
    hi = x.astype(BF16)
    r1 = x - hi.astype(F32)
    mid = r1.astype(BF16)
    lo = (r1 - mid.astype(F32)).astype(BF16)
    out = jnp.dot(tri_bf16, hi, preferred_element_type=F32)
    out = out + jnp.dot(tri_bf16, mid, preferred_element_type=F32)
    return out + jnp.dot(tri_bf16, lo, preferred_element_type=F32)


def _ssd_kernel(*refs, reverse, accumulate):
    if accumulate:
        x_ref, b_ref, c_ref, dt_ref, dtb_ref, alog_ref, yin_ref, y_ref, state_ref = refs
    else:
        x_ref, b_ref, c_ref, dt_ref, dtb_ref, alog_ref, y_ref, state_ref = refs
        yin_ref = None

    @pl.when(pl.program_id(1) == 0)
    def _():
        state_ref[...] = jnp.zeros(state_ref.shape, F32)

    for sub in (range(SSD_CHUNKS_PER_STEP - 1, -1, -1) if reverse else range(SSD_CHUNKS_PER_STEP)):
        _ssd_chunk(x_ref, b_ref, c_ref, dt_ref, dtb_ref, alog_ref, yin_ref, y_ref, state_ref,
                   slice(sub * SSD_CHUNK, (sub + 1) * SSD_CHUNK), reverse)


def _ssd_chunk(x_ref, b_ref, c_ref, dt_ref, dtb_ref, alog_ref, yin_ref, y_ref, state_ref, rows, reverse):
    lane0 = SSD_HEADS if reverse else 0
    accumulate = yin_ref is not None
    dt = _softplus(dt_ref[rows, :] + dtb_ref[...])
    da = dt * (-jnp.exp(alog_ref[...]))
    row = lax.broadcasted_iota(I32, (SSD_CHUNK, SSD_CHUNK), 0)
    col = lax.broadcasted_iota(I32, (SSD_CHUNK, SSD_CHUNK), 1)
    tri = (col >= row) if reverse else (col <= row)
    cum = _exact_tri_matmul(tri.astype(BF16), da)
    cum_t = cum.T
    dt_t = dt.T
    edge = 0 if reverse else SSD_CHUNK - 1
    total_b = jnp.broadcast_to(cum_t[:, edge:edge + 1], (LANES, SSD_CHUNK))
    to_end_t = dt_t * jnp.exp(total_b - cum_t)
    chunk_decay_b = jnp.exp(total_b)
    src_t = cum_t - jnp.log(dt_t)
    first_head = lax.broadcasted_iota(I32, (1, 2 * SSD_HEAD_DIM), 1) < SSD_HEAD_DIM
    pairs = HEADS_PER_SSD_GROUP // 2

    for g in range(SSD_GROUPS):
        bg = b_ref[rows, g * SSD_STATE:(g + 1) * SSD_STATE]
        cg = c_ref[rows, g * SSD_STATE:(g + 1) * SSD_STATE]
        cb = lax.dot_general(cg.astype(BF16), bg.astype(BF16), (((1,), (1,)), ((), ())),
                             preferred_element_type=F32)
        bg_t = bg.T
        for pr in range(pairs):
            h0 = g * HEADS_PER_SSD_GROUP + 2 * pr
            cols = slice(h0 * SSD_HEAD_DIM, (h0 + 2) * SSD_HEAD_DIM)
            x_pair = x_ref[rows, cols].astype(BF16)
            state = state_ref[g * pairs + pr]
            rhs = jnp.concatenate([x_pair, state.astype(BF16)], axis=0)
            res = []
            new = []
            for e in range(2):
                ln = lane0 + h0 + e
                cum_l = jnp.broadcast_to(cum[:, ln:ln + 1], (SSD_CHUNK, SSD_CHUNK))
                decay = jnp.exp(jnp.where(tri, cum_l - src_t[ln:ln + 1, :], NEG_INF))
                lhs = jnp.concatenate([(cb * decay).astype(BF16), (cg * jnp.exp(cum_l)).astype(BF16)], axis=1)
                res.append(jnp.dot(lhs, rhs, preferred_element_type=F32))
                b_scaled = (bg_t * to_end_t[ln:ln + 1, :]).astype(BF16)
                new.append(jnp.dot(b_scaled, x_pair, preferred_element_type=F32))
            y_pair = jnp.where(first_head, res[0], res[1])
            if accumulate:
                y_pair = y_pair + yin_ref[rows, cols]
            y_ref[rows, cols] = y_pair
            ln0 = lane0 + h0
            dec = jnp.where(first_head, chunk_decay_b[ln0:ln0 + 1, :], chunk_decay_b[ln0 + 1:ln0 + 2, :])
            state_ref[g * pairs + pr] = state * dec + jnp.where(first_head, new[0], new[1])


def _ssd(xbc, dt_raw, dt_bias, a_log, y_in, row_off, batch, seq, reverse):
    rows = SSD_CHUNK * SSD_CHUNKS_PER_STEP
    n_c = seq // rows
    accumulate = y_in is not None

    def loc(b, c):
        return b * n_c + (n_c - 1 - c if reverse else c)

    def glob(b, c):
        return row_off // rows + loc(b, c)

    in_specs = [pl.BlockSpec((rows, D_INNER), lambda b, c: (loc(b, c), 0)),
                pl.BlockSpec((rows, SSD_GROUPS * SSD_STATE), lambda b, c: (loc(b, c), 2)),
                pl.BlockSpec((rows, SSD_GROUPS * SSD_STATE), lambda b, c: (loc(b, c), 3)),
                pl.BlockSpec((rows, DT_COLS), lambda b, c: (glob(b, c), 0)),
                pl.BlockSpec((1, DT_COLS), lambda b, c: (0, 0)),
                pl.BlockSpec((1, DT_COLS), lambda b, c: (0, 0))]
    args = [xbc, xbc, xbc, dt_raw, dt_bias, a_log]
    aliases = {}
    if accumulate:
        in_specs.append(pl.BlockSpec((rows, D_INNER), lambda b, c: (loc(b, c), 0)))
        args.append(y_in)
        aliases = {len(args) - 1: 0}
    return pl.pallas_call(
        functools.partial(_ssd_kernel, reverse=reverse, accumulate=accumulate),
        grid=(batch, n_c),
        in_specs=in_specs,
        out_specs=pl.BlockSpec((rows, D_INNER), lambda b, c: (loc(b, c), 0)),
        out_shape=jax.ShapeDtypeStruct((batch * seq, D_INNER), F32),
        scratch_shapes=[pltpu.VMEM((SSD_HEADS // 2, SSD_STATE, 2 * SSD_HEAD_DIM), F32)],
        input_output_aliases=aliases,
        compiler_params=_cparams(("arbitrary", "arbitrary")),
        name="ssd_bwd" if reverse else "ssd_fwd",
    )(*args)


def _ssd_out_kernel(y_ref, x_ref, z_ref, dskip_ref, nw_ref, w_ref, o_ref):
    y = y_ref[...] + dskip_ref[...] * x_ref[...]
    y = y * _silu(z_ref[...])
    ms = jnp.mean(y * y, axis=-1, keepdims=True)
    yn = y * lax.rsqrt(ms + NORM_EPS) * nw_ref[...]
    o_ref[...] = jnp.dot(yn.astype(BF16), w_ref[...], preferred_element_type=F32)


def _ssd_out(y, xbc, proj, d_skip_cols, norm_w, w_branch, row_off, tm=512):
    t = y.shape[0]
    return pl.pallas_call(
        _ssd_out_kernel,
        grid=(t // tm,),
        in_specs=[pl.BlockSpec((tm, D_INNER), lambda i: (i, 0)),
                  pl.BlockSpec((tm, D_INNER), lambda i: (i, 0)),
                  pl.BlockSpec((tm, D_INNER), lambda i: (row_off // tm + i, COL_Z // D_INNER)),
                  pl.BlockSpec((1, D_INNER), lambda i: (0, 0)),
                  pl.BlockSpec((1, D_INNER), lambda i: (0, 0)),
                  pl.BlockSpec((D_INNER, D_MODEL), lambda i: (0, 0))],
        out_specs=pl.BlockSpec((tm, D_MODEL), lambda i: (i, 0)),
        out_shape=jax.ShapeDtypeStruct((t, D_MODEL), F32),
        compiler_params=_cparams(("arbitrary",)),
        name="ssd_out",
    )(y, xbc, proj, d_skip_cols, norm_w, w_branch)


def _prep_kernel(q_ref, k_ref, v_ref, cos_ref, sa_ref, sb_ref, qo_ref, ko_ref, vo_ref, *, dil):
    rows = q_ref.shape[0] // dil

    def phase(ref, p):
        if dil == 1:
            return ref[...]
        return ref[pl.ds(p, rows, stride=dil), :]

    for p in range(dil):
        cos = phase(cos_ref, p)
        sa = phase(sa_ref, p)
        sb = phase(sb_ref, p)

        def rot(t):
            return (t * cos + pltpu.roll(t, LANES - ROT_HALF, axis=1) * sa
                    + pltpu.roll(t, ROT_HALF, axis=1) * sb)

        qo_ref[0, p] = (rot(phase(q_ref, p)) * (ATTN_HEAD_DIM ** -0.5)).astype(qo_ref.dtype)
        ko_ref[0, p] = rot(phase(k_ref, p)).astype(ko_ref.dtype)
        vo_ref[0, p] = phase(v_ref, p).astype(vo_ref.dtype)


def _prep(proj, tables, gi, dil, row_off, batch, seq, rows_per_step=2048):
    r = rows_per_step
    n_i = seq // r
    n_c = ATTN_GROUP_COLS // LANES
    m = seq // dil
    cos_t, sa_t, sb_t = tables

    def in_blk(col):
        cb = (col + gi * ATTN_GROUP_COLS) // LANES
        return pl.BlockSpec((r, LANES), lambda b, i, c: ((row_off + b * seq) // r + i, cb + c))

    tab = pl.BlockSpec((r, LANES), lambda b, i, c: (i, 0))
    out = pl.BlockSpec((1, dil, r // dil, LANES), lambda b, i, c: (b, 0, i, c))
    shape = jax.ShapeDtypeStruct((batch, dil, m, ATTN_GROUP_COLS), BF16)
    return pl.pallas_call(
        functools.partial(_prep_kernel, dil=dil),
        grid=(batch, n_i, n_c),
        in_specs=[in_blk(COL_Q), in_blk(COL_K), in_blk(COL_V), tab, tab, tab],
        out_specs=[out, out, out],
        out_shape=[shape, shape, shape],
        compiler_params=_cparams(("arbitrary", "arbitrary", "arbitrary")),
        name=f"attn_prep_d{dil}",
    )(proj, proj, proj, cos_t, sa_t, sb_t)


def _band_attn_kernel(q_ref, kp_ref, kc_ref, kn_ref, vp_ref, vc_ref, vn_ref, o_ref, lse_ref, *, m_len):
    i = pl.program_id(1)
    tq = ATTN_TQ
    n_sub = q_ref.shape[1] // tq
    tk = tq + 2 * BAND_RADIUS
    q_all = q_ref[0].astype(BF16)
    k_all = jnp.concatenate([kp_ref[0], kc_ref[0], kn_ref[0]], axis=0).astype(BF16)
    v_all = jnp.concatenate([vp_ref[0], vc_ref[0], vn_ref[0]], axis=0).astype(BF16)
    first_head = lax.broadcasted_iota(I32, (1, LANES), 1) < ATTN_HEAD_DIM
    ones = jnp.ones((tk, LANES), BF16)
    zero = jnp.zeros((), BF16)
    for sub in range(n_sub):
        start = (i * n_sub + sub) * tq
        rows = slice(sub * tq, (sub + 1) * tq)
        q = q_all[rows]
        k = k_all[sub * tq:sub * tq + tk]
        v = v_all[sub * tq:sub * tq + tk]
        qpos = start + lax.broadcasted_iota(I32, (tq, tk), 0)
        kpos = start - BAND_RADIUS + lax.broadcasted_iota(I32, (tq, tk), 1)
        valid = (jnp.abs(qpos - kpos) <= BAND_RADIUS) & (kpos >= 0) & (kpos < m_len)
        for pr in range(ATTN_GROUP_COLS // LANES):
            cols = slice(pr * LANES, (pr + 1) * LANES)
            q_pair, k_pair = q[:, cols], k[:, cols]
            v_ext = jnp.concatenate([v[:, cols], ones], axis=1)
            outs, lses = [], []
            for sel in (first_head, jnp.logical_not(first_head)):
                s = lax.dot_general(jnp.where(sel, q_pair, zero), k_pair, (((1,), (1,)), ((), ())),
                                    preferred_element_type=F32)
                s = jnp.where(valid, s, NEG_INF)
                mx = jnp.max(s, axis=-1, keepdims=True)
                p = jnp.exp(s - mx).astype(BF16)
                ov = jnp.dot(p, v_ext, preferred_element_type=F32)
                den = ov[:, LANES:]
                outs.append(ov[:, :LANES] / den)
                lses.append(mx + jnp.log(den))
            o_ref[0, rows, cols] = jnp.where(first_head, outs[0], outs[1])
            lse_ref[0, rows, cols] = jnp.where(first_head, lses[0], lses[1])


def _band_attn(q, k, v):
    b, d, m, c = q.shape
    n = b * d
    q, k, v = (a.reshape(n, m, c) for a in (q, k, v))
    tq = ATTN_TQ * (2 if m % (2 * ATTN_TQ) == 0 else 1)
    per = tq // BAND_RADIUS
    last = m // BAND_RADIUS - 1
    main = pl.BlockSpec((1, tq, c), lambda s, i: (s, i, 0))
    prev = pl.BlockSpec((1, BAND_RADIUS, c), lambda s, i: (s, jnp.maximum(i * per - 1, 0), 0))
    nxt = pl.BlockSpec((1, BAND_RADIUS, c), lambda s, i: (s, jnp.minimum((i + 1) * per, last), 0))
    shape = jax.ShapeDtypeStruct((n, m, c), F32)
    o, lse = pl.pallas_call(
        functools.partial(_band_attn_kernel, m_len=m),
        grid=(n, m // tq),
        in_specs=[main, prev, main, nxt, prev, main, nxt],
        out_specs=[main, main],
        out_shape=[shape, shape],
        compiler_params=_cparams(("arbitrary", "arbitrary")),
        name=f"band_attn_m{m}",
    )(q, k, k, k, v, v, v)
    return o.reshape(b, d, m, c), lse.reshape(b, d, m, c)


def _layer_norm(x, g, b):
    mu = jnp.mean(x, axis=-1, keepdims=True)
    xc = x - mu
    var = jnp.mean(xc * xc, axis=-1, keepdims=True)
    return xc * lax.rsqrt(var + NORM_EPS) * g + b


def _merge_kernel(x_ref, yssd_ref, gs_ref, ga_ref, o1_ref, l1_ref, o2_ref, l2_ref, o3_ref, l3_ref,
                  g1_ref, sc2_ref, sh2_ref, wba_ref, wout_ref, lng_ref, lnb_ref, wr_ref, *rest, n_prev):
    x1_ref, h2_ref, logit_ref, scr_ref = rest[n_prev:]
    tm = x_ref.shape[0]

    def interleaved(ref, dil, slot):
        if dil == 1:
            return ref[0, 0]
        rows = tm // dil
        n_c = ATTN_GROUP_COLS // LANES
        for p in range(dil):
            for c in range(n_c):
                scr_ref[slot * n_c + c, pl.ds(p, rows, stride=dil), :] = ref[0, p, :, c * LANES:(c + 1) * LANES]
        return jnp.concatenate([scr_ref[slot * n_c + c] for c in range(n_c)], axis=1)

    outs = []
    lses = []
    slot = 0
    for (o_ref, l_ref), (_, dil) in zip(((o1_ref, l1_ref), (o2_ref, l2_ref), (o3_ref, l3_ref)), ATTN_PATTERNS):
        outs.append(interleaved(o_ref, dil, slot))
        lses.append(interleaved(l_ref, dil, slot + 1))
        slot += 2
    mx = jnp.maximum(jnp.maximum(lses[0], lses[1]), lses[2])
    es = [jnp.exp(l - mx) for l in lses]
    den = es[0] + es[1] + es[2]
    y_att = (es[0] * outs[0] + es[1] * outs[1] + es[2] * outs[2]) / den
    y_attn = jnp.dot(y_att.astype(BF16), wba_ref[...], preferred_element_type=F32)
    merged = _sigmoid(gs_ref[...]) * yssd_ref[...] + _sigmoid(ga_ref[...]) * y_attn
    mix = jnp.dot(merged.astype(BF16), wout_ref[...], preferred_element_type=F32)
    x1 = _layer_norm(ALPHA * x_ref[...] + g1_ref[0] * mix, lng_ref[...], lnb_ref[...])
    x1_ref[...] = x1
    h2 = x1 * (1.0 + sc2_ref[0]) + sh2_ref[0]
    _store_rows(h2_ref, (), h2)
    logit_ref[...] = lax.dot_general(wr_ref[...], h2.astype(BF16), (((1,), (1,)), ((), ())),
                                     preferred_element_type=F32)


def _merge(x, y_ssd, proj, attn, gate1, scale2, shift2, w_ba, w_out, ln_g, ln_b, w_router_t,
           batch, seq, out_row_off, t_all, prev, tm=512):
    n_i = seq // tm
    out0 = out_row_off // tm

    def rows(col_blk):
        return pl.BlockSpec((tm, D_MODEL), lambda b, i: (b * n_i + i, col_blk))

    def local(width):
        return pl.BlockSpec((tm, width), lambda b, i: (b * n_i + i, 0))

    def modv():
        return pl.BlockSpec((1, 1, D_MODEL), lambda b, i: (b, 0, 0))

    def full(shape):
        return pl.BlockSpec(shape, lambda b, i: tuple(0 for _ in shape))

    attn_specs = []
    attn_args = []
    for (o, lse), (_, dil) in zip(attn, ATTN_PATTERNS):
        spec = pl.BlockSpec((1, dil, tm // dil, ATTN_GROUP_COLS), lambda b, i: (b, 0, i, 0))
        attn_specs += [spec, spec]
        attn_args += [o, lse]
    n_in = 4 + len(attn_args) + 8
    return pl.pallas_call(
        functools.partial(_merge_kernel, n_prev=0 if prev is None else 3),
        grid=(batch, n_i),
        in_specs=[rows(0), local(D_MODEL), rows(COL_GSSD // D_MODEL), rows(COL_GATTN // D_MODEL)] + attn_specs
                 + [modv(), modv(), modv(), full((ATTN_GROUP_COLS, D_MODEL)), full((D_MODEL, D_MODEL)),
                    full((1, D_MODEL)), full((1, D_MODEL)), full((N_EXPERTS, D_MODEL))]
                 + ([pl.BlockSpec(memory_space=pl.ANY)] * 3 if prev is not None else []),
        out_specs=[pl.BlockSpec((tm, D_MODEL), lambda b, i: (out0 + b * n_i + i, 0)),
                   pl.BlockSpec((tm * ROW_SUBLANES, LANES), lambda b, i: (out0 + b * n_i + i, 0)),
                   pl.BlockSpec((N_EXPERTS, tm), lambda b, i: (0, out0 + b * n_i + i))],
        out_shape=[jax.ShapeDtypeStruct((t_all, D_MODEL), F32),
                   jax.ShapeDtypeStruct((t_all * ROW_SUBLANES, LANES), jnp.uint32),
                   jax.ShapeDtypeStruct((N_EXPERTS, t_all), F32)],
        scratch_shapes=[pltpu.VMEM((2 * ATTN_GROUPS * (ATTN_GROUP_COLS // LANES), tm, LANES), F32)],
        input_output_aliases={n_in + j: j for j in range(3)} if prev is not None else {},
        compiler_params=_cparams(("arbitrary", "arbitrary")),
        name="mixer_merge",
    )(x, y_ssd, proj, proj, *attn_args, gate1, scale2, shift2, w_ba, w_out, ln_g, ln_b, w_router_t,
      *(prev if prev is not None else ()))


def _route_kernel(logit_ref, bias_ref, idx_ref, w_ref, rank_ref, cnt_ref, run_ref):
    tm = logit_ref.shape[1]

    @pl.when(pl.program_id(0) == 0)
    def _():
        run_ref[...] = jnp.zeros(run_ref.shape, F32)

    scores = _sigmoid(logit_ref[...])
    sel = scores + bias_ref[...]
    iota_g = lax.broadcasted_iota(I32, (EXPERTS_PER_GROUP, tm), 0)
    grp = []
    for g in range(N_EXPERT_GROUPS):
        sg = sel[g * EXPERTS_PER_GROUP:(g + 1) * EXPERTS_PER_GROUP, :]
        top1 = jnp.max(sg, axis=0, keepdims=True)
        first = jnp.min(jnp.where(sg == top1, iota_g, EXPERTS_PER_GROUP), axis=0, keepdims=True)
        top2 = jnp.max(jnp.where(iota_g == first, NEG_INF, sg), axis=0, keepdims=True)
        grp.append(top1 + top2)
    gs = jnp.concatenate(grp, axis=0)
    iota_n = lax.broadcasted_iota(I32, (N_EXPERT_GROUPS, tm), 0)
    keep_g = jnp.zeros((N_EXPERT_GROUPS, tm), F32)
    for _ in range(TOPK_GROUPS):
        best = jnp.max(gs, axis=0, keepdims=True)
        first = jnp.min(jnp.where(gs == best, iota_n, N_EXPERT_GROUPS), axis=0, keepdims=True)
        hit = iota_n == first
        keep_g = jnp.where(hit, 1.0, keep_g)
        gs = jnp.where(hit, NEG_INF, gs)
    cand = jnp.concatenate(
        [jnp.where(keep_g[g:g + 1, :] > 0.0, sel[g * EXPERTS_PER_GROUP:(g + 1) * EXPERTS_PER_GROUP, :], NEG_INF)
         for g in range(N_EXPERT_GROUPS)], axis=0)
    iota_e = lax.broadcasted_iota(I32, (N_EXPERTS, tm), 0)
    chosen = jnp.zeros((N_EXPERTS, tm), F32)
    idxs = []
    ws = []
    for _ in range(TOP_K):
        best = jnp.max(cand, axis=0, keepdims=True)
        first = jnp.min(jnp.where(cand == best, iota_e, N_EXPERTS), axis=0, keepdims=True)
        hit = iota_e == first
        idxs.append(first)
        ws.append(jnp.sum(jnp.where(hit, scores, 0.0), axis=0, keepdims=True))
        chosen = jnp.where(hit, 1.0, chosen)
        cand = jnp.where(hit, NEG_INF, cand)
    top_w = jnp.concatenate(ws, axis=0)
    top_w = top_w / jnp.sum(top_w, axis=0, keepdims=True) * ROUTED_SCALE
    idx_ref[...] = jnp.concatenate(idxs, axis=0)
    w_ref[...] = top_w
    s_i = lax.broadcasted_iota(I32, (tm, tm), 0)
    t_i = lax.broadcasted_iota(I32, (tm, tm), 1)
    before = (s_i < t_i).astype(BF16)
    prior = jnp.dot(chosen.astype(BF16), before, preferred_element_type=F32) + run_ref[...]
    ranks = [jnp.sum(jnp.where(iota_e == idxs[k], prior, 0.0), axis=0, keepdims=True) for k in range(TOP_K)]
    rank_ref[...] = jnp.concatenate(ranks, axis=0).astype(I32)
    run_ref[...] = run_ref[...] + jnp.sum(chosen, axis=1, keepdims=True)
    cnt_ref[...] = run_ref[...].astype(I32)


def _route(logits_t, bias_col, tm=512):
    t = logits_t.shape[1]
    tok = pl.BlockSpec((TOP_K, tm), lambda i: (0, i))
    return pl.pallas_call(
        _route_kernel,
        grid=(t // tm,),
        in_specs=[pl.BlockSpec((N_EXPERTS, tm), lambda i: (0, i)),
                  pl.BlockSpec((N_EXPERTS, 1), lambda i: (0, 0))],
        out_specs=[tok, tok, tok, pl.BlockSpec((N_EXPERTS, 1), lambda i: (0, 0))],
        out_shape=[jax.ShapeDtypeStruct((TOP_K, t), I32), jax.ShapeDtypeStruct((TOP_K, t), F32),
                   jax.ShapeDtypeStruct((TOP_K, t), I32), jax.ShapeDtypeStruct((N_EXPERTS, 1), I32)],
        scratch_shapes=[pltpu.VMEM((N_EXPERTS, 1), F32)],
        compiler_params=_cparams(("arbitrary",)),
        name="route",
    )(logits_t, bias_col)


def _dest_kernel(idx_ref, rank_ref, start_ref, o_ref):
    tm = idx_ref.shape[1]
    iota_e = lax.broadcasted_iota(I32, (N_EXPERTS, tm), 0)
    start = start_ref[...].astype(F32)
    for k in range(TOP_K):
        base = jnp.sum(jnp.where(iota_e == idx_ref[k:k + 1, :], start, 0.0), axis=0, keepdims=True)
        o_ref[0, :, k * tm:(k + 1) * tm] = (base.astype(I32) + rank_ref[k:k + 1, :]) * ROW_SUBLANES


def _dest(top_idx, rank, pad_start_col, tm):
    t = top_idx.shape[1]
    tok = pl.BlockSpec((TOP_K, tm), lambda i: (0, i))
    out = pl.pallas_call(
        _dest_kernel,
        grid=(t // tm,),
        in_specs=[tok, tok, pl.BlockSpec((N_EXPERTS, 1), lambda i: (0, 0))],
        out_specs=pl.BlockSpec((1, 1, TOP_K * tm), lambda i: (i, 0, 0)),
        out_shape=jax.ShapeDtypeStruct((t // tm, 1, TOP_K * tm), I32),
        compiler_params=_cparams(("arbitrary",)),
        name="moe_dest",
    )(top_idx, rank, pad_start_col)
    return out.reshape(t // tm, TOP_K * tm)


def _dispatch_kernel(pad_end_ref, dest_hbm, h_ref, xs_hbm, *scratch, tm, n_sub):
    dest_smem = scratch[:n_sub]
    zeros_ref, sem_idx, sem_zero, sem_row = scratch[n_sub:]
    i = pl.program_id(0)
    idx_copies = [pltpu.make_async_copy(dest_hbm.at[i * n_sub + j], dest_smem[j], sem_idx) for j in range(n_sub)]
    for c in idx_copies:
        c.start()

    blk_rows = EXPERT_BLOCK * ROW_SUBLANES

    def zero_copy(e):
        start = pl.multiple_of(jnp.maximum(pad_end_ref[e] - EXPERT_BLOCK, 0) * ROW_SUBLANES, blk_rows)
        return pltpu.make_async_copy(zeros_ref, xs_hbm.at[pl.ds(start, blk_rows)], sem_zero)

    @pl.when(i == 0)
    def _():
        zeros_ref[...] = jnp.zeros(zeros_ref.shape, zeros_ref.dtype)

        def start_zero(e, carry):
            zero_copy(e).start()
            return carry

        def wait_zero(e, carry):
            zero_copy(e).wait()
            return carry

        lax.fori_loop(0, N_EXPERTS, start_zero, 0)
        lax.fori_loop(0, N_EXPERTS, wait_zero, 0)

    for c in idx_copies:
        c.wait()

    def row_copy(j, t, k):
        dst = pl.multiple_of(dest_smem[j][k * tm + t], ROW_SUBLANES)
        src = pl.multiple_of((j * tm + t) * ROW_SUBLANES, ROW_SUBLANES)
        return pltpu.make_async_copy(h_ref.at[pl.ds(src, ROW_SUBLANES)],
                                     xs_hbm.at[pl.ds(dst, ROW_SUBLANES)], sem_row)

    for j in range(n_sub):
        def start_rows(t, carry, j=j):
            for k in range(TOP_K):
                row_copy(j, t, k).start(priority=k % DMA_PRIORITIES)
            return carry

        lax.fori_loop(0, tm, start_rows, 0)

    for j in range(n_sub):
        def wait_rows(t, carry, j=j):
            for k in range(TOP_K):
                row_copy(j, t, k).wait()
            return carry

        lax.fori_loop(0, tm, wait_rows, 0)


def _dispatch(pad_end, dest_tiles, h2, n_rows, tm, n_sub=4):
    n_tiles = dest_tiles.shape[0]
    assert n_tiles % n_sub == 0
    return pl.pallas_call(
        functools.partial(_dispatch_kernel, tm=tm, n_sub=n_sub),
        grid_spec=pltpu.PrefetchScalarGridSpec(
            num_scalar_prefetch=1,
            grid=(n_tiles // n_sub,),
            in_specs=[pl.BlockSpec(memory_space=pl.ANY),
                      pl.BlockSpec((n_sub * tm * ROW_SUBLANES, LANES), lambda i, pe: (i, 0))],
            out_specs=pl.BlockSpec(memory_space=pl.ANY),
            scratch_shapes=[pltpu.SMEM((TOP_K * tm,), I32) for _ in range(n_sub)] + [
                            pltpu.VMEM((EXPERT_BLOCK * ROW_SUBLANES, LANES), jnp.uint32),
                            pltpu.SemaphoreType.DMA, pltpu.SemaphoreType.DMA, pltpu.SemaphoreType.DMA]),
        out_shape=jax.ShapeDtypeStruct((n_rows * ROW_SUBLANES, LANES), jnp.uint32),
        compiler_params=_cparams(("arbitrary",)),
        name="moe_dispatch",
    )(pad_end, dest_tiles, h2)


def _experts_kernel(blk_exp_ref, n_valid_ref, xs_ref, wg_ref, wu_ref, wd_ref, ys_ref, wgu_ref, wdn_ref):
    i = pl.program_id(0)

    @pl.when(i < n_valid_ref[0])
    def _():
        @pl.when((i == 0) | (blk_exp_ref[i] != blk_exp_ref[jnp.maximum(i - 1, 0)]))
        def _():
            wgu_ref[:, :EXPERT_HIDDEN] = wg_ref[0].astype(BF16)
            wgu_ref[:, EXPERT_HIDDEN:] = wu_ref[0].astype(BF16)
            wdn_ref[...] = wd_ref[0].astype(BF16)

        x = jnp.concatenate([c.astype(BF16) for c in _load_row_pieces(xs_ref, (), EXPERT_BLOCK)], axis=1)
        gu = jnp.dot(x, wgu_ref[...], preferred_element_type=F32)
        hmid = (_silu(gu[:, :EXPERT_HIDDEN]) * gu[:, EXPERT_HIDDEN:]).astype(BF16)
        _store_rows(ys_ref, (), jnp.dot(hmid, wdn_ref[...], preferred_element_type=F32))


def _experts(blk_exp, n_valid, xs, w_gate, w_up, w_down):
    n_blocks = xs.shape[0] // (EXPERT_BLOCK * ROW_SUBLANES)

    def row_map(i, be, nv):
        return (jnp.minimum(i, nv[0] - 1), 0)

    def w_map(i, be, nv):
        return (be[i], 0, 0)

    return pl.pallas_call(
        _experts_kernel,
        grid_spec=pltpu.PrefetchScalarGridSpec(
            num_scalar_prefetch=2,
            grid=(n_blocks,),
            in_specs=[pl.BlockSpec((EXPERT_BLOCK * ROW_SUBLANES, LANES), row_map),
                      pl.BlockSpec((1, D_MODEL, EXPERT_HIDDEN), w_map),
                      pl.BlockSpec((1, D_MODEL, EXPERT_HIDDEN), w_map),
                      pl.BlockSpec((1, EXPERT_HIDDEN, D_MODEL), w_map)],
            out_specs=pl.BlockSpec((EXPERT_BLOCK * ROW_SUBLANES, LANES), row_map),
            scratch_shapes=[pltpu.VMEM((D_MODEL, 2 * EXPERT_HIDDEN), BF16),
                            pltpu.VMEM((EXPERT_HIDDEN, D_MODEL), BF16)]),
        out_shape=jax.ShapeDtypeStruct(xs.shape, jnp.uint32),
        compiler_params=_cparams(("arbitrary",)),
        name="moe_experts",
    )(blk_exp, n_valid, xs, w_gate, w_up, w_down)


def _combine_kernel(dest_hbm, ys_hbm, w_ref, x1_ref, h2_ref, g2_ref, wsg_ref, wsu_ref, wsd_ref,
                    lng_ref, lnb_ref, o_ref, dest0, dest1, rows0, rows1, sem_idx, sem0, sem1, *, tm, tile_off):
    i = pl.program_id(0)
    n_i = pl.num_programs(0)
    dest_smem, rows_ref, sem_row = (dest0, dest1), (rows0, rows1), (sem0, sem1)

    def row_copy(s, t, k):
        src = pl.multiple_of(dest_smem[s][k * tm + t], ROW_SUBLANES)
        dst = pl.multiple_of(t * ROW_SUBLANES, ROW_SUBLANES)
        return pltpu.make_async_copy(ys_hbm.at[pl.ds(src, ROW_SUBLANES)],
                                     rows_ref[s].at[k, pl.ds(dst, ROW_SUBLANES)], sem_row[s])

    def start_gather(tile, s):
        idx_copy = pltpu.make_async_copy(dest_hbm.at[tile + tile_off], dest_smem[s], sem_idx)
        idx_copy.start()
        idx_copy.wait()

        def start_rows(t, carry):
            for k in range(TOP_K):
                row_copy(s, t, k).start(priority=k % DMA_PRIORITIES)
            return carry

        lax.fori_loop(0, tm, start_rows, 0)

    @pl.when(i == 0)
    def _():
        start_gather(0, 0)

    for s in range(2):
        if s == 0:
            start_gather(2 * i + 1, 1)
        else:
            @pl.when(i + 1 < n_i)
            def _():
                start_gather(2 * i + 2, 0)

        rows = slice(s * tm, (s + 1) * tm)
        h2 = jnp.concatenate(
            [c.astype(BF16) for c in _load_row_pieces(h2_ref, (), tm, s * tm * ROW_SUBLANES)], axis=1)
        g = jnp.dot(h2, wsg_ref[...], preferred_element_type=F32)
        u = jnp.dot(h2, wsu_ref[...], preferred_element_type=F32)
        ffn = jnp.dot((_silu(g) * u).astype(BF16), wsd_ref[...], preferred_element_type=F32)

        def wait_rows(t, carry):
            for k in range(TOP_K):
                row_copy(s, t, k).wait()
            return carry

        lax.fori_loop(0, tm, wait_rows, 0)
        w = w_ref[rows, :]
        n_pieces = D_MODEL // LANES
        acc = [ffn[:, j * LANES:(j + 1) * LANES] for j in range(n_pieces)]
        for k in range(TOP_K):
            wk = jnp.broadcast_to(w[:, k:k + 1], (tm, LANES))
            pieces = _load_row_pieces(rows_ref[s], (k,), tm)
            acc = [a + wk * p for a, p in zip(acc, pieces)]
        ffn = jnp.concatenate(acc, axis=1)
        o_ref[rows, :] = _layer_norm(ALPHA * x1_ref[rows, :] + g2_ref[0] * ffn, lng_ref[...], lnb_ref[...])


def _combine(dest_tiles, ys, top_w_rows, x1, h2, gate2, w_sg, w_su, w_sd, ln_g, ln_b, seg, tm, tile_off, n_tiles):
    assert tile_off % 2 == 0 and n_tiles % 2 == 0 and seg % (2 * tm) == 0
    per_seg = seg // (2 * tm)
    off = tile_off // 2

    def full(shape):
        return pl.BlockSpec(shape, lambda i: tuple(0 for _ in shape))

    return pl.pallas_call(
        functools.partial(_combine_kernel, tm=tm, tile_off=tile_off),
        grid=(n_tiles // 2,),
        in_specs=[pl.BlockSpec(memory_space=pl.ANY), pl.BlockSpec(memory_space=pl.ANY),
                  pl.BlockSpec((2 * tm, TOP_K), lambda i: (i + off, 0)),
                  pl.BlockSpec((2 * tm, D_MODEL), lambda i: (i + off, 0)),
                  pl.BlockSpec((2 * tm * ROW_SUBLANES, LANES), lambda i: (i + off, 0)),
                  pl.BlockSpec((1, 1, D_MODEL), lambda i: ((i + off) // per_seg, 0, 0)),
                  full((D_MODEL, SHARED_HIDDEN)), full((D_MODEL, SHARED_HIDDEN)), full((SHARED_HIDDEN, D_MODEL)),
                  full((1, D_MODEL)), full((1, D_MODEL))],
        out_specs=pl.BlockSpec((2 * tm, D_MODEL), lambda i: (i, 0)),
        out_shape=jax.ShapeDtypeStruct((n_tiles * tm, D_MODEL), F32),
        scratch_shapes=[pltpu.SMEM((TOP_K * tm,), I32), pltpu.SMEM((TOP_K * tm,), I32),
                        pltpu.VMEM((TOP_K, tm * ROW_SUBLANES, LANES), jnp.uint32),
                        pltpu.VMEM((TOP_K, tm * ROW_SUBLANES, LANES), jnp.uint32),
                        pltpu.SemaphoreType.DMA, pltpu.SemaphoreType.DMA, pltpu.SemaphoreType.DMA],
        compiler_params=_cparams(("arbitrary",)),
        name="moe_combine",
    )(dest_tiles, ys, top_w_rows, x1, h2, gate2, w_sg, w_su, w_sd, ln_g, ln_b)


def _rope_tables(seq):
    inv_freq = ROPE_THETA ** (-jnp.arange(ROT_HALF, dtype=F32) * 2.0 / ROT_DIM)
    ang = jnp.arange(seq, dtype=F32)[:, None] * inv_freq[None, :]
    cos, sin = jnp.cos(ang), jnp.sin(ang)
    rest = ATTN_HEAD_DIM - ROT_DIM
    ones = jnp.ones((seq, rest), F32)
    zeros = jnp.zeros((seq, rest), F32)
    zh = jnp.zeros((seq, ROT_HALF), F32)
    reps = LANES // ATTN_HEAD_DIM
    cos_t = jnp.tile(jnp.concatenate([cos, cos, ones], axis=1), (1, reps))
    sa_t = jnp.tile(jnp.concatenate([-sin, zh, zeros], axis=1), (1, reps))
    sb_t = jnp.tile(jnp.concatenate([zh, sin, zeros], axis=1), (1, reps))
    return cos_t, sa_t, sb_t


def _split_in_weights(w_in):
    cuts = (D_INNER, CONV_DIM, SSD_HEADS, SSD_HEADS, ATTN_DIM, ATTN_DIM, ATTN_DIM, D_MODEL, D_MODEL)
    offs = [0]
    for c in cuts:
        offs.append(offs[-1] + c)
    z, xbc, dtf, dtb, q, k, v, gs, ga = (w_in[:, offs[i]:offs[i + 1]] for i in range(len(cuts)))
    w_main = jnp.concatenate([z, xbc, gs, ga, q, k, v], axis=1).astype(BF16)
    assert COL_GSSD % D_MODEL == 0 and COL_Q % ATTN_GROUP_COLS == 0 and w_main.shape[1] == N_MAIN
    pad = jnp.zeros((D_MODEL, DT_COLS - 2 * SSD_HEADS), F32)
    w_dt = jnp.concatenate([dtf, dtb, pad], axis=1).astype(BF16)
    return w_main, w_dt


def kernel(x_prompt, x_sample, c_prompt, c_sample, w_ada, b_ada, w_in, conv_w, conv_b, dt_bias_fwd, dt_bias_bwd, a_log_fwd, a_log_bwd, d_skip, ssd_norm_w, w_branch_ssd, w_branch_attn, w_out, ln1_g, ln1_b, w_router, router_bias, w_exp_gate, w_exp_up, w_exp_down, w_sh_gate, w_sh_up, w_sh_down, ln2_g, ln2_b):
    assert w_ada.shape[0] == DEPTH
    groups = ((x_prompt, c_prompt), (x_sample, c_sample))
    shapes = [(x.shape[0], x.shape[1]) for x, _ in groups]
    seg = math.gcd(*(s for _, s in shapes))
    tokens = [b * s for b, s in shapes]
    t_all = sum(tokens)
    row_offs = [0, tokens[0]]

    c_all = jnp.concatenate([c_prompt, c_sample], axis=0)
    n_req = c_all.shape[0]
    c_pad = jnp.zeros((-(-n_req // SUBLANES) * SUBLANES, D_MODEL), F32).at[:n_req].set(c_all)
    ada = _ada(c_pad, w_ada[0], b_ada)[:n_req]
    req_offs = [0, shapes[0][0]]
    reps = jnp.array([s // seg for b, s in shapes for _ in range(b)], dtype=I32)
    n_seg = t_all // seg
    gate2 = jnp.repeat(ada[:, 5 * D_MODEL:], reps, axis=0, total_repeat_length=n_seg).reshape(n_seg, 1, D_MODEL)

    w_main, w_dt = _split_in_weights(w_in[0])
    dt_pad = jnp.zeros((DT_COLS - 2 * SSD_HEADS,), F32)
    dt_bias = jnp.concatenate([dt_bias_fwd[0], dt_bias_bwd[0], dt_pad]).reshape(1, DT_COLS)
    a_log = jnp.concatenate([a_log_fwd[0], a_log_bwd[0], dt_pad]).reshape(1, DT_COLS)
    d_skip_cols = jnp.repeat(d_skip[0], SSD_HEAD_DIM).reshape(1, D_INNER)
    w_bs = w_branch_ssd[0].astype(BF16)
    w_ba = w_branch_attn[0].astype(BF16)
    w_o = w_out[0].astype(BF16)
    w_router_t = w_router[0].T.astype(BF16)

    merged = None
    for (x, _), (batch, seq), row_off, req_off in zip(groups, shapes, row_offs, req_offs):
        shift1, scale1, gate1, shift2, scale2, _ = (
            a.reshape(batch, 1, D_MODEL) for a in jnp.split(ada[req_off:req_off + batch], 6, axis=-1))
        x2d = x.reshape(batch * seq, D_MODEL)
        proj = _inproj(x2d, scale1, shift1, w_main, seq, tm=1024, tn=2560, out_dtype=F32)
        dt_raw = _inproj(x2d, scale1, shift1, w_dt, seq, tm=1024, tn=DT_COLS, out_dtype=F32)
        xbc = _conv(proj, conv_w[0], conv_b, 0, batch, seq)
        y = _ssd(xbc, dt_raw, dt_bias, a_log, None, 0, batch, seq, reverse=False)
        y = _ssd(xbc, dt_raw, dt_bias, a_log, y, 0, batch, seq, reverse=True)
        y_ssd = _ssd_out(y, xbc, proj, d_skip_cols, ssd_norm_w, w_bs, 0)
        tables = _rope_tables(seq)
        attn = []
        for gi, (_, dil) in enumerate(ATTN_PATTERNS):
            q, k, v = _prep(proj, tables, gi, dil, 0, batch, seq)
            attn.append(_band_attn(q, k, v))
        merged = _merge(x2d, y_ssd, proj, attn, gate1, scale2, shift2, w_ba, w_o, ln1_g, ln1_b, w_router_t,
                        batch, seq, row_off, t_all, merged)
    x1, h2, logits_t = merged

    top_idx, top_w, rank, counts = _route(logits_t, router_bias[0].reshape(N_EXPERTS, 1))
    counts = counts[:, 0]
    padded = (counts + EXPERT_BLOCK - 1) // EXPERT_BLOCK * EXPERT_BLOCK
    pad_end = jnp.cumsum(padded)
    pad_start = pad_end - padded
    n_blocks = (t_all * TOP_K + N_EXPERTS * (EXPERT_BLOCK - 1)) // EXPERT_BLOCK
    n_rows = n_blocks * EXPERT_BLOCK
    tm_moe = 256
    dest_tiles = _dest(top_idx, rank, pad_start.astype(I32).reshape(N_EXPERTS, 1), tm_moe)
    n_valid = (pad_end[-1] // EXPERT_BLOCK).astype(I32).reshape(1)
    blk_start = jnp.minimum(jnp.arange(n_blocks, dtype=I32) * EXPERT_BLOCK, pad_end[-1] - 1)
    blk_exp = jnp.sum((pad_end[None, :] <= blk_start[:, None]).astype(I32), axis=1)
    blk_exp = jnp.minimum(blk_exp, N_EXPERTS - 1)

    xs = _dispatch(pad_end.astype(I32), dest_tiles, h2, n_rows, tm_moe)
    ys = _experts(blk_exp, n_valid, xs, w_exp_gate[0], w_exp_up[0], w_exp_down[0])
    outs = []
    for (x, _), n_tok, row_off in zip(groups, tokens, row_offs):
        out = _combine(dest_tiles, ys, top_w.T, x1, h2, gate2, w_sh_gate[0].astype(BF16), w_sh_up[0].astype(BF16),
                       w_sh_down[0].astype(BF16), ln2_g, ln2_b, seg, tm_moe, row_off // tm_moe, n_tok // tm_moe)
        outs.append(out.reshape(x.shape))
    return tuple(outs)
```

```python
import functools
import math

import jax
import jax.numpy as jnp
from jax import lax
from jax.experimental import pallas as pl
from jax.experimental.pallas import tpu as pltpu

F32 = jnp.float32
BF16 = jnp.bfloat16
I32 = jnp.int32

D_MODEL = 1024
D_INNER = 2048
SSD_HEADS = 32
SSD_HEAD_DIM = 64
SSD_GROUPS = 8
SSD_STATE = 128
SSD_CHUNK = 128
SSD_CHUNKS_PER_STEP = 4
HEADS_PER_SSD_GROUP = SSD_HEADS // SSD_GROUPS
CONV_WIDTH = 5
CONV_PAD = CONV_WIDTH // 2
CONV_DIM = D_INNER + 2 * SSD_GROUPS * SSD_STATE
ATTN_PATTERNS = ((128, 1), (512, 4), (2048, 16))
ATTN_GROUPS = len(ATTN_PATTERNS)
ATTN_HEADS_PER_GROUP = 8
ATTN_HEAD_DIM = 64
ATTN_GROUP_COLS = ATTN_HEADS_PER_GROUP * ATTN_HEAD_DIM
ATTN_DIM = ATTN_GROUPS * ATTN_GROUP_COLS
ROT_DIM = ATTN_HEAD_DIM // 4
ROT_HALF = ROT_DIM // 2
ROPE_THETA = 500000.0
N_EXPERTS = 256
EXPERT_HIDDEN = 256
TOP_K = 8
N_EXPERT_GROUPS = 8
EXPERTS_PER_GROUP = N_EXPERTS // N_EXPERT_GROUPS
TOPK_GROUPS = 4
ROUTED_SCALE = 2.5
SHARED_HIDDEN = 256
EXPERT_BLOCK = 512
PACKED_COLS = D_MODEL // 2
ROW_SUBLANES = PACKED_COLS // 128
DEPTH = 1
ALPHA = (2.0 * DEPTH) ** 0.25
NORM_EPS = 1e-5

COL_Z = 0
COL_XBC = COL_Z + D_INNER
COL_GSSD = COL_XBC + CONV_DIM
COL_GATTN = COL_GSSD + D_MODEL
COL_Q = COL_GATTN + D_MODEL
COL_K = COL_Q + ATTN_DIM
COL_V = COL_K + ATTN_DIM
N_MAIN = COL_V + ATTN_DIM
DT_COLS = 128

LANES = 128
SUBLANES = 8
VMEM_LIMIT = 56 * 1024 * 1024

DMA_PRIORITIES = 2

BAND_RADIUS = 64
ATTN_TQ = 128
NEG_INF = float("-inf")


def _cparams(sem):
    return pltpu.CompilerParams(dimension_semantics=sem, vmem_limit_bytes=VMEM_LIMIT)


def _sigmoid(x):
    return 1.0 / (1.0 + jnp.exp(-x))


def _silu(x):
    return x * _sigmoid(x)


def _softplus(x):
    return jnp.maximum(x, 0.0) + jnp.log(1.0 + jnp.exp(-jnp.abs(x)))


def _pack_halves(x):
    c = x.shape[1] // 2
    hi = lax.bitcast_convert_type(x[:, :c].astype(BF16).astype(F32), jnp.uint32)
    lo = lax.bitcast_convert_type(x[:, c:].astype(BF16).astype(F32), jnp.uint32)
    return hi | (lo >> 16)


def _unpack_halves(u):
    left = lax.bitcast_convert_type(u & jnp.uint32(0xFFFF0000), F32)
    right = lax.bitcast_convert_type(u << 16, F32)
    return left, right


def _store_rows(ref, lead, x):
    rows = x.shape[0]
    packed = _pack_halves(x)
    for j in range(ROW_SUBLANES):
        idx = lead + (pl.ds(j, rows, stride=ROW_SUBLANES), slice(None))
        ref[idx] = packed[:, j * LANES:(j + 1) * LANES]


def _load_row_pieces(ref, lead, rows, base=0):
    lefts, rights = [], []
    for j in range(ROW_SUBLANES):
        idx = lead + (pl.ds(base + j, rows, stride=ROW_SUBLANES), slice(None))
        left, right = _unpack_halves(ref[idx])
        lefts.append(left)
        rights.append(right)
    return lefts + rights


def _ada_kernel(c_ref, w_ref, b_ref, o_ref):
    c = _silu(c_ref[...]).astype(BF16)
    o_ref[...] = jnp.dot(c, w_ref[...].astype(BF16), preferred_element_type=F32) + b_ref[...]


def _ada(c_pad, w_ada, b_ada):
    rows = c_pad.shape[0]
    n = w_ada.shape[1]
    tn = 1536
    return pl.pallas_call(
        _ada_kernel,
        grid=(n // tn,),
        in_specs=[pl.BlockSpec((rows, D_MODEL), lambda j: (0, 0)),
                  pl.BlockSpec((D_MODEL, tn), lambda j: (0, j)),
                  pl.BlockSpec((1, tn), lambda j: (0, j))],
        out_specs=pl.BlockSpec((rows, tn), lambda j: (0, j)),
        out_shape=jax.ShapeDtypeStruct((rows, n), F32),
        compiler_params=_cparams(("arbitrary",)),
        name="ada",
    )(c_pad, w_ada, b_ada)


def _inproj_kernel(x_ref, sc_ref, sh_ref, w_ref, wdt_ref, o_ref, dt_ref, h_ref):
    @pl.when(pl.program_id(1) == 0)
    def _():
        h_ref[...] = (x_ref[...] * (1.0 + sc_ref[0]) + sh_ref[0]).astype(BF16)
        dt_ref[...] = jnp.dot(h_ref[...], wdt_ref[...], preferred_element_type=F32)

    o_ref[...] = jnp.dot(h_ref[...], w_ref[...], preferred_element_type=F32).astype(o_ref.dtype)


def _inproj(x, scale, shift, w, w_dt, seg, tm, tn, out_dtype):
    t = x.shape[0]
    n = w.shape[1]
    per_seg = seg // tm
    return pl.pallas_call(
        _inproj_kernel,
        grid=(t // tm, n // tn),
        in_specs=[pl.BlockSpec((tm, D_MODEL), lambda i, j: (i, 0)),
                  pl.BlockSpec((1, 1, D_MODEL), lambda i, j: (i // per_seg, 0, 0)),
                  pl.BlockSpec((1, 1, D_MODEL), lambda i, j: (i // per_seg, 0, 0)),
                  pl.BlockSpec((D_MODEL, tn), lambda i, j: (0, j)),
                  pl.BlockSpec((D_MODEL, DT_COLS), lambda i, j: (0, 0))],
        out_specs=[pl.BlockSpec((tm, tn), lambda i, j: (i, j)),
                   pl.BlockSpec((tm, DT_COLS), lambda i, j: (i, 0))],
        out_shape=[jax.ShapeDtypeStruct((t, n), out_dtype), jax.ShapeDtypeStruct((t, DT_COLS), F32)],
        scratch_shapes=[pltpu.VMEM((tm, D_MODEL), BF16)],
        compiler_params=_cparams(("arbitrary", "arbitrary")),
        name="inproj",
    )(x, scale, shift, w, w_dt)


def _conv_kernel(xp_ref, x_ref, xn_ref, w_ref, b_ref, o_ref, *, n_l):
    l = pl.program_id(1)
    tl = x_ref.shape[0]
    xp = jnp.where(l > 0, xp_ref[...], 0.0)
    xn = jnp.where(l < n_l - 1, xn_ref[...], 0.0)
    xx = jnp.concatenate([xp, x_ref[...], xn], axis=0)
    acc = b_ref[...] + w_ref[CONV_PAD:CONV_PAD + 1, :] * x_ref[...]
    for k in range(CONV_WIDTH):
        if k != CONV_PAD:
            shifted = pltpu.roll(xx, (CONV_PAD - k) % xx.shape[0], axis=0)
            acc = acc + w_ref[k:k + 1, :] * shifted[SUBLANES:SUBLANES + tl, :]
    o_ref[...] = _silu(acc)


def _conv(proj, conv_w, conv_b, row_off, batch, seq, tl=2048, tc=512):
    t = proj.shape[0]
    n_l = seq // tl
    n_c = CONV_DIM // tc
    c0 = COL_XBC // tc
    r8 = tl // SUBLANES
    last8 = t // SUBLANES - 1

    def row_blk(b, l):
        return (row_off + b * seq) // tl + l

    return pl.pallas_call(
        functools.partial(_conv_kernel, n_l=n_l),
        grid=(batch, n_l, n_c),
        in_specs=[pl.BlockSpec((SUBLANES, tc), lambda b, l, c: (jnp.maximum(row_blk(b, l) * r8 - 1, 0), c0 + c)),
                  pl.BlockSpec((tl, tc), lambda b, l, c: (row_blk(b, l), c0 + c)),
                  pl.BlockSpec((SUBLANES, tc), lambda b, l, c: (jnp.minimum((row_blk(b, l) + 1) * r8, last8), c0 + c)),
                  pl.BlockSpec((CONV_WIDTH, tc), lambda b, l, c: (0, c)),
                  pl.BlockSpec((1, tc), lambda b, l, c: (0, c))],
        out_specs=pl.BlockSpec((tl, tc), lambda b, l, c: (b * n_l + l, c)),
        out_shape=jax.ShapeDtypeStruct((batch * seq, CONV_DIM), F32),
        compiler_params=_cparams(("arbitrary", "arbitrary", "arbitrary")),
        name="conv",
    )(proj, proj, proj, conv_w, conv_b)


def _exact_tri_matmul(tri_bf16, x):
    hi = x.astype(BF16)
    r1 = x - hi.astype(F32)
    mid = r1.astype(BF16)
    lo = (r1 - mid.astype(F32)).astype(BF16)
    out = jnp.dot(tri_bf16, hi, preferred_element_type=F32)
    out = out + jnp.dot(tri_bf16, mid, preferred_element_type=F32)
    return out + jnp.dot(tri_bf16, lo, preferred_element_type=F32)


def _ssd_kernel(*refs, reverse, accumulate):
    if accumulate:
        x_ref, b_ref, c_ref, dt_ref, dtb_ref, alog_ref, yin_ref, y_ref, state_ref = refs
    else:
        x_ref, b_ref, c_ref, dt_ref, dtb_ref, alog_ref, y_ref, state_ref = refs
        yin_ref = None

    @pl.when(pl.program_id(1) == 0)
    def _():
        state_ref[...] = jnp.zeros(state_ref.shape, F32)

    for sub in (range(SSD_CHUNKS_PER_STEP - 1, -1, -1) if reverse else range(SSD_CHUNKS_PER_STEP)):
        _ssd_chunk(x_ref, b_ref, c_ref, dt_ref, dtb_ref, alog_ref, yin_ref, y_ref, state_ref,
                   slice(sub * SSD_CHUNK, (sub + 1) * SSD_CHUNK), reverse)


def _ssd_chunk(x_ref, b_ref, c_ref, dt_ref, dtb_ref, alog_ref, yin_ref, y_ref, state_ref, rows, reverse):
    lane0 = SSD_HEADS if reverse else 0
    accumulate = yin_ref is not None
    dt = _softplus(dt_ref[rows, :] + dtb_ref[...])
    da = dt * (-jnp.exp(alog_ref[...]))
    row = lax.broadcasted_iota(I32, (SSD_CHUNK, SSD_CHUNK), 0)
    col = lax.broadcasted_iota(I32, (SSD_CHUNK, SSD_CHUNK), 1)
    tri = (col >= row) if reverse else (col <= row)
    cum = _exact_tri_matmul(tri.astype(BF16), da)
    cum_t = cum.T
    dt_t = dt.T
    edge = 0 if reverse else SSD_CHUNK - 1
    total_b = jnp.broadcast_to(cum_t[:, edge:edge + 1], (LANES, SSD_CHUNK))
    to_end_t = dt_t * jnp.exp(total_b - cum_t)
    chunk_decay_b = jnp.exp(total_b)
    src_t = cum_t - jnp.log(dt_t)
    first_head = lax.broadcasted_iota(I32, (1, 2 * SSD_HEAD_DIM), 1) < SSD_HEAD_DIM
    pairs = HEADS_PER_SSD_GROUP // 2

    for g in range(SSD_GROUPS):
        bg = b_ref[rows, g * SSD_STATE:(g + 1) * SSD_STATE]
        cg = c_ref[rows, g * SSD_STATE:(g + 1) * SSD_STATE]
        cb = lax.dot_general(cg.astype(BF16), bg.astype(BF16), (((1,), (1,)), ((), ())),
                             preferred_element_type=F32)
        bg_t = bg.T
        for pr in range(pairs):
            h0 = g * HEADS_PER_SSD_GROUP + 2 * pr
            cols = slice(h0 * SSD_HEAD_DIM, (h0 + 2) * SSD_HEAD_DIM)
            x_pair = x_ref[rows, cols].astype(BF16)
            state = state_ref[g * pairs + pr]
            rhs = jnp.concatenate([x_pair, state.astype(BF16)], axis=0)
            res = []
            new = []
            for e in range(2):
                ln = lane0 + h0 + e
                cum_l = jnp.broadcast_to(cum[:, ln:ln + 1], (SSD_CHUNK, SSD_CHUNK))
                decay = jnp.exp(jnp.where(tri, cum_l - src_t[ln:ln + 1, :], NEG_INF))
                lhs = jnp.concatenate([(cb * decay).astype(BF16), (cg * jnp.exp(cum_l)).astype(BF16)], axis=1)
                res.append(jnp.dot(lhs, rhs, preferred_element_type=F32))
                b_scaled = (bg_t * to_end_t[ln:ln + 1, :]).astype(BF16)
                new.append(jnp.dot(b_scaled, x_pair, preferred_element_type=F32))
            y_pair = jnp.where(first_head, res[0], res[1])
            if accumulate:
                y_pair = y_pair + yin_ref[rows, cols]
            y_ref[rows, cols] = y_pair
            ln0 = lane0 + h0
            dec = jnp.where(first_head, chunk_decay_b[ln0:ln0 + 1, :], chunk_decay_b[ln0 + 1:ln0 + 2, :])
            state_ref[g * pairs + pr] = state * dec + jnp.where(first_head, new[0], new[1])


def _ssd(xbc, dt_raw, dt_bias, a_log, y_in, row_off, batch, seq, reverse):
    rows = SSD_CHUNK * SSD_CHUNKS_PER_STEP
    n_c = seq // rows
    accumulate = y_in is not None

    def loc(b, c):
        return b * n_c + (n_c - 1 - c if reverse else c)

    def glob(b, c):
        return row_off // rows + loc(b, c)

    in_specs = [pl.BlockSpec((rows, D_INNER), lambda b, c: (loc(b, c), 0)),
                pl.BlockSpec((rows, SSD_GROUPS * SSD_STATE), lambda b, c: (loc(b, c), 2)),
                pl.BlockSpec((rows, SSD_GROUPS * SSD_STATE), lambda b, c: (loc(b, c), 3)),
                pl.BlockSpec((rows, DT_COLS), lambda b, c: (glob(b, c), 0)),
                pl.BlockSpec((1, DT_COLS), lambda b, c: (0, 0)),
                pl.BlockSpec((1, DT_COLS), lambda b, c: (0, 0))]
    args = [xbc, xbc, xbc, dt_raw, dt_bias, a_log]
    aliases = {}
    if accumulate:
        in_specs.append(pl.BlockSpec((rows, D_INNER), lambda b, c: (loc(b, c), 0)))
        args.append(y_in)
        aliases = {len(args) - 1: 0}
    return pl.pallas_call(
        functools.partial(_ssd_kernel, reverse=reverse, accumulate=accumulate),
        grid=(batch, n_c),
        in_specs=in_specs,
        out_specs=pl.BlockSpec((rows, D_INNER), lambda b, c: (loc(b, c), 0)),
        out_shape=jax.ShapeDtypeStruct((batch * seq, D_INNER), F32),
        scratch_shapes=[pltpu.VMEM((SSD_HEADS // 2, SSD_STATE, 2 * SSD_HEAD_DIM), F32)],
        input_output_aliases=aliases,
        compiler_params=_cparams(("arbitrary", "arbitrary")),
        name="ssd_bwd" if reverse else "ssd_fwd",
    )(*args)


def _ssd_out_kernel(y_ref, x_ref, z_ref, dskip_ref, nw_ref, w_ref, o_ref):
    y = y_ref[...] + dskip_ref[...] * x_ref[...]
    y = y * _silu(z_ref[...])
    ms = jnp.mean(y * y, axis=-1, keepdims=True)
    yn = y * lax.rsqrt(ms + NORM_EPS) * nw_ref[...]
    o_ref[...] = jnp.dot(yn.astype(BF16), w_ref[...], preferred_element_type=F32)


def _ssd_out(y, xbc, proj, d_skip_cols, norm_w, w_branch, row_off, tm=512):
    t = y.shape[0]
    return pl.pallas_call(
        _ssd_out_kernel,
        grid=(t // tm,),
        in_specs=[pl.BlockSpec((tm, D_INNER), lambda i: (i, 0)),
                  pl.BlockSpec((tm, D_INNER), lambda i: (i, 0)),
                  pl.BlockSpec((tm, D_INNER), lambda i: (row_off // tm + i, COL_Z // D_INNER)),
                  pl.BlockSpec((1, D_INNER), lambda i: (0, 0)),
                  pl.BlockSpec((1, D_INNER), lambda i: (0, 0)),
                  pl.BlockSpec((D_INNER, D_MODEL), lambda i: (0, 0))],
        out_specs=pl.BlockSpec((tm, D_MODEL), lambda i: (i, 0)),
        out_shape=jax.ShapeDtypeStruct((t, D_MODEL), F32),
        compiler_params=_cparams(("arbitrary",)),
        name="ssd_out",
    )(y, xbc, proj, d_skip_cols, norm_w, w_branch)


def _prep_kernel(q_ref, k_ref, v_ref, cos_ref, sa_ref, sb_ref, qo_ref, ko_ref, vo_ref, *, dil):
    rows = q_ref.shape[0] // dil

    def phase(ref, p):
        if dil == 1:
            return ref[...]
        return ref[pl.ds(p, rows, stride=dil), :]

    for p in range(dil):
        cos = phase(cos_ref, p)
        sa = phase(sa_ref, p)
        sb = phase(sb_ref, p)

        def rot(t):
            return (t * cos + pltpu.roll(t, LANES - ROT_HALF, axis=1) * sa
                    + pltpu.roll(t, ROT_HALF, axis=1) * sb)

        qo_ref[0, p] = (rot(phase(q_ref, p)) * (ATTN_HEAD_DIM ** -0.5)).astype(qo_ref.dtype)
        ko_ref[0, p] = rot(phase(k_ref, p)).astype(ko_ref.dtype)
        vo_ref[0, p] = phase(v_ref, p).astype(vo_ref.dtype)


def _prep(proj, tables, gi, dil, row_off, batch, seq, rows_per_step=2048):
    r = rows_per_step
    n_i = seq // r
    n_c = ATTN_GROUP_COLS // LANES
    m = seq // dil
    cos_t, sa_t, sb_t = tables

    def in_blk(col):
        cb = (col + gi * ATTN_GROUP_COLS) // LANES
        return pl.BlockSpec((r, LANES), lambda b, i, c: ((row_off + b * seq) // r + i, cb + c))

    tab = pl.BlockSpec((r, LANES), lambda b, i, c: (i, 0))
    out = pl.BlockSpec((1, dil, r // dil, LANES), lambda b, i, c: (b, 0, i, c))
    shape = jax.ShapeDtypeStruct((batch, dil, m, ATTN_GROUP_COLS), BF16)
    return pl.pallas_call(
        functools.partial(_prep_kernel, dil=dil),
        grid=(batch, n_i, n_c),
        in_specs=[in_blk(COL_Q), in_blk(COL_K), in_blk(COL_V), tab, tab, tab],
        out_specs=[out, out, out],
        out_shape=[shape, shape, shape],
        compiler_params=_cparams(("arbitrary", "arbitrary", "arbitrary")),
        name=f"attn_prep_d{dil}",
    )(proj, proj, proj, cos_t, sa_t, sb_t)


def _band_attn_kernel(q_ref, kp_ref, kc_ref, kn_ref, vp_ref, vc_ref, vn_ref, o_ref, lse_ref, *, m_len):
    i = pl.program_id(1)
    tq = ATTN_TQ
    n_sub = q_ref.shape[1] // tq
    tk = tq + 2 * BAND_RADIUS
    q_all = q_ref[0].astype(BF16)
    k_all = jnp.concatenate([kp_ref[0], kc_ref[0], kn_ref[0]], axis=0).astype(BF16)
    v_all = jnp.concatenate([vp_ref[0], vc_ref[0], vn_ref[0]], axis=0).astype(BF16)
    first_head = lax.broadcasted_iota(I32, (1, LANES), 1) < ATTN_HEAD_DIM
    ones = jnp.ones((tk, LANES), BF16)
    zero = jnp.zeros((), BF16)
    for sub in range(n_sub):
        start = (i * n_sub + sub) * tq
        rows = slice(sub * tq, (sub + 1) * tq)
        q = q_all[rows]
        k = k_all[sub * tq:sub * tq + tk]
        v = v_all[sub * tq:sub * tq + tk]
        qpos = start + lax.broadcasted_iota(I32, (tq, tk), 0)
        kpos = start - BAND_RADIUS + lax.broadcasted_iota(I32, (tq, tk), 1)
        valid = (jnp.abs(qpos - kpos) <= BAND_RADIUS) & (kpos >= 0) & (kpos < m_len)
        for pr in range(ATTN_GROUP_COLS // LANES):
            cols = slice(pr * LANES, (pr + 1) * LANES)
            q_pair, k_pair = q[:, cols], k[:, cols]
            v_ext = jnp.concatenate([v[:, cols], ones], axis=1)
            outs, lses = [], []
            for sel in (first_head, jnp.logical_not(first_head)):
                s = lax.dot_general(jnp.where(sel, q_pair, zero), k_pair, (((1,), (1,)), ((), ())),
                                    preferred_element_type=F32)
                s = jnp.where(valid, s, NEG_INF)
                mx = jnp.max(s, axis=-1, keepdims=True)
                p = jnp.exp(s - mx).astype(BF16)
                ov = jnp.dot(p, v_ext, preferred_element_type=F32)
                den = ov[:, LANES:]
                outs.append(ov[:, :LANES] / den)
                lses.append(mx + jnp.log(den))
            o_ref[0, rows, cols] = jnp.where(first_head, outs[0], outs[1])
            lse_ref[0, rows, cols] = jnp.where(first_head, lses[0], lses[1])


def _band_attn(q, k, v):
    b, d, m, c = q.shape
    n = b * d
    q, k, v = (a.reshape(n, m, c) for a in (q, k, v))
    tq = ATTN_TQ * (2 if m % (2 * ATTN_TQ) == 0 else 1)
    per = tq // BAND_RADIUS
    last = m // BAND_RADIUS - 1
    main = pl.BlockSpec((1, tq, c), lambda s, i: (s, i, 0))
    prev = pl.BlockSpec((1, BAND_RADIUS, c), lambda s, i: (s, jnp.maximum(i * per - 1, 0), 0))
    nxt = pl.BlockSpec((1, BAND_RADIUS, c), lambda s, i: (s, jnp.minimum((i + 1) * per, last), 0))
    shape = jax.ShapeDtypeStruct((n, m, c), F32)
    o, lse = pl.pallas_call(
        functools.partial(_band_attn_kernel, m_len=m),
        grid=(n, m // tq),
        in_specs=[main, prev, main, nxt, prev, main, nxt],
        out_specs=[main, main],
        out_shape=[shape, shape],
        compiler_params=_cparams(("arbitrary", "arbitrary")),
        name=f"band_attn_m{m}",
    )(q, k, k, k, v, v, v)
    return o.reshape(b, d, m, c), lse.reshape(b, d, m, c)


def _layer_norm(x, g, b):
    mu = jnp.mean(x, axis=-1, keepdims=True)
    xc = x - mu
    var = jnp.mean(xc * xc, axis=-1, keepdims=True)
    return xc * lax.rsqrt(var + NORM_EPS) * g + b


def _merge_kernel(x_ref, yssd_ref, gs_ref, ga_ref, o1_ref, l1_ref, o2_ref, l2_ref, o3_ref, l3_ref,
                  g1_ref, sc2_ref, sh2_ref, wba_ref, wout_ref, lng_ref, lnb_ref, wr_ref, *rest, n_prev):
    x1_ref, h2_ref, logit_ref, scr_ref = rest[n_prev:]
    tm = x_ref.shape[0]

    def interleaved(ref, dil, slot):
        if dil == 1:
            return ref[0, 0]
        rows = tm // dil
        n_c = ATTN_GROUP_COLS // LANES
        for p in range(dil):
            for c in range(n_c):
                scr_ref[slot * n_c + c, pl.ds(p, rows, stride=dil), :] = ref[0, p, :, c * LANES:(c + 1) * LANES]
        return jnp.concatenate([scr_ref[slot * n_c + c] for c in range(n_c)], axis=1)

    outs = []
    lses = []
    slot = 0
    for (o_ref, l_ref), (_, dil) in zip(((o1_ref, l1_ref), (o2_ref, l2_ref), (o3_ref, l3_ref)), ATTN_PATTERNS):
        outs.append(interleaved(o_ref, dil, slot))
        lses.append(interleaved(l_ref, dil, slot + 1))
        slot += 2
    mx = jnp.maximum(jnp.maximum(lses[0], lses[1]), lses[2])
    es = [jnp.exp(l - mx) for l in lses]
    den = es[0] + es[1] + es[2]
    y_att = (es[0] * outs[0] + es[1] * outs[1] + es[2] * outs[2]) / den
    y_attn = jnp.dot(y_att.astype(BF16), wba_ref[...], preferred_element_type=F32)
    merged = _sigmoid(gs_ref[...]) * yssd_ref[...] + _sigmoid(ga_ref[...]) * y_attn
    mix = jnp.dot(merged.astype(BF16), wout_ref[...], preferred_element_type=F32)
    x1 = _layer_norm(ALPHA * x_ref[...] + g1_ref[0] * mix, lng_ref[...], lnb_ref[...])
    x1_ref[...] = x1
    h2 = x1 * (1.0 + sc2_ref[0]) + sh2_ref[0]
    _store_rows(h2_ref, (), h2)
    logit_ref[...] = lax.dot_general(wr_ref[...], h2.astype(BF16), (((1,), (1,)), ((), ())),
                                     preferred_element_type=F32)


def _merge(x, y_ssd, proj, attn, gate1, scale2, shift2, w_ba, w_out, ln_g, ln_b, w_router_t,
           batch, seq, out_row_off, t_all, prev, tm=512):
    n_i = seq // tm
    out0 = out_row_off // tm

    def rows(col_blk):
        return pl.BlockSpec((tm, D_MODEL), lambda b, i: (b * n_i + i, col_blk))

    def local(width):
        return pl.BlockSpec((tm, width), lambda b, i: (b * n_i + i, 0))

    def modv():
        return pl.BlockSpec((1, 1, D_MODEL), lambda b, i: (b, 0, 0))

    def full(shape):
        return pl.BlockSpec(shape, lambda b, i: tuple(0 for _ in shape))

    attn_specs = []
    attn_args = []
    for (o, lse), (_, dil) in zip(attn, ATTN_PATTERNS):
        spec = pl.BlockSpec((1, dil, tm // dil, ATTN_GROUP_COLS), lambda b, i: (b, 0, i, 0))
        attn_specs += [spec, spec]
        attn_args += [o, lse]
    n_in = 4 + len(attn_args) + 8
    return pl.pallas_call(
        functools.partial(_merge_kernel, n_prev=0 if prev is None else 3),
        grid=(batch, n_i),
        in_specs=[rows(0), local(D_MODEL), rows(COL_GSSD // D_MODEL), rows(COL_GATTN // D_MODEL)] + attn_specs
                 + [modv(), modv(), modv(), full((ATTN_GROUP_COLS, D_MODEL)), full((D_MODEL, D_MODEL)),
                    full((1, D_MODEL)), full((1, D_MODEL)), full((N_EXPERTS, D_MODEL))]
                 + ([pl.BlockSpec(memory_space=pl.ANY)] * 3 if prev is not None else []),
        out_specs=[pl.BlockSpec((tm, D_MODEL), lambda b, i: (out0 + b * n_i + i, 0)),
                   pl.BlockSpec((tm * ROW_SUBLANES, LANES), lambda b, i: (out0 + b * n_i + i, 0)),
                   pl.BlockSpec((N_EXPERTS, tm), lambda b, i: (0, out0 + b * n_i + i))],
        out_shape=[jax.ShapeDtypeStruct((t_all, D_MODEL), F32),
                   jax.ShapeDtypeStruct((t_all * ROW_SUBLANES, LANES), jnp.uint32),
                   jax.ShapeDtypeStruct((N_EXPERTS, t_all), F32)],
        scratch_shapes=[pltpu.VMEM((2 * ATTN_GROUPS * (ATTN_GROUP_COLS // LANES), tm, LANES), F32)],
        input_output_aliases={n_in + j: j for j in range(3)} if prev is not None else {},
        compiler_params=_cparams(("arbitrary", "arbitrary")),
        name="mixer_merge",
    )(x, y_ssd, proj, proj, *attn_args, gate1, scale2, shift2, w_ba, w_out, ln_g, ln_b, w_router_t,
      *(prev if prev is not None else ()))


def _route_kernel(logit_ref, bias_ref, idx_ref, w_ref, rank_ref, cnt_ref, run_ref):
    tm = logit_ref.shape[1]

    @pl.when(pl.program_id(0) == 0)
    def _():
        run_ref[...] = jnp.zeros(run_ref.shape, F32)

    scores = _sigmoid(logit_ref[...])
    sel = scores + bias_ref[...]
    iota_g = lax.broadcasted_iota(I32, (EXPERTS_PER_GROUP, tm), 0)
    grp = []
    for g in range(N_EXPERT_GROUPS):
        sg = sel[g * EXPERTS_PER_GROUP:(g + 1) * EXPERTS_PER_GROUP, :]
        top1 = jnp.max(sg, axis=0, keepdims=True)
        first = jnp.min(jnp.where(sg == top1, iota_g, EXPERTS_PER_GROUP), axis=0, keepdims=True)
        top2 = jnp.max(jnp.where(iota_g == first, NEG_INF, sg), axis=0, keepdims=True)
        grp.append(top1 + top2)
    gs = jnp.concatenate(grp, axis=0)
    iota_n = lax.broadcasted_iota(I32, (N_EXPERT_GROUPS, tm), 0)
    keep_g = jnp.zeros((N_EXPERT_GROUPS, tm), F32)
    for _ in range(TOPK_GROUPS):
        best = jnp.max(gs, axis=0, keepdims=True)
        first = jnp.min(jnp.where(gs == best, iota_n, N_EXPERT_GROUPS), axis=0, keepdims=True)
        hit = iota_n == first
        keep_g = jnp.where(hit, 1.0, keep_g)
        gs = jnp.where(hit, NEG_INF, gs)
    cand = jnp.concatenate(
        [jnp.where(keep_g[g:g + 1, :] > 0.0, sel[g * EXPERTS_PER_GROUP:(g + 1) * EXPERTS_PER_GROUP, :], NEG_INF)
         for g in range(N_EXPERT_GROUPS)], axis=0)
    iota_e = lax.broadcasted_iota(I32, (N_EXPERTS, tm), 0)
    chosen = jnp.zeros((N_EXPERTS, tm), F32)
    idxs = []
    ws = []
    for _ in range(TOP_K):
        best = jnp.max(cand, axis=0, keepdims=True)
        first = jnp.min(jnp.where(cand == best, iota_e, N_EXPERTS), axis=0, keepdims=True)
        hit = iota_e == first
        idxs.append(first)
        ws.append(jnp.sum(jnp.where(hit, scores, 0.0), axis=0, keepdims=True))
        chosen = jnp.where(hit, 1.0, chosen)
        cand = jnp.where(hit, NEG_INF, cand)
    top_w = jnp.concatenate(ws, axis=0)
    top_w = top_w / jnp.sum(top_w, axis=0, keepdims=True) * ROUTED_SCALE
    idx_ref[...] = jnp.concatenate(idxs, axis=0)
    w_ref[...] = top_w
    s_i = lax.broadcasted_iota(I32, (tm, tm), 0)
    t_i = lax.broadcasted_iota(I32, (tm, tm), 1)
    before = (s_i < t_i).astype(BF16)
    prior = jnp.dot(chosen.astype(BF16), before, preferred_element_type=F32) + run_ref[...]
    ranks = [jnp.sum(jnp.where(iota_e == idxs[k], prior, 0.0), axis=0, keepdims=True) for k in range(TOP_K)]
    rank_ref[...] = jnp.concatenate(ranks, axis=0).astype(I32)
    run_ref[...] = run_ref[...] + jnp.sum(chosen, axis=1, keepdims=True)
    cnt_ref[...] = run_ref[...].astype(I32)


def _route(logits_t, bias_col, tm=512):
    t = logits_t.shape[1]
    tok = pl.BlockSpec((TOP_K, tm), lambda i: (0, i))
    return pl.pallas_call(
        _route_kernel,
        grid=(t // tm,),
        in_specs=[pl.BlockSpec((N_EXPERTS, tm), lambda i: (0, i)),
                  pl.BlockSpec((N_EXPERTS, 1), lambda i: (0, 0))],
        out_specs=[tok, tok, tok, pl.BlockSpec((N_EXPERTS, 1), lambda i: (0, 0))],
        out_shape=[jax.ShapeDtypeStruct((TOP_K, t), I32), jax.ShapeDtypeStruct((TOP_K, t), F32),
                   jax.ShapeDtypeStruct((TOP_K, t), I32), jax.ShapeDtypeStruct((N_EXPERTS, 1), I32)],
        scratch_shapes=[pltpu.VMEM((N_EXPERTS, 1), F32)],
        compiler_params=_cparams(("arbitrary",)),
        name="route",
    )(logits_t, bias_col)


def _dest_kernel(idx_ref, rank_ref, start_ref, o_ref):
    tm = idx_ref.shape[1]
    iota_e = lax.broadcasted_iota(I32, (N_EXPERTS, tm), 0)
    start = start_ref[...].astype(F32)
    for k in range(TOP_K):
        base = jnp.sum(jnp.where(iota_e == idx_ref[k:k + 1, :], start, 0.0), axis=0, keepdims=True)
        o_ref[0, :, k * tm:(k + 1) * tm] = (base.astype(I32) + rank_ref[k:k + 1, :]) * ROW_SUBLANES


def _dest(top_idx, rank, pad_start_col, tm):
    t = top_idx.shape[1]
    tok = pl.BlockSpec((TOP_K, tm), lambda i: (0, i))
    out = pl.pallas_call(
        _dest_kernel,
        grid=(t // tm,),
        in_specs=[tok, tok, pl.BlockSpec((N_EXPERTS, 1), lambda i: (0, 0))],
        out_specs=pl.BlockSpec((1, 1, TOP_K * tm), lambda i: (i, 0, 0)),
        out_shape=jax.ShapeDtypeStruct((t // tm, 1, TOP_K * tm), I32),
        compiler_params=_cparams(("arbitrary",)),
        name="moe_dest",
    )(top_idx, rank, pad_start_col)
    return out.reshape(t // tm, TOP_K * tm)


def _dispatch_kernel(pad_end_ref, dest_hbm, h_ref, xs_hbm, *scratch, tm, n_sub):
    dest_smem = scratch[:n_sub]
    zeros_ref, sem_idx, sem_zero, sem_row = scratch[n_sub:]
    i = pl.program_id(0)
    idx_copies = [pltpu.make_async_copy(dest_hbm.at[i * n_sub + j], dest_smem[j], sem_idx) for j in range(n_sub)]
    for c in idx_copies:
        c.start()

    blk_rows = EXPERT_BLOCK * ROW_SUBLANES

    def zero_copy(e):
        start = pl.multiple_of(jnp.maximum(pad_end_ref[e] - EXPERT_BLOCK, 0) * ROW_SUBLANES, blk_rows)
        return pltpu.make_async_copy(zeros_ref, xs_hbm.at[pl.ds(start, blk_rows)], sem_zero)

    @pl.when(i == 0)
    def _():
        zeros_ref[...] = jnp.zeros(zeros_ref.shape, zeros_ref.dtype)

        def start_zero(e, carry):
            zero_copy(e).start()
            return carry

        def wait_zero(e, carry):
            zero_copy(e).wait()
            return carry

        lax.fori_loop(0, N_EXPERTS, start_zero, 0)
        lax.fori_loop(0, N_EXPERTS, wait_zero, 0)

    for c in idx_copies:
        c.wait()

    def row_copy(j, t, k):
        dst = pl.multiple_of(dest_smem[j][k * tm + t], ROW_SUBLANES)
        src = pl.multiple_of((j * tm + t) * ROW_SUBLANES, ROW_SUBLANES)
        return pltpu.make_async_copy(h_ref.at[pl.ds(src, ROW_SUBLANES)],
                                     xs_hbm.at[pl.ds(dst, ROW_SUBLANES)], sem_row)

    for j in range(n_sub):
        def start_rows(t, carry, j=j):
            for k in range(TOP_K):
                row_copy(j, t, k).start(priority=k % DMA_PRIORITIES)
            return carry

        lax.fori_loop(0, tm, start_rows, 0)

    for j in range(n_sub):
        def wait_rows(t, carry, j=j):
            for k in range(TOP_K):
                row_copy(j, t, k).wait()
            return carry

        lax.fori_loop(0, tm, wait_rows, 0)


def _dispatch(pad_end, dest_tiles, h2, n_rows, tm, n_sub=4):
    n_tiles = dest_tiles.shape[0]
    assert n_tiles % n_sub == 0
    return pl.pallas_call(
        functools.partial(_dispatch_kernel, tm=tm, n_sub=n_sub),
        grid_spec=pltpu.PrefetchScalarGridSpec(
            num_scalar_prefetch=1,
            grid=(n_tiles // n_sub,),
            in_specs=[pl.BlockSpec(memory_space=pl.ANY),
                      pl.BlockSpec((n_sub * tm * ROW_SUBLANES, LANES), lambda i, pe: (i, 0))],
            out_specs=pl.BlockSpec(memory_space=pl.ANY),
            scratch_shapes=[pltpu.SMEM((TOP_K * tm,), I32) for _ in range(n_sub)] + [
                            pltpu.VMEM((EXPERT_BLOCK * ROW_SUBLANES, LANES), jnp.uint32),
                            pltpu.SemaphoreType.DMA, pltpu.SemaphoreType.DMA, pltpu.SemaphoreType.DMA]),
        out_shape=jax.ShapeDtypeStruct((n_rows * ROW_SUBLANES, LANES), jnp.uint32),
        compiler_params=_cparams(("arbitrary",)),
        name="moe_dispatch",
    )(pad_end, dest_tiles, h2)


def _experts_kernel(blk_exp_ref, n_valid_ref, xs_ref, wg_ref, wu_ref, wd_ref, ys_ref, wgu_ref, wdn_ref):
    i = pl.program_id(0)

    @pl.when(i < n_valid_ref[0])
    def _():
        @pl.when((i == 0) | (blk_exp_ref[i] != blk_exp_ref[jnp.maximum(i - 1, 0)]))
        def _():
            wgu_ref[:, :EXPERT_HIDDEN] = wg_ref[0].astype(BF16)
            wgu_ref[:, EXPERT_HIDDEN:] = wu_ref[0].astype(BF16)
            wdn_ref[...] = wd_ref[0].astype(BF16)

        x = jnp.concatenate([c.astype(BF16) for c in _load_row_pieces(xs_ref, (), EXPERT_BLOCK)], axis=1)
        gu = jnp.dot(x, wgu_ref[...], preferred_element_type=F32)
        hmid = (_silu(gu[:, :EXPERT_HIDDEN]) * gu[:, EXPERT_HIDDEN:]).astype(BF16)
        _store_rows(ys_ref, (), jnp.dot(hmid, wdn_ref[...], preferred_element_type=F32))


def _experts(blk_exp, n_valid, xs, w_gate, w_up, w_down):
    n_blocks = xs.shape[0] // (EXPERT_BLOCK * ROW_SUBLANES)

    def row_map(i, be, nv):
        return (jnp.minimum(i, nv[0] - 1), 0)

    def w_map(i, be, nv):
        return (be[i], 0, 0)

    return pl.pallas_call(
        _experts_kernel,
        grid_spec=pltpu.PrefetchScalarGridSpec(
            num_scalar_prefetch=2,
            grid=(n_blocks,),
            in_specs=[pl.BlockSpec((EXPERT_BLOCK * ROW_SUBLANES, LANES), row_map),
                      pl.BlockSpec((1, D_MODEL, EXPERT_HIDDEN), w_map),
                      pl.BlockSpec((1, D_MODEL, EXPERT_HIDDEN), w_map),
                      pl.BlockSpec((1, EXPERT_HIDDEN, D_MODEL), w_map)],
            out_specs=pl.BlockSpec((EXPERT_BLOCK * ROW_SUBLANES, LANES), row_map),
            scratch_shapes=[pltpu.VMEM((D_MODEL, 2 * EXPERT_HIDDEN), BF16),
                            pltpu.VMEM((EXPERT_HIDDEN, D_MODEL), BF16)]),
        out_shape=jax.ShapeDtypeStruct(xs.shape, jnp.uint32),
        compiler_params=_cparams(("arbitrary",)),
        name="moe_experts",
    )(blk_exp, n_valid, xs, w_gate, w_up, w_down)


def _combine_kernel(dest_hbm, ys_hbm, w_ref, x1_ref, h2_ref, g2_ref, wsg_ref, wsu_ref, wsd_ref,
                    lng_ref, lnb_ref, o_ref, dest0, dest1, rows0, rows1, sem_idx, sem0, sem1, *, tm, tile_off):
    i = pl.program_id(0)
    n_i = pl.num_programs(0)
    dest_smem, rows_ref, sem_row = (dest0, dest1), (rows0, rows1), (sem0, sem1)

    def row_copy(s, t, k):
        src = pl.multiple_of(dest_smem[s][k * tm + t], ROW_SUBLANES)
        dst = pl.multiple_of(t * ROW_SUBLANES, ROW_SUBLANES)
        return pltpu.make_async_copy(ys_hbm.at[pl.ds(src, ROW_SUBLANES)],
                                     rows_ref[s].at[k, pl.ds(dst, ROW_SUBLANES)], sem_row[s])

    def start_gather(tile, s):
        idx_copy = pltpu.make_async_copy(dest_hbm.at[tile + tile_off], dest_smem[s], sem_idx)
        idx_copy.start()
        idx_copy.wait()

        def start_rows(t, carry):
            for k in range(TOP_K):
                row_copy(s, t, k).start(priority=k % DMA_PRIORITIES)
            return carry

        lax.fori_loop(0, tm, start_rows, 0)

    @pl.when(i == 0)
    def _():
        start_gather(0, 0)

    for s in range(2):
        if s == 0:
            start_gather(2 * i + 1, 1)
        else:
            @pl.when(i + 1 < n_i)
            def _():
                start_gather(2 * i + 2, 0)

        rows = slice(s * tm, (s + 1) * tm)
        h2 = jnp.concatenate(
            [c.astype(BF16) for c in _load_row_pieces(h2_ref, (), tm, s * tm * ROW_SUBLANES)], axis=1)
        g = jnp.dot(h2, wsg_ref[...], preferred_element_type=F32)
        u = jnp.dot(h2, wsu_ref[...], preferred_element_type=F32)
        ffn = jnp.dot((_silu(g) * u).astype(BF16), wsd_ref[...], preferred_element_type=F32)

        def wait_rows(t, carry):
            for k in range(TOP_K):
                row_copy(s, t, k).wait()
            return carry

        lax.fori_loop(0, tm, wait_rows, 0)
        w = w_ref[rows, :]
        n_pieces = D_MODEL // LANES
        acc = [ffn[:, j * LANES:(j + 1) * LANES] for j in range(n_pieces)]
        for k in range(TOP_K):
            wk = jnp.broadcast_to(w[:, k:k + 1], (tm, LANES))
            pieces = _load_row_pieces(rows_ref[s], (k,), tm)
            acc = [a + wk * p for a, p in zip(acc, pieces)]
        ffn = jnp.concatenate(acc, axis=1)
        o_ref[rows, :] = _layer_norm(ALPHA * x1_ref[rows, :] + g2_ref[0] * ffn, lng_ref[...], lnb_ref[...])


def _combine(dest_tiles, ys, top_w_rows, x1, h2, gate2, w_sg, w_su, w_sd, ln_g, ln_b, seg, tm, tile_off, n_tiles):
    assert tile_off % 2 == 0 and n_tiles % 2 == 0 and seg % (2 * tm) == 0
    per_seg = seg // (2 * tm)
    off = tile_off // 2

    def full(shape):
        return pl.BlockSpec(shape, lambda i: tuple(0 for _ in shape))

    return pl.pallas_call(
        functools.partial(_combine_kernel, tm=tm, tile_off=tile_off),
        grid=(n_tiles // 2,),
        in_specs=[pl.BlockSpec(memory_space=pl.ANY), pl.BlockSpec(memory_space=pl.ANY),
                  pl.BlockSpec((2 * tm, TOP_K), lambda i: (i + off, 0)),
                  pl.BlockSpec((2 * tm, D_MODEL), lambda i: (i + off, 0)),
                  pl.BlockSpec((2 * tm * ROW_SUBLANES, LANES), lambda i: (i + off, 0)),
                  pl.BlockSpec((1, 1, D_MODEL), lambda i: ((i + off) // per_seg, 0, 0)),
                  full((D_MODEL, SHARED_HIDDEN)), full((D_MODEL, SHARED_HIDDEN)), full((SHARED_HIDDEN, D_MODEL)),
                  full((1, D_MODEL)), full((1, D_MODEL))],
        out_specs=pl.BlockSpec((2 * tm, D_MODEL), lambda i: (i, 0)),
        out_shape=jax.ShapeDtypeStruct((n_tiles * tm, D_MODEL), F32),
        scratch_shapes=[pltpu.SMEM((TOP_K * tm,), I32), pltpu.SMEM((TOP_K * tm,), I32),
                        pltpu.VMEM((TOP_K, tm * ROW_SUBLANES, LANES), jnp.uint32),
                        pltpu.VMEM((TOP_K, tm * ROW_SUBLANES, LANES), jnp.uint32),
                        pltpu.SemaphoreType.DMA, pltpu.SemaphoreType.DMA, pltpu.SemaphoreType.DMA],
        compiler_params=_cparams(("arbitrary",)),
        name="moe_combine",
    )(dest_tiles, ys, top_w_rows, x1, h2, gate2, w_sg, w_su, w_sd, ln_g, ln_b)


def _rope_tables(seq):
    inv_freq = ROPE_THETA ** (-jnp.arange(ROT_HALF, dtype=F32) * 2.0 / ROT_DIM)
    ang = jnp.arange(seq, dtype=F32)[:, None] * inv_freq[None, :]
    cos, sin = jnp.cos(ang), jnp.sin(ang)
    rest = ATTN_HEAD_DIM - ROT_DIM
    ones = jnp.ones((seq, rest), F32)
    zeros = jnp.zeros((seq, rest), F32)
    zh = jnp.zeros((seq, ROT_HALF), F32)
    reps = LANES // ATTN_HEAD_DIM
    cos_t = jnp.tile(jnp.concatenate([cos, cos, ones], axis=1), (1, reps))
    sa_t = jnp.tile(jnp.concatenate([-sin, zh, zeros], axis=1), (1, reps))
    sb_t = jnp.tile(jnp.concatenate([zh, sin, zeros], axis=1), (1, reps))
    return cos_t, sa_t, sb_t


def _split_in_weights(w_in):
    cuts = (D_INNER, CONV_DIM, SSD_HEADS, SSD_HEADS, ATTN_DIM, ATTN_DIM, ATTN_DIM, D_MODEL, D_MODEL)
    offs = [0]
    for c in cuts:
        offs.append(offs[-1] + c)
    z, xbc, dtf, dtb, q, k, v, gs, ga = (w_in[:, offs[i]:offs[i + 1]] for i in range(len(cuts)))
    w_main = jnp.concatenate([z, xbc, gs, ga, q, k, v], axis=1).astype(BF16)
    assert COL_GSSD % D_MODEL == 0 and COL_Q % ATTN_GROUP_COLS == 0 and w_main.shape[1] == N_MAIN
    pad = jnp.zeros((D_MODEL, DT_COLS - 2 * SSD_HEADS), F32)
    w_dt = jnp.concatenate([dtf, dtb, pad], axis=1).astype(BF16)
    return w_main, w_dt


def kernel(x_prompt, x_sample, c_prompt, c_sample, w_ada, b_ada, w_in, conv_w, conv_b, dt_bias_fwd, dt_bias_bwd, a_log_fwd, a_log_bwd, d_skip, ssd_norm_w, w_branch_ssd, w_branch_attn, w_out, ln1_g, ln1_b, w_router, router_bias, w_exp_gate, w_exp_up, w_exp_down, w_sh_gate, w_sh_up, w_sh_down, ln2_g, ln2_b):
    assert w_ada.shape[0] == DEPTH
    groups = ((x_prompt, c_prompt), (x_sample, c_sample))
    shapes = [(x.shape[0], x.shape[1]) for x, _ in groups]
    seg = math.gcd(*(s for _, s in shapes))
    tokens = [b * s for b, s in shapes]
    t_all = sum(tokens)
    row_offs = [0, tokens[0]]

    c_all = jnp.concatenate([c_prompt, c_sample], axis=0)
    n_req = c_all.shape[0]
    c_pad = jnp.zeros((-(-n_req // SUBLANES) * SUBLANES, D_MODEL), F32).at[:n_req].set(c_all)
    ada = _ada(c_pad, w_ada[0], b_ada)[:n_req]
    req_offs = [0, shapes[0][0]]
    reps = jnp.array([s // seg for b, s in shapes for _ in range(b)], dtype=I32)
    n_seg = t_all // seg
    gate2 = jnp.repeat(ada[:, 5 * D_MODEL:], reps, axis=0, total_repeat_length=n_seg).reshape(n_seg, 1, D_MODEL)

    w_main, w_dt = _split_in_weights(w_in[0])
    dt_pad = jnp.zeros((DT_COLS - 2 * SSD_HEADS,), F32)
    dt_bias = jnp.concatenate([dt_bias_fwd[0], dt_bias_bwd[0], dt_pad]).reshape(1, DT_COLS)
    a_log = jnp.concatenate([a_log_fwd[0], a_log_bwd[0], dt_pad]).reshape(1, DT_COLS)
    d_skip_cols = jnp.repeat(d_skip[0], SSD_HEAD_DIM).reshape(1, D_INNER)
    w_bs = w_branch_ssd[0].astype(BF16)
    w_ba = w_branch_attn[0].astype(BF16)
    w_o = w_out[0].astype(BF16)
    w_router_t = w_router[0].T.astype(BF16)

    merged = None
    for (x, _), (batch, seq), row_off, req_off in zip(groups, shapes, row_offs, req_offs):
        shift1, scale1, gate1, shift2, scale2, _ = (
            a.reshape(batch, 1, D_MODEL) for a in jnp.split(ada[req_off:req_off + batch], 6, axis=-1))
        x2d = x.reshape(batch * seq, D_MODEL)
        proj, dt_raw = _inproj(x2d, scale1, shift1, w_main, w_dt, seq, tm=1024, tn=2560, out_dtype=F32)
        xbc = _conv(proj, conv_w[0], conv_b, 0, batch, seq)
        y = _ssd(xbc, dt_raw, dt_bias, a_log, None, 0, batch, seq, reverse=False)
        y = _ssd(xbc, dt_raw, dt_bias, a_log, y, 0, batch, seq, reverse=True)
        y_ssd = _ssd_out(y, xbc, proj, d_skip_cols, ssd_norm_w, w_bs, 0)
        tables = _rope_tables(seq)
        attn = []
        for gi, (_, dil) in enumerate(ATTN_PATTERNS):
            q, k, v = _prep(proj, tables, gi, dil, 0, batch, seq)
            attn.append(_band_attn(q, k, v))
        merged = _merge(x2d, y_ssd, proj, attn, gate1, scale2, shift2, w_ba, w_o, ln1_g, ln1_b, w_router_t,
                        batch, seq, row_off, t_all, merged)
    x1, h2, logits_t = merged

    top_idx, top_w, rank, counts = _route(logits_t, router_bias[0].reshape(N_EXPERTS, 1))
    counts = counts[:, 0]
    padded = (counts + EXPERT_BLOCK - 1) // EXPERT_BLOCK * EXPERT_BLOCK
    pad_end = jnp.cumsum(padded)
    pad_start = pad_end - padded
    n_blocks = (t_all * TOP_K + N_EXPERTS * (EXPERT_BLOCK - 1)) // EXPERT_BLOCK
    n_rows = n_blocks * EXPERT_BLOCK
    tm_moe = 256
    dest_tiles = _dest(top_idx, rank, pad_start.astype(I32).reshape(N_EXPERTS, 1), tm_moe)
    n_valid = (pad_end[-1] // EXPERT_BLOCK).astype(I32).reshape(1)
    blk_start = jnp.minimum(jnp.arange(n_blocks, dtype=I32) * EXPERT_BLOCK, pad_end[-1] - 1)
    blk_exp = jnp.sum((pad_end[None, :] <= blk_start[:, None]).astype(I32), axis=1)
    blk_exp = jnp.minimum(blk_exp, N_EXPERTS - 1)

    xs = _dispatch(pad_end.astype(I32), dest_tiles, h2, n_rows, tm_moe)
    ys = _experts(blk_exp, n_valid, xs, w_exp_gate[0], w_exp_up[0], w_exp_down[0])
    outs = []
    for (x, _), n_tok, row_off in zip(groups, tokens, row_offs):
        out = _combine(dest_tiles, ys, top_w.T, x1, h2, gate2, w_sh_gate[0].astype(BF16), w_sh_up[0].astype(BF16),
                       w_sh_down[0].astype(BF16), ln2_g, ln2_b, seg, tm_moe, row_off // tm_moe, n_tok // tm_moe)
        outs.append(out.reshape(x.shape))
    return tuple(outs)
```
